```python
import math
import jax, jax.numpy as jnp
from jax import lax
import numpy as np

D_MODEL = 1024
BATCH = 8
SEQ = 4096
DEPTH = 1

GDN_HEADS = 4
GDN_DK = 128
GDN_DV = 128
CONV_K = 4
CHUNK = 64
ATTN_HEADS = 8
ATTN_DH = 64
DILATED_PAIRS = ((128, 1), (512, 4), (2048, 16))
QBLK = 128
NUM_BUCKETS = 32
REL_MAX_DIST = 2048
N_GROUPS = 4
EXPERTS_PER_GROUP = 8
N_EXPERTS = N_GROUPS * EXPERTS_PER_GROUP
TOP_K = 2
D_EXPERT = 256
MOE_BLK = 128

EPS = 1e-6
NEG_INF = -1e30

GDN_QK_W = GDN_HEADS * GDN_DK
GDN_V_W = GDN_HEADS * GDN_DV
ATTN_W = ATTN_HEADS * ATTN_DH
MIX_W = GDN_V_W + ATTN_W
GDN_CONV_W = 2 * GDN_QK_W + GDN_V_W
IN_COLS = GDN_CONV_W + GDN_V_W + 2 * GDN_HEADS + 3 * ATTN_W

kernel_name = "hybrid_gdn_dilated_attn_hmoe_block"


def rmsnorm(x, g):
    xf = x.astype(jnp.float32)
    y = xf * lax.rsqrt(jnp.mean(xf * xf, axis=-1, keepdims=True) + EPS)
    return (y * g.astype(jnp.float32)).astype(x.dtype)


def l2norm(x):
    return x * lax.rsqrt(jnp.sum(x * x, axis=-1, keepdims=True) + EPS)


def causal_dwconv(x, w):
    k = w.shape[0]
    return lax.conv_general_dilated(
        x, w.astype(x.dtype)[:, None, :], window_strides=(1,), padding=[(k - 1, 0)],
        dimension_numbers=("NWC", "WIO", "NWC"), feature_group_count=x.shape[-1])


def t5_bucket(dist):
    max_exact = NUM_BUCKETS // 2
    d_f = jnp.maximum(dist, 1).astype(jnp.float32)
    large = max_exact + (jnp.log(d_f / max_exact) / math.log(REL_MAX_DIST / max_exact)
                         * (NUM_BUCKETS - max_exact)).astype(jnp.int32)
    large = jnp.minimum(large, NUM_BUCKETS - 1)
    return jnp.where(dist < max_exact, dist, large)


def gated_delta_rule(q, k, v, beta, g):
    bsz, seq, nh, dk = q.shape
    dv = v.shape[-1]
    n = seq // CHUNK

    def chunks(t):
        return t.reshape(bsz, n, CHUNK, nh, -1).transpose(0, 3, 1, 2, 4)

    q, k, v = chunks(q), chunks(k), chunks(v)
    beta = beta.reshape(bsz, n, CHUNK, nh).transpose(0, 3, 1, 2)
    G = jnp.cumsum(g.reshape(bsz, n, CHUNK, nh).transpose(0, 3, 1, 2), axis=-1)
    idx = jnp.arange(CHUNK)
    diff = G[..., :, None] - G[..., None, :]
    dec_incl = jnp.exp(jnp.where(idx[:, None] >= idx[None, :], diff, -jnp.inf))
    dec_strict = jnp.where(idx[:, None] > idx[None, :], dec_incl, 0.0)
    kb = k * beta[..., None]
    a_mat = jnp.einsum("bhnik,bhnjk->bhnij", kb, k) * dec_strict + jnp.eye(CHUNK, dtype=q.dtype)
    rhs = jnp.concatenate([v * beta[..., None], kb * jnp.exp(G)[..., None]], axis=-1)
    sol = lax.linalg.triangular_solve(a_mat, rhs, left_side=True, lower=True, unit_diagonal=True)
    u0, w = sol[..., :dv], sol[..., dv:]
    a_qk = jnp.einsum("bhnik,bhnjk->bhnij", q, k) * dec_incl
    qg = q * jnp.exp(G)[..., None]
    kg = k * jnp.exp(G[..., -1:] - G)[..., None]
    gl = jnp.exp(G[..., -1])

    def step(state, inp):
        u0_c, w_c, qg_c, kg_c, aqk_c, gl_c = inp
        u = u0_c - jnp.einsum("bhck,bhkv->bhcv", w_c, state)
        o = jnp.einsum("bhck,bhkv->bhcv", qg_c, state) + jnp.einsum("bhij,bhjv->bhiv", aqk_c, u)
        state = gl_c[..., None, None] * state + jnp.einsum("bhck,bhcv->bhkv", kg_c, u)
        return state, o

    xs = tuple(jnp.moveaxis(t, 2, 0) for t in (u0, w, qg, kg, a_qk, gl))
    s0 = jnp.zeros((bsz, nh, dk, dv), jnp.float32)
    _, o = lax.scan(step, s0, xs)
    return o.transpose(1, 0, 3, 2, 4).reshape(bsz, seq, nh, dv)


def dilated_branch(q, k, v, rel_bias, window, dil):
    bsz, seq, nh, dh = q.shape
    steps = window // dil
    L = seq // dil
    Lp = -(-L // QBLK) * QBLK
    nb = Lp // QBLK

    def to_sub(t):
        t = t.reshape(bsz, L, dil, nh, dh).transpose(0, 2, 1, 3, 4)
        t = jnp.pad(t, ((0, 0), (0, 0), (0, Lp - L), (0, 0), (0, 0)))
        return t.reshape(bsz, dil, nb, QBLK, nh, dh)

    def band(t):
        prev = jnp.pad(t, ((0, 0), (0, 0), (1, 0), (0, 0), (0, 0), (0, 0)))[:, :, :-1]
        return jnp.concatenate([prev, t], axis=3)

    qb = to_sub(q)
    kk, vv = band(to_sub(k)), band(to_sub(v))
    s = jnp.einsum("brnqhd,brnkhd->brnhqk", qb, kk) * (dh ** -0.5)
    qi = jnp.arange(QBLK)[:, None]
    kj = jnp.arange(2 * QBLK)[None, :]
    delta = qi + QBLK - kj
    in_win = (delta >= 0) & (delta <= steps)
    bias = rel_bias.astype(jnp.float32)[t5_bucket(jnp.maximum(delta, 0) * dil)]
    s = s + bias.transpose(2, 0, 1)
    key_pos = jnp.arange(nb)[:, None] * QBLK - QBLK + kj
    valid = in_win[None] & (key_pos >= 0)[:, None, :]
    s = jnp.where(valid[:, None], s, NEG_INF)
    m = jnp.max(s, axis=-1, keepdims=True)
    p = jnp.exp(s - m)
    den = jnp.sum(p, axis=-1)
    o = jnp.einsum("brnhqk,brnkhd->brnqhd", p, vv) / den.transpose(0, 1, 2, 4, 3)[..., None]
    lse = (m[..., 0] + jnp.log(den)).transpose(0, 1, 2, 4, 3)
    o = o.reshape(bsz, dil, Lp, nh, dh)[:, :, :L].transpose(0, 2, 1, 3, 4).reshape(bsz, seq, nh, dh)
    lse = lse.reshape(bsz, dil, Lp, nh)[:, :, :L].transpose(0, 2, 1, 3).reshape(bsz, seq, nh)
    return o, lse


def hybrid_mixer(h, w_in, w_conv, a_log, dt_bias, gdn_norm, attn_norm, w_out, rel_bias):
    bsz, seq, _ = h.shape
    f32 = jnp.float32
    p = h @ w_in
    s1 = GDN_CONV_W
    s2 = s1 + GDN_V_W
    s3 = s2 + GDN_HEADS
    s4 = s3 + GDN_HEADS
    qkv_a, z, b_raw, a_raw, qkv_b = p[..., :s1], p[..., s1:s2], p[..., s2:s3], p[..., s3:s4], p[..., s4:]

    qkv_a = jax.nn.silu(causal_dwconv(qkv_a, w_conv)).astype(f32)
    qa = l2norm(qkv_a[..., :GDN_QK_W].reshape(bsz, seq, GDN_HEADS, GDN_DK)) * (GDN_DK ** -0.5)
    ka = l2norm(qkv_a[..., GDN_QK_W:2 * GDN_QK_W].reshape(bsz, seq, GDN_HEADS, GDN_DK))
    va = qkv_a[..., 2 * GDN_QK_W:].reshape(bsz, seq, GDN_HEADS, GDN_DV)
    beta = jax.nn.sigmoid(b_raw.astype(f32))
    g = -jnp.exp(a_log.astype(f32)) * jax.nn.softplus(a_raw.astype(f32) + dt_bias.astype(f32))
    oa = gated_delta_rule(qa, ka, va, beta, g)
    oa = rmsnorm(oa, gdn_norm) * jax.nn.silu(z.astype(f32).reshape(bsz, seq, GDN_HEADS, GDN_DV))
    oa = oa.reshape(bsz, seq, GDN_V_W)

    qkv_b = qkv_b.astype(f32)
    qb = qkv_b[..., :ATTN_W].reshape(bsz, seq, ATTN_HEADS, ATTN_DH)
    kb = qkv_b[..., ATTN_W:2 * ATTN_W].reshape(bsz, seq, ATTN_HEADS, ATTN_DH)
    vb = qkv_b[..., 2 * ATTN_W:].reshape(bsz, seq, ATTN_HEADS, ATTN_DH)
    outs, lses = [], []
    for window, dil in DILATED_PAIRS:
        o_i, lse_i = dilated_branch(qb, kb, vb, rel_bias, window, dil)
        outs.append(o_i)
        lses.append(lse_i)
    wts = jax.nn.softmax(jnp.stack(lses, axis=-1), axis=-1)
    ob = jnp.sum(jnp.stack(outs, axis=-1) * wts[:, :, :, None, :], axis=-1)
    ob = rmsnorm(ob.reshape(bsz, seq, ATTN_W), attn_norm)

    y = jnp.concatenate([oa, ob], axis=-1).astype(h.dtype)
    return y @ w_out


def hierarchical_moe(h, w_rg, w_re, w_gate, w_up, w_down):
    bsz, seq, d = h.shape
    f32 = jnp.float32
    T = bsz * seq
    xt = h.reshape(T, d)
    xf = xt.astype(f32)
    gp = jax.nn.softmax(xf @ w_rg.astype(f32), axis=-1)
    gprob, gidx = lax.top_k(gp, 1)
    el = (xf @ w_re.astype(f32)).reshape(T, N_GROUPS, EXPERTS_PER_GROUP)
    el = jnp.take_along_axis(el, gidx[:, :, None], axis=1)[:, 0]
    eprob, eidx = lax.top_k(jax.nn.softmax(el, axis=-1), TOP_K)
    eprob = eprob / jnp.sum(eprob, axis=-1, keepdims=True)
    gates = gprob * eprob
    expert = gidx * EXPERTS_PER_GROUP + eidx

    A = T * TOP_K
    flat_e = expert.reshape(A).astype(jnp.int32)
    flat_tok = jnp.repeat(jnp.arange(T, dtype=jnp.int32), TOP_K)
    flat_w = gates.reshape(A)
    order = jnp.argsort(flat_e)
    se, stok, sw = flat_e[order], flat_tok[order], flat_w[order]
    counts = jax.ops.segment_sum(jnp.ones((A,), jnp.int32), flat_e, num_segments=N_EXPERTS)
    starts = jnp.cumsum(counts) - counts
    padded = (counts + MOE_BLK - 1) // MOE_BLK * MOE_BLK
    pends = jnp.cumsum(padded)
    pstarts = pends - padded
    dest = pstarts[se] + jnp.arange(A, dtype=jnp.int32) - starts[se]
    P = -(-A // MOE_BLK) * MOE_BLK + N_EXPERTS * MOE_BLK
    nblk = P // MOE_BLK
    buf_tok = jnp.zeros((P,), jnp.int32).at[dest].set(stok)
    buf_w = jnp.zeros((P,), f32).at[dest].set(sw)
    blk_start = jnp.arange(nblk, dtype=jnp.int32) * MOE_BLK
    blk_e = jnp.minimum(jnp.searchsorted(pends, blk_start, side="right"), N_EXPERTS - 1)
    xs = xt[buf_tok].reshape(nblk, MOE_BLK, d)

    def expert_block(args):
        xb, e = args
        hid = jax.nn.silu(xb @ w_gate[e]) * (xb @ w_up[e])
        return hid @ w_down[e]

    ys = lax.map(expert_block, (xs, blk_e)).reshape(P, d)
    out = jnp.zeros((T, d), ys.dtype).at[buf_tok].add(ys * buf_w[:, None].astype(ys.dtype))
    return out.reshape(bsz, seq, d)


def setup_inputs(seed: int = 0) -> dict:
    key = jax.random.key(seed)
    ks = jax.random.split(key, 22)
    f32 = jnp.float32
    D = D_MODEL

    def nrm(k, shape, scale):
        return jax.random.normal(k, shape, f32) * scale

    dt = jnp.exp(jax.random.uniform(ks[8], (DEPTH, GDN_HEADS), f32, math.log(1e-3), math.log(1e-1)))
    return {
        "x": nrm(ks[0], (BATCH, SEQ, D), 1.0),
        "c": nrm(ks[1], (BATCH, D), 1.0),
        "w_ada": nrm(ks[2], (DEPTH, D, 6 * D), 0.5 * D ** -0.5),
        "b_ada": nrm(ks[3], (DEPTH, 6 * D), 0.02),
        "norm_mix": 1.0 + nrm(ks[4], (DEPTH, D), 0.02),
        "w_in": nrm(ks[5], (DEPTH, D, IN_COLS), D ** -0.5),
        "w_conv": nrm(ks[6], (DEPTH, CONV_K, GDN_CONV_W), CONV_K ** -0.5),
        "a_log": jnp.log(jax.random.uniform(ks[7], (DEPTH, GDN_HEADS), f32, 1.0, 16.0)),
        "dt_bias": dt + jnp.log(-jnp.expm1(-dt)),
        "gdn_norm": 1.0 + nrm(ks[9], (DEPTH, GDN_DV), 0.02),
        "attn_norm": 1.0 + nrm(ks[10], (DEPTH, ATTN_W), 0.02),
        "w_out": nrm(ks[11], (DEPTH, MIX_W, D), MIX_W ** -0.5),
        "rel_bias": nrm(ks[12], (NUM_BUCKETS, ATTN_HEADS), 0.5),
        "norm_ffn": 1.0 + nrm(ks[13], (DEPTH, D), 0.02),
        "w_router_group": nrm(ks[14], (DEPTH, D, N_GROUPS), D ** -0.5),
        "w_router_expert": nrm(ks[15], (DEPTH, D, N_EXPERTS), D ** -0.5),
        "w_gate": nrm(ks[16], (DEPTH, N_EXPERTS, D, D_EXPERT), D ** -0.5),
        "w_up": nrm(ks[17], (DEPTH, N_EXPERTS, D, D_EXPERT), D ** -0.5),
        "w_down": nrm(ks[18], (DEPTH, N_EXPERTS, D_EXPERT, D), D_EXPERT ** -0.5),
        "norm_final": 1.0 + nrm(ks[19], (D,), 0.02),
    }


def reference(x, c, w_ada, b_ada, norm_mix, w_in, w_conv, a_log, dt_bias, gdn_norm, attn_norm,
              w_out, rel_bias, norm_ffn, w_router_group, w_router_expert, w_gate, w_up, w_down,
              norm_final):
    for l in range(DEPTH):
        mod = c @ w_ada[l] + b_ada[l]
        sh_a, sc_a, gt_a, sh_f, sc_f, gt_f = jnp.split(mod[:, None, :], 6, axis=-1)
        h = rmsnorm(x, norm_mix[l]) * (1.0 + sc_a) + sh_a
        x = x + gt_a * hybrid_mixer(h, w_in[l], w_conv[l], a_log[l], dt_bias[l], gdn_norm[l],
                                    attn_norm[l], w_out[l], rel_bias)
        h = rmsnorm(x, norm_ffn[l]) * (1.0 + sc_f) + sh_f
        x = x + gt_f * hierarchical_moe(h, w_router_group[l], w_router_expert[l], w_gate[l],
                                        w_up[l], w_down[l])
    return rmsnorm(x, norm_final)
```

```python
import functools
import math

import jax
import jax.numpy as jnp
from jax import lax
from jax.experimental import pallas as pl
from jax.experimental.pallas import tpu as pltpu

D_MODEL = 1024
GDN_HEADS = 4
GDN_DK = 128
GDN_DV = 128
CONV_K = 4
CHUNK = 64
ATTN_HEADS = 8
ATTN_DH = 64
DILATED_PAIRS = ((128, 1), (512, 4), (2048, 16))
QBLK = 128
NUM_BUCKETS = 32
REL_MAX_DIST = 2048
N_GROUPS = 4
EXPERTS_PER_GROUP = 8
N_EXPERTS = N_GROUPS * EXPERTS_PER_GROUP
TOP_K = 2
D_EXPERT = 256
EPS = 1e-6
NEG_INF = -1e30

GDN_QK_W = GDN_HEADS * GDN_DK
GDN_V_W = GDN_HEADS * GDN_DV
ATTN_W = ATTN_HEADS * ATTN_DH
GDN_CONV_W = 2 * GDN_QK_W + GDN_V_W
LANES = 128
SMALL_W = LANES
W_COLS = GDN_CONV_W + GDN_V_W + SMALL_W + 3 * ATTN_W

TM_PROJ = 512
TM_POST = 256
TM_ROWS = 256
GDN_TOK = 256
EXPERT_BLK = 256
VMEM_LIMIT = 56 * 1024 * 1024

F32 = jnp.float32
BF16 = jnp.bfloat16
HIGHEST = lax.Precision.HIGHEST


def _sigmoid(x):
    return 1.0 / (1.0 + jnp.exp(-x))


def _dot(a, b, precision=None):
    return jnp.dot(a, b, preferred_element_type=F32, precision=precision)


def _dot_nt(a, b, precision=None):
    return lax.dot_general(a, b, (((1,), (1,)), ((), ())), preferred_element_type=F32,
                           precision=precision)


def _dot_tn(a, b, precision=None):
    return lax.dot_general(a, b, (((0,), (0,)), ((), ())), preferred_element_type=F32,
                           precision=precision)


def _params(sem):
    return pltpu.CompilerParams(dimension_semantics=sem, vmem_limit_bytes=VMEM_LIMIT)


def _adaln_kernel(c_ref, w_ref, b_ref, o_ref):
    o_ref[...] = _dot(c_ref[...], w_ref[...], HIGHEST) + b_ref[...]


def _adaln(c, w, b):
    bsz, d = c.shape
    n = w.shape[1]
    tn = 1024
    return pl.pallas_call(
        _adaln_kernel,
        grid=(n // tn,),
        in_specs=[pl.BlockSpec((bsz, d), lambda j: (0, 0)),
                  pl.BlockSpec((d, tn), lambda j: (0, j)),
                  pl.BlockSpec((1, tn), lambda j: (0, j))],
        out_specs=pl.BlockSpec((bsz, tn), lambda j: (0, j)),
        out_shape=jax.ShapeDtypeStruct((bsz, n), F32),
        compiler_params=_params(("arbitrary",)),
        name="adaln",
    )(c, w, b.reshape(1, n))


def _inproj_kernel(tiles_per_seq, x_ref, sc_ref, sh_ref, g_ref, w_ref, wconv_ref,
                   qkva_ref, z_ref, ba_ref, qkvb_ref, pa_scr):
    i = pl.program_id(0)
    tm = x_ref.shape[0]
    x = x_ref[...]
    h = x * lax.rsqrt(jnp.mean(x * x, axis=-1, keepdims=True) + EPS) * g_ref[...]
    h = h * (1.0 + sc_ref[0]) + sh_ref[0]
    hb = h.astype(BF16)

    @pl.when(i % tiles_per_seq == 0)
    def _():
        pa_scr[0:8, :] = jnp.zeros((8, GDN_CONV_W), F32)

    pa = _dot(hb, w_ref[:, 0:GDN_CONV_W])
    pa_scr[8:8 + tm, :] = pa
    acc = pa * wconv_ref[CONV_K - 1:CONV_K, :]
    for j in range(CONV_K - 1):
        acc = acc + pa_scr[pl.ds(8 - (CONV_K - 1) + j, tm), :] * wconv_ref[j:j + 1, :]
    qkva_ref[...] = (acc * _sigmoid(acc)).astype(qkva_ref.dtype)
    pa_scr[0:8, :] = pa_scr[tm:tm + 8, :]

    c0 = GDN_CONV_W
    z_ref[...] = _dot(hb, w_ref[:, c0:c0 + GDN_V_W]).astype(z_ref.dtype)
    c0 += GDN_V_W
    ba_ref[...] = _dot(hb, w_ref[:, c0:c0 + SMALL_W])
    c0 += SMALL_W
    qkvb_ref[...] = _dot(hb, w_ref[:, c0:c0 + 3 * ATTN_W]).astype(qkvb_ref.dtype)


def _inproj(x2, sc, sh, g, wcat, wconv, seq):
    t, d = x2.shape
    tm = TM_PROJ
    tps = seq // tm
    mod_spec = pl.BlockSpec((1, 1, d), lambda i: (i // tps, 0, 0))
    return pl.pallas_call(
        functools.partial(_inproj_kernel, tps),
        grid=(t // tm,),
        in_specs=[pl.BlockSpec((tm, d), lambda i: (i, 0)),
                  mod_spec, mod_spec,
                  pl.BlockSpec((1, d), lambda i: (0, 0)),
                  pl.BlockSpec((d, W_COLS), lambda i: (0, 0)),
                  pl.BlockSpec((CONV_K, GDN_CONV_W), lambda i: (0, 0))],
        out_specs=[pl.BlockSpec((tm, GDN_CONV_W), lambda i: (i, 0)),
                   pl.BlockSpec((tm, GDN_V_W), lambda i: (i, 0)),
                   pl.BlockSpec((tm, SMALL_W), lambda i: (i, 0)),
                   pl.BlockSpec((tm, 3 * ATTN_W), lambda i: (i, 0))],
        out_shape=[jax.ShapeDtypeStruct((t, GDN_CONV_W), BF16),
                   jax.ShapeDtypeStruct((t, GDN_V_W), BF16),
                   jax.ShapeDtypeStruct((t, SMALL_W), F32),
                   jax.ShapeDtypeStruct((t, 3 * ATTN_W), BF16)],
        scratch_shapes=[pltpu.VMEM((tm + 8, GDN_CONV_W), F32)],
        compiler_params=_params(("arbitrary",)),
        name="inproj",
    )(x2, sc, sh, g, wcat, wconv)


def _softplus(x):
    return jnp.maximum(x, 0.0) + jnp.log(1.0 + jnp.exp(-jnp.abs(x)))


def _unit_lower_inverse(n_strict):
    c = n_strict.shape[0]
    row = lax.broadcasted_iota(jnp.int32, (c, c), 0)
    col = lax.broadcasted_iota(jnp.int32, (c, c), 1)
    p = jnp.where(row == col, 1.0, 0.0) - n_strict
    m = _dot(n_strict, n_strict, HIGHEST)
    steps = int(math.log2(c)) - 1
    for j in range(steps):
        p = p + _dot(p, m, HIGHEST)
        if j + 1 < steps:
            m = _dot(m, m, HIGHEST)
    return p


def _gdn_kernel(steps_per_seq, qkv_ref, ba_ref, bat_ref, gp_ref, gpt_ref, lblk_ref, ublk_ref,
                o_ref, state_scr):
    i = pl.program_id(0)

    @pl.when(i % steps_per_seq == 0)
    def _():
        state_scr[...] = jnp.zeros_like(state_scr)

    nchunk = GDN_TOK // CHUNK
    ba = ba_ref[...]
    bat = bat_ref[...]
    a_vec = -jnp.exp(gp_ref[0:1, :])
    g_tile = a_vec * _softplus(ba + gp_ref[1:2, :])
    g_cum = _dot(lblk_ref[...], g_tile, HIGHEST)
    a_col = -jnp.exp(gpt_ref[:, 0:1])
    gt_tile = a_col * _softplus(bat + gpt_ref[:, 1:2])
    gt_cum = _dot(gt_tile, ublk_ref[...], HIGHEST)
    beta_tile = _sigmoid(ba)

    row = lax.broadcasted_iota(jnp.int32, (CHUNK, CHUNK), 0)
    col = lax.broadcasted_iota(jnp.int32, (CHUNK, CHUNK), 1)
    incl = row >= col
    strict = row > col

    for h in range(GDN_HEADS):
        q_raw = qkv_ref[:, h * GDN_DK:(h + 1) * GDN_DK].astype(F32)
        k_raw = qkv_ref[:, GDN_QK_W + h * GDN_DK:GDN_QK_W + (h + 1) * GDN_DK].astype(F32)
        v = qkv_ref[:, 2 * GDN_QK_W + h * GDN_DV:2 * GDN_QK_W + (h + 1) * GDN_DV].astype(F32)
        qn = q_raw * lax.rsqrt(jnp.sum(q_raw * q_raw, axis=-1, keepdims=True) + EPS) * (GDN_DK ** -0.5)
        kn = k_raw * lax.rsqrt(jnp.sum(k_raw * k_raw, axis=-1, keepdims=True) + EPS)
        beta = beta_tile[:, h:h + 1]
        gc_all = g_cum[:, GDN_HEADS + h:GDN_HEADS + h + 1]
        eg = jnp.exp(gc_all)
        kb = kn * beta
        vb = v * beta
        kbg = kb * eg
        qg = qn * eg
        state = state_scr[h]
        for c in range(nchunk):
            sl = slice(c * CHUNK, (c + 1) * CHUNK)
            g_c = gc_all[sl]
            g_r = gt_cum[GDN_HEADS + h:GDN_HEADS + h + 1, sl]
            diff = g_c - g_r
            dec = jnp.exp(diff)
            dec_incl = jnp.where(incl, dec, 0.0)
            dec_strict = jnp.where(strict, dec, 0.0)
            k_c = kn[sl].astype(BF16)
            a_str = _dot_nt(kb[sl].astype(BF16), k_c) * dec_strict
            tinv = _unit_lower_inverse(a_str)
            rhs = jnp.concatenate([vb[sl], kbg[sl]], axis=1)
            sol = _dot(tinv, rhs, HIGHEST)
            u0 = sol[:, :GDN_DV]
            w = sol[:, GDN_DV:]
            a_qk = _dot_nt(qn[sl].astype(BF16), k_c) * dec_incl
            g_last = g_c[CHUNK - 1:CHUNK, :]
            kg = kn[sl] * jnp.exp(g_last - g_c)
            sb = state.astype(BF16)
            u = u0 - _dot(w.astype(BF16), sb)
            o = _dot(qg[sl].astype(BF16), sb) + _dot(a_qk.astype(BF16), u.astype(BF16))
            state = jnp.exp(g_last) * state + _dot_tn(kg.astype(BF16), u.astype(BF16))
            o_ref[sl, h * GDN_DV:(h + 1) * GDN_DV] = o.astype(o_ref.dtype)
        state_scr[h] = state


def _gdn(qkva, ba, bat, gp, gpt, lblk, ublk, seq):
    t = qkva.shape[0]
    tok = GDN_TOK
    sps = seq // tok
    const = lambda i: (0, 0)
    return pl.pallas_call(
        functools.partial(_gdn_kernel, sps),
        grid=(t // tok,),
        in_specs=[pl.BlockSpec((tok, GDN_CONV_W), lambda i: (i, 0)),
                  pl.BlockSpec((tok, SMALL_W), lambda i: (i, 0)),
                  pl.BlockSpec((8, tok), lambda i: (0, i)),
                  pl.BlockSpec((8, LANES), const),
                  pl.BlockSpec((8, LANES), const),
                  pl.BlockSpec((tok, tok), const),
                  pl.BlockSpec((tok, tok), const)],
        out_specs=pl.BlockSpec((tok, GDN_V_W), lambda i: (i, 0)),
        out_shape=jax.ShapeDtypeStruct((t, GDN_V_W), F32),
        scratch_shapes=[pltpu.VMEM((GDN_HEADS, GDN_DK, GDN_DV), F32)],
        compiler_params=_params(("arbitrary",)),
        name="gdn",
    )(qkva, ba, bat, gp, gpt, lblk, ublk)


def _attn_kernel(q_ref, kp_ref, kc_ref, vp_ref, vc_ref, bias_ref, o_ref, lse_ref):
    n = pl.program_id(2)
    q = q_ref[0]
    k = jnp.concatenate([kp_ref[0], kc_ref[0]], axis=0)
    v = jnp.concatenate([vp_ref[0], vc_ref[0]], axis=0)
    col = lax.broadcasted_iota(jnp.int32, (QBLK, 2 * QBLK), 1)
    lane = lax.broadcasted_iota(jnp.int32, (QBLK, LANES), 1)
    no_prev = jnp.logical_and(n == 0, col < QBLK)
    outs = []
    lse_tile = jnp.zeros((QBLK, LANES), F32)
    for h in range(ATTN_HEADS):
        hs = slice(h * ATTN_DH, (h + 1) * ATTN_DH)
        s = _dot_nt(q[:, hs], k[:, hs]) * (ATTN_DH ** -0.5) + bias_ref[h]
        s = jnp.where(no_prev, NEG_INF, s)
        m = jnp.max(s, axis=-1, keepdims=True)
        p = jnp.exp(s - m)
        den = jnp.sum(p, axis=-1, keepdims=True)
        outs.append(_dot(p.astype(BF16), v[:, hs]) / den)
        lse_tile = jnp.where(lane == h, m + jnp.log(den), lse_tile)
    o_ref[0] = jnp.concatenate(outs, axis=1).astype(o_ref.dtype)
    lse_ref[0] = lse_tile


def _attn_branch(qkvb, bias, bsz, seq, dil):
    ln = seq // dil
    nb = ln // QBLK
    xv = qkvb.reshape(bsz, ln, dil * 3 * ATTN_W)
    w = ATTN_W
    cur = lambda part: (lambda b, r, n: (b, n, r * 3 + part))
    prev = lambda part: (lambda b, r, n: (b, jnp.maximum(n - 1, 0), r * 3 + part))
    o, lse = pl.pallas_call(
        _attn_kernel,
        grid=(bsz, dil, nb),
        in_specs=[pl.BlockSpec((1, QBLK, w), cur(0)),
                  pl.BlockSpec((1, QBLK, w), prev(1)),
                  pl.BlockSpec((1, QBLK, w), cur(1)),
                  pl.BlockSpec((1, QBLK, w), prev(2)),
                  pl.BlockSpec((1, QBLK, w), cur(2)),
                  pl.BlockSpec((ATTN_HEADS, QBLK, 2 * QBLK), lambda b, r, n: (0, 0, 0))],
        out_specs=[pl.BlockSpec((1, QBLK, w), lambda b, r, n: (b, n, r)),
                   pl.BlockSpec((1, QBLK, LANES), lambda b, r, n: (b, n, r))],
        out_shape=[jax.ShapeDtypeStruct((bsz, ln, dil * w), BF16),
                   jax.ShapeDtypeStruct((bsz, ln, dil * LANES), F32)],
        compiler_params=_params(("arbitrary", "arbitrary", "arbitrary")),
        name=f"attn_d{dil}",
    )(xv, xv, xv, xv, xv, bias)
    return o.reshape(bsz * seq, w), lse.reshape(bsz * seq, LANES)


def _t5_bucket(dist):
    max_exact = NUM_BUCKETS // 2
    d_f = jnp.maximum(dist, 1).astype(F32)
    large = max_exact + (jnp.log(d_f / max_exact) / math.log(REL_MAX_DIST / max_exact)
                         * (NUM_BUCKETS - max_exact)).astype(jnp.int32)
    large = jnp.minimum(large, NUM_BUCKETS - 1)
    return jnp.where(dist < max_exact, dist, large)


def _bias_table(rel_bias, window, dil):
    steps = window // dil
    qi = jnp.arange(QBLK)[:, None]
    kj = jnp.arange(2 * QBLK)[None, :]
    delta = qi + QBLK - kj
    in_win = (delta >= 0) & (delta <= steps)
    bias = rel_bias.astype(F32)[_t5_bucket(jnp.maximum(delta, 0) * dil)]
    return jnp.where(in_win[None], bias.transpose(2, 0, 1), NEG_INF)


def _split_bf16(x):
    hi = x.astype(BF16)
    lo = (x - hi.astype(F32)).astype(BF16)
    return hi, lo


def _post_kernel(og_ref, z_ref, o1_ref, o2_ref, o3_ref, l1_ref, l2_ref, l3_ref, x_ref,
                 gta_ref, scf_ref, shf_ref, gnorm_ref, anorm_ref, wout_ref, nffn_ref,
                 wrh_ref, wrl_ref, exp_ref, ls_ref,
                 x1_ref, h2_ref, route_ref, cnt_ref, carry_scr):
    i = pl.program_id(0)

    @pl.when(i == 0)
    def _():
        carry_scr[...] = jnp.zeros_like(carry_scr)

    heads = []
    for h in range(GDN_HEADS):
        hs = slice(h * GDN_DV, (h + 1) * GDN_DV)
        seg = og_ref[:, hs]
        nrm = seg * lax.rsqrt(jnp.mean(seg * seg, axis=-1, keepdims=True) + EPS) * gnorm_ref[...]
        zz = z_ref[:, hs].astype(F32)
        heads.append((nrm * (zz * _sigmoid(zz))).astype(BF16))
    oa = jnp.concatenate(heads, axis=1)

    l1, l2, l3 = l1_ref[...], l2_ref[...], l3_ref[...]
    m = jnp.maximum(jnp.maximum(l1, l2), l3)
    e1, e2, e3 = jnp.exp(l1 - m), jnp.exp(l2 - m), jnp.exp(l3 - m)
    inv = 1.0 / (e1 + e2 + e3)
    ob = jnp.zeros(o1_ref.shape, F32)
    for e, o_ref in ((e1, o1_ref), (e2, o2_ref), (e3, o3_ref)):
        hi, lo = _split_bf16(e * inv)
        wexp = _dot(hi, exp_ref[...]) + _dot(lo, exp_ref[...])
        ob = ob + wexp * o_ref[...].astype(F32)
    ob = ob * lax.rsqrt(jnp.mean(ob * ob, axis=-1, keepdims=True) + EPS) * anorm_ref[...]

    mix = _dot(oa, wout_ref[0:GDN_V_W, :]) + _dot(ob.astype(BF16), wout_ref[GDN_V_W:, :])
    x1 = x_ref[...] + gta_ref[0] * mix
    x1_ref[...] = x1
    h2 = x1 * lax.rsqrt(jnp.mean(x1 * x1, axis=-1, keepdims=True) + EPS) * nffn_ref[...]
    h2 = h2 * (1.0 + scf_ref[0]) + shf_ref[0]
    h2_ref[...] = h2

    hh, hl = _split_bf16(h2)
    logits = _dot(hh, wrh_ref[...]) + _dot(hh, wrl_ref[...]) + _dot(hl, wrh_ref[...])
    tm = logits.shape[0]
    lane = lax.broadcasted_iota(jnp.int32, (tm, LANES), 1).astype(F32)
    big = float(LANES)
    gmask = lane < N_GROUPS
    glog = jnp.where(gmask, logits, NEG_INF)
    gmax = jnp.max(glog, axis=-1, keepdims=True)
    gidx = jnp.min(jnp.where(jnp.logical_and(gmask, glog == gmax), lane, big), axis=-1, keepdims=True)
    gprob = 1.0 / jnp.sum(jnp.where(gmask, jnp.exp(glog - gmax), 0.0), axis=-1, keepdims=True)
    lo_lane = N_GROUPS + EXPERTS_PER_GROUP * gidx
    emask = jnp.logical_and(lane >= lo_lane, lane < lo_lane + EXPERTS_PER_GROUP)
    elog = jnp.where(emask, logits, NEG_INF)
    m1 = jnp.max(elog, axis=-1, keepdims=True)
    i1 = jnp.min(jnp.where(jnp.logical_and(emask, elog == m1), lane, big), axis=-1, keepdims=True)
    emask2 = jnp.logical_and(emask, lane != i1)
    elog2 = jnp.where(emask2, logits, NEG_INF)
    m2 = jnp.max(elog2, axis=-1, keepdims=True)
    i2 = jnp.min(jnp.where(jnp.logical_and(emask2, elog2 == m2), lane, big), axis=-1, keepdims=True)
    r = jnp.exp(m2 - m1)
    gate1 = gprob / (1.0 + r)
    gate2 = gprob * r / (1.0 + r)
    ex1 = i1 - N_GROUPS
    ex2 = i2 - N_GROUPS

    hit1 = lane == ex1
    hit2 = lane == ex2
    onehot = jnp.where(jnp.logical_or(hit1, hit2), 1.0, 0.0)
    pref = _dot(ls_ref[...], onehot.astype(BF16)) + carry_scr[...]
    rank1 = jnp.sum(jnp.where(hit1, pref, 0.0), axis=-1, keepdims=True)
    rank2 = jnp.sum(jnp.where(hit2, pref, 0.0), axis=-1, keepdims=True)
    carry = carry_scr[...] + jnp.sum(onehot, axis=0, keepdims=True)
    carry_scr[...] = carry
    cnt_ref[...] = jnp.broadcast_to(carry, cnt_ref.shape)

    route = jnp.zeros((tm, LANES), F32)
    for idx, val in enumerate((ex1, ex2, gate1, gate2, rank1, rank2)):
        route = jnp.where(lane == idx, val, route)
    route_ref[...] = route


def _post(og, z, os_, ls_, x2, gta, scf, shf, gnorm, anorm, wout, nffn, wrh, wrl, expand, lstrict, seq):
    t, d = x2.shape
    tm = TM_POST
    tps = seq // tm
    tile = lambda w: pl.BlockSpec((tm, w), lambda i: (i, 0))
    const2 = lambda a: pl.BlockSpec(a.shape, lambda i: (0, 0))
    mod_spec = pl.BlockSpec((1, 1, d), lambda i: (i // tps, 0, 0))
    return pl.pallas_call(
        _post_kernel,
        grid=(t // tm,),
        in_specs=[tile(GDN_V_W), tile(GDN_V_W), tile(ATTN_W), tile(ATTN_W), tile(ATTN_W),
                  tile(LANES), tile(LANES), tile(LANES), tile(d),
                  mod_spec, mod_spec, mod_spec,
                  const2(gnorm), const2(anorm), const2(wout), const2(nffn),
                  const2(wrh), const2(wrl), const2(expand), const2(lstrict)],
        out_specs=[tile(d), tile(d), tile(LANES), pl.BlockSpec((8, LANES), lambda i: (0, 0))],
        out_shape=[jax.ShapeDtypeStruct((t, d), F32),
                   jax.ShapeDtypeStruct((t, d), F32),
                   jax.ShapeDtypeStruct((t, LANES), F32),
                   jax.ShapeDtypeStruct((8, LANES), F32)],
        scratch_shapes=[pltpu.VMEM((1, LANES), F32)],
        compiler_params=_params(("arbitrary",)),
        name="post",
    )(og, z, *os_, *ls_, x2, gta, scf, shf, gnorm, anorm, wout, nffn, wrh, wrl, expand, lstrict)


def _dispatch_kernel(dest_ref, h2_ref, xs_in_ref, xs_ref, sem):
    del xs_in_ref
    i = pl.program_id(0)
    tm = h2_ref.shape[0]

    def row_copy(r, k):
        d = dest_ref[(i * tm + r) * TOP_K + k]
        return pltpu.make_async_copy(h2_ref.at[pl.ds(r, 1)], xs_ref.at[pl.ds(d, 1)], sem)

    def start(r, carry):
        for k in range(TOP_K):
            row_copy(r, k).start()
        return carry

    def wait(r, carry):
        for k in range(TOP_K):
            row_copy(r, k).wait()
        return carry

    lax.fori_loop(0, tm, start, 0)
    lax.fori_loop(0, tm, wait, 0)


def _dispatch(dest, h2, xs_init):
    t, d = h2.shape
    tm = TM_ROWS
    grid_spec = pltpu.PrefetchScalarGridSpec(
        num_scalar_prefetch=1,
        grid=(t // tm,),
        in_specs=[pl.BlockSpec((tm, d), lambda i, dest: (i, 0)),
                  pl.BlockSpec(memory_space=pl.ANY)],
        out_specs=pl.BlockSpec(memory_space=pl.ANY),
        scratch_shapes=[pltpu.SemaphoreType.DMA],
    )
    return pl.pallas_call(
        _dispatch_kernel,
        grid_spec=grid_spec,
        out_shape=jax.ShapeDtypeStruct(xs_init.shape, xs_init.dtype),
        input_output_aliases={2: 0},
        compiler_params=_params(("arbitrary",)),
        name="dispatch",
    )(dest, h2, xs_init)


def _expert_kernel(blk_e_ref, nused_ref, xs_ref, wg_ref, wu_ref, wd_ref, ys_ref):
    b = pl.program_id(0)

    @pl.when(b < nused_ref[0])
    def _():
        x = xs_ref[...].astype(BF16)
        g = _dot(x, wg_ref[0])
        u = _dot(x, wu_ref[0])
        hid = (g * _sigmoid(g)) * u
        ys_ref[...] = _dot(hid.astype(BF16), wd_ref[0])

    @pl.when(b >= nused_ref[0])
    def _():
        ys_ref[...] = jnp.zeros_like(ys_ref)


def _experts(blk_e, nused, xs, wg, wu, wd):
    p, d = xs.shape
    blk = EXPERT_BLK
    row_map = lambda b, be, nu: (jnp.minimum(b, nu[0] - 1), 0)
    w_map = lambda b, be, nu: (be[b], 0, 0)
    grid_spec = pltpu.PrefetchScalarGridSpec(
        num_scalar_prefetch=2,
        grid=(p // blk,),
        in_specs=[pl.BlockSpec((blk, d), row_map),
                  pl.BlockSpec((1, d, D_EXPERT), w_map),
                  pl.BlockSpec((1, d, D_EXPERT), w_map),
                  pl.BlockSpec((1, D_EXPERT, d), w_map)],
        out_specs=pl.BlockSpec((blk, d), lambda b, be, nu: (b, 0)),
    )
    return pl.pallas_call(
        _expert_kernel,
        grid_spec=grid_spec,
        out_shape=jax.ShapeDtypeStruct((p, d), F32),
        compiler_params=_params(("arbitrary",)),
        name="experts",
    )(blk_e, nused, xs, wg, wu, wd)


def _combine_kernel(final_norm, dest_ref, ys_ref, x1_ref, route_ref, gtf_ref, nf_ref, o_ref, ybuf, sem):
    i = pl.program_id(0)
    tm = x1_ref.shape[0]

    def row_copy(r, k):
        d = dest_ref[(i * tm + r) * TOP_K + k]
        return pltpu.make_async_copy(ys_ref.at[pl.ds(d, 1)], ybuf.at[k, pl.ds(r, 1)], sem)

    def start(r, carry):
        for k in range(TOP_K):
            row_copy(r, k).start()
        return carry

    def wait(r, carry):
        for k in range(TOP_K):
            row_copy(r, k).wait()
        return carry

    lax.fori_loop(0, tm, start, 0)
    lax.fori_loop(0, tm, wait, 0)
    route = route_ref[...]
    moe = ybuf[0] * route[:, 2:3] + ybuf[1] * route[:, 3:4]
    x2 = x1_ref[...] + gtf_ref[0] * moe
    if final_norm:
        x2 = x2 * lax.rsqrt(jnp.mean(x2 * x2, axis=-1, keepdims=True) + EPS) * nf_ref[...]
    o_ref[...] = x2


def _combine(dest, ys, x1, route, gtf, nf, seq, final_norm):
    t, d = x1.shape
    tm = TM_ROWS
    tps = seq // tm
    grid_spec = pltpu.PrefetchScalarGridSpec(
        num_scalar_prefetch=1,
        grid=(t // tm,),
        in_specs=[pl.BlockSpec(memory_space=pl.ANY),
                  pl.BlockSpec((tm, d), lambda i, dest: (i, 0)),
                  pl.BlockSpec((tm, LANES), lambda i, dest: (i, 0)),
                  pl.BlockSpec((1, 1, d), lambda i, dest: (i // tps, 0, 0)),
                  pl.BlockSpec((1, d), lambda i, dest: (0, 0))],
        out_specs=pl.BlockSpec((tm, d), lambda i, dest: (i, 0)),
        scratch_shapes=[pltpu.VMEM((TOP_K, tm, d), F32), pltpu.SemaphoreType.DMA],
    )
    return pl.pallas_call(
        functools.partial(_combine_kernel, final_norm),
        grid_spec=grid_spec,
        out_shape=jax.ShapeDtypeStruct((t, d), F32),
        compiler_params=_params(("arbitrary",)),
        name="combine",
    )(dest, ys, x1, route, gtf, nf)


def _block_tri(n, chunk, lower):
    r = jnp.arange(n)[:, None]
    c = jnp.arange(n)[None, :]
    same = (r // chunk) == (c // chunk)
    tri = (r >= c) if lower else (r <= c)
    return jnp.where(same & tri, 1.0, 0.0).astype(F32)


def _layer(x2, c, bsz, seq, w_ada, b_ada, norm_mix, w_in, w_conv, a_log, dt_bias, gdn_norm,
           attn_norm, w_out, rel_bias, norm_ffn, w_rg, w_re, w_gate, w_up, w_down):
    t, d = x2.shape
    mod = _adaln(c, w_ada, b_ada)
    sh_a, sc_a, gt_a, sh_f, sc_f, gt_f = [m.reshape(bsz, 1, d) for m in jnp.split(mod, 6, axis=-1)]

    s1 = GDN_CONV_W
    s2 = s1 + GDN_V_W
    s4 = s2 + 2 * GDN_HEADS
    small = jnp.pad(w_in[:, s2:s4], ((0, 0), (0, SMALL_W - 2 * GDN_HEADS)))
    wcat = jnp.concatenate([w_in[:, :s2], small, w_in[:, s4:]], axis=1).astype(BF16)
    qkva, z, ba, qkvb = _inproj(x2, sc_a, sh_a, norm_mix.reshape(1, d), wcat, w_conv, seq)

    pad4 = lambda v: jnp.pad(v.astype(F32), (GDN_HEADS, LANES - 2 * GDN_HEADS))
    gp = jnp.zeros((8, LANES), F32).at[0].set(pad4(a_log)).at[1].set(pad4(dt_bias))
    og = _gdn(qkva, ba, ba[:, :8].T, gp, _gp_cols(a_log, dt_bias),
              _block_tri(GDN_TOK, CHUNK, True), _block_tri(GDN_TOK, CHUNK, False), seq)

    outs, lses = [], []
    for window, dil in DILATED_PAIRS:
        o_i, lse_i = _attn_branch(qkvb, _bias_table(rel_bias, window, dil), bsz, seq, dil)
        outs.append(o_i)
        lses.append(lse_i)

    expand = jnp.where((jnp.arange(LANES)[:, None] == jnp.arange(ATTN_W)[None, :] // ATTN_DH), 1.0, 0.0).astype(BF16)
    lstrict = jnp.where(jnp.arange(TM_POST)[:, None] > jnp.arange(TM_POST)[None, :], 1.0, 0.0).astype(BF16)
    wr = jnp.pad(jnp.concatenate([w_rg, w_re], axis=1).astype(F32), ((0, 0), (0, LANES - N_GROUPS - N_EXPERTS)))
    wrh = wr.astype(BF16)
    wrl = (wr - wrh.astype(F32)).astype(BF16)
    x1, h2, route, cnt = _post(og, z, outs, lses, x2, gt_a, sc_f, sh_f,
                               gdn_norm.reshape(1, GDN_DV), attn_norm.reshape(1, ATTN_W),
                               w_out.astype(BF16), norm_ffn.reshape(1, d), wrh, wrl, expand, lstrict, seq)

    blk = EXPERT_BLK
    counts = cnt[0, :N_EXPERTS].astype(jnp.int32)
    padded = (counts + blk - 1) // blk * blk
    pends = jnp.cumsum(padded)
    pstarts = pends - padded
    eids = route[:, 0:TOP_K].astype(jnp.int32)
    ranks = route[:, 4:4 + TOP_K].astype(jnp.int32)
    dest = (pstarts[eids] + ranks).reshape(t * TOP_K)
    a = t * TOP_K
    p = -(-a // blk) * blk + N_EXPERTS * blk
    nblk = p // blk
    blk_e = jnp.minimum(jnp.searchsorted(pends, jnp.arange(nblk, dtype=jnp.int32) * blk, side="right"),
                        N_EXPERTS - 1).astype(jnp.int32)
    nused = (pends[-1] // blk).astype(jnp.int32).reshape(1)

    xs = _dispatch(dest, h2, jnp.zeros((p, d), F32))
    ys = _experts(blk_e, nused, xs, w_gate.astype(BF16), w_up.astype(BF16), w_down.astype(BF16))
    return ys, dest, x1, route, gt_f


def _gp_cols(a_log, dt_bias):
    z = jnp.zeros((8, LANES), F32)
    z = z.at[GDN_HEADS:2 * GDN_HEADS, 0].set(a_log.astype(F32))
    z = z.at[GDN_HEADS:2 * GDN_HEADS, 1].set(dt_bias.astype(F32))
    return z


def kernel(x, c, w_ada, b_ada, norm_mix, w_in, w_conv, a_log, dt_bias, gdn_norm, attn_norm, w_out,
           rel_bias, norm_ffn, w_router_group, w_router_expert, w_gate, w_up, w_down, norm_final):
    bsz, seq, d = x.shape
    depth = w_ada.shape[0]
    x2 = x.reshape(bsz * seq, d)
    for l in range(depth):
        ys, dest, x1, route, gt_f = _layer(
            x2, c, bsz, seq, w_ada[l], b_ada[l], norm_mix[l], w_in[l], w_conv[l], a_log[l], dt_bias[l],
            gdn_norm[l], attn_norm[l], w_out[l], rel_bias, norm_ffn[l], w_router_group[l],
            w_router_expert[l], w_gate[l], w_up[l], w_down[l])
        x2 = _combine(dest, ys, x1, route, gt_f, norm_final.reshape(1, d), seq, l == depth - 1)
    return x2.reshape(bsz, seq, d)
```

```python
import functools
import math

import jax
import jax.numpy as jnp
import numpy as np
from jax import lax
from jax.experimental import pallas as pl
from jax.experimental.pallas import tpu as pltpu

D_MODEL = 1024
GDN_HEADS = 4
GDN_DK = 128
GDN_DV = 128
CONV_K = 4
CHUNK = 64
ATTN_HEADS = 8
ATTN_DH = 64
DILATED_PAIRS = ((128, 1), (512, 4), (2048, 16))
QBLK = 128
NUM_BUCKETS = 32
REL_MAX_DIST = 2048
N_GROUPS = 4
EXPERTS_PER_GROUP = 8
N_EXPERTS = N_GROUPS * EXPERTS_PER_GROUP
TOP_K = 2
D_EXPERT = 256
EPS = 1e-6
NEG_INF = -1e30

GDN_QK_W = GDN_HEADS * GDN_DK
GDN_V_W = GDN_HEADS * GDN_DV
ATTN_W = ATTN_HEADS * ATTN_DH
GDN_CONV_W = 2 * GDN_QK_W + GDN_V_W
LANES = 128
SMALL_W = LANES
W_COLS = GDN_CONV_W + GDN_V_W + SMALL_W + 3 * ATTN_W

TM_PROJ = 512
TM_POST = 256
TM_ROWS = 256
GDN_TOK = 256
GDN_GROUP = 2
INV_BASE = 8
EXPERT_BLK = 256
VMEM_LIMIT = 56 * 1024 * 1024

F32 = jnp.float32
BF16 = jnp.bfloat16
HIGHEST = lax.Precision.HIGHEST


def _sigmoid(x):
    return 1.0 / (1.0 + jnp.exp(-x))


def _dot(a, b, precision=None):
    return jnp.dot(a, b, preferred_element_type=F32, precision=precision)


def _dot_nt(a, b, precision=None):
    return lax.dot_general(a, b, (((1,), (1,)), ((), ())), preferred_element_type=F32,
                           precision=precision)


def _dot_tn(a, b, precision=None):
    return lax.dot_general(a, b, (((0,), (0,)), ((), ())), preferred_element_type=F32,
                           precision=precision)


def _params(sem):
    return pltpu.CompilerParams(dimension_semantics=sem, vmem_limit_bytes=VMEM_LIMIT)


def _adaln_kernel(c_ref, w_ref, b_ref, o_ref):
    o_ref[...] = _dot(c_ref[...], w_ref[...], HIGHEST) + b_ref[...]


def _adaln(c, w, b):
    bsz, d = c.shape
    n = w.shape[1]
    tn = 1024
    return pl.pallas_call(
        _adaln_kernel,
        grid=(n // tn,),
        in_specs=[pl.BlockSpec((bsz, d), lambda j: (0, 0)),
                  pl.BlockSpec((d, tn), lambda j: (0, j)),
                  pl.BlockSpec((1, tn), lambda j: (0, j))],
        out_specs=pl.BlockSpec((bsz, tn), lambda j: (0, j)),
        out_shape=jax.ShapeDtypeStruct((bsz, n), F32),
        compiler_params=_params(("arbitrary",)),
        name="adaln",
    )(c, w, b.reshape(1, n))


def _inproj_kernel(tiles_per_seq, x_ref, sc_ref, sh_ref, g_ref, w_ref, wconv_ref,
                   qkva_ref, z_ref, ba_ref, *rest):
    qkvb_refs = rest[:len(DILATED_PAIRS)]
    pa_scr, pb_scr = rest[len(DILATED_PAIRS):]
    i = pl.program_id(0)
    tm = x_ref.shape[0]
    x = x_ref[...]
    h = x * lax.rsqrt(jnp.mean(x * x, axis=-1, keepdims=True) + EPS) * g_ref[...]
    h = h * (1.0 + sc_ref[0]) + sh_ref[0]
    hb = h.astype(BF16)

    @pl.when(i % tiles_per_seq == 0)
    def _():
        pa_scr[0:8, :] = jnp.zeros((8, GDN_CONV_W), F32)

    pa = _dot(hb, w_ref[:, 0:GDN_CONV_W])
    pa_scr[8:8 + tm, :] = pa
    acc = pa * wconv_ref[CONV_K - 1:CONV_K, :]
    for j in range(CONV_K - 1):
        acc = acc + pa_scr[pl.ds(8 - (CONV_K - 1) + j, tm), :] * wconv_ref[j:j + 1, :]
    qkva_ref[...] = (acc * _sigmoid(acc)).astype(qkva_ref.dtype)
    pa_scr[0:8, :] = pa_scr[tm:tm + 8, :]

    c0 = GDN_CONV_W
    z_ref[...] = _dot(hb, w_ref[:, c0:c0 + GDN_V_W]).astype(z_ref.dtype)
    c0 += GDN_V_W
    ba_ref[...] = _dot(hb, w_ref[:, c0:c0 + SMALL_W])
    c0 += SMALL_W
    pb = _dot(hb, w_ref[:, c0:c0 + 3 * ATTN_W])
    ncol = 3 * ATTN_W // LANES
    for j in range(ncol):
        pb_scr[j] = pb[:, j * LANES:(j + 1) * LANES]
    for (_, dil), ref in zip(DILATED_PAIRS, qkvb_refs):
        if dil == 1:
            ref[...] = pb.astype(ref.dtype)
            continue
        for r in range(dil):
            for j in range(ncol):
                c1 = r * 3 * ATTN_W + j * LANES
                ref[:, c1:c1 + LANES] = pb_scr[j, pl.ds(r, tm // dil, stride=dil), :].astype(ref.dtype)


def _inproj(x2, sc, sh, g, wcat, wconv, seq):
    t, d = x2.shape
    tm = TM_PROJ
    tps = seq // tm
    mod_spec = pl.BlockSpec((1, 1, d), lambda i: (i // tps, 0, 0))
    return pl.pallas_call(
        functools.partial(_inproj_kernel, tps),
        grid=(t // tm,),
        in_specs=[pl.BlockSpec((tm, d), lambda i: (i, 0)),
                  mod_spec, mod_spec,
                  pl.BlockSpec((1, d), lambda i: (0, 0)),
                  pl.BlockSpec((d, W_COLS), lambda i: (0, 0)),
                  pl.BlockSpec((CONV_K, GDN_CONV_W), lambda i: (0, 0))],
        out_specs=[pl.BlockSpec((tm, GDN_CONV_W), lambda i: (i, 0)),
                   pl.BlockSpec((tm, GDN_V_W), lambda i: (i, 0)),
                   pl.BlockSpec((tm, SMALL_W), lambda i: (i, 0))]
                  + [pl.BlockSpec((tm // dil, dil * 3 * ATTN_W), lambda i: (i, 0)) for _, dil in DILATED_PAIRS],
        out_shape=[jax.ShapeDtypeStruct((t, GDN_CONV_W), BF16),
                   jax.ShapeDtypeStruct((t, GDN_V_W), BF16),
                   jax.ShapeDtypeStruct((t, SMALL_W), F32)]
                  + [jax.ShapeDtypeStruct((t // dil, dil * 3 * ATTN_W), BF16) for _, dil in DILATED_PAIRS],
        scratch_shapes=[pltpu.VMEM((tm + 8, GDN_CONV_W), F32),
                        pltpu.VMEM((3 * ATTN_W // LANES, tm, LANES), F32)],
        compiler_params=_params(("arbitrary",)),
        name="inproj",
    )(x2, sc, sh, g, wcat, wconv)


def _softplus(x):
    return jnp.maximum(x, 0.0) + jnp.log(1.0 + jnp.exp(-jnp.abs(x)))


def _unit_lower_inverses(n_list):
    c = n_list[0].shape[0]
    row = lax.broadcasted_iota(jnp.int32, (c, c), 0)
    col = lax.broadcasted_iota(jnp.int32, (c, c), 1)
    eye = jnp.where(row == col, 1.0, 0.0)
    same_base = (row // INV_BASE) == (col // INV_BASE)
    n0 = [jnp.where(same_base, n, 0.0) for n in n_list]
    p = [eye - x for x in n0]
    m = [x.astype(BF16) for x in n0]
    for j in range(int(math.log2(INV_BASE)) - 1):
        m = [_dot(x, x).astype(BF16) for x in m]
        p = [x + _dot(x.astype(BF16), y) for x, y in zip(p, m)]
    size = INV_BASE
    while size < c:
        sibling = jnp.logical_and((row // size) % 2 == 1, (col // size) == (row // size) - 1)
        cb = [jnp.where(sibling, n, 0.0).astype(BF16) for n in n_list]
        pb = [x.astype(BF16) for x in p]
        pc = [_dot(x, y).astype(BF16) for x, y in zip(pb, cb)]
        p = [x - _dot(y, z) for x, y, z in zip(p, pc, pb)]
        size *= 2
    return p


def _split3_bf16(x):
    h1 = x.astype(BF16)
    r1 = x - h1.astype(F32)
    h2 = r1.astype(BF16)
    h3 = (r1 - h2.astype(F32)).astype(BF16)
    return h1, h2, h3


def _gdn_kernel(steps_per_seq, qkv_ref, ba_ref, bat_ref, gp_ref, gpt_ref, lblk_ref, ublk_ref,
                o_ref, state_scr):
    i = pl.program_id(0)

    @pl.when(i % steps_per_seq == 0)
    def _():
        state_scr[...] = jnp.zeros_like(state_scr)

    nchunk = GDN_TOK // CHUNK
    ba = ba_ref[...]
    bat = bat_ref[...]
    a_vec = -jnp.exp(gp_ref[0:1, :])
    g_tile = a_vec * _softplus(ba + gp_ref[1:2, :])
    lblk = lblk_ref[...]
    g_cum = sum(_dot(lblk, part) for part in _split3_bf16(g_tile))
    a_col = -jnp.exp(gpt_ref[:, 0:1])
    gt_tile = a_col * _softplus(bat + gpt_ref[:, 1:2])
    ublk = ublk_ref[...]
    gt_cum = sum(_dot(part, ublk) for part in _split3_bf16(gt_tile))
    beta_tile = _sigmoid(ba)

    row = lax.broadcasted_iota(jnp.int32, (CHUNK, CHUNK), 0)
    col = lax.broadcasted_iota(jnp.int32, (CHUNK, CHUNK), 1)
    incl = row >= col
    strict = row > col

    for h0 in range(0, GDN_HEADS, GDN_GROUP):
        heads = range(h0, h0 + GDN_GROUP)
        kb_l, k_l, q_l, rhs_l, qg_l, kg_l, gl_l, inc_l, str_l = ([] for _ in range(9))
        for h in heads:
            q_raw = qkv_ref[:, h * GDN_DK:(h + 1) * GDN_DK].astype(F32)
            k_raw = qkv_ref[:, GDN_QK_W + h * GDN_DK:GDN_QK_W + (h + 1) * GDN_DK].astype(F32)
            v = qkv_ref[:, 2 * GDN_QK_W + h * GDN_DV:2 * GDN_QK_W + (h + 1) * GDN_DV].astype(F32)
            qn = q_raw * lax.rsqrt(jnp.sum(q_raw * q_raw, axis=-1, keepdims=True) + EPS) * (GDN_DK ** -0.5)
            kn = k_raw * lax.rsqrt(jnp.sum(k_raw * k_raw, axis=-1, keepdims=True) + EPS)
            beta = beta_tile[:, h:h + 1]
            gc_all = g_cum[:, GDN_HEADS + h:GDN_HEADS + h + 1]
            eg = jnp.exp(gc_all)
            kb = kn * beta
            rhs = jnp.concatenate([v * beta, kb * eg], axis=1).astype(BF16)
            qg = qn * eg
            kb16, k16, q16 = kb.astype(BF16), kn.astype(BF16), qn.astype(BF16)
            for c in range(nchunk):
                sl = slice(c * CHUNK, (c + 1) * CHUNK)
                g_c = gc_all[sl]
                g_r = gt_cum[GDN_HEADS + h:GDN_HEADS + h + 1, sl]
                dec = jnp.exp(g_c - g_r)
                inc_l.append(jnp.where(incl, dec, 0.0))
                str_l.append(jnp.where(strict, dec, 0.0))
                g_last = g_c[CHUNK - 1:CHUNK, :]
                kg_l.append((kn[sl] * jnp.exp(g_last - g_c)).astype(BF16))
                gl_l.append(jnp.exp(g_last))
                kb_l.append(kb16[sl])
                k_l.append(k16[sl])
                q_l.append(q16[sl])
                rhs_l.append(rhs[sl])
                qg_l.append(qg[sl])
        nprob = len(k_l)
        n_l = [_dot_nt(kb_l[j], k_l[j]) * str_l[j] for j in range(nprob)]
        aqk_l = [(_dot_nt(q_l[j], k_l[j]) * inc_l[j]).astype(BF16) for j in range(nprob)]
        tinv_l = _unit_lower_inverses(n_l)
        sol_l = [_dot(tinv_l[j].astype(BF16), rhs_l[j]).astype(BF16) for j in range(nprob)]
        a2_l = [_dot(aqk_l[j], sol_l[j]) for j in range(nprob)]
        k2_l = [_dot_tn(kg_l[j], sol_l[j]) for j in range(nprob)]
        states = [state_scr[h] for h in heads]
        for c in range(nchunk):
            sl = slice(c * CHUNK, (c + 1) * CHUNK)
            for hi, h in enumerate(heads):
                j = hi * nchunk + c
                sb = states[hi].astype(BF16)
                qeff = (qg_l[j] - a2_l[j][:, GDN_DV:]).astype(BF16)
                o = _dot(qeff, sb) + a2_l[j][:, :GDN_DV]
                states[hi] = gl_l[j] * states[hi] + k2_l[j][:, :GDN_DV] - _dot(k2_l[j][:, GDN_DV:].astype(BF16), sb)
                o_ref[sl, h * GDN_DV:(h + 1) * GDN_DV] = o.astype(o_ref.dtype)
        for hi, h in enumerate(heads):
            state_scr[h] = states[hi]


def _gdn(qkva, ba, bat, gp, gpt, lblk, ublk, seq):
    t = qkva.shape[0]
    tok = GDN_TOK
    sps = seq // tok
    const = lambda i: (0, 0)
    return pl.pallas_call(
        functools.partial(_gdn_kernel, sps),
        grid=(t // tok,),
        in_specs=[pl.BlockSpec((tok, GDN_CONV_W), lambda i: (i, 0)),
                  pl.BlockSpec((tok, SMALL_W), lambda i: (i, 0)),
                  pl.BlockSpec((8, tok), lambda i: (0, i)),
                  pl.BlockSpec((8, LANES), const),
                  pl.BlockSpec((8, LANES), const),
                  pl.BlockSpec((tok, tok), const),
                  pl.BlockSpec((tok, tok), const)],
        out_specs=pl.BlockSpec((tok, GDN_V_W), lambda i: (i, 0)),
        out_shape=jax.ShapeDtypeStruct((t, GDN_V_W), F32),
        scratch_shapes=[pltpu.VMEM((GDN_HEADS, GDN_DK, GDN_DV), F32)],
        compiler_params=_params(("arbitrary",)),
        name="gdn",
    )(qkva, ba, bat, gp, gpt, lblk, ublk)


def _attn_kernel(q_ref, kp_ref, kc_ref, vp_ref, vc_ref, bucket_ref, rb_ref, o_ref, lse_ref, bias_ref):
    n = pl.program_id(2)

    @pl.when(jnp.logical_and(jnp.logical_and(pl.program_id(0) == 0, pl.program_id(1) == 0), n == 0))
    def _():
        bucket = bucket_ref[...]
        for h in range(ATTN_HEADS):
            acc = jnp.full((QBLK, 2 * QBLK), NEG_INF, F32)
            for b in range(NUM_BUCKETS):
                acc = jnp.where(bucket == b, rb_ref[b, h], acc)
            bias_ref[h] = acc

    q = q_ref[0]
    k = jnp.concatenate([kp_ref[0], kc_ref[0]], axis=0)
    v = jnp.concatenate([vp_ref[0], vc_ref[0]], axis=0)
    col = lax.broadcasted_iota(jnp.int32, (QBLK, 2 * QBLK), 1)
    lane = lax.broadcasted_iota(jnp.int32, (QBLK, LANES), 1)
    no_prev = jnp.logical_and(n == 0, col < QBLK)
    outs = []
    lse_tile = jnp.zeros((QBLK, LANES), F32)
    for h in range(ATTN_HEADS):
        hs = slice(h * ATTN_DH, (h + 1) * ATTN_DH)
        s = _dot_nt(q[:, hs], k[:, hs]) * (ATTN_DH ** -0.5) + bias_ref[h]
        s = jnp.where(no_prev, NEG_INF, s)
        m = jnp.max(s, axis=-1, keepdims=True)
        p = jnp.exp(s - m)
        den = jnp.sum(p, axis=-1, keepdims=True)
        outs.append(_dot(p.astype(BF16), v[:, hs]) / den)
        lse_tile = jnp.where(lane == h, m + jnp.log(den), lse_tile)
    o_ref[0] = jnp.concatenate(outs, axis=1).astype(o_ref.dtype)
    lse_ref[0] = lse_tile


def _attn_branch(qkvb, bucket, rel_bias, bsz, seq, dil):
    ln = seq // dil
    nb = ln // QBLK
    xv = qkvb.reshape(bsz, ln, dil * 3 * ATTN_W)
    w = ATTN_W
    cur = lambda part: (lambda b, r, n: (b, n, r * 3 + part))
    prev = lambda part: (lambda b, r, n: (b, jnp.maximum(n - 1, 0), r * 3 + part))
    o, lse = pl.pallas_call(
        _attn_kernel,
        grid=(bsz, dil, nb),
        in_specs=[pl.BlockSpec((1, QBLK, w), cur(0)),
                  pl.BlockSpec((1, QBLK, w), prev(1)),
                  pl.BlockSpec((1, QBLK, w), cur(1)),
                  pl.BlockSpec((1, QBLK, w), prev(2)),
                  pl.BlockSpec((1, QBLK, w), cur(2)),
                  pl.BlockSpec((QBLK, 2 * QBLK), lambda b, r, n: (0, 0)),
                  pl.BlockSpec(memory_space=pltpu.SMEM)],
        out_specs=[pl.BlockSpec((1, QBLK, w), lambda b, r, n: (b, n, r)),
                   pl.BlockSpec((1, QBLK, LANES), lambda b, r, n: (b, n, r))],
        out_shape=[jax.ShapeDtypeStruct((bsz, ln, dil * w), BF16),
                   jax.ShapeDtypeStruct((bsz, ln, dil * LANES), F32)],
        scratch_shapes=[pltpu.VMEM((ATTN_HEADS, QBLK, 2 * QBLK), F32)],
        compiler_params=_params(("arbitrary", "arbitrary", "arbitrary")),
        name=f"attn_d{dil}",
    )(xv, xv, xv, xv, xv, bucket, rel_bias)
    return o.reshape(bsz * ln, dil * w), lse.reshape(bsz * ln, dil * LANES)


def _bucket_table(window, dil):
    steps = window // dil
    qi = np.arange(QBLK)[:, None]
    kj = np.arange(2 * QBLK)[None, :]
    delta = qi + QBLK - kj
    dist = np.maximum(delta, 0) * dil
    max_exact = NUM_BUCKETS // 2
    d_f = np.maximum(dist, 1).astype(np.float32)
    large = max_exact + (np.log(d_f / max_exact) / math.log(REL_MAX_DIST / max_exact)
                         * (NUM_BUCKETS - max_exact)).astype(np.int32)
    bucket = np.where(dist < max_exact, dist, np.minimum(large, NUM_BUCKETS - 1))
    return np.where((delta >= 0) & (delta <= steps), bucket, -1).astype(np.int32)


def _split_bf16(x):
    hi = x.astype(BF16)
    lo = (x - hi.astype(F32)).astype(BF16)
    return hi, lo


def _post_kernel(og_ref, z_ref, o1_ref, o2_ref, o3_ref, l1_ref, l2_ref, l3_ref, x_ref,
                 gta_ref, scf_ref, shf_ref, gnorm_ref, anorm_ref, wout_ref, nffn_ref,
                 wrh_ref, wrl_ref, exp_ref, ls_ref,
                 x1_ref, h2_ref, route_ref, cnt_ref, carry_scr, *tok_scr):
    i = pl.program_id(0)
    tm = x_ref.shape[0]

    @pl.when(i == 0)
    def _():
        carry_scr[...] = jnp.zeros_like(carry_scr)

    nbr = len(DILATED_PAIRS)
    ob_scr, lse_scr = tok_scr[:nbr], tok_scr[nbr:]
    for (_, dil), o_ref, l_ref, o_s, l_s in zip(DILATED_PAIRS, (o1_ref, o2_ref, o3_ref),
                                                 (l1_ref, l2_ref, l3_ref), ob_scr, lse_scr):
        for r in range(dil):
            rows = pl.ds(r, tm // dil, stride=dil) if dil > 1 else slice(None)
            for j in range(ATTN_W // LANES):
                c1 = r * ATTN_W + j * LANES
                o_s[j, rows, :] = o_ref[:, c1:c1 + LANES].astype(F32)
            l_s[rows, :] = l_ref[:, r * LANES:(r + 1) * LANES]

    heads = []
    for h in range(GDN_HEADS):
        hs = slice(h * GDN_DV, (h + 1) * GDN_DV)
        seg = og_ref[:, hs]
        nrm = seg * lax.rsqrt(jnp.mean(seg * seg, axis=-1, keepdims=True) + EPS) * gnorm_ref[...]
        zz = z_ref[:, hs].astype(F32)
        heads.append((nrm * (zz * _sigmoid(zz))).astype(BF16))
    oa = jnp.concatenate(heads, axis=1)

    l1, l2, l3 = (l_s[...] for l_s in lse_scr)
    m = jnp.maximum(jnp.maximum(l1, l2), l3)
    e1, e2, e3 = jnp.exp(l1 - m), jnp.exp(l2 - m), jnp.exp(l3 - m)
    inv = 1.0 / (e1 + e2 + e3)
    ob = jnp.zeros((tm, ATTN_W), F32)
    for e, o_s in zip((e1, e2, e3), ob_scr):
        hi, lo = _split_bf16(e * inv)
        wexp = _dot(hi, exp_ref[...]) + _dot(lo, exp_ref[...])
        ob = ob + wexp * jnp.concatenate([o_s[j] for j in range(ATTN_W // LANES)], axis=1)
    ob = ob * lax.rsqrt(jnp.mean(ob * ob, axis=-1, keepdims=True) + EPS) * anorm_ref[...]

    mix = _dot(oa, wout_ref[0:GDN_V_W, :]) + _dot(ob.astype(BF16), wout_ref[GDN_V_W:, :])
    x1 = x_ref[...] + gta_ref[0] * mix
    x1_ref[...] = x1
    h2 = x1 * lax.rsqrt(jnp.mean(x1 * x1, axis=-1, keepdims=True) + EPS) * nffn_ref[...]
    h2 = h2 * (1.0 + scf_ref[0]) + shf_ref[0]
    h2_ref[...] = h2

    hh, hl = _split_bf16(h2)
    logits = _dot(hh, wrh_ref[...]) + _dot(hh, wrl_ref[...]) + _dot(hl, wrh_ref[...])
    tm = logits.shape[0]
    lane = lax.broadcasted_iota(jnp.int32, (tm, LANES), 1).astype(F32)
    big = float(LANES)
    gmask = lane < N_GROUPS
    glog = jnp.where(gmask, logits, NEG_INF)
    gmax = jnp.max(glog, axis=-1, keepdims=True)
    gidx = jnp.min(jnp.where(jnp.logical_and(gmask, glog == gmax), lane, big), axis=-1, keepdims=True)
    gprob = 1.0 / jnp.sum(jnp.where(gmask, jnp.exp(glog - gmax), 0.0), axis=-1, keepdims=True)
    lo_lane = N_GROUPS + EXPERTS_PER_GROUP * gidx
    emask = jnp.logical_and(lane >= lo_lane, lane < lo_lane + EXPERTS_PER_GROUP)
    elog = jnp.where(emask, logits, NEG_INF)
    m1 = jnp.max(elog, axis=-1, keepdims=True)
    i1 = jnp.min(jnp.where(jnp.logical_and(emask, elog == m1), lane, big), axis=-1, keepdims=True)
    emask2 = jnp.logical_and(emask, lane != i1)
    elog2 = jnp.where(emask2, logits, NEG_INF)
    m2 = jnp.max(elog2, axis=-1, keepdims=True)
    i2 = jnp.min(jnp.where(jnp.logical_and(emask2, elog2 == m2), lane, big), axis=-1, keepdims=True)
    r = jnp.exp(m2 - m1)
    gate1 = gprob / (1.0 + r)
    gate2 = gprob * r / (1.0 + r)
    ex1 = i1 - N_GROUPS
    ex2 = i2 - N_GROUPS

    hit1 = lane == ex1
    hit2 = lane == ex2
    onehot = jnp.where(jnp.logical_or(hit1, hit2), 1.0, 0.0)
    pref = _dot(ls_ref[...], onehot.astype(BF16)) + carry_scr[...]
    rank1 = jnp.sum(jnp.where(hit1, pref, 0.0), axis=-1, keepdims=True)
    rank2 = jnp.sum(jnp.where(hit2, pref, 0.0), axis=-1, keepdims=True)
    carry = carry_scr[...] + jnp.sum(onehot, axis=0, keepdims=True)
    carry_scr[...] = carry
    cnt_ref[...] = jnp.broadcast_to(carry, cnt_ref.shape)

    route = jnp.zeros((tm, LANES), F32)
    for idx, val in enumerate((ex1, ex2, gate1, gate2, rank1, rank2)):
        route = jnp.where(lane == idx, val, route)
    route_ref[...] = route


def _post(og, z, os_, ls_, x2, gta, scf, shf, gnorm, anorm, wout, nffn, wrh, wrl, expand, lstrict, seq):
    t, d = x2.shape
    tm = TM_POST
    tps = seq // tm
    tile = lambda w: pl.BlockSpec((tm, w), lambda i: (i, 0))
    const2 = lambda a: pl.BlockSpec(a.shape, lambda i: (0, 0))
    mod_spec = pl.BlockSpec((1, 1, d), lambda i: (i // tps, 0, 0))
    return pl.pallas_call(
        _post_kernel,
        grid=(t // tm,),
        in_specs=[tile(GDN_V_W), tile(GDN_V_W)]
                 + [pl.BlockSpec((tm // dil, dil * ATTN_W), lambda i: (i, 0)) for _, dil in DILATED_PAIRS]
                 + [pl.BlockSpec((tm // dil, dil * LANES), lambda i: (i, 0)) for _, dil in DILATED_PAIRS]
                 + [tile(d),
                  mod_spec, mod_spec, mod_spec,
                  const2(gnorm), const2(anorm), const2(wout), const2(nffn),
                  const2(wrh), const2(wrl), const2(expand), const2(lstrict)],
        out_specs=[tile(d), tile(d), tile(LANES), pl.BlockSpec((8, LANES), lambda i: (0, 0))],
        out_shape=[jax.ShapeDtypeStruct((t, d), F32),
                   jax.ShapeDtypeStruct((t, d), F32),
                   jax.ShapeDtypeStruct((t, LANES), F32),
                   jax.ShapeDtypeStruct((8, LANES), F32)],
        scratch_shapes=[pltpu.VMEM((1, LANES), F32)]
                       + [pltpu.VMEM((ATTN_W // LANES, tm, LANES), F32) for _ in DILATED_PAIRS]
                       + [pltpu.VMEM((tm, LANES), F32) for _ in DILATED_PAIRS],
        compiler_params=_params(("arbitrary",)),
        name="post",
    )(og, z, *os_, *ls_, x2, gta, scf, shf, gnorm, anorm, wout, nffn, wrh, wrl, expand, lstrict)


def _dispatch_kernel(dest_ref, h2_ref, xs_in_ref, xs_ref, sem):
    del xs_in_ref
    i = pl.program_id(0)
    tm = h2_ref.shape[0]

    def row_copy(r, k):
        d = dest_ref[(i * tm + r) * TOP_K + k]
        return pltpu.make_async_copy(h2_ref.at[pl.ds(r, 1)], xs_ref.at[pl.ds(d, 1)], sem)

    def start(r, carry):
        for k in range(TOP_K):
            row_copy(r, k).start()
        return carry

    def wait(r, carry):
        for k in range(TOP_K):
            row_copy(r, k).wait()
        return carry

    lax.fori_loop(0, tm, start, 0)
    lax.fori_loop(0, tm, wait, 0)


def _dispatch(dest, h2, xs_init):
    t, d = h2.shape
    tm = TM_ROWS
    grid_spec = pltpu.PrefetchScalarGridSpec(
        num_scalar_prefetch=1,
        grid=(t // tm,),
        in_specs=[pl.BlockSpec((tm, d), lambda i, dest: (i, 0)),
                  pl.BlockSpec(memory_space=pl.ANY)],
        out_specs=pl.BlockSpec(memory_space=pl.ANY),
        scratch_shapes=[pltpu.SemaphoreType.DMA],
    )
    return pl.pallas_call(
        _dispatch_kernel,
        grid_spec=grid_spec,
        out_shape=jax.ShapeDtypeStruct(xs_init.shape, xs_init.dtype),
        input_output_aliases={2: 0},
        compiler_params=_params(("arbitrary",)),
        name="dispatch",
    )(dest, h2, xs_init)


def _expert_kernel(blk_e_ref, nused_ref, xs_ref, wg_ref, wu_ref, wd_ref, ys_ref):
    b = pl.program_id(0)

    @pl.when(b < nused_ref[0])
    def _():
        x = xs_ref[...].astype(BF16)
        g = _dot(x, wg_ref[0])
        u = _dot(x, wu_ref[0])
        hid = (g * _sigmoid(g)) * u
        ys_ref[...] = _dot(hid.astype(BF16), wd_ref[0])

    @pl.when(b >= nused_ref[0])
    def _():
        ys_ref[...] = jnp.zeros_like(ys_ref)


def _experts(blk_e, nused, xs, wg, wu, wd):
    p, d = xs.shape
    blk = EXPERT_BLK
    row_map = lambda b, be, nu: (jnp.minimum(b, nu[0] - 1), 0)
    w_map = lambda b, be, nu: (be[b], 0, 0)
    grid_spec = pltpu.PrefetchScalarGridSpec(
        num_scalar_prefetch=2,
        grid=(p // blk,),
        in_specs=[pl.BlockSpec((blk, d), row_map),
                  pl.BlockSpec((1, d, D_EXPERT), w_map),
                  pl.BlockSpec((1, d, D_EXPERT), w_map),
                  pl.BlockSpec((1, D_EXPERT, d), w_map)],
        out_specs=pl.BlockSpec((blk, d), lambda b, be, nu: (b, 0)),
    )
    return pl.pallas_call(
        _expert_kernel,
        grid_spec=grid_spec,
        out_shape=jax.ShapeDtypeStruct((p, d), F32),
        compiler_params=_params(("arbitrary",)),
        name="experts",
    )(blk_e, nused, xs, wg, wu, wd)


def _combine_kernel(final_norm, dest_ref, ys_ref, x1_ref, route_ref, gtf_ref, nf_ref, o_ref, ybuf, sem):
    i = pl.program_id(0)
    tm = x1_ref.shape[0]

    def row_copy(r, k):
        d = dest_ref[(i * tm + r) * TOP_K + k]
        return pltpu.make_async_copy(ys_ref.at[pl.ds(d, 1)], ybuf.at[k, pl.ds(r, 1)], sem)

    def start(r, carry):
        for k in range(TOP_K):
            row_copy(r, k).start()
        return carry

    def wait(r, carry):
        for k in range(TOP_K):
            row_copy(r, k).wait()
        return carry

    lax.fori_loop(0, tm, start, 0)
    lax.fori_loop(0, tm, wait, 0)
    route = route_ref[...]
    moe = ybuf[0] * route[:, 2:3] + ybuf[1] * route[:, 3:4]
    x2 = x1_ref[...] + gtf_ref[0] * moe
    if final_norm:
        x2 = x2 * lax.rsqrt(jnp.mean(x2 * x2, axis=-1, keepdims=True) + EPS) * nf_ref[...]
    o_ref[...] = x2


def _combine(dest, ys, x1, route, gtf, nf, seq, final_norm):
    t, d = x1.shape
    tm = TM_ROWS
    tps = seq // tm
    grid_spec = pltpu.PrefetchScalarGridSpec(
        num_scalar_prefetch=1,
        grid=(t // tm,),
        in_specs=[pl.BlockSpec(memory_space=pl.ANY),
                  pl.BlockSpec((tm, d), lambda i, dest: (i, 0)),
                  pl.BlockSpec((tm, LANES), lambda i, dest: (i, 0)),
                  pl.BlockSpec((1, 1, d), lambda i, dest: (i // tps, 0, 0)),
                  pl.BlockSpec((1, d), lambda i, dest: (0, 0))],
        out_specs=pl.BlockSpec((tm, d), lambda i, dest: (i, 0)),
        scratch_shapes=[pltpu.VMEM((TOP_K, tm, d), F32), pltpu.SemaphoreType.DMA],
    )
    return pl.pallas_call(
        functools.partial(_combine_kernel, final_norm),
        grid_spec=grid_spec,
        out_shape=jax.ShapeDtypeStruct((t, d), F32),
        compiler_params=_params(("arbitrary",)),
        name="combine",
    )(dest, ys, x1, route, gtf, nf)


def _block_tri(n, chunk, lower):
    r = jnp.arange(n)[:, None]
    c = jnp.arange(n)[None, :]
    same = (r // chunk) == (c // chunk)
    tri = (r >= c) if lower else (r <= c)
    return jnp.where(same & tri, 1.0, 0.0).astype(BF16)


def _layer(x2, c, bsz, seq, w_ada, b_ada, norm_mix, w_in, w_conv, a_log, dt_bias, gdn_norm,
           attn_norm, w_out, rel_bias, norm_ffn, w_rg, w_re, w_gate, w_up, w_down):
    t, d = x2.shape
    mod = _adaln(c, w_ada, b_ada)
    sh_a, sc_a, gt_a, sh_f, sc_f, gt_f = [m.reshape(bsz, 1, d) for m in jnp.split(mod, 6, axis=-1)]

    s1 = GDN_CONV_W
    s2 = s1 + GDN_V_W
    s4 = s2 + 2 * GDN_HEADS
    small = jnp.pad(w_in[:, s2:s4], ((0, 0), (0, SMALL_W - 2 * GDN_HEADS)))
    wcat = jnp.concatenate([w_in[:, :s2], small, w_in[:, s4:]], axis=1).astype(BF16)
    qkva, z, ba, *qkvb = _inproj(x2, sc_a, sh_a, norm_mix.reshape(1, d), wcat, w_conv, seq)

    pad4 = lambda v: jnp.pad(v.astype(F32), (GDN_HEADS, LANES - 2 * GDN_HEADS))
    gp = jnp.zeros((8, LANES), F32).at[0].set(pad4(a_log)).at[1].set(pad4(dt_bias))
    og = _gdn(qkva, ba, ba[:, :8].T, gp, _gp_cols(a_log, dt_bias),
              _block_tri(GDN_TOK, CHUNK, True), _block_tri(GDN_TOK, CHUNK, False), seq)

    outs, lses = [], []
    for (window, dil), qkvb_d in zip(DILATED_PAIRS, qkvb):
        o_i, lse_i = _attn_branch(qkvb_d, jnp.asarray(_bucket_table(window, dil)), rel_bias.astype(F32),
                                  bsz, seq, dil)
        outs.append(o_i)
        lses.append(lse_i)

    expand = jnp.where((jnp.arange(LANES)[:, None] == jnp.arange(ATTN_W)[None, :] // ATTN_DH), 1.0, 0.0).astype(BF16)
    lstrict = jnp.where(jnp.arange(TM_POST)[:, None] > jnp.arange(TM_POST)[None, :], 1.0, 0.0).astype(BF16)
    wr = jnp.pad(jnp.concatenate([w_rg, w_re], axis=1).astype(F32), ((0, 0), (0, LANES - N_GROUPS - N_EXPERTS)))
    wrh = wr.astype(BF16)
    wrl = (wr - wrh.astype(F32)).astype(BF16)
    x1, h2, route, cnt = _post(og, z, outs, lses, x2, gt_a, sc_f, sh_f,
                               gdn_norm.reshape(1, GDN_DV), attn_norm.reshape(1, ATTN_W),
                               w_out.astype(BF16), norm_ffn.reshape(1, d), wrh, wrl, expand, lstrict, seq)

    blk = EXPERT_BLK
    counts = cnt[0, :N_EXPERTS].astype(jnp.int32)
    padded = (counts + blk - 1) // blk * blk
    pends = jnp.cumsum(padded)
    pstarts = pends - padded
    eids = route[:, 0:TOP_K].astype(jnp.int32)
    ranks = route[:, 4:4 + TOP_K].astype(jnp.int32)
    expert_ids = jnp.arange(N_EXPERTS, dtype=jnp.int32)
    seg_start = jnp.sum(jnp.where(eids[..., None] == expert_ids, pstarts, 0), axis=-1)
    dest = (seg_start + ranks).reshape(t * TOP_K)
    a = t * TOP_K
    p = -(-a // blk) * blk + N_EXPERTS * blk
    nblk = p // blk
    blk_start = jnp.arange(nblk, dtype=jnp.int32) * blk
    blk_e = jnp.minimum(jnp.sum((pends[None, :] <= blk_start[:, None]).astype(jnp.int32), axis=1),
                        N_EXPERTS - 1)
    nused = (pends[-1] // blk).astype(jnp.int32).reshape(1)

    xs = _dispatch(dest, h2, jnp.zeros((p, d), F32))
    ys = _experts(blk_e, nused, xs, w_gate.astype(BF16), w_up.astype(BF16), w_down.astype(BF16))
    return ys, dest, x1, route, gt_f


def _gp_cols(a_log, dt_bias):
    z = jnp.zeros((8, LANES), F32)
    z = z.at[GDN_HEADS:2 * GDN_HEADS, 0].set(a_log.astype(F32))
    z = z.at[GDN_HEADS:2 * GDN_HEADS, 1].set(dt_bias.astype(F32))
    return z


def kernel(x, c, w_ada, b_ada, norm_mix, w_in, w_conv, a_log, dt_bias, gdn_norm, attn_norm, w_out,
           rel_bias, norm_ffn, w_router_group, w_router_expert, w_gate, w_up, w_down, norm_final):
    bsz, seq, d = x.shape
    depth = w_ada.shape[0]
    x2 = x.reshape(bsz * seq, d)
    for l in range(depth):
        ys, dest, x1, route, gt_f = _layer(
            x2, c, bsz, seq, w_ada[l], b_ada[l], norm_mix[l], w_in[l], w_conv[l], a_log[l], dt_bias[l],
            gdn_norm[l], attn_norm[l], w_out[l], rel_bias, norm_ffn[l], w_router_group[l],
            w_router_expert[l], w_gate[l], w_up[l], w_down[l])
        x2 = _combine(dest, ys, x1, route, gt_f, norm_final.reshape(1, d), seq, l == depth - 1)
    return x2.reshape(bsz, seq, d)
```

```python
import functools
import math

import jax
import jax.numpy as jnp
import numpy as np
from jax import lax
from jax.experimental import pallas as pl
from jax.experimental.pallas import tpu as pltpu

D_MODEL = 1024
GDN_HEADS = 4
GDN_DK = 128
GDN_DV = 128
CONV_K = 4
CHUNK = 64
ATTN_HEADS = 8
ATTN_DH = 64
DILATED_PAIRS = ((128, 1), (512, 4), (2048, 16))
QBLK = 128
NUM_BUCKETS = 32
REL_MAX_DIST = 2048
N_GROUPS = 4
EXPERTS_PER_GROUP = 8
N_EXPERTS = N_GROUPS * EXPERTS_PER_GROUP
TOP_K = 2
D_EXPERT = 256
EPS = 1e-6
NEG_INF = -1e30

GDN_QK_W = GDN_HEADS * GDN_DK
GDN_V_W = GDN_HEADS * GDN_DV
ATTN_W = ATTN_HEADS * ATTN_DH
GDN_CONV_W = 2 * GDN_QK_W + GDN_V_W
LANES = 128
SMALL_W = LANES
W_COLS = GDN_CONV_W + GDN_V_W + SMALL_W + 3 * ATTN_W

TM_PROJ = 512
TM_POST = 256
TM_ROWS = 256
GDN_TOK = 256
GDN_GROUP = 2
INV_BASE = 8
EXPERT_BLK = 256
ROW_UNROLL = 8
VMEM_LIMIT = 56 * 1024 * 1024

F32 = jnp.float32
BF16 = jnp.bfloat16
HIGHEST = lax.Precision.HIGHEST


def _sigmoid(x):
    return 1.0 / (1.0 + jnp.exp(-x))


def _dot(a, b, precision=None):
    return jnp.dot(a, b, preferred_element_type=F32, precision=precision)


def _dot_nt(a, b, precision=None):
    return lax.dot_general(a, b, (((1,), (1,)), ((), ())), preferred_element_type=F32,
                           precision=precision)


def _dot_tn(a, b, precision=None):
    return lax.dot_general(a, b, (((0,), (0,)), ((), ())), preferred_element_type=F32,
                           precision=precision)


def _params(sem):
    return pltpu.CompilerParams(dimension_semantics=sem, vmem_limit_bytes=VMEM_LIMIT)


def _adaln_kernel(c_ref, w_ref, b_ref, o_ref):
    o_ref[...] = _dot(c_ref[...], w_ref[...], HIGHEST) + b_ref[...]


def _adaln(c, w, b):
    bsz, d = c.shape
    n = w.shape[1]
    tn = 1024
    return pl.pallas_call(
        _adaln_kernel,
        grid=(n // tn,),
        in_specs=[pl.BlockSpec((bsz, d), lambda j: (0, 0)),
                  pl.BlockSpec((d, tn), lambda j: (0, j)),
                  pl.BlockSpec((1, tn), lambda j: (0, j))],
        out_specs=pl.BlockSpec((bsz, tn), lambda j: (0, j)),
        out_shape=jax.ShapeDtypeStruct((bsz, n), F32),
        compiler_params=_params(("arbitrary",)),
        name="adaln",
    )(c, w, b.reshape(1, n))


def _inproj_kernel(tiles_per_seq, x_ref, sc_ref, sh_ref, g_ref, w_ref, wconv_ref,
                   qkva_ref, z_ref, ba_ref, *rest):
    qkvb_refs = rest[:len(DILATED_PAIRS)]
    pa_scr, pb_scr = rest[len(DILATED_PAIRS):]
    i = pl.program_id(0)
    tm = x_ref.shape[0]
    x = x_ref[...]
    h = x * lax.rsqrt(jnp.mean(x * x, axis=-1, keepdims=True) + EPS) * g_ref[...]
    h = h * (1.0 + sc_ref[0]) + sh_ref[0]
    hb = h.astype(BF16)

    @pl.when(i % tiles_per_seq == 0)
    def _():
        pa_scr[0:8, :] = jnp.zeros((8, GDN_CONV_W), F32)

    pa = _dot(hb, w_ref[:, 0:GDN_CONV_W])
    pa_scr[8:8 + tm, :] = pa
    acc = pa * wconv_ref[CONV_K - 1:CONV_K, :]
    for j in range(CONV_K - 1):
        acc = acc + pa_scr[pl.ds(8 - (CONV_K - 1) + j, tm), :] * wconv_ref[j:j + 1, :]
    qkva_ref[...] = (acc * _sigmoid(acc)).astype(qkva_ref.dtype)
    pa_scr[0:8, :] = pa_scr[tm:tm + 8, :]

    c0 = GDN_CONV_W
    z_ref[...] = _dot(hb, w_ref[:, c0:c0 + GDN_V_W]).astype(z_ref.dtype)
    c0 += GDN_V_W
    ba_ref[...] = _dot(hb, w_ref[:, c0:c0 + SMALL_W])
    c0 += SMALL_W
    pb = _dot(hb, w_ref[:, c0:c0 + 3 * ATTN_W])
    ncol = 3 * ATTN_W // LANES
    for j in range(ncol):
        pb_scr[j] = pb[:, j * LANES:(j + 1) * LANES]
    for (_, dil), ref in zip(DILATED_PAIRS, qkvb_refs):
        if dil == 1:
            ref[...] = pb.astype(ref.dtype)
            continue
        for r in range(dil):
            for j in range(ncol):
                c1 = r * 3 * ATTN_W + j * LANES
                ref[:, c1:c1 + LANES] = pb_scr[j, pl.ds(r, tm // dil, stride=dil), :].astype(ref.dtype)


def _inproj(x2, sc, sh, g, wcat, wconv, seq):
    t, d = x2.shape
    tm = TM_PROJ
    tps = seq // tm
    mod_spec = pl.BlockSpec((1, 1, d), lambda i: (i // tps, 0, 0))
    return pl.pallas_call(
        functools.partial(_inproj_kernel, tps),
        grid=(t // tm,),
        in_specs=[pl.BlockSpec((tm, d), lambda i: (i, 0)),
                  mod_spec, mod_spec,
                  pl.BlockSpec((1, d), lambda i: (0, 0)),
                  pl.BlockSpec((d, W_COLS), lambda i: (0, 0)),
                  pl.BlockSpec((CONV_K, GDN_CONV_W), lambda i: (0, 0))],
        out_specs=[pl.BlockSpec((tm, GDN_CONV_W), lambda i: (i, 0)),
                   pl.BlockSpec((tm, GDN_V_W), lambda i: (i, 0)),
                   pl.BlockSpec((tm, SMALL_W), lambda i: (i, 0))]
                  + [pl.BlockSpec((tm // dil, dil * 3 * ATTN_W), lambda i: (i, 0)) for _, dil in DILATED_PAIRS],
        out_shape=[jax.ShapeDtypeStruct((t, GDN_CONV_W), BF16),
                   jax.ShapeDtypeStruct((t, GDN_V_W), BF16),
                   jax.ShapeDtypeStruct((t, SMALL_W), F32)]
                  + [jax.ShapeDtypeStruct((t // dil, dil * 3 * ATTN_W), BF16) for _, dil in DILATED_PAIRS],
        scratch_shapes=[pltpu.VMEM((tm + 8, GDN_CONV_W), F32),
                        pltpu.VMEM((3 * ATTN_W // LANES, tm, LANES), F32)],
        compiler_params=_params(("arbitrary",)),
        name="inproj",
    )(x2, sc, sh, g, wcat, wconv)


def _softplus(x):
    return jnp.maximum(x, 0.0) + jnp.log(1.0 + jnp.exp(-jnp.abs(x)))


def _unit_lower_inverses(n_list):
    c = n_list[0].shape[0]
    row = lax.broadcasted_iota(jnp.int32, (c, c), 0)
    col = lax.broadcasted_iota(jnp.int32, (c, c), 1)
    eye = jnp.where(row == col, 1.0, 0.0)
    same_base = (row // INV_BASE) == (col // INV_BASE)
    n0 = [jnp.where(same_base, n, 0.0) for n in n_list]
    p = [eye - x for x in n0]
    m = [x.astype(BF16) for x in n0]
    for j in range(int(math.log2(INV_BASE)) - 1):
        m = [_dot(x, x).astype(BF16) for x in m]
        p = [x + _dot(x.astype(BF16), y) for x, y in zip(p, m)]
    size = INV_BASE
    while size < c:
        sibling = jnp.logical_and((row // size) % 2 == 1, (col // size) == (row // size) - 1)
        cb = [jnp.where(sibling, n, 0.0).astype(BF16) for n in n_list]
        pb = [x.astype(BF16) for x in p]
        pc = [_dot(x, y).astype(BF16) for x, y in zip(pb, cb)]
        p = [x - _dot(y, z) for x, y, z in zip(p, pc, pb)]
        size *= 2
    return p


def _split3_bf16(x):
    h1 = x.astype(BF16)
    r1 = x - h1.astype(F32)
    h2 = r1.astype(BF16)
    h3 = (r1 - h2.astype(F32)).astype(BF16)
    return h1, h2, h3


def _gdn_kernel(steps_per_seq, qkv_ref, ba_ref, bat_ref, gp_ref, gpt_ref, lblk_ref, ublk_ref,
                o_ref, state_scr):
    i = pl.program_id(0)

    @pl.when(i % steps_per_seq == 0)
    def _():
        state_scr[...] = jnp.zeros_like(state_scr)

    nchunk = GDN_TOK // CHUNK
    ba = ba_ref[...]
    bat = bat_ref[...]
    a_vec = -jnp.exp(gp_ref[0:1, :])
    g_tile = a_vec * _softplus(ba + gp_ref[1:2, :])
    lblk = lblk_ref[...]
    g_cum = sum(_dot(lblk, part) for part in _split3_bf16(g_tile))
    a_col = -jnp.exp(gpt_ref[:, 0:1])
    gt_tile = a_col * _softplus(bat + gpt_ref[:, 1:2])
    ublk = ublk_ref[...]
    gt_cum = sum(_dot(part, ublk) for part in _split3_bf16(gt_tile))
    beta_tile = _sigmoid(ba)

    row = lax.broadcasted_iota(jnp.int32, (CHUNK, CHUNK), 0)
    col = lax.broadcasted_iota(jnp.int32, (CHUNK, CHUNK), 1)
    incl = row >= col
    strict = row > col

    for h0 in range(0, GDN_HEADS, GDN_GROUP):
        heads = range(h0, h0 + GDN_GROUP)
        kb_l, k_l, q_l, rhs_l, qg_l, kg_l, gl_l, inc_l, str_l = ([] for _ in range(9))
        for h in heads:
            q_raw = qkv_ref[:, h * GDN_DK:(h + 1) * GDN_DK].astype(F32)
            k_raw = qkv_ref[:, GDN_QK_W + h * GDN_DK:GDN_QK_W + (h + 1) * GDN_DK].astype(F32)
            v = qkv_ref[:, 2 * GDN_QK_W + h * GDN_DV:2 * GDN_QK_W + (h + 1) * GDN_DV].astype(F32)
            qn = q_raw * lax.rsqrt(jnp.sum(q_raw * q_raw, axis=-1, keepdims=True) + EPS) * (GDN_DK ** -0.5)
            kn = k_raw * lax.rsqrt(jnp.sum(k_raw * k_raw, axis=-1, keepdims=True) + EPS)
            beta = beta_tile[:, h:h + 1]
            gc_all = g_cum[:, GDN_HEADS + h:GDN_HEADS + h + 1]
            eg = jnp.exp(gc_all)
            kb = kn * beta
            rhs = jnp.concatenate([v * beta, kb * eg], axis=1).astype(BF16)
            qg = qn * eg
            kb16, k16, q16 = kb.astype(BF16), kn.astype(BF16), qn.astype(BF16)
            for c in range(nchunk):
                sl = slice(c * CHUNK, (c + 1) * CHUNK)
                g_c = gc_all[sl]
                g_r = gt_cum[GDN_HEADS + h:GDN_HEADS + h + 1, sl]
                dec = jnp.exp(g_c - g_r)
                inc_l.append(jnp.where(incl, dec, 0.0))
                str_l.append(jnp.where(strict, dec, 0.0))
                g_last = g_c[CHUNK - 1:CHUNK, :]
                kg_l.append((kn[sl] * jnp.exp(g_last - g_c)).astype(BF16))
                gl_l.append(jnp.exp(g_last))
                kb_l.append(kb16[sl])
                k_l.append(k16[sl])
                q_l.append(q16[sl])
                rhs_l.append(rhs[sl])
                qg_l.append(qg[sl])
        nprob = len(k_l)
        n_l = [_dot_nt(kb_l[j], k_l[j]) * str_l[j] for j in range(nprob)]
        aqk_l = [(_dot_nt(q_l[j], k_l[j]) * inc_l[j]).astype(BF16) for j in range(nprob)]
        tinv_l = _unit_lower_inverses(n_l)
        sol_l = [_dot(tinv_l[j].astype(BF16), rhs_l[j]).astype(BF16) for j in range(nprob)]
        a2_l = [_dot(aqk_l[j], sol_l[j]) for j in range(nprob)]
        k2_l = [_dot_tn(kg_l[j], sol_l[j]) for j in range(nprob)]
        states = [state_scr[h] for h in heads]
        for c in range(nchunk):
            sl = slice(c * CHUNK, (c + 1) * CHUNK)
            for hi, h in enumerate(heads):
                j = hi * nchunk + c
                sb = states[hi].astype(BF16)
                qeff = (qg_l[j] - a2_l[j][:, GDN_DV:]).astype(BF16)
                o = _dot(qeff, sb) + a2_l[j][:, :GDN_DV]
                states[hi] = gl_l[j] * states[hi] + k2_l[j][:, :GDN_DV] - _dot(k2_l[j][:, GDN_DV:].astype(BF16), sb)
                o_ref[sl, h * GDN_DV:(h + 1) * GDN_DV] = o.astype(o_ref.dtype)
        for hi, h in enumerate(heads):
            state_scr[h] = states[hi]


def _gdn(qkva, ba, bat, gp, gpt, lblk, ublk, seq):
    t = qkva.shape[0]
    tok = GDN_TOK
    sps = seq // tok
    const = lambda i: (0, 0)
    return pl.pallas_call(
        functools.partial(_gdn_kernel, sps),
        grid=(t // tok,),
        in_specs=[pl.BlockSpec((tok, GDN_CONV_W), lambda i: (i, 0)),
                  pl.BlockSpec((tok, SMALL_W), lambda i: (i, 0)),
                  pl.BlockSpec((8, tok), lambda i: (0, i)),
                  pl.BlockSpec((8, LANES), const),
                  pl.BlockSpec((8, LANES), const),
                  pl.BlockSpec((tok, tok), const),
                  pl.BlockSpec((tok, tok), const)],
        out_specs=pl.BlockSpec((tok, GDN_V_W), lambda i: (i, 0)),
        out_shape=jax.ShapeDtypeStruct((t, GDN_V_W), F32),
        scratch_shapes=[pltpu.VMEM((GDN_HEADS, GDN_DK, GDN_DV), F32)],
        compiler_params=_params(("arbitrary",)),
        name="gdn",
    )(qkva, ba, bat, gp, gpt, lblk, ublk)


def _attn_kernel(q_ref, kp_ref, kc_ref, vp_ref, vc_ref, bucket_ref, rb_ref, o_ref, lse_ref, bias_ref):
    n = pl.program_id(2)

    @pl.when(jnp.logical_and(jnp.logical_and(pl.program_id(0) == 0, pl.program_id(1) == 0), n == 0))
    def _():
        bucket = bucket_ref[...]
        for h in range(ATTN_HEADS):
            acc = jnp.full((QBLK, 2 * QBLK), NEG_INF, F32)
            for b in range(NUM_BUCKETS):
                acc = jnp.where(bucket == b, rb_ref[b, h], acc)
            bias_ref[h] = acc

    q = q_ref[0]
    k = jnp.concatenate([kp_ref[0], kc_ref[0]], axis=0)
    v = jnp.concatenate([vp_ref[0], vc_ref[0]], axis=0)
    col = lax.broadcasted_iota(jnp.int32, (QBLK, 2 * QBLK), 1)
    lane = lax.broadcasted_iota(jnp.int32, (QBLK, LANES), 1)
    no_prev = jnp.logical_and(n == 0, col < QBLK)
    outs = []
    lse_tile = jnp.zeros((QBLK, LANES), F32)
    per_group = LANES // ATTN_DH
    scale = ATTN_DH ** -0.5
    for grp in range(ATTN_HEADS // per_group):
        gs = slice(grp * LANES, (grp + 1) * LANES)
        qg, kg, vg = q[:, gs] * scale, k[:, gs], v[:, gs]
        o_grp = jnp.zeros((QBLK, LANES), F32)
        for j in range(per_group):
            h = grp * per_group + j
            mine = jnp.logical_and(lane >= j * ATTN_DH, lane < (j + 1) * ATTN_DH)
            s = _dot_nt(jnp.where(mine, qg, jnp.zeros_like(qg)), kg) + bias_ref[h]
            s = jnp.where(no_prev, NEG_INF, s)
            m = jnp.max(s, axis=-1, keepdims=True)
            p = jnp.exp(s - m)
            den = jnp.sum(p, axis=-1, keepdims=True)
            o_grp = jnp.where(mine, _dot(p.astype(BF16), vg) / den, o_grp)
            lse_tile = jnp.where(lane == h, m + jnp.log(den), lse_tile)
        outs.append(o_grp)
    o_ref[0] = jnp.concatenate(outs, axis=1).astype(o_ref.dtype)
    lse_ref[0] = lse_tile


def _attn_branch(qkvb, bucket, rel_bias, bsz, seq, dil):
    ln = seq // dil
    nb = ln // QBLK
    xv = qkvb.reshape(bsz, ln, dil * 3 * ATTN_W)
    w = ATTN_W
    cur = lambda part: (lambda b, r, n: (b, n, r * 3 + part))
    prev = lambda part: (lambda b, r, n: (b, jnp.maximum(n - 1, 0), r * 3 + part))
    o, lse = pl.pallas_call(
        _attn_kernel,
        grid=(bsz, dil, nb),
        in_specs=[pl.BlockSpec((1, QBLK, w), cur(0)),
                  pl.BlockSpec((1, QBLK, w), prev(1)),
                  pl.BlockSpec((1, QBLK, w), cur(1)),
                  pl.BlockSpec((1, QBLK, w), prev(2)),
                  pl.BlockSpec((1, QBLK, w), cur(2)),
                  pl.BlockSpec((QBLK, 2 * QBLK), lambda b, r, n: (0, 0)),
                  pl.BlockSpec(memory_space=pltpu.SMEM)],
        out_specs=[pl.BlockSpec((1, QBLK, w), lambda b, r, n: (b, n, r)),
                   pl.BlockSpec((1, QBLK, LANES), lambda b, r, n: (b, n, r))],
        out_shape=[jax.ShapeDtypeStruct((bsz, ln, dil * w), BF16),
                   jax.ShapeDtypeStruct((bsz, ln, dil * LANES), F32)],
        scratch_shapes=[pltpu.VMEM((ATTN_HEADS, QBLK, 2 * QBLK), F32)],
        compiler_params=_params(("arbitrary", "arbitrary", "arbitrary")),
        name=f"attn_d{dil}",
    )(xv, xv, xv, xv, xv, bucket, rel_bias)
    return o.reshape(bsz * ln, dil * w), lse.reshape(bsz * ln, dil * LANES)


def _bucket_table(window, dil):
    steps = window // dil
    qi = np.arange(QBLK)[:, None]
    kj = np.arange(2 * QBLK)[None, :]
    delta = qi + QBLK - kj
    dist = np.maximum(delta, 0) * dil
    max_exact = NUM_BUCKETS // 2
    d_f = np.maximum(dist, 1).astype(np.float32)
    large = max_exact + (np.log(d_f / max_exact) / math.log(REL_MAX_DIST / max_exact)
                         * (NUM_BUCKETS - max_exact)).astype(np.int32)
    bucket = np.where(dist < max_exact, dist, np.minimum(large, NUM_BUCKETS - 1))
    return np.where((delta >= 0) & (delta <= steps), bucket, -1).astype(np.int32)


def _split_bf16(x):
    hi = x.astype(BF16)
    lo = (x - hi.astype(F32)).astype(BF16)
    return hi, lo


def _post_kernel(og_ref, z_ref, o1_ref, o2_ref, o3_ref, l1_ref, l2_ref, l3_ref, x_ref,
                 gta_ref, scf_ref, shf_ref, gnorm_ref, anorm_ref, wout_ref, nffn_ref,
                 wrh_ref, wrl_ref, exp_ref, ls_ref,
                 x1_ref, h2_ref, route_ref, cnt_ref, carry_scr, *tok_scr):
    i = pl.program_id(0)
    tm = x_ref.shape[0]

    @pl.when(i == 0)
    def _():
        carry_scr[...] = jnp.zeros_like(carry_scr)

    nbr = len(DILATED_PAIRS)
    ob_scr, lse_scr = tok_scr[:nbr], tok_scr[nbr:]
    for (_, dil), o_ref, l_ref, o_s, l_s in zip(DILATED_PAIRS, (o1_ref, o2_ref, o3_ref),
                                                 (l1_ref, l2_ref, l3_ref), ob_scr, lse_scr):
        for r in range(dil):
            rows = pl.ds(r, tm // dil, stride=dil) if dil > 1 else slice(None)
            for j in range(ATTN_W // LANES):
                c1 = r * ATTN_W + j * LANES
                o_s[j, rows, :] = o_ref[:, c1:c1 + LANES].astype(F32)
            l_s[rows, :] = l_ref[:, r * LANES:(r + 1) * LANES]

    heads = []
    for h in range(GDN_HEADS):
        hs = slice(h * GDN_DV, (h + 1) * GDN_DV)
        seg = og_ref[:, hs]
        nrm = seg * lax.rsqrt(jnp.mean(seg * seg, axis=-1, keepdims=True) + EPS) * gnorm_ref[...]
        zz = z_ref[:, hs].astype(F32)
        heads.append((nrm * (zz * _sigmoid(zz))).astype(BF16))
    oa = jnp.concatenate(heads, axis=1)

    l1, l2, l3 = (l_s[...] for l_s in lse_scr)
    m = jnp.maximum(jnp.maximum(l1, l2), l3)
    e1, e2, e3 = jnp.exp(l1 - m), jnp.exp(l2 - m), jnp.exp(l3 - m)
    inv = 1.0 / (e1 + e2 + e3)
    ob = jnp.zeros((tm, ATTN_W), F32)
    for e, o_s in zip((e1, e2, e3), ob_scr):
        hi, lo = _split_bf16(e * inv)
        wexp = _dot(hi, exp_ref[...]) + _dot(lo, exp_ref[...])
        ob = ob + wexp * jnp.concatenate([o_s[j] for j in range(ATTN_W // LANES)], axis=1)
    ob = ob * lax.rsqrt(jnp.mean(ob * ob, axis=-1, keepdims=True) + EPS) * anorm_ref[...]

    mix = _dot(oa, wout_ref[0:GDN_V_W, :]) + _dot(ob.astype(BF16), wout_ref[GDN_V_W:, :])
    x1 = x_ref[...] + gta_ref[0] * mix
    x1_ref[...] = x1
    h2 = x1 * lax.rsqrt(jnp.mean(x1 * x1, axis=-1, keepdims=True) + EPS) * nffn_ref[...]
    h2 = h2 * (1.0 + scf_ref[0]) + shf_ref[0]
    h2_ref[...] = h2

    hh, hl = _split_bf16(h2)
    logits = _dot(hh, wrh_ref[...]) + _dot(hh, wrl_ref[...]) + _dot(hl, wrh_ref[...])
    tm = logits.shape[0]
    lane = lax.broadcasted_iota(jnp.int32, (tm, LANES), 1).astype(F32)
    big = float(LANES)
    gmask = lane < N_GROUPS
    glog = jnp.where(gmask, logits, NEG_INF)
    gmax = jnp.max(glog, axis=-1, keepdims=True)
    gidx = jnp.min(jnp.where(jnp.logical_and(gmask, glog == gmax), lane, big), axis=-1, keepdims=True)
    gprob = 1.0 / jnp.sum(jnp.where(gmask, jnp.exp(glog - gmax), 0.0), axis=-1, keepdims=True)
    lo_lane = N_GROUPS + EXPERTS_PER_GROUP * gidx
    emask = jnp.logical_and(lane >= lo_lane, lane < lo_lane + EXPERTS_PER_GROUP)
    elog = jnp.where(emask, logits, NEG_INF)
    m1 = jnp.max(elog, axis=-1, keepdims=True)
    i1 = jnp.min(jnp.where(jnp.logical_and(emask, elog == m1), lane, big), axis=-1, keepdims=True)
    emask2 = jnp.logical_and(emask, lane != i1)
    elog2 = jnp.where(emask2, logits, NEG_INF)
    m2 = jnp.max(elog2, axis=-1, keepdims=True)
    i2 = jnp.min(jnp.where(jnp.logical_and(emask2, elog2 == m2), lane, big), axis=-1, keepdims=True)
    r = jnp.exp(m2 - m1)
    gate1 = gprob / (1.0 + r)
    gate2 = gprob * r / (1.0 + r)
    ex1 = i1 - N_GROUPS
    ex2 = i2 - N_GROUPS

    hit1 = lane == ex1
    hit2 = lane == ex2
    onehot = jnp.where(jnp.logical_or(hit1, hit2), 1.0, 0.0)
    pref = _dot(ls_ref[...], onehot.astype(BF16)) + carry_scr[...]
    rank1 = jnp.sum(jnp.where(hit1, pref, 0.0), axis=-1, keepdims=True)
    rank2 = jnp.sum(jnp.where(hit2, pref, 0.0), axis=-1, keepdims=True)
    carry = carry_scr[...] + jnp.sum(onehot, axis=0, keepdims=True)
    carry_scr[...] = carry
    cnt_ref[...] = jnp.broadcast_to(carry, cnt_ref.shape)

    route = jnp.zeros((tm, LANES), F32)
    for idx, val in enumerate((ex1, ex2, gate1, gate2, rank1, rank2)):
        route = jnp.where(lane == idx, val, route)
    route_ref[...] = route


def _post(og, z, os_, ls_, x2, gta, scf, shf, gnorm, anorm, wout, nffn, wrh, wrl, expand, lstrict, seq):
    t, d = x2.shape
    tm = TM_POST
    tps = seq // tm
    tile = lambda w: pl.BlockSpec((tm, w), lambda i: (i, 0))
    const2 = lambda a: pl.BlockSpec(a.shape, lambda i: (0, 0))
    mod_spec = pl.BlockSpec((1, 1, d), lambda i: (i // tps, 0, 0))
    return pl.pallas_call(
        _post_kernel,
        grid=(t // tm,),
        in_specs=[tile(GDN_V_W), tile(GDN_V_W)]
                 + [pl.BlockSpec((tm // dil, dil * ATTN_W), lambda i: (i, 0)) for _, dil in DILATED_PAIRS]
                 + [pl.BlockSpec((tm // dil, dil * LANES), lambda i: (i, 0)) for _, dil in DILATED_PAIRS]
                 + [tile(d),
                  mod_spec, mod_spec, mod_spec,
                  const2(gnorm), const2(anorm), const2(wout), const2(nffn),
                  const2(wrh), const2(wrl), const2(expand), const2(lstrict)],
        out_specs=[tile(d), tile(d), tile(LANES), pl.BlockSpec((8, LANES), lambda i: (0, 0))],
        out_shape=[jax.ShapeDtypeStruct((t, d), F32),
                   jax.ShapeDtypeStruct((t, d), F32),
                   jax.ShapeDtypeStruct((t, LANES), F32),
                   jax.ShapeDtypeStruct((8, LANES), F32)],
        scratch_shapes=[pltpu.VMEM((1, LANES), F32)]
                       + [pltpu.VMEM((ATTN_W // LANES, tm, LANES), F32) for _ in DILATED_PAIRS]
                       + [pltpu.VMEM((tm, LANES), F32) for _ in DILATED_PAIRS],
        compiler_params=_params(("arbitrary",)),
        name="post",
    )(og, z, *os_, *ls_, x2, gta, scf, shf, gnorm, anorm, wout, nffn, wrh, wrl, expand, lstrict)


def _dispatch_kernel(dest_ref, h2_ref, xs_in_ref, xs_ref, sem):
    del xs_in_ref
    i = pl.program_id(0)
    tm = h2_ref.shape[0]

    def start(r, carry):
        for k in range(TOP_K):
            d = dest_ref[(i * tm + r) * TOP_K + k]
            pltpu.make_async_copy(h2_ref.at[pl.ds(r, 1)], xs_ref.at[pl.ds(d, 1)], sem).start()
        return carry

    lax.fori_loop(0, tm, start, 0, unroll=ROW_UNROLL)
    for k in range(TOP_K):
        pltpu.make_async_copy(h2_ref, xs_ref.at[pl.ds(0, tm)], sem).wait()


def _dispatch(dest, h2, xs_init):
    t, d = h2.shape
    tm = TM_ROWS
    grid_spec = pltpu.PrefetchScalarGridSpec(
        num_scalar_prefetch=1,
        grid=(t // tm,),
        in_specs=[pl.BlockSpec((tm, d), lambda i, dest: (i, 0)),
                  pl.BlockSpec(memory_space=pl.ANY)],
        out_specs=pl.BlockSpec(memory_space=pl.ANY),
        scratch_shapes=[pltpu.SemaphoreType.DMA],
    )
    return pl.pallas_call(
        _dispatch_kernel,
        grid_spec=grid_spec,
        out_shape=jax.ShapeDtypeStruct(xs_init.shape, xs_init.dtype),
        input_output_aliases={2: 0},
        compiler_params=_params(("arbitrary",)),
        name="dispatch",
    )(dest, h2, xs_init)


def _expert_kernel(blk_e_ref, nused_ref, xs_ref, wg_ref, wu_ref, wd_ref, ys_ref):
    b = pl.program_id(0)

    @pl.when(b < nused_ref[0])
    def _():
        x = xs_ref[...].astype(BF16)
        g = _dot(x, wg_ref[0])
        u = _dot(x, wu_ref[0])
        hid = (g * _sigmoid(g)) * u
        ys_ref[...] = _dot(hid.astype(BF16), wd_ref[0])

    @pl.when(b >= nused_ref[0])
    def _():
        ys_ref[...] = jnp.zeros_like(ys_ref)


def _experts(blk_e, nused, xs, wg, wu, wd):
    p, d = xs.shape
    blk = EXPERT_BLK
    row_map = lambda b, be, nu: (jnp.minimum(b, nu[0] - 1), 0)
    w_map = lambda b, be, nu: (be[b], 0, 0)
    grid_spec = pltpu.PrefetchScalarGridSpec(
        num_scalar_prefetch=2,
        grid=(p // blk,),
        in_specs=[pl.BlockSpec((blk, d), row_map),
                  pl.BlockSpec((1, d, D_EXPERT), w_map),
                  pl.BlockSpec((1, d, D_EXPERT), w_map),
                  pl.BlockSpec((1, D_EXPERT, d), w_map)],
        out_specs=pl.BlockSpec((blk, d), lambda b, be, nu: (b, 0)),
    )
    return pl.pallas_call(
        _expert_kernel,
        grid_spec=grid_spec,
        out_shape=jax.ShapeDtypeStruct((p, d), F32),
        compiler_params=_params(("arbitrary",)),
        name="experts",
    )(blk_e, nused, xs, wg, wu, wd)


def _combine_kernel(final_norm, dest_ref, ys_ref, x1_ref, route_ref, gtf_ref, nf_ref, o_ref, ybuf, sem):
    i = pl.program_id(0)
    tm = x1_ref.shape[0]
    slot = i % 2

    def gather_tile(tile, slot_):
        def start(r, carry):
            for k in range(TOP_K):
                d = dest_ref[(tile * tm + r) * TOP_K + k]
                pltpu.make_async_copy(ys_ref.at[pl.ds(d, 1)], ybuf.at[slot_, k, pl.ds(r, 1)],
                                      sem.at[slot_]).start()
            return carry
        lax.fori_loop(0, tm, start, 0, unroll=ROW_UNROLL)

    @pl.when(i == 0)
    def _():
        gather_tile(0, 0)

    @pl.when(i + 1 < pl.num_programs(0))
    def _():
        gather_tile(i + 1, 1 - slot)

    for k in range(TOP_K):
        pltpu.make_async_copy(ys_ref.at[pl.ds(0, tm)], ybuf.at[slot, k], sem.at[slot]).wait()
    route = route_ref[...]
    moe = ybuf[slot, 0] * route[:, 2:3] + ybuf[slot, 1] * route[:, 3:4]
    x2 = x1_ref[...] + gtf_ref[0] * moe
    if final_norm:
        x2 = x2 * lax.rsqrt(jnp.mean(x2 * x2, axis=-1, keepdims=True) + EPS) * nf_ref[...]
    o_ref[...] = x2


def _combine(dest, ys, x1, route, gtf, nf, seq, final_norm):
    t, d = x1.shape
    tm = TM_ROWS
    tps = seq // tm
    grid_spec = pltpu.PrefetchScalarGridSpec(
        num_scalar_prefetch=1,
        grid=(t // tm,),
        in_specs=[pl.BlockSpec(memory_space=pl.ANY),
                  pl.BlockSpec((tm, d), lambda i, dest: (i, 0)),
                  pl.BlockSpec((tm, LANES), lambda i, dest: (i, 0)),
                  pl.BlockSpec((1, 1, d), lambda i, dest: (i // tps, 0, 0)),
                  pl.BlockSpec((1, d), lambda i, dest: (0, 0))],
        out_specs=pl.BlockSpec((tm, d), lambda i, dest: (i, 0)),
        scratch_shapes=[pltpu.VMEM((2, TOP_K, tm, d), F32), pltpu.SemaphoreType.DMA((2,))],
    )
    return pl.pallas_call(
        functools.partial(_combine_kernel, final_norm),
        grid_spec=grid_spec,
        out_shape=jax.ShapeDtypeStruct((t, d), F32),
        compiler_params=_params(("arbitrary",)),
        name="combine",
    )(dest, ys, x1, route, gtf, nf)


def _block_tri(n, chunk, lower):
    r = jnp.arange(n)[:, None]
    c = jnp.arange(n)[None, :]
    same = (r // chunk) == (c // chunk)
    tri = (r >= c) if lower else (r <= c)
    return jnp.where(same & tri, 1.0, 0.0).astype(BF16)


def _layer(x2, c, bsz, seq, w_ada, b_ada, norm_mix, w_in, w_conv, a_log, dt_bias, gdn_norm,
           attn_norm, w_out, rel_bias, norm_ffn, w_rg, w_re, w_gate, w_up, w_down):
    t, d = x2.shape
    mod = _adaln(c, w_ada, b_ada)
    sh_a, sc_a, gt_a, sh_f, sc_f, gt_f = [m.reshape(bsz, 1, d) for m in jnp.split(mod, 6, axis=-1)]

    s1 = GDN_CONV_W
    s2 = s1 + GDN_V_W
    s4 = s2 + 2 * GDN_HEADS
    small = jnp.pad(w_in[:, s2:s4], ((0, 0), (0, SMALL_W - 2 * GDN_HEADS)))
    wcat = jnp.concatenate([w_in[:, :s2], small, w_in[:, s4:]], axis=1).astype(BF16)
    qkva, z, ba, *qkvb = _inproj(x2, sc_a, sh_a, norm_mix.reshape(1, d), wcat, w_conv, seq)

    pad4 = lambda v: jnp.pad(v.astype(F32), (GDN_HEADS, LANES - 2 * GDN_HEADS))
    gp = jnp.zeros((8, LANES), F32).at[0].set(pad4(a_log)).at[1].set(pad4(dt_bias))
    og = _gdn(qkva, ba, ba[:, :8].T, gp, _gp_cols(a_log, dt_bias),
              _block_tri(GDN_TOK, CHUNK, True), _block_tri(GDN_TOK, CHUNK, False), seq)

    outs, lses = [], []
    for (window, dil), qkvb_d in zip(DILATED_PAIRS, qkvb):
        o_i, lse_i = _attn_branch(qkvb_d, jnp.asarray(_bucket_table(window, dil)), rel_bias.astype(F32),
                                  bsz, seq, dil)
        outs.append(o_i)
        lses.append(lse_i)

    expand = jnp.where((jnp.arange(LANES)[:, None] == jnp.arange(ATTN_W)[None, :] // ATTN_DH), 1.0, 0.0).astype(BF16)
    lstrict = jnp.where(jnp.arange(TM_POST)[:, None] > jnp.arange(TM_POST)[None, :], 1.0, 0.0).astype(BF16)
    wr = jnp.pad(jnp.concatenate([w_rg, w_re], axis=1).astype(F32), ((0, 0), (0, LANES - N_GROUPS - N_EXPERTS)))
    wrh = wr.astype(BF16)
    wrl = (wr - wrh.astype(F32)).astype(BF16)
    x1, h2, route, cnt = _post(og, z, outs, lses, x2, gt_a, sc_f, sh_f,
                               gdn_norm.reshape(1, GDN_DV), attn_norm.reshape(1, ATTN_W),
                               w_out.astype(BF16), norm_ffn.reshape(1, d), wrh, wrl, expand, lstrict, seq)

    blk = EXPERT_BLK
    counts = cnt[0, :N_EXPERTS].astype(jnp.int32)
    padded = (counts + blk - 1) // blk * blk
    pends = jnp.cumsum(padded)
    pstarts = pends - padded
    eids = route[:, 0:TOP_K].astype(jnp.int32)
    ranks = route[:, 4:4 + TOP_K].astype(jnp.int32)
    expert_ids = jnp.arange(N_EXPERTS, dtype=jnp.int32)
    seg_start = jnp.sum(jnp.where(eids[..., None] == expert_ids, pstarts, 0), axis=-1)
    dest = (seg_start + ranks).reshape(t * TOP_K)
    a = t * TOP_K
    p = -(-a // blk) * blk + N_EXPERTS * blk
    nblk = p // blk
    blk_start = jnp.arange(nblk, dtype=jnp.int32) * blk
    blk_e = jnp.minimum(jnp.sum((pends[None, :] <= blk_start[:, None]).astype(jnp.int32), axis=1),
                        N_EXPERTS - 1)
    nused = (pends[-1] // blk).astype(jnp.int32).reshape(1)

    xs = _dispatch(dest, h2, jnp.zeros((p, d), F32))
    ys = _experts(blk_e, nused, xs, w_gate.astype(BF16), w_up.astype(BF16), w_down.astype(BF16))
    return ys, dest, x1, route, gt_f


def _gp_cols(a_log, dt_bias):
    z = jnp.zeros((8, LANES), F32)
    z = z.at[GDN_HEADS:2 * GDN_HEADS, 0].set(a_log.astype(F32))
    z = z.at[GDN_HEADS:2 * GDN_HEADS, 1].set(dt_bias.astype(F32))
    return z


def kernel(x, c, w_ada, b_ada, norm_mix, w_in, w_conv, a_log, dt_bias, gdn_norm, attn_norm, w_out,
           rel_bias, norm_ffn, w_router_group, w_router_expert, w_gate, w_up, w_down, norm_final):
    bsz, seq, d = x.shape
    depth = w_ada.shape[0]
    x2 = x.reshape(bsz * seq, d)
    for l in range(depth):
        ys, dest, x1, route, gt_f = _layer(
            x2, c, bsz, seq, w_ada[l], b_ada[l], norm_mix[l], w_in[l], w_conv[l], a_log[l], dt_bias[l],
            gdn_norm[l], attn_norm[l], w_out[l], rel_bias, norm_ffn[l], w_router_group[l],
            w_router_expert[l], w_gate[l], w_up[l], w_down[l])
        x2 = _combine(dest, ys, x1, route, gt_f, norm_final.reshape(1, d), seq, l == depth - 1)
    return x2.reshape(bsz, seq, d)
```

```python
import functools
import math

import jax
import jax.numpy as jnp
import numpy as np
from jax import lax
from jax.experimental import pallas as pl
from jax.experimental.pallas import tpu as pltpu

D_MODEL = 1024
GDN_HEADS = 4
GDN_DK = 128
GDN_DV = 128
CONV_K = 4
CHUNK = 64
ATTN_HEADS = 8
ATTN_DH = 64
DILATED_PAIRS = ((128, 1), (512, 4), (2048, 16))
QBLK = 128
NUM_BUCKETS = 32
REL_MAX_DIST = 2048
N_GROUPS = 4
EXPERTS_PER_GROUP = 8
N_EXPERTS = N_GROUPS * EXPERTS_PER_GROUP
TOP_K = 2
D_EXPERT = 256
EPS = 1e-6
NEG_INF = -1e30

GDN_QK_W = GDN_HEADS * GDN_DK
GDN_V_W = GDN_HEADS * GDN_DV
ATTN_W = ATTN_HEADS * ATTN_DH
GDN_CONV_W = 2 * GDN_QK_W + GDN_V_W
LANES = 128
SUBLANES = 8
SMALL_W = LANES
W_COLS = GDN_CONV_W + GDN_V_W + SMALL_W + 3 * ATTN_W

TM_PROJ = 512
TM_POST = 256
TM_ROWS = 256
GDN_TOK = 512
GDN_GROUP = 4
INV_BASE = 8
EXPERT_BLK = 256
ROW_UNROLL = 8
VMEM_LIMIT = 56 * 1024 * 1024

F32 = jnp.float32
BF16 = jnp.bfloat16
HIGHEST = lax.Precision.HIGHEST


def _sigmoid(x):
    return 1.0 / (1.0 + jnp.exp(-x))


def _dot(a, b, precision=None):
    return jnp.dot(a, b, preferred_element_type=F32, precision=precision)


def _dot_nt(a, b, precision=None):
    return lax.dot_general(a, b, (((1,), (1,)), ((), ())), preferred_element_type=F32,
                           precision=precision)


def _dot_tn(a, b, precision=None):
    return lax.dot_general(a, b, (((0,), (0,)), ((), ())), preferred_element_type=F32,
                           precision=precision)


def _params(sem):
    return pltpu.CompilerParams(dimension_semantics=sem, vmem_limit_bytes=VMEM_LIMIT)


def _adaln_kernel(c_ref, w_ref, b_ref, o_ref):
    o_ref[...] = _dot(c_ref[...], w_ref[...], HIGHEST) + b_ref[...]


def _adaln(c, w, b):
    bsz, d = c.shape
    n = w.shape[1]
    tn = 1024
    return pl.pallas_call(
        _adaln_kernel,
        grid=(n // tn,),
        in_specs=[pl.BlockSpec((bsz, d), lambda j: (0, 0)),
                  pl.BlockSpec((d, tn), lambda j: (0, j)),
                  pl.BlockSpec((1, tn), lambda j: (0, j))],
        out_specs=pl.BlockSpec((bsz, tn), lambda j: (0, j)),
        out_shape=jax.ShapeDtypeStruct((bsz, n), F32),
        compiler_params=_params(("arbitrary",)),
        name="adaln",
    )(c, w, b.reshape(1, n))


def _inproj_kernel(tiles_per_seq, x_ref, sc_ref, sh_ref, g_ref, w_ref, wconv_ref,
                   qkva_ref, z_ref, ba_ref, *rest):
    qkvb_refs = rest[:len(DILATED_PAIRS)]
    pa_scr, pb_scr = rest[len(DILATED_PAIRS):]
    i = pl.program_id(0)
    tm = x_ref.shape[0]
    x = x_ref[...]
    h = x * lax.rsqrt(jnp.mean(x * x, axis=-1, keepdims=True) + EPS) * g_ref[...]
    h = h * (1.0 + sc_ref[0]) + sh_ref[0]
    hb = h.astype(BF16)

    @pl.when(i % tiles_per_seq == 0)
    def _():
        pa_scr[0:8, :] = jnp.zeros((8, GDN_CONV_W), F32)

    pa = _dot(hb, w_ref[:, 0:GDN_CONV_W])
    pa_scr[8:8 + tm, :] = pa
    acc = pa * wconv_ref[CONV_K - 1:CONV_K, :]
    for j in range(CONV_K - 1):
        acc = acc + pa_scr[pl.ds(8 - (CONV_K - 1) + j, tm), :] * wconv_ref[j:j + 1, :]
    qkva_ref[...] = (acc * _sigmoid(acc)).astype(qkva_ref.dtype)
    pa_scr[0:8, :] = pa_scr[tm:tm + 8, :]

    c0 = GDN_CONV_W
    z_ref[...] = _dot(hb, w_ref[:, c0:c0 + GDN_V_W]).astype(z_ref.dtype)
    c0 += GDN_V_W
    ba_ref[...] = _dot(hb, w_ref[:, c0:c0 + SMALL_W])
    c0 += SMALL_W
    pb = _dot(hb, w_ref[:, c0:c0 + 3 * ATTN_W])
    ncol = 3 * ATTN_W // LANES
    for j in range(ncol):
        pb_scr[j] = pb[:, j * LANES:(j + 1) * LANES]
    for (_, dil), ref in zip(DILATED_PAIRS, qkvb_refs):
        if dil == 1:
            ref[...] = pb.astype(ref.dtype)
            continue
        for r in range(dil):
            for j in range(ncol):
                c1 = r * 3 * ATTN_W + j * LANES
                ref[:, c1:c1 + LANES] = pb_scr[j, pl.ds(r, tm // dil, stride=dil), :].astype(ref.dtype)


def _inproj(x2, sc, sh, g, wcat, wconv, seq):
    t, d = x2.shape
    tm = TM_PROJ
    tps = seq // tm
    mod_spec = pl.BlockSpec((1, 1, d), lambda i: (i // tps, 0, 0))
    return pl.pallas_call(
        functools.partial(_inproj_kernel, tps),
        grid=(t // tm,),
        in_specs=[pl.BlockSpec((tm, d), lambda i: (i, 0)),
                  mod_spec, mod_spec,
                  pl.BlockSpec((1, d), lambda i: (0, 0)),
                  pl.BlockSpec((d, W_COLS), lambda i: (0, 0)),
                  pl.BlockSpec((CONV_K, GDN_CONV_W), lambda i: (0, 0))],
        out_specs=[pl.BlockSpec((tm, GDN_CONV_W), lambda i: (i, 0)),
                   pl.BlockSpec((tm, GDN_V_W), lambda i: (i, 0)),
                   pl.BlockSpec((tm, SMALL_W), lambda i: (i, 0))]
                  + [pl.BlockSpec((tm // dil, dil * 3 * ATTN_W), lambda i: (i, 0)) for _, dil in DILATED_PAIRS],
        out_shape=[jax.ShapeDtypeStruct((t, GDN_CONV_W), BF16),
                   jax.ShapeDtypeStruct((t, GDN_V_W), BF16),
                   jax.ShapeDtypeStruct((t, SMALL_W), F32)]
                  + [jax.ShapeDtypeStruct((t // dil, dil * 3 * ATTN_W), BF16) for _, dil in DILATED_PAIRS],
        scratch_shapes=[pltpu.VMEM((tm + 8, GDN_CONV_W), F32),
                        pltpu.VMEM((3 * ATTN_W // LANES, tm, LANES), F32)],
        compiler_params=_params(("arbitrary",)),
        name="inproj",
    )(x2, sc, sh, g, wcat, wconv)


def _softplus(x):
    return jnp.maximum(x, 0.0) + jnp.log(1.0 + jnp.exp(-jnp.abs(x)))


def _unit_lower_inverses(n_list):
    c = n_list[0].shape[0]
    row = lax.broadcasted_iota(jnp.int32, (c, c), 0)
    col = lax.broadcasted_iota(jnp.int32, (c, c), 1)
    eye = jnp.where(row == col, 1.0, 0.0)
    same_base = (row // INV_BASE) == (col // INV_BASE)
    n0 = [jnp.where(same_base, n, 0.0) for n in n_list]
    p = [eye - x for x in n0]
    m = [x.astype(BF16) for x in n0]
    for j in range(int(math.log2(INV_BASE)) - 1):
        m = [_dot(x, x).astype(BF16) for x in m]
        p = [x + _dot(x.astype(BF16), y) for x, y in zip(p, m)]
    size = INV_BASE
    while size < c:
        sibling = jnp.logical_and((row // size) % 2 == 1, (col // size) == (row // size) - 1)
        cb = [jnp.where(sibling, n, 0.0).astype(BF16) for n in n_list]
        pb = [x.astype(BF16) for x in p]
        pc = [_dot(x, y).astype(BF16) for x, y in zip(pb, cb)]
        p = [x - _dot(y, z) for x, y, z in zip(p, pc, pb)]
        size *= 2
    return p


def _split3_bf16(x):
    h1 = x.astype(BF16)
    r1 = x - h1.astype(F32)
    h2 = r1.astype(BF16)
    h3 = (r1 - h2.astype(F32)).astype(BF16)
    return h1, h2, h3


def _gdn_kernel(steps_per_seq, qkv_ref, ba_ref, bat_ref, gp_ref, gpt_ref, lblk_ref, ublk_ref,
                o_ref, state_scr):
    i = pl.program_id(0)

    @pl.when(i % steps_per_seq == 0)
    def _():
        state_scr[...] = jnp.zeros_like(state_scr)

    nchunk = GDN_TOK // CHUNK
    ba = ba_ref[...]
    bat = bat_ref[...]
    a_vec = -jnp.exp(gp_ref[0:1, :])
    g_tile = a_vec * _softplus(ba + gp_ref[1:2, :])
    lblk = lblk_ref[...]
    g_cum = sum(_dot(lblk, part) for part in _split3_bf16(g_tile))
    a_col = -jnp.exp(gpt_ref[:, 0:1])
    gt_tile = a_col * _softplus(bat + gpt_ref[:, 1:2])
    ublk = ublk_ref[...]
    gt_cum = sum(_dot(part, ublk) for part in _split3_bf16(gt_tile))
    beta_tile = _sigmoid(ba)

    row = lax.broadcasted_iota(jnp.int32, (CHUNK, CHUNK), 0)
    col = lax.broadcasted_iota(jnp.int32, (CHUNK, CHUNK), 1)
    incl = row >= col
    strict = row > col

    for h0 in range(0, GDN_HEADS, GDN_GROUP):
        heads = range(h0, h0 + GDN_GROUP)
        kb_l, k_l, q_l, rhs_l, qg_l, kg_l, gl_l, inc_l, str_l = ([] for _ in range(9))
        for h in heads:
            q_raw = qkv_ref[:, h * GDN_DK:(h + 1) * GDN_DK].astype(F32)
            k_raw = qkv_ref[:, GDN_QK_W + h * GDN_DK:GDN_QK_W + (h + 1) * GDN_DK].astype(F32)
            v = qkv_ref[:, 2 * GDN_QK_W + h * GDN_DV:2 * GDN_QK_W + (h + 1) * GDN_DV].astype(F32)
            qn = q_raw * lax.rsqrt(jnp.sum(q_raw * q_raw, axis=-1, keepdims=True) + EPS) * (GDN_DK ** -0.5)
            kn = k_raw * lax.rsqrt(jnp.sum(k_raw * k_raw, axis=-1, keepdims=True) + EPS)
            beta = beta_tile[:, h:h + 1]
            gc_all = g_cum[:, GDN_HEADS + h:GDN_HEADS + h + 1]
            eg = jnp.exp(gc_all)
            kb = kn * beta
            rhs = jnp.concatenate([v * beta, kb * eg], axis=1).astype(BF16)
            qg = qn * eg
            kb16, k16, q16 = kb.astype(BF16), kn.astype(BF16), qn.astype(BF16)
            for c in range(nchunk):
                sl = slice(c * CHUNK, (c + 1) * CHUNK)
                g_c = gc_all[sl]
                g_r = gt_cum[GDN_HEADS + h:GDN_HEADS + h + 1, sl]
                dec = jnp.exp(g_c - g_r)
                inc_l.append(jnp.where(incl, dec, 0.0))
                str_l.append(jnp.where(strict, dec, 0.0))
                g_last = g_c[CHUNK - 1:CHUNK, :]
                kg_l.append((kn[sl] * jnp.exp(g_last - g_c)).astype(BF16))
                gl_l.append(jnp.exp(g_last))
                kb_l.append(kb16[sl])
                k_l.append(k16[sl])
                q_l.append(q16[sl])
                rhs_l.append(rhs[sl])
                qg_l.append(qg[sl])
        nprob = len(k_l)
        n_l = [_dot_nt(kb_l[j], k_l[j]) * str_l[j] for j in range(nprob)]
        aqk_l = [(_dot_nt(q_l[j], k_l[j]) * inc_l[j]).astype(BF16) for j in range(nprob)]
        tinv_l = _unit_lower_inverses(n_l)
        sol_l = [_dot(tinv_l[j].astype(BF16), rhs_l[j]).astype(BF16) for j in range(nprob)]
        a2_l = [_dot(aqk_l[j], sol_l[j]) for j in range(nprob)]
        k2_l = [_dot_tn(kg_l[j], sol_l[j]) for j in range(nprob)]
        states = [state_scr[h] for h in heads]
        for c in range(nchunk):
            sl = slice(c * CHUNK, (c + 1) * CHUNK)
            for hi, h in enumerate(heads):
                j = hi * nchunk + c
                sb = states[hi].astype(BF16)
                qeff = (qg_l[j] - a2_l[j][:, GDN_DV:]).astype(BF16)
                o = _dot(qeff, sb) + a2_l[j][:, :GDN_DV]
                states[hi] = gl_l[j] * states[hi] + k2_l[j][:, :GDN_DV] - _dot(k2_l[j][:, GDN_DV:].astype(BF16), sb)
                o_ref[sl, h * GDN_DV:(h + 1) * GDN_DV] = o.astype(o_ref.dtype)
        for hi, h in enumerate(heads):
            state_scr[h] = states[hi]


def _gdn(qkva, ba, bat, gp, gpt, lblk, ublk, seq):
    t = qkva.shape[0]
    tok = GDN_TOK
    sps = seq // tok
    const = lambda i: (0, 0)
    return pl.pallas_call(
        functools.partial(_gdn_kernel, sps),
        grid=(t // tok,),
        in_specs=[pl.BlockSpec((tok, GDN_CONV_W), lambda i: (i, 0)),
                  pl.BlockSpec((tok, SMALL_W), lambda i: (i, 0)),
                  pl.BlockSpec((8, tok), lambda i: (0, i)),
                  pl.BlockSpec((8, LANES), const),
                  pl.BlockSpec((8, LANES), const),
                  pl.BlockSpec((tok, tok), const),
                  pl.BlockSpec((tok, tok), const)],
        out_specs=pl.BlockSpec((tok, GDN_V_W), lambda i: (i, 0)),
        out_shape=jax.ShapeDtypeStruct((t, GDN_V_W), F32),
        scratch_shapes=[pltpu.VMEM((GDN_HEADS, GDN_DK, GDN_DV), F32)],
        compiler_params=_params(("arbitrary",)),
        name="gdn",
    )(qkva, ba, bat, gp, gpt, lblk, ublk)


def _attn_kernel(q_ref, kp_ref, kc_ref, vp_ref, vc_ref, bucket_ref, rb_ref, o_ref, lse_ref, bias_ref):
    n = pl.program_id(2)

    @pl.when(jnp.logical_and(jnp.logical_and(pl.program_id(0) == 0, pl.program_id(1) == 0), n == 0))
    def _():
        bucket = bucket_ref[...]
        for h in range(ATTN_HEADS):
            acc = jnp.full((QBLK, 2 * QBLK), NEG_INF, F32)
            for b in range(NUM_BUCKETS):
                acc = jnp.where(bucket == b, rb_ref[b, h], acc)
            bias_ref[h] = acc

    q = q_ref[0]
    k = jnp.concatenate([kp_ref[0], kc_ref[0]], axis=0)
    v = jnp.concatenate([vp_ref[0], vc_ref[0]], axis=0)
    col = lax.broadcasted_iota(jnp.int32, (QBLK, 2 * QBLK), 1)
    lane = lax.broadcasted_iota(jnp.int32, (QBLK, LANES), 1)
    no_prev = jnp.logical_and(n == 0, col < QBLK)
    outs = []
    lse_tile = jnp.zeros((QBLK, LANES), F32)
    per_group = LANES // ATTN_DH
    scale = ATTN_DH ** -0.5
    mine = [jnp.logical_and(lane >= j * ATTN_DH, lane < (j + 1) * ATTN_DH) for j in range(per_group)]
    s_l = []
    for h in range(ATTN_HEADS):
        grp, j = divmod(h, per_group)
        gs = slice(grp * LANES, (grp + 1) * LANES)
        qg = q[:, gs] * scale
        s = _dot_nt(jnp.where(mine[j], qg, jnp.zeros_like(qg)), k[:, gs]) + bias_ref[h]
        s_l.append(jnp.where(no_prev, NEG_INF, s))
    m_l = [jnp.max(s, axis=-1, keepdims=True) for s in s_l]
    p_l = [jnp.exp(s - m) for s, m in zip(s_l, m_l)]
    den_l = [jnp.sum(p, axis=-1, keepdims=True) for p in p_l]
    pv_l = [_dot(p_l[h].astype(BF16), v[:, (h // per_group) * LANES:(h // per_group + 1) * LANES])
            for h in range(ATTN_HEADS)]
    for grp in range(ATTN_HEADS // per_group):
        o_grp = jnp.zeros((QBLK, LANES), F32)
        for j in range(per_group):
            h = grp * per_group + j
            o_grp = jnp.where(mine[j], pv_l[h] / den_l[h], o_grp)
            lse_tile = jnp.where(lane == h, m_l[h] + jnp.log(den_l[h]), lse_tile)
        outs.append(o_grp)
    o_ref[0] = jnp.concatenate(outs, axis=1).astype(o_ref.dtype)
    lse_ref[0] = lse_tile


def _attn_branch(qkvb, bucket, rel_bias, bsz, seq, dil):
    ln = seq // dil
    nb = ln // QBLK
    xv = qkvb.reshape(bsz, ln, dil * 3 * ATTN_W)
    w = ATTN_W
    cur = lambda part: (lambda b, r, n: (b, n, r * 3 + part))
    prev = lambda part: (lambda b, r, n: (b, jnp.maximum(n - 1, 0), r * 3 + part))
    o, lse = pl.pallas_call(
        _attn_kernel,
        grid=(bsz, dil, nb),
        in_specs=[pl.BlockSpec((1, QBLK, w), cur(0)),
                  pl.BlockSpec((1, QBLK, w), prev(1)),
                  pl.BlockSpec((1, QBLK, w), cur(1)),
                  pl.BlockSpec((1, QBLK, w), prev(2)),
                  pl.BlockSpec((1, QBLK, w), cur(2)),
                  pl.BlockSpec((QBLK, 2 * QBLK), lambda b, r, n: (0, 0)),
                  pl.BlockSpec(memory_space=pltpu.SMEM)],
        out_specs=[pl.BlockSpec((1, QBLK, w), lambda b, r, n: (b, n, r)),
                   pl.BlockSpec((1, QBLK, LANES), lambda b, r, n: (b, n, r))],
        out_shape=[jax.ShapeDtypeStruct((bsz, ln, dil * w), BF16),
                   jax.ShapeDtypeStruct((bsz, ln, dil * LANES), F32)],
        scratch_shapes=[pltpu.VMEM((ATTN_HEADS, QBLK, 2 * QBLK), F32)],
        compiler_params=_params(("arbitrary", "arbitrary", "arbitrary")),
        name=f"attn_d{dil}",
    )(xv, xv, xv, xv, xv, bucket, rel_bias)
    return o.reshape(bsz * ln, dil * w), lse.reshape(bsz * ln, dil * LANES)


def _bucket_table(window, dil):
    steps = window // dil
    qi = np.arange(QBLK)[:, None]
    kj = np.arange(2 * QBLK)[None, :]
    delta = qi + QBLK - kj
    dist = np.maximum(delta, 0) * dil
    max_exact = NUM_BUCKETS // 2
    d_f = np.maximum(dist, 1).astype(np.float32)
    large = max_exact + (np.log(d_f / max_exact) / math.log(REL_MAX_DIST / max_exact)
                         * (NUM_BUCKETS - max_exact)).astype(np.int32)
    bucket = np.where(dist < max_exact, dist, np.minimum(large, NUM_BUCKETS - 1))
    return np.where((delta >= 0) & (delta <= steps), bucket, -1).astype(np.int32)


def _split_bf16(x):
    hi = x.astype(BF16)
    lo = (x - hi.astype(F32)).astype(BF16)
    return hi, lo


def _post_kernel(og_ref, z_ref, o1_ref, o2_ref, o3_ref, l1_ref, l2_ref, l3_ref, x_ref,
                 gta_ref, scf_ref, shf_ref, gnorm_ref, anorm_ref, wout_ref, nffn_ref,
                 wrh_ref, wrl_ref, exp_ref, ls_ref,
                 x1_ref, h2_ref, route_ref, cnt_ref, carry_scr, *tok_scr):
    i = pl.program_id(0)
    tm = x_ref.shape[0]

    @pl.when(i == 0)
    def _():
        carry_scr[...] = jnp.zeros_like(carry_scr)

    nbr = len(DILATED_PAIRS)
    ob_scr, lse_scr = tok_scr[:nbr], tok_scr[nbr:]
    for (_, dil), o_ref, l_ref, o_s, l_s in zip(DILATED_PAIRS, (o1_ref, o2_ref, o3_ref),
                                                 (l1_ref, l2_ref, l3_ref), ob_scr, lse_scr):
        for r in range(dil):
            rows = pl.ds(r, tm // dil, stride=dil) if dil > 1 else slice(None)
            for j in range(ATTN_W // LANES):
                c1 = r * ATTN_W + j * LANES
                o_s[j, rows, :] = o_ref[:, c1:c1 + LANES].astype(F32)
            l_s[rows, :] = l_ref[:, r * LANES:(r + 1) * LANES]

    heads = []
    for h in range(GDN_HEADS):
        hs = slice(h * GDN_DV, (h + 1) * GDN_DV)
        seg = og_ref[:, hs]
        nrm = seg * lax.rsqrt(jnp.mean(seg * seg, axis=-1, keepdims=True) + EPS) * gnorm_ref[...]
        zz = z_ref[:, hs].astype(F32)
        heads.append((nrm * (zz * _sigmoid(zz))).astype(BF16))
    oa = jnp.concatenate(heads, axis=1)

    l1, l2, l3 = (l_s[...] for l_s in lse_scr)
    m = jnp.maximum(jnp.maximum(l1, l2), l3)
    e1, e2, e3 = jnp.exp(l1 - m), jnp.exp(l2 - m), jnp.exp(l3 - m)
    inv = 1.0 / (e1 + e2 + e3)
    ob = jnp.zeros((tm, ATTN_W), F32)
    for e, o_s in zip((e1, e2, e3), ob_scr):
        hi, lo = _split_bf16(e * inv)
        wexp = _dot(hi, exp_ref[...]) + _dot(lo, exp_ref[...])
        ob = ob + wexp * jnp.concatenate([o_s[j] for j in range(ATTN_W // LANES)], axis=1)
    ob = ob * lax.rsqrt(jnp.mean(ob * ob, axis=-1, keepdims=True) + EPS) * anorm_ref[...]

    mix = _dot(oa, wout_ref[0:GDN_V_W, :]) + _dot(ob.astype(BF16), wout_ref[GDN_V_W:, :])
    x1 = x_ref[...] + gta_ref[0] * mix
    x1_ref[...] = x1
    h2 = x1 * lax.rsqrt(jnp.mean(x1 * x1, axis=-1, keepdims=True) + EPS) * nffn_ref[...]
    h2 = h2 * (1.0 + scf_ref[0]) + shf_ref[0]
    h2_ref[...] = h2

    hh, hl = _split_bf16(h2)
    logits = _dot(hh, wrh_ref[...]) + _dot(hh, wrl_ref[...]) + _dot(hl, wrh_ref[...])
    tm = logits.shape[0]
    lane = lax.broadcasted_iota(jnp.int32, (tm, LANES), 1).astype(F32)
    big = float(LANES)
    gmask = lane < N_GROUPS
    glog = jnp.where(gmask, logits, NEG_INF)
    gmax = jnp.max(glog, axis=-1, keepdims=True)
    gidx = jnp.min(jnp.where(jnp.logical_and(gmask, glog == gmax), lane, big), axis=-1, keepdims=True)
    gprob = 1.0 / jnp.sum(jnp.where(gmask, jnp.exp(glog - gmax), 0.0), axis=-1, keepdims=True)
    lo_lane = N_GROUPS + EXPERTS_PER_GROUP * gidx
    emask = jnp.logical_and(lane >= lo_lane, lane < lo_lane + EXPERTS_PER_GROUP)
    elog = jnp.where(emask, logits, NEG_INF)
    m1 = jnp.max(elog, axis=-1, keepdims=True)
    i1 = jnp.min(jnp.where(jnp.logical_and(emask, elog == m1), lane, big), axis=-1, keepdims=True)
    emask2 = jnp.logical_and(emask, lane != i1)
    elog2 = jnp.where(emask2, logits, NEG_INF)
    m2 = jnp.max(elog2, axis=-1, keepdims=True)
    i2 = jnp.min(jnp.where(jnp.logical_and(emask2, elog2 == m2), lane, big), axis=-1, keepdims=True)
    r = jnp.exp(m2 - m1)
    gate1 = gprob / (1.0 + r)
    gate2 = gprob * r / (1.0 + r)
    ex1 = i1 - N_GROUPS
    ex2 = i2 - N_GROUPS

    hit1 = lane == ex1
    hit2 = lane == ex2
    onehot = jnp.where(jnp.logical_or(hit1, hit2), 1.0, 0.0)
    pref = _dot(ls_ref[...], onehot.astype(BF16)) + carry_scr[...]
    rank1 = jnp.sum(jnp.where(hit1, pref, 0.0), axis=-1, keepdims=True)
    rank2 = jnp.sum(jnp.where(hit2, pref, 0.0), axis=-1, keepdims=True)
    carry = carry_scr[...] + jnp.sum(onehot, axis=0, keepdims=True)
    carry_scr[...] = carry
    cnt_ref[...] = jnp.broadcast_to(carry, cnt_ref.shape)

    route = jnp.zeros((tm, LANES), F32)
    for idx, val in enumerate((ex1, ex2, gate1, gate2, rank1, rank2)):
        route = jnp.where(lane == idx, val, route)
    route_ref[...] = route


def _post(og, z, os_, ls_, x2, gta, scf, shf, gnorm, anorm, wout, nffn, wrh, wrl, expand, lstrict, seq):
    t, d = x2.shape
    tm = TM_POST
    tps = seq // tm
    tile = lambda w: pl.BlockSpec((tm, w), lambda i: (i, 0))
    const2 = lambda a: pl.BlockSpec(a.shape, lambda i: (0, 0))
    mod_spec = pl.BlockSpec((1, 1, d), lambda i: (i // tps, 0, 0))
    return pl.pallas_call(
        _post_kernel,
        grid=(t // tm,),
        in_specs=[tile(GDN_V_W), tile(GDN_V_W)]
                 + [pl.BlockSpec((tm // dil, dil * ATTN_W), lambda i: (i, 0)) for _, dil in DILATED_PAIRS]
                 + [pl.BlockSpec((tm // dil, dil * LANES), lambda i: (i, 0)) for _, dil in DILATED_PAIRS]
                 + [tile(d),
                  mod_spec, mod_spec, mod_spec,
                  const2(gnorm), const2(anorm), const2(wout), const2(nffn),
                  const2(wrh), const2(wrl), const2(expand), const2(lstrict)],
        out_specs=[tile(d), tile(d), tile(LANES), pl.BlockSpec((8, LANES), lambda i: (0, 0))],
        out_shape=[jax.ShapeDtypeStruct((t, d), F32),
                   jax.ShapeDtypeStruct((t, d), F32),
                   jax.ShapeDtypeStruct((t, LANES), F32),
                   jax.ShapeDtypeStruct((8, LANES), F32)],
        scratch_shapes=[pltpu.VMEM((1, LANES), F32)]
                       + [pltpu.VMEM((ATTN_W // LANES, tm, LANES), F32) for _ in DILATED_PAIRS]
                       + [pltpu.VMEM((tm, LANES), F32) for _ in DILATED_PAIRS],
        compiler_params=_params(("arbitrary",)),
        name="post",
    )(og, z, *os_, *ls_, x2, gta, scf, shf, gnorm, anorm, wout, nffn, wrh, wrl, expand, lstrict)


def _dispatch_kernel(dest_ref, fill_ref, pad_ref, h2_ref, xs_ref, zero_scr, sem, fill_sem):
    i = pl.program_id(0)
    tm = h2_ref.shape[0]

    @pl.when(i == 0)
    def _():
        zero_scr[...] = jnp.zeros_like(zero_scr)

        def pieces(e, act):
            base, pad = fill_ref[e], pad_ref[e]
            head = pad & (SUBLANES - 1)
            for j in range(SUBLANES - 1):
                @pl.when(j < head)
                def _():
                    act(pltpu.make_async_copy(zero_scr.at[pl.ds(0, 1)], xs_ref.at[pl.ds(base + j, 1)],
                                              fill_sem))
            off = base + head
            for bit in range(SUBLANES.bit_length() - 1, EXPERT_BLK.bit_length() - 1):
                size = 1 << bit

                @pl.when((pad >> bit) & 1 == 1)
                def _():
                    act(pltpu.make_async_copy(zero_scr.at[pl.ds(0, size)],
                                              xs_ref.at[pl.ds(pl.multiple_of(off, SUBLANES), size)], fill_sem))
                off = off + (pad & size)

        def fill(e, carry):
            pieces(e, lambda cp: cp.start())
            return carry

        def drain(e, carry):
            pieces(e, lambda cp: cp.wait())
            return carry

        lax.fori_loop(0, N_EXPERTS, fill, 0)
        lax.fori_loop(0, N_EXPERTS, drain, 0)

    def start(r, carry):
        for k in range(TOP_K):
            d = dest_ref[(i * tm + r) * TOP_K + k]
            pltpu.make_async_copy(h2_ref.at[pl.ds(r, 1)], xs_ref.at[pl.ds(d, 1)], sem).start()
        return carry

    lax.fori_loop(0, tm, start, 0, unroll=ROW_UNROLL)
    for k in range(TOP_K):
        pltpu.make_async_copy(h2_ref, xs_ref.at[pl.ds(0, tm)], sem).wait()


def _dispatch(dest, fill_start, pad_rows, h2, p):
    t, d = h2.shape
    tm = TM_ROWS
    grid_spec = pltpu.PrefetchScalarGridSpec(
        num_scalar_prefetch=3,
        grid=(t // tm,),
        in_specs=[pl.BlockSpec((tm, d), lambda i, *_: (i, 0))],
        out_specs=pl.BlockSpec(memory_space=pl.ANY),
        scratch_shapes=[pltpu.VMEM((EXPERT_BLK // 2, d), F32), pltpu.SemaphoreType.DMA,
                        pltpu.SemaphoreType.DMA],
    )
    return pl.pallas_call(
        _dispatch_kernel,
        grid_spec=grid_spec,
        out_shape=jax.ShapeDtypeStruct((p, d), F32),
        compiler_params=_params(("arbitrary",)),
        name="dispatch",
    )(dest, fill_start, pad_rows, h2)


def _expert_kernel(blk_e_ref, nused_ref, xs_ref, wg_ref, wu_ref, wd_ref, ys_ref):
    b = pl.program_id(0)

    @pl.when(b < nused_ref[0])
    def _():
        x = xs_ref[...].astype(BF16)
        g = _dot(x, wg_ref[0])
        u = _dot(x, wu_ref[0])
        hid = (g * _sigmoid(g)) * u
        ys_ref[...] = _dot(hid.astype(BF16), wd_ref[0])

    @pl.when(b >= nused_ref[0])
    def _():
        ys_ref[...] = jnp.zeros_like(ys_ref)


def _experts(blk_e, nused, xs, wg, wu, wd):
    p, d = xs.shape
    blk = EXPERT_BLK
    row_map = lambda b, be, nu: (jnp.minimum(b, nu[0] - 1), 0)
    w_map = lambda b, be, nu: (be[b], 0, 0)
    grid_spec = pltpu.PrefetchScalarGridSpec(
        num_scalar_prefetch=2,
        grid=(p // blk,),
        in_specs=[pl.BlockSpec((blk, d), row_map),
                  pl.BlockSpec((1, d, D_EXPERT), w_map),
                  pl.BlockSpec((1, d, D_EXPERT), w_map),
                  pl.BlockSpec((1, D_EXPERT, d), w_map)],
        out_specs=pl.BlockSpec((blk, d), lambda b, be, nu: (b, 0)),
    )
    return pl.pallas_call(
        _expert_kernel,
        grid_spec=grid_spec,
        out_shape=jax.ShapeDtypeStruct((p, d), F32),
        compiler_params=_params(("arbitrary",)),
        name="experts",
    )(blk_e, nused, xs, wg, wu, wd)


def _combine_kernel(final_norm, dest_ref, ys_ref, x1_ref, route_ref, gtf_ref, nf_ref, o_ref, ybuf, sem):
    i = pl.program_id(0)
    tm = x1_ref.shape[0]
    slot = i % 2

    def gather_tile(tile, slot_):
        def start(r, carry):
            for k in range(TOP_K):
                d = dest_ref[(tile * tm + r) * TOP_K + k]
                pltpu.make_async_copy(ys_ref.at[pl.ds(d, 1)], ybuf.at[slot_, k, pl.ds(r, 1)],
                                      sem.at[slot_]).start()
            return carry
        lax.fori_loop(0, tm, start, 0, unroll=ROW_UNROLL)

    @pl.when(i == 0)
    def _():
        gather_tile(0, 0)

    @pl.when(i + 1 < pl.num_programs(0))
    def _():
        gather_tile(i + 1, 1 - slot)

    for k in range(TOP_K):
        pltpu.make_async_copy(ys_ref.at[pl.ds(0, tm)], ybuf.at[slot, k], sem.at[slot]).wait()
    route = route_ref[...]
    moe = ybuf[slot, 0] * route[:, 2:3] + ybuf[slot, 1] * route[:, 3:4]
    x2 = x1_ref[...] + gtf_ref[0] * moe
    if final_norm:
        x2 = x2 * lax.rsqrt(jnp.mean(x2 * x2, axis=-1, keepdims=True) + EPS) * nf_ref[...]
    o_ref[...] = x2


def _combine(dest, ys, x1, route, gtf, nf, seq, final_norm):
    t, d = x1.shape
    tm = TM_ROWS
    tps = seq // tm
    grid_spec = pltpu.PrefetchScalarGridSpec(
        num_scalar_prefetch=1,
        grid=(t // tm,),
        in_specs=[pl.BlockSpec(memory_space=pl.ANY),
                  pl.BlockSpec((tm, d), lambda i, dest: (i, 0)),
                  pl.BlockSpec((tm, LANES), lambda i, dest: (i, 0)),
                  pl.BlockSpec((1, 1, d), lambda i, dest: (i // tps, 0, 0)),
                  pl.BlockSpec((1, d), lambda i, dest: (0, 0))],
        out_specs=pl.BlockSpec((tm, d), lambda i, dest: (i, 0)),
        scratch_shapes=[pltpu.VMEM((2, TOP_K, tm, d), F32), pltpu.SemaphoreType.DMA((2,))],
    )
    return pl.pallas_call(
        functools.partial(_combine_kernel, final_norm),
        grid_spec=grid_spec,
        out_shape=jax.ShapeDtypeStruct((t, d), F32),
        compiler_params=_params(("arbitrary",)),
        name="combine",
    )(dest, ys, x1, route, gtf, nf)


def _block_tri(n, chunk, lower):
    r = jnp.arange(n)[:, None]
    c = jnp.arange(n)[None, :]
    same = (r // chunk) == (c // chunk)
    tri = (r >= c) if lower else (r <= c)
    return jnp.where(same & tri, 1.0, 0.0).astype(BF16)


def _layer(x2, c, bsz, seq, w_ada, b_ada, norm_mix, w_in, w_conv, a_log, dt_bias, gdn_norm,
           attn_norm, w_out, rel_bias, norm_ffn, w_rg, w_re, w_gate, w_up, w_down):
    t, d = x2.shape
    mod = _adaln(c, w_ada, b_ada)
    sh_a, sc_a, gt_a, sh_f, sc_f, gt_f = [m.reshape(bsz, 1, d) for m in jnp.split(mod, 6, axis=-1)]

    s1 = GDN_CONV_W
    s2 = s1 + GDN_V_W
    s4 = s2 + 2 * GDN_HEADS
    small = jnp.pad(w_in[:, s2:s4], ((0, 0), (0, SMALL_W - 2 * GDN_HEADS)))
    wcat = jnp.concatenate([w_in[:, :s2], small, w_in[:, s4:]], axis=1).astype(BF16)
    qkva, z, ba, *qkvb = _inproj(x2, sc_a, sh_a, norm_mix.reshape(1, d), wcat, w_conv, seq)

    pad4 = lambda v: jnp.pad(v.astype(F32), (GDN_HEADS, LANES - 2 * GDN_HEADS))
    gp = jnp.zeros((8, LANES), F32).at[0].set(pad4(a_log)).at[1].set(pad4(dt_bias))
    og = _gdn(qkva, ba, ba[:, :8].T, gp, _gp_cols(a_log, dt_bias),
              _block_tri(GDN_TOK, CHUNK, True), _block_tri(GDN_TOK, CHUNK, False), seq)

    outs, lses = [], []
    for (window, dil), qkvb_d in zip(DILATED_PAIRS, qkvb):
        o_i, lse_i = _attn_branch(qkvb_d, jnp.asarray(_bucket_table(window, dil)), rel_bias.astype(F32),
                                  bsz, seq, dil)
        outs.append(o_i)
        lses.append(lse_i)

    expand = jnp.where((jnp.arange(LANES)[:, None] == jnp.arange(ATTN_W)[None, :] // ATTN_DH), 1.0, 0.0).astype(BF16)
    lstrict = jnp.where(jnp.arange(TM_POST)[:, None] > jnp.arange(TM_POST)[None, :], 1.0, 0.0).astype(BF16)
    wr = jnp.pad(jnp.concatenate([w_rg, w_re], axis=1).astype(F32), ((0, 0), (0, LANES - N_GROUPS - N_EXPERTS)))
    wrh = wr.astype(BF16)
    wrl = (wr - wrh.astype(F32)).astype(BF16)
    x1, h2, route, cnt = _post(og, z, outs, lses, x2, gt_a, sc_f, sh_f,
                               gdn_norm.reshape(1, GDN_DV), attn_norm.reshape(1, ATTN_W),
                               w_out.astype(BF16), norm_ffn.reshape(1, d), wrh, wrl, expand, lstrict, seq)

    blk = EXPERT_BLK
    counts = cnt[0, :N_EXPERTS].astype(jnp.int32)
    padded = (counts + blk - 1) // blk * blk
    pends = jnp.cumsum(padded)
    pstarts = pends - padded
    eids = route[:, 0:TOP_K].astype(jnp.int32)
    ranks = route[:, 4:4 + TOP_K].astype(jnp.int32)
    expert_ids = jnp.arange(N_EXPERTS, dtype=jnp.int32)
    seg_start = jnp.sum(jnp.where(eids[..., None] == expert_ids, pstarts, 0), axis=-1)
    dest = (seg_start + ranks).reshape(t * TOP_K)
    a = t * TOP_K
    p = -(-a // blk) * blk + N_EXPERTS * blk
    nblk = p // blk
    blk_start = jnp.arange(nblk, dtype=jnp.int32) * blk
    blk_e = jnp.minimum(jnp.sum((pends[None, :] <= blk_start[:, None]).astype(jnp.int32), axis=1),
                        N_EXPERTS - 1)
    nused = (pends[-1] // blk).astype(jnp.int32).reshape(1)

    xs = _dispatch(dest, pstarts + counts, padded - counts, h2, p)
    ys = _experts(blk_e, nused, xs, w_gate.astype(BF16), w_up.astype(BF16), w_down.astype(BF16))
    return ys, dest, x1, route, gt_f


def _gp_cols(a_log, dt_bias):
    z = jnp.zeros((8, LANES), F32)
    z = z.at[GDN_HEADS:2 * GDN_HEADS, 0].set(a_log.astype(F32))
    z = z.at[GDN_HEADS:2 * GDN_HEADS, 1].set(dt_bias.astype(F32))
    return z


def kernel(x, c, w_ada, b_ada, norm_mix, w_in, w_conv, a_log, dt_bias, gdn_norm, attn_norm, w_out,
           rel_bias, norm_ffn, w_router_group, w_router_expert, w_gate, w_up, w_down, norm_final):
    bsz, seq, d = x.shape
    depth = w_ada.shape[0]
    x2 = x.reshape(bsz * seq, d)
    for l in range(depth):
        ys, dest, x1, route, gt_f = _layer(
            x2, c, bsz, seq, w_ada[l], b_ada[l], norm_mix[l], w_in[l], w_conv[l], a_log[l], dt_bias[l],
            gdn_norm[l], attn_norm[l], w_out[l], rel_bias, norm_ffn[l], w_router_group[l],
            w_router_expert[l], w_gate[l], w_up[l], w_down[l])
        x2 = _combine(dest, ys, x1, route, gt_f, norm_final.reshape(1, d), seq, l == depth - 1)
    return x2.reshape(bsz, seq, d)
```

```python
import functools
import math

import jax
import jax.numpy as jnp
import numpy as np
from jax import lax
from jax.experimental import pallas as pl
from jax.experimental.pallas import tpu as pltpu

D_MODEL = 1024
GDN_HEADS = 4
GDN_DK = 128
GDN_DV = 128
CONV_K = 4
CHUNK = 64
ATTN_HEADS = 8
ATTN_DH = 64
DILATED_PAIRS = ((128, 1), (512, 4), (2048, 16))
QBLK = 128
NUM_BUCKETS = 32
REL_MAX_DIST = 2048
N_GROUPS = 4
EXPERTS_PER_GROUP = 8
N_EXPERTS = N_GROUPS * EXPERTS_PER_GROUP
TOP_K = 2
D_EXPERT = 256
EPS = 1e-6
NEG_INF = -1e30

GDN_QK_W = GDN_HEADS * GDN_DK
GDN_V_W = GDN_HEADS * GDN_DV
ATTN_W = ATTN_HEADS * ATTN_DH
GDN_CONV_W = 2 * GDN_QK_W + GDN_V_W
LANES = 128
SUBLANES = 8
SMALL_W = LANES
W_COLS = GDN_CONV_W + GDN_V_W + SMALL_W + 3 * ATTN_W

TM_PROJ = 512
CONV_BLK = 128
HEAD_ROWS = 16
TM_POST = 256
TM_ROWS = 256
GDN_TOK = 512
GDN_GROUP = 4
INV_BASE = 8
EXPERT_BLK = 256
ROW_UNROLL = 8
VMEM_LIMIT = 56 * 1024 * 1024

F32 = jnp.float32
BF16 = jnp.bfloat16
HIGHEST = lax.Precision.HIGHEST


def _sigmoid(x):
    return 1.0 / (1.0 + jnp.exp(-x))


def _dot(a, b, precision=None):
    return jnp.dot(a, b, preferred_element_type=F32, precision=precision)


def _dot_nt(a, b, precision=None):
    return lax.dot_general(a, b, (((1,), (1,)), ((), ())), preferred_element_type=F32,
                           precision=precision)


def _dot_tn(a, b, precision=None):
    return lax.dot_general(a, b, (((0,), (0,)), ((), ())), preferred_element_type=F32,
                           precision=precision)


def _params(sem):
    return pltpu.CompilerParams(dimension_semantics=sem, vmem_limit_bytes=VMEM_LIMIT)


def _adaln_kernel(c_ref, w_ref, b_ref, o_ref):
    o_ref[...] = _dot(c_ref[...], w_ref[...], HIGHEST) + b_ref[...]


def _adaln(c, w, b):
    bsz, d = c.shape
    n = w.shape[1]
    tn = 1024
    return pl.pallas_call(
        _adaln_kernel,
        grid=(n // tn,),
        in_specs=[pl.BlockSpec((bsz, d), lambda j: (0, 0)),
                  pl.BlockSpec((d, tn), lambda j: (0, j)),
                  pl.BlockSpec((1, tn), lambda j: (0, j))],
        out_specs=pl.BlockSpec((bsz, tn), lambda j: (0, j)),
        out_shape=jax.ShapeDtypeStruct((bsz, n), F32),
        compiler_params=_params(("arbitrary",)),
        name="adaln",
    )(c, w, b.reshape(1, n))


def _inproj_kernel(tiles_per_seq, x_ref, sc_ref, sh_ref, g_ref, w_ref, wconv_ref, shift_ref,
                   qkva_ref, z_ref, ba_ref, *rest):
    qkvb_refs = rest[:len(DILATED_PAIRS)]
    win_scr, pb_scr, pb_next_scr = rest[len(DILATED_PAIRS):]
    i = pl.program_id(0)
    tm = x_ref.shape[0]
    x = x_ref[...]
    h = x * lax.rsqrt(jnp.mean(x * x, axis=-1, keepdims=True) + EPS) * g_ref[...]
    h = h * (1.0 + sc_ref[0]) + sh_ref[0]
    hb = h.astype(BF16)

    @pl.when(i % tiles_per_seq == 0)
    def _():
        win_scr[0:HEAD_ROWS, :] = jnp.zeros((HEAD_ROWS, GDN_CONV_W), F32)

    pa = _dot(hb, w_ref[:, 0:GDN_CONV_W])
    pa16 = pa.astype(BF16)
    for b in range(tm // CONV_BLK):
        rs = slice(b * CONV_BLK, (b + 1) * CONV_BLK)
        acc = pa[rs] * wconv_ref[CONV_K - 1:CONV_K, :]
        for j in range(CONV_K - 1):
            if b == 0:
                shifted = _dot(shift_ref[j, :, CONV_BLK:], pa16[rs])
            else:
                shifted = _dot(shift_ref[j], pa16[(b - 1) * CONV_BLK:(b + 1) * CONV_BLK])
            acc = acc + shifted * wconv_ref[j:j + 1, :]
        qkva_ref[rs, :] = (acc * _sigmoid(acc)).astype(qkva_ref.dtype)
    win_scr[HEAD_ROWS:2 * HEAD_ROWS, :] = pa[0:HEAD_ROWS]
    acc = pa[0:HEAD_ROWS] * wconv_ref[CONV_K - 1:CONV_K, :]
    for j in range(CONV_K - 1):
        acc = acc + win_scr[pl.ds(HEAD_ROWS - (CONV_K - 1) + j, HEAD_ROWS), :] * wconv_ref[j:j + 1, :]
    qkva_ref[0:HEAD_ROWS, :] = (acc * _sigmoid(acc)).astype(qkva_ref.dtype)
    win_scr[0:HEAD_ROWS, :] = pa[tm - HEAD_ROWS:tm]

    c0 = GDN_CONV_W
    z_ref[...] = _dot(hb, w_ref[:, c0:c0 + GDN_V_W]).astype(z_ref.dtype)
    c0 += GDN_V_W
    ba_ref[...] = _dot(hb, w_ref[:, c0:c0 + SMALL_W])
    c0 += SMALL_W
    pb = _dot(hb, w_ref[:, c0:c0 + 3 * ATTN_W])
    ncol = 3 * ATTN_W // LANES
    for j in range(ncol):
        pb_scr[j] = pb[:, j * LANES:(j + 1) * LANES]
    prev_dil, prev_scr, next_scr = 1, pb_scr, pb_next_scr
    for bi, ((_, dil), ref) in enumerate(zip(DILATED_PAIRS, qkvb_refs)):
        if dil == 1:
            ref[...] = pb.astype(ref.dtype)
            continue
        f = dil // prev_dil
        n = tm // dil
        keep = bi + 1 < len(DILATED_PAIRS)
        for r in range(prev_dil):
            for q in range(f):
                r_new = r + prev_dil * q
                for j in range(ncol):
                    rows = prev_scr[j, pl.ds(r * (tm // prev_dil) + q, n, stride=f), :]
                    c1 = r_new * 3 * ATTN_W + j * LANES
                    ref[:, c1:c1 + LANES] = rows.astype(ref.dtype)
                    if keep:
                        next_scr[j, r_new * n:(r_new + 1) * n, :] = rows
        prev_dil, prev_scr, next_scr = dil, next_scr, prev_scr


def _inproj(x2, sc, sh, g, wcat, wconv, seq):
    t, d = x2.shape
    tm = TM_PROJ
    tps = seq // tm
    mod_spec = pl.BlockSpec((1, 1, d), lambda i: (i // tps, 0, 0))
    return pl.pallas_call(
        functools.partial(_inproj_kernel, tps),
        grid=(t // tm,),
        in_specs=[pl.BlockSpec((tm, d), lambda i: (i, 0)),
                  mod_spec, mod_spec,
                  pl.BlockSpec((1, d), lambda i: (0, 0)),
                  pl.BlockSpec((d, W_COLS), lambda i: (0, 0)),
                  pl.BlockSpec((CONV_K, GDN_CONV_W), lambda i: (0, 0)),
                  pl.BlockSpec((CONV_K - 1, CONV_BLK, 2 * CONV_BLK), lambda i: (0, 0, 0))],
        out_specs=[pl.BlockSpec((tm, GDN_CONV_W), lambda i: (i, 0)),
                   pl.BlockSpec((tm, GDN_V_W), lambda i: (i, 0)),
                   pl.BlockSpec((tm, SMALL_W), lambda i: (i, 0))]
                  + [pl.BlockSpec((tm // dil, dil * 3 * ATTN_W), lambda i: (i, 0)) for _, dil in DILATED_PAIRS],
        out_shape=[jax.ShapeDtypeStruct((t, GDN_CONV_W), BF16),
                   jax.ShapeDtypeStruct((t, GDN_V_W), BF16),
                   jax.ShapeDtypeStruct((t, SMALL_W), F32)]
                  + [jax.ShapeDtypeStruct((t // dil, dil * 3 * ATTN_W), BF16) for _, dil in DILATED_PAIRS],
        scratch_shapes=[pltpu.VMEM((2 * HEAD_ROWS, GDN_CONV_W), F32),
                        pltpu.VMEM((3 * ATTN_W // LANES, tm, LANES), F32),
                        pltpu.VMEM((3 * ATTN_W // LANES, tm, LANES), F32)],
        compiler_params=_params(("arbitrary",)),
        name="inproj",
    )(x2, sc, sh, g, wcat, wconv, _shift_bands())


def _shift_bands():
    t = np.arange(CONV_BLK)[:, None]
    c = np.arange(2 * CONV_BLK)[None, :]
    bands = [(c == CONV_BLK + t - (CONV_K - 1 - j)) for j in range(CONV_K - 1)]
    return jnp.asarray(np.stack(bands).astype(np.float32), dtype=BF16)


def _softplus(x):
    return jnp.maximum(x, 0.0) + jnp.log(1.0 + jnp.exp(-jnp.abs(x)))


def _unit_lower_inverses(n_list):
    c = n_list[0].shape[0]
    row = lax.broadcasted_iota(jnp.int32, (c, c), 0)
    col = lax.broadcasted_iota(jnp.int32, (c, c), 1)
    eye = jnp.where(row == col, 1.0, 0.0)
    same_base = (row // INV_BASE) == (col // INV_BASE)
    n0 = [jnp.where(same_base, n, 0.0) for n in n_list]
    p = [eye - x for x in n0]
    m = [x.astype(BF16) for x in n0]
    for j in range(int(math.log2(INV_BASE)) - 1):
        m = [_dot(x, x).astype(BF16) for x in m]
        p = [x + _dot(x.astype(BF16), y) for x, y in zip(p, m)]
    size = INV_BASE
    while size < c:
        sibling = jnp.logical_and((row // size) % 2 == 1, (col // size) == (row // size) - 1)
        cb = [jnp.where(sibling, n, 0.0).astype(BF16) for n in n_list]
        pb = [x.astype(BF16) for x in p]
        pc = [_dot(x, y).astype(BF16) for x, y in zip(pb, cb)]
        p = [x - _dot(y, z) for x, y, z in zip(p, pc, pb)]
        size *= 2
    return p


def _split3_bf16(x):
    h1 = x.astype(BF16)
    r1 = x - h1.astype(F32)
    h2 = r1.astype(BF16)
    h3 = (r1 - h2.astype(F32)).astype(BF16)
    return h1, h2, h3


def _gdn_kernel(steps_per_seq, qkv_ref, ba_ref, bat_ref, gp_ref, gpt_ref, lblk_ref, ublk_ref,
                o_ref, state_scr):
    i = pl.program_id(0)

    @pl.when(i % steps_per_seq == 0)
    def _():
        state_scr[...] = jnp.zeros_like(state_scr)

    nchunk = GDN_TOK // CHUNK
    ba = ba_ref[...]
    bat = bat_ref[...]
    a_vec = -jnp.exp(gp_ref[0:1, :])
    g_tile = a_vec * _softplus(ba + gp_ref[1:2, :])
    lblk = lblk_ref[...]
    g_cum = sum(_dot(lblk, part) for part in _split3_bf16(g_tile))
    a_col = -jnp.exp(gpt_ref[:, 0:1])
    gt_tile = a_col * _softplus(bat + gpt_ref[:, 1:2])
    ublk = ublk_ref[...]
    gt_cum = sum(_dot(part, ublk) for part in _split3_bf16(gt_tile))
    beta_tile = _sigmoid(ba)

    row = lax.broadcasted_iota(jnp.int32, (CHUNK, CHUNK), 0)
    col = lax.broadcasted_iota(jnp.int32, (CHUNK, CHUNK), 1)
    incl = row >= col
    strict = row > col

    for h0 in range(0, GDN_HEADS, GDN_GROUP):
        heads = range(h0, h0 + GDN_GROUP)
        kb_l, k_l, q_l, rhs_l, qg_l, kg_l, gl_l, inc_l, str_l = ([] for _ in range(9))
        for h in heads:
            q_raw = qkv_ref[:, h * GDN_DK:(h + 1) * GDN_DK].astype(F32)
            k_raw = qkv_ref[:, GDN_QK_W + h * GDN_DK:GDN_QK_W + (h + 1) * GDN_DK].astype(F32)
            v = qkv_ref[:, 2 * GDN_QK_W + h * GDN_DV:2 * GDN_QK_W + (h + 1) * GDN_DV].astype(F32)
            qn = q_raw * lax.rsqrt(jnp.sum(q_raw * q_raw, axis=-1, keepdims=True) + EPS) * (GDN_DK ** -0.5)
            kn = k_raw * lax.rsqrt(jnp.sum(k_raw * k_raw, axis=-1, keepdims=True) + EPS)
            beta = beta_tile[:, h:h + 1]
            gc_all = g_cum[:, GDN_HEADS + h:GDN_HEADS + h + 1]
            eg = jnp.exp(gc_all)
            kb = kn * beta
            rhs = jnp.concatenate([v * beta, kb * eg], axis=1).astype(BF16)
            qg = qn * eg
            kb16, k16, q16 = kb.astype(BF16), kn.astype(BF16), qn.astype(BF16)
            for c in range(nchunk):
                sl = slice(c * CHUNK, (c + 1) * CHUNK)
                g_c = gc_all[sl]
                g_r = gt_cum[GDN_HEADS + h:GDN_HEADS + h + 1, sl]
                dec = jnp.exp(g_c - g_r)
                inc_l.append(jnp.where(incl, dec, 0.0))
                str_l.append(jnp.where(strict, dec, 0.0))
                g_last = g_c[CHUNK - 1:CHUNK, :]
                kg_l.append((kn[sl] * jnp.exp(g_last - g_c)).astype(BF16))
                gl_l.append(jnp.exp(g_last))
                kb_l.append(kb16[sl])
                k_l.append(k16[sl])
                q_l.append(q16[sl])
                rhs_l.append(rhs[sl])
                qg_l.append(qg[sl])
        nprob = len(k_l)
        n_l = [_dot_nt(kb_l[j], k_l[j]) * str_l[j] for j in range(nprob)]
        aqk_l = [(_dot_nt(q_l[j], k_l[j]) * inc_l[j]).astype(BF16) for j in range(nprob)]
        tinv_l = _unit_lower_inverses(n_l)
        sol_l = [_dot(tinv_l[j].astype(BF16), rhs_l[j]).astype(BF16) for j in range(nprob)]
        a2_l = [_dot(aqk_l[j], sol_l[j]) for j in range(nprob)]
        k2_l = [_dot_tn(kg_l[j], sol_l[j]) for j in range(nprob)]
        states = [state_scr[h] for h in heads]
        for c in range(nchunk):
            sl = slice(c * CHUNK, (c + 1) * CHUNK)
            for hi, h in enumerate(heads):
                j = hi * nchunk + c
                sb = states[hi].astype(BF16)
                qeff = (qg_l[j] - a2_l[j][:, GDN_DV:]).astype(BF16)
                o = _dot(qeff, sb) + a2_l[j][:, :GDN_DV]
                states[hi] = gl_l[j] * states[hi] + k2_l[j][:, :GDN_DV] - _dot(k2_l[j][:, GDN_DV:].astype(BF16), sb)
                o_ref[sl, h * GDN_DV:(h + 1) * GDN_DV] = o.astype(o_ref.dtype)
        for hi, h in enumerate(heads):
            state_scr[h] = states[hi]


def _gdn(qkva, ba, bat, gp, gpt, lblk, ublk, seq):
    t = qkva.shape[0]
    tok = GDN_TOK
    sps = seq // tok
    const = lambda i: (0, 0)
    return pl.pallas_call(
        functools.partial(_gdn_kernel, sps),
        grid=(t // tok,),
        in_specs=[pl.BlockSpec((tok, GDN_CONV_W), lambda i: (i, 0)),
                  pl.BlockSpec((tok, SMALL_W), lambda i: (i, 0)),
                  pl.BlockSpec((8, tok), lambda i: (0, i)),
                  pl.BlockSpec((8, LANES), const),
                  pl.BlockSpec((8, LANES), const),
                  pl.BlockSpec((tok, tok), const),
                  pl.BlockSpec((tok, tok), const)],
        out_specs=pl.BlockSpec((tok, GDN_V_W), lambda i: (i, 0)),
        out_shape=jax.ShapeDtypeStruct((t, GDN_V_W), F32),
        scratch_shapes=[pltpu.VMEM((GDN_HEADS, GDN_DK, GDN_DV), F32)],
        compiler_params=_params(("arbitrary",)),
        name="gdn",
    )(qkva, ba, bat, gp, gpt, lblk, ublk)


def _attn_kernel(q_ref, kp_ref, kc_ref, vp_ref, vc_ref, bucket_ref, rb_ref, o_ref, lse_ref, bias_ref):
    n = pl.program_id(2)

    @pl.when(jnp.logical_and(jnp.logical_and(pl.program_id(0) == 0, pl.program_id(1) == 0), n == 0))
    def _():
        bucket = bucket_ref[...]
        for h in range(ATTN_HEADS):
            acc = jnp.full((QBLK, 2 * QBLK), NEG_INF, F32)
            for b in range(NUM_BUCKETS):
                acc = jnp.where(bucket == b, rb_ref[b, h], acc)
            bias_ref[h] = acc

    q = q_ref[0]
    k = jnp.concatenate([kp_ref[0], kc_ref[0]], axis=0)
    v = jnp.concatenate([vp_ref[0], vc_ref[0]], axis=0)
    col = lax.broadcasted_iota(jnp.int32, (QBLK, 2 * QBLK), 1)
    lane = lax.broadcasted_iota(jnp.int32, (QBLK, LANES), 1)
    no_prev = jnp.logical_and(n == 0, col < QBLK)
    outs = []
    lse_tile = jnp.zeros((QBLK, LANES), F32)
    per_group = LANES // ATTN_DH
    scale = ATTN_DH ** -0.5
    mine = [jnp.logical_and(lane >= j * ATTN_DH, lane < (j + 1) * ATTN_DH) for j in range(per_group)]
    s_l = []
    for h in range(ATTN_HEADS):
        grp, j = divmod(h, per_group)
        gs = slice(grp * LANES, (grp + 1) * LANES)
        qg = q[:, gs] * scale
        s = _dot_nt(jnp.where(mine[j], qg, jnp.zeros_like(qg)), k[:, gs]) + bias_ref[h]
        s_l.append(jnp.where(no_prev, NEG_INF, s))
    m_l = [jnp.max(s, axis=-1, keepdims=True) for s in s_l]
    p_l = [jnp.exp(s - m) for s, m in zip(s_l, m_l)]
    den_l = [jnp.sum(p, axis=-1, keepdims=True) for p in p_l]
    pv_l = [_dot(p_l[h].astype(BF16), v[:, (h // per_group) * LANES:(h // per_group + 1) * LANES])
            for h in range(ATTN_HEADS)]
    for grp in range(ATTN_HEADS // per_group):
        o_grp = jnp.zeros((QBLK, LANES), F32)
        for j in range(per_group):
            h = grp * per_group + j
            o_grp = jnp.where(mine[j], pv_l[h] / den_l[h], o_grp)
            lse_tile = jnp.where(lane == h, m_l[h] + jnp.log(den_l[h]), lse_tile)
        outs.append(o_grp)
    o_ref[0] = jnp.concatenate(outs, axis=1).astype(o_ref.dtype)
    lse_ref[0] = lse_tile


def _attn_branch(qkvb, bucket, rel_bias, bsz, seq, dil):
    ln = seq // dil
    nb = ln // QBLK
    xv = qkvb.reshape(bsz, ln, dil * 3 * ATTN_W)
    w = ATTN_W
    cur = lambda part: (lambda b, r, n: (b, n, r * 3 + part))
    prev = lambda part: (lambda b, r, n: (b, jnp.maximum(n - 1, 0), r * 3 + part))
    o, lse = pl.pallas_call(
        _attn_kernel,
        grid=(bsz, dil, nb),
        in_specs=[pl.BlockSpec((1, QBLK, w), cur(0)),
                  pl.BlockSpec((1, QBLK, w), prev(1)),
                  pl.BlockSpec((1, QBLK, w), cur(1)),
                  pl.BlockSpec((1, QBLK, w), prev(2)),
                  pl.BlockSpec((1, QBLK, w), cur(2)),
                  pl.BlockSpec((QBLK, 2 * QBLK), lambda b, r, n: (0, 0)),
                  pl.BlockSpec(memory_space=pltpu.SMEM)],
        out_specs=[pl.BlockSpec((1, QBLK, w), lambda b, r, n: (b, n, r)),
                   pl.BlockSpec((1, QBLK, LANES), lambda b, r, n: (b, n, r))],
        out_shape=[jax.ShapeDtypeStruct((bsz, ln, dil * w), BF16),
                   jax.ShapeDtypeStruct((bsz, ln, dil * LANES), F32)],
        scratch_shapes=[pltpu.VMEM((ATTN_HEADS, QBLK, 2 * QBLK), F32)],
        compiler_params=_params(("arbitrary", "arbitrary", "arbitrary")),
        name=f"attn_d{dil}",
    )(xv, xv, xv, xv, xv, bucket, rel_bias)
    return o.reshape(bsz * ln, dil * w), lse.reshape(bsz * ln, dil * LANES)


def _bucket_table(window, dil):
    steps = window // dil
    qi = np.arange(QBLK)[:, None]
    kj = np.arange(2 * QBLK)[None, :]
    delta = qi + QBLK - kj
    dist = np.maximum(delta, 0) * dil
    max_exact = NUM_BUCKETS // 2
    d_f = np.maximum(dist, 1).astype(np.float32)
    large = max_exact + (np.log(d_f / max_exact) / math.log(REL_MAX_DIST / max_exact)
                         * (NUM_BUCKETS - max_exact)).astype(np.int32)
    bucket = np.where(dist < max_exact, dist, np.minimum(large, NUM_BUCKETS - 1))
    return np.where((delta >= 0) & (delta <= steps), bucket, -1).astype(np.int32)


def _split_bf16(x):
    hi = x.astype(BF16)
    lo = (x - hi.astype(F32)).astype(BF16)
    return hi, lo


def _post_kernel(og_ref, z_ref, o1_ref, o2_ref, o3_ref, l1_ref, l2_ref, l3_ref, x_ref,
                 gta_ref, scf_ref, shf_ref, gnorm_ref, anorm_ref, wout_ref, nffn_ref,
                 wrh_ref, wrl_ref, exp_ref, ls_ref,
                 x1_ref, h2_ref, route_ref, cnt_ref, carry_scr, *tok_scr):
    i = pl.program_id(0)
    tm = x_ref.shape[0]

    @pl.when(i == 0)
    def _():
        carry_scr[...] = jnp.zeros_like(carry_scr)

    nbr = len(DILATED_PAIRS)
    ob_scr, lse_scr = tok_scr[:nbr], tok_scr[nbr:]
    for (_, dil), o_ref, l_ref, o_s, l_s in zip(DILATED_PAIRS, (o1_ref, o2_ref, o3_ref),
                                                 (l1_ref, l2_ref, l3_ref), ob_scr, lse_scr):
        for r in range(dil):
            rows = pl.ds(r, tm // dil, stride=dil) if dil > 1 else slice(None)
            for j in range(ATTN_W // LANES):
                c1 = r * ATTN_W + j * LANES
                o_s[j, rows, :] = o_ref[:, c1:c1 + LANES].astype(F32)
            l_s[rows, :] = l_ref[:, r * LANES:(r + 1) * LANES]

    heads = []
    for h in range(GDN_HEADS):
        hs = slice(h * GDN_DV, (h + 1) * GDN_DV)
        seg = og_ref[:, hs]
        nrm = seg * lax.rsqrt(jnp.mean(seg * seg, axis=-1, keepdims=True) + EPS) * gnorm_ref[...]
        zz = z_ref[:, hs].astype(F32)
        heads.append((nrm * (zz * _sigmoid(zz))).astype(BF16))
    oa = jnp.concatenate(heads, axis=1)

    l1, l2, l3 = (l_s[...] for l_s in lse_scr)
    m = jnp.maximum(jnp.maximum(l1, l2), l3)
    e1, e2, e3 = jnp.exp(l1 - m), jnp.exp(l2 - m), jnp.exp(l3 - m)
    inv = 1.0 / (e1 + e2 + e3)
    ob = jnp.zeros((tm, ATTN_W), F32)
    for e, o_s in zip((e1, e2, e3), ob_scr):
        hi, lo = _split_bf16(e * inv)
        wexp = _dot(hi, exp_ref[...]) + _dot(lo, exp_ref[...])
        ob = ob + wexp * jnp.concatenate([o_s[j] for j in range(ATTN_W // LANES)], axis=1)
    ob = ob * lax.rsqrt(jnp.mean(ob * ob, axis=-1, keepdims=True) + EPS) * anorm_ref[...]

    mix = _dot(oa, wout_ref[0:GDN_V_W, :]) + _dot(ob.astype(BF16), wout_ref[GDN_V_W:, :])
    x1 = x_ref[...] + gta_ref[0] * mix
    x1_ref[...] = x1
    h2 = x1 * lax.rsqrt(jnp.mean(x1 * x1, axis=-1, keepdims=True) + EPS) * nffn_ref[...]
    h2 = h2 * (1.0 + scf_ref[0]) + shf_ref[0]
    h2_ref[...] = h2

    hh, hl = _split_bf16(h2)
    logits = _dot(hh, wrh_ref[...]) + _dot(hh, wrl_ref[...]) + _dot(hl, wrh_ref[...])
    tm = logits.shape[0]
    lane = lax.broadcasted_iota(jnp.int32, (tm, LANES), 1).astype(F32)
    big = float(LANES)
    gmask = lane < N_GROUPS
    glog = jnp.where(gmask, logits, NEG_INF)
    gmax = jnp.max(glog, axis=-1, keepdims=True)
    gidx = jnp.min(jnp.where(jnp.logical_and(gmask, glog == gmax), lane, big), axis=-1, keepdims=True)
    gprob = 1.0 / jnp.sum(jnp.where(gmask, jnp.exp(glog - gmax), 0.0), axis=-1, keepdims=True)
    lo_lane = N_GROUPS + EXPERTS_PER_GROUP * gidx
    emask = jnp.logical_and(lane >= lo_lane, lane < lo_lane + EXPERTS_PER_GROUP)
    elog = jnp.where(emask, logits, NEG_INF)
    m1 = jnp.max(elog, axis=-1, keepdims=True)
    i1 = jnp.min(jnp.where(jnp.logical_and(emask, elog == m1), lane, big), axis=-1, keepdims=True)
    emask2 = jnp.logical_and(emask, lane != i1)
    elog2 = jnp.where(emask2, logits, NEG_INF)
    m2 = jnp.max(elog2, axis=-1, keepdims=True)
    i2 = jnp.min(jnp.where(jnp.logical_and(emask2, elog2 == m2), lane, big), axis=-1, keepdims=True)
    r = jnp.exp(m2 - m1)
    gate1 = gprob / (1.0 + r)
    gate2 = gprob * r / (1.0 + r)
    ex1 = i1 - N_GROUPS
    ex2 = i2 - N_GROUPS

    hit1 = lane == ex1
    hit2 = lane == ex2
    onehot = jnp.where(jnp.logical_or(hit1, hit2), 1.0, 0.0)
    pref = _dot(ls_ref[...], onehot.astype(BF16)) + carry_scr[...]
    rank1 = jnp.sum(jnp.where(hit1, pref, 0.0), axis=-1, keepdims=True)
    rank2 = jnp.sum(jnp.where(hit2, pref, 0.0), axis=-1, keepdims=True)
    carry = carry_scr[...] + jnp.sum(onehot, axis=0, keepdims=True)
    carry_scr[...] = carry
    cnt_ref[...] = jnp.broadcast_to(carry, cnt_ref.shape)

    route = jnp.zeros((tm, LANES), F32)
    for idx, val in enumerate((ex1, ex2, gate1, gate2, rank1, rank2)):
        route = jnp.where(lane == idx, val, route)
    route_ref[...] = route


def _post(og, z, os_, ls_, x2, gta, scf, shf, gnorm, anorm, wout, nffn, wrh, wrl, expand, lstrict, seq):
    t, d = x2.shape
    tm = TM_POST
    tps = seq // tm
    tile = lambda w: pl.BlockSpec((tm, w), lambda i: (i, 0))
    const2 = lambda a: pl.BlockSpec(a.shape, lambda i: (0, 0))
    mod_spec = pl.BlockSpec((1, 1, d), lambda i: (i // tps, 0, 0))
    return pl.pallas_call(
        _post_kernel,
        grid=(t // tm,),
        in_specs=[tile(GDN_V_W), tile(GDN_V_W)]
                 + [pl.BlockSpec((tm // dil, dil * ATTN_W), lambda i: (i, 0)) for _, dil in DILATED_PAIRS]
                 + [pl.BlockSpec((tm // dil, dil * LANES), lambda i: (i, 0)) for _, dil in DILATED_PAIRS]
                 + [tile(d),
                  mod_spec, mod_spec, mod_spec,
                  const2(gnorm), const2(anorm), const2(wout), const2(nffn),
                  const2(wrh), const2(wrl), const2(expand), const2(lstrict)],
        out_specs=[tile(d), tile(d), tile(LANES), pl.BlockSpec((8, LANES), lambda i: (0, 0))],
        out_shape=[jax.ShapeDtypeStruct((t, d), F32),
                   jax.ShapeDtypeStruct((t, d), F32),
                   jax.ShapeDtypeStruct((t, LANES), F32),
                   jax.ShapeDtypeStruct((8, LANES), F32)],
        scratch_shapes=[pltpu.VMEM((1, LANES), F32)]
                       + [pltpu.VMEM((ATTN_W // LANES, tm, LANES), F32) for _ in DILATED_PAIRS]
                       + [pltpu.VMEM((tm, LANES), F32) for _ in DILATED_PAIRS],
        compiler_params=_params(("arbitrary",)),
        name="post",
    )(og, z, *os_, *ls_, x2, gta, scf, shf, gnorm, anorm, wout, nffn, wrh, wrl, expand, lstrict)


def _dispatch_kernel(dest_ref, fill_ref, pad_ref, nused_ref, h2_ref, xs_ref, zero_scr, sem, fill_sem):
    i = pl.program_id(0)
    tm = h2_ref.shape[0]

    @pl.when(i == 0)
    def _():
        zero_scr[...] = jnp.zeros_like(zero_scr)

        def pieces(e, act):
            base, pad = fill_ref[e], pad_ref[e]
            head = pad & (SUBLANES - 1)
            for j in range(SUBLANES - 1):
                @pl.when(j < head)
                def _():
                    act(pltpu.make_async_copy(zero_scr.at[pl.ds(0, 1)], xs_ref.at[pl.ds(base + j, 1)],
                                              fill_sem))
            off = base + head
            for bit in range(SUBLANES.bit_length() - 1, EXPERT_BLK.bit_length() - 1):
                size = 1 << bit

                @pl.when((pad >> bit) & 1 == 1)
                def _():
                    act(pltpu.make_async_copy(zero_scr.at[pl.ds(0, size)],
                                              xs_ref.at[pl.ds(pl.multiple_of(off, SUBLANES), size)], fill_sem))
                off = off + (pad & size)

        def fill(e, carry):
            pieces(e, lambda cp: cp.start())
            return carry

        def drain(e, carry):
            pieces(e, lambda cp: cp.wait())
            return carry

        lax.fori_loop(0, N_EXPERTS, fill, 0)
        lax.fori_loop(0, N_EXPERTS, drain, 0)

        def tail(b, act):
            for half in range(EXPERT_BLK // zero_scr.shape[0]):
                row0 = pl.multiple_of(b * EXPERT_BLK + half * zero_scr.shape[0], SUBLANES)
                act(pltpu.make_async_copy(zero_scr, xs_ref.at[pl.ds(row0, zero_scr.shape[0])], fill_sem))

        def tail_fill(b, carry):
            tail(b, lambda cp: cp.start())
            return carry

        def tail_drain(b, carry):
            tail(b, lambda cp: cp.wait())
            return carry

        nblk = xs_ref.shape[0] // EXPERT_BLK
        lax.fori_loop(nused_ref[0], nblk, tail_fill, 0)
        lax.fori_loop(nused_ref[0], nblk, tail_drain, 0)

    def start(r, carry):
        for k in range(TOP_K):
            d = dest_ref[(i * tm + r) * TOP_K + k]
            pltpu.make_async_copy(h2_ref.at[pl.ds(r, 1)], xs_ref.at[pl.ds(d, 1)], sem).start()
        return carry

    lax.fori_loop(0, tm, start, 0, unroll=ROW_UNROLL)
    for k in range(TOP_K):
        pltpu.make_async_copy(h2_ref, xs_ref.at[pl.ds(0, tm)], sem).wait()


def _dispatch(dest, fill_start, pad_rows, nused, h2, p):
    t, d = h2.shape
    tm = TM_ROWS
    grid_spec = pltpu.PrefetchScalarGridSpec(
        num_scalar_prefetch=4,
        grid=(t // tm,),
        in_specs=[pl.BlockSpec((tm, d), lambda i, *_: (i, 0))],
        out_specs=pl.BlockSpec(memory_space=pl.ANY),
        scratch_shapes=[pltpu.VMEM((EXPERT_BLK // 2, d), F32), pltpu.SemaphoreType.DMA,
                        pltpu.SemaphoreType.DMA],
    )
    return pl.pallas_call(
        _dispatch_kernel,
        grid_spec=grid_spec,
        out_shape=jax.ShapeDtypeStruct((p, d), F32),
        compiler_params=_params(("arbitrary",)),
        name="dispatch",
    )(dest, fill_start, pad_rows, nused, h2)


def _expert_kernel(blk_e_ref, nused_ref, xs_ref, wg_ref, wu_ref, wd_ref, ys_ref):
    b = pl.program_id(0)

    @pl.when(b < nused_ref[0])
    def _():
        x = xs_ref[...].astype(BF16)
        g = _dot(x, wg_ref[0])
        u = _dot(x, wu_ref[0])
        hid = (g * _sigmoid(g)) * u
        ys_ref[...] = _dot(hid.astype(BF16), wd_ref[0])

    @pl.when(b >= nused_ref[0])
    def _():
        ys_ref[...] = jnp.zeros_like(ys_ref)


def _experts(blk_e, nused, xs, wg, wu, wd):
    p, d = xs.shape
    blk = EXPERT_BLK
    row_map = lambda b, be, nu: (jnp.minimum(b, nu[0] - 1), 0)
    w_map = lambda b, be, nu: (be[b], 0, 0)
    grid_spec = pltpu.PrefetchScalarGridSpec(
        num_scalar_prefetch=2,
        grid=(p // blk,),
        in_specs=[pl.BlockSpec((blk, d), row_map),
                  pl.BlockSpec((1, d, D_EXPERT), w_map),
                  pl.BlockSpec((1, d, D_EXPERT), w_map),
                  pl.BlockSpec((1, D_EXPERT, d), w_map)],
        out_specs=pl.BlockSpec((blk, d), lambda b, be, nu: (b, 0)),
    )
    return pl.pallas_call(
        _expert_kernel,
        grid_spec=grid_spec,
        out_shape=jax.ShapeDtypeStruct((p, d), F32),
        compiler_params=_params(("arbitrary",)),
        name="experts",
    )(blk_e, nused, xs, wg, wu, wd)


def _combine_kernel(final_norm, dest_ref, ys_ref, x1_ref, route_ref, gtf_ref, nf_ref, o_ref, ybuf, sem):
    i = pl.program_id(0)
    tm = x1_ref.shape[0]
    slot = i % 2

    def gather_tile(tile, slot_):
        def start(r, carry):
            for k in range(TOP_K):
                d = dest_ref[(tile * tm + r) * TOP_K + k]
                pltpu.make_async_copy(ys_ref.at[pl.ds(d, 1)], ybuf.at[slot_, k, pl.ds(r, 1)],
                                      sem.at[slot_]).start()
            return carry
        lax.fori_loop(0, tm, start, 0, unroll=ROW_UNROLL)

    @pl.when(i == 0)
    def _():
        gather_tile(0, 0)

    @pl.when(i + 1 < pl.num_programs(0))
    def _():
        gather_tile(i + 1, 1 - slot)

    for k in range(TOP_K):
        pltpu.make_async_copy(ys_ref.at[pl.ds(0, tm)], ybuf.at[slot, k], sem.at[slot]).wait()
    route = route_ref[...]
    moe = ybuf[slot, 0] * route[:, 2:3] + ybuf[slot, 1] * route[:, 3:4]
    x2 = x1_ref[...] + gtf_ref[0] * moe
    if final_norm:
        x2 = x2 * lax.rsqrt(jnp.mean(x2 * x2, axis=-1, keepdims=True) + EPS) * nf_ref[...]
    o_ref[...] = x2


def _combine(dest, ys, x1, route, gtf, nf, seq, final_norm):
    t, d = x1.shape
    tm = TM_ROWS
    tps = seq // tm
    grid_spec = pltpu.PrefetchScalarGridSpec(
        num_scalar_prefetch=1,
        grid=(t // tm,),
        in_specs=[pl.BlockSpec(memory_space=pl.ANY),
                  pl.BlockSpec((tm, d), lambda i, dest: (i, 0)),
                  pl.BlockSpec((tm, LANES), lambda i, dest: (i, 0)),
                  pl.BlockSpec((1, 1, d), lambda i, dest: (i // tps, 0, 0)),
                  pl.BlockSpec((1, d), lambda i, dest: (0, 0))],
        out_specs=pl.BlockSpec((tm, d), lambda i, dest: (i, 0)),
        scratch_shapes=[pltpu.VMEM((2, TOP_K, tm, d), F32), pltpu.SemaphoreType.DMA((2,))],
    )
    return pl.pallas_call(
        functools.partial(_combine_kernel, final_norm),
        grid_spec=grid_spec,
        out_shape=jax.ShapeDtypeStruct((t, d), F32),
        compiler_params=_params(("arbitrary",)),
        name="combine",
    )(dest, ys, x1, route, gtf, nf)


def _block_tri(n, chunk, lower):
    r = jnp.arange(n)[:, None]
    c = jnp.arange(n)[None, :]
    same = (r // chunk) == (c // chunk)
    tri = (r >= c) if lower else (r <= c)
    return jnp.where(same & tri, 1.0, 0.0).astype(BF16)


def _layer(x2, c, bsz, seq, w_ada, b_ada, norm_mix, w_in, w_conv, a_log, dt_bias, gdn_norm,
           attn_norm, w_out, rel_bias, norm_ffn, w_rg, w_re, w_gate, w_up, w_down):
    t, d = x2.shape
    mod = _adaln(c, w_ada, b_ada)
    sh_a, sc_a, gt_a, sh_f, sc_f, gt_f = [m.reshape(bsz, 1, d) for m in jnp.split(mod, 6, axis=-1)]

    s1 = GDN_CONV_W
    s2 = s1 + GDN_V_W
    s4 = s2 + 2 * GDN_HEADS
    small = jnp.pad(w_in[:, s2:s4], ((0, 0), (0, SMALL_W - 2 * GDN_HEADS)))
    wcat = jnp.concatenate([w_in[:, :s2], small, w_in[:, s4:]], axis=1).astype(BF16)
    qkva, z, ba, *qkvb = _inproj(x2, sc_a, sh_a, norm_mix.reshape(1, d), wcat, w_conv, seq)

    pad4 = lambda v: jnp.pad(v.astype(F32), (GDN_HEADS, LANES - 2 * GDN_HEADS))
    gp = jnp.zeros((8, LANES), F32).at[0].set(pad4(a_log)).at[1].set(pad4(dt_bias))
    og = _gdn(qkva, ba, ba[:, :8].T, gp, _gp_cols(a_log, dt_bias),
              _block_tri(GDN_TOK, CHUNK, True), _block_tri(GDN_TOK, CHUNK, False), seq)

    outs, lses = [], []
    for (window, dil), qkvb_d in zip(DILATED_PAIRS, qkvb):
        o_i, lse_i = _attn_branch(qkvb_d, jnp.asarray(_bucket_table(window, dil)), rel_bias.astype(F32),
                                  bsz, seq, dil)
        outs.append(o_i)
        lses.append(lse_i)

    expand = jnp.where((jnp.arange(LANES)[:, None] == jnp.arange(ATTN_W)[None, :] // ATTN_DH), 1.0, 0.0).astype(BF16)
    lstrict = jnp.where(jnp.arange(TM_POST)[:, None] > jnp.arange(TM_POST)[None, :], 1.0, 0.0).astype(BF16)
    wr = jnp.pad(jnp.concatenate([w_rg, w_re], axis=1).astype(F32), ((0, 0), (0, LANES - N_GROUPS - N_EXPERTS)))
    wrh = wr.astype(BF16)
    wrl = (wr - wrh.astype(F32)).astype(BF16)
    x1, h2, route, cnt = _post(og, z, outs, lses, x2, gt_a, sc_f, sh_f,
                               gdn_norm.reshape(1, GDN_DV), attn_norm.reshape(1, ATTN_W),
                               w_out.astype(BF16), norm_ffn.reshape(1, d), wrh, wrl, expand, lstrict, seq)

    blk = EXPERT_BLK
    counts = cnt[0, :N_EXPERTS].astype(jnp.int32)
    padded = (counts + blk - 1) // blk * blk
    pends = jnp.cumsum(padded)
    pstarts = pends - padded
    eids = route[:, 0:TOP_K].astype(jnp.int32)
    ranks = route[:, 4:4 + TOP_K].astype(jnp.int32)
    expert_ids = jnp.arange(N_EXPERTS, dtype=jnp.int32)
    seg_start = jnp.sum(jnp.where(eids[..., None] == expert_ids, pstarts, 0), axis=-1)
    dest = (seg_start + ranks).reshape(t * TOP_K)
    a = t * TOP_K
    p = -(-a // blk) * blk + N_EXPERTS * blk
    nblk = p // blk
    blk_start = jnp.arange(nblk, dtype=jnp.int32) * blk
    blk_e = jnp.minimum(jnp.sum((pends[None, :] <= blk_start[:, None]).astype(jnp.int32), axis=1),
                        N_EXPERTS - 1)
    nused = (pends[-1] // blk).astype(jnp.int32).reshape(1)

    xs = _dispatch(dest, pstarts + counts, padded - counts, nused, h2, p)
    ys = _experts(blk_e, nused, xs, w_gate.astype(BF16), w_up.astype(BF16), w_down.astype(BF16))
    return ys, dest, x1, route, gt_f


def _gp_cols(a_log, dt_bias):
    z = jnp.zeros((8, LANES), F32)
    z = z.at[GDN_HEADS:2 * GDN_HEADS, 0].set(a_log.astype(F32))
    z = z.at[GDN_HEADS:2 * GDN_HEADS, 1].set(dt_bias.astype(F32))
    return z


def kernel(x, c, w_ada, b_ada, norm_mix, w_in, w_conv, a_log, dt_bias, gdn_norm, attn_norm, w_out,
           rel_bias, norm_ffn, w_router_group, w_router_expert, w_gate, w_up, w_down, norm_final):
    bsz, seq, d = x.shape
    depth = w_ada.shape[0]
    x2 = x.reshape(bsz * seq, d)
    for l in range(depth):
        ys, dest, x1, route, gt_f = _layer(
            x2, c, bsz, seq, w_ada[l], b_ada[l], norm_mix[l], w_in[l], w_conv[l], a_log[l], dt_bias[l],
            gdn_norm[l], attn_norm[l], w_out[l], rel_bias, norm_ffn[l], w_router_group[l],
            w_router_expert[l], w_gate[l], w_up[l], w_down[l])
        x2 = _combine(dest, ys, x1, route, gt_f, norm_final.reshape(1, d), seq, l == depth - 1)
    return x2.reshape(bsz, seq, d)
```

```python
import functools
import math

import jax
import jax.numpy as jnp
import numpy as np
from jax import lax
from jax.experimental import pallas as pl
from jax.experimental.pallas import tpu as pltpu

D_MODEL = 1024
GDN_HEADS = 4
GDN_DK = 128
GDN_DV = 128
CONV_K = 4
CHUNK = 64
ATTN_HEADS = 8
ATTN_DH = 64
DILATED_PAIRS = ((128, 1), (512, 4), (2048, 16))
QBLK = 128
NUM_BUCKETS = 32
REL_MAX_DIST = 2048
N_GROUPS = 4
EXPERTS_PER_GROUP = 8
N_EXPERTS = N_GROUPS * EXPERTS_PER_GROUP
TOP_K = 2
D_EXPERT = 256
EPS = 1e-6
NEG_INF = -1e30

GDN_QK_W = GDN_HEADS * GDN_DK
GDN_V_W = GDN_HEADS * GDN_DV
ATTN_W = ATTN_HEADS * ATTN_DH
GDN_CONV_W = 2 * GDN_QK_W + GDN_V_W
LANES = 128
SUBLANES = 8
SMALL_W = LANES
W_COLS = GDN_CONV_W + GDN_V_W + SMALL_W + 3 * ATTN_W

TM_PROJ = 512
CONV_BLK = 128
HEAD_ROWS = 16
TM_POST = 256
TM_ROWS = 256
GDN_TOK = 512
GDN_GROUP = 4
INV_BASE = 8
ATTN_SUB = 2
EXPERT_BLK = 256
ROW_UNROLL = 8
VMEM_LIMIT = 56 * 1024 * 1024

F32 = jnp.float32
BF16 = jnp.bfloat16
HIGHEST = lax.Precision.HIGHEST


def _sigmoid(x):
    return 1.0 / (1.0 + jnp.exp(-x))


def _dot(a, b, precision=None):
    return jnp.dot(a, b, preferred_element_type=F32, precision=precision)


def _dot_nt(a, b, precision=None):
    return lax.dot_general(a, b, (((1,), (1,)), ((), ())), preferred_element_type=F32,
                           precision=precision)


def _dot_tn(a, b, precision=None):
    return lax.dot_general(a, b, (((0,), (0,)), ((), ())), preferred_element_type=F32,
                           precision=precision)


def _params(sem):
    return pltpu.CompilerParams(dimension_semantics=sem, vmem_limit_bytes=VMEM_LIMIT)


def _adaln_kernel(c_ref, w_ref, b_ref, o_ref):
    o_ref[...] = _dot(c_ref[...], w_ref[...], HIGHEST) + b_ref[...]


def _adaln(c, w, b):
    bsz, d = c.shape
    n = w.shape[1]
    tn = 1024
    return pl.pallas_call(
        _adaln_kernel,
        grid=(n // tn,),
        in_specs=[pl.BlockSpec((bsz, d), lambda j: (0, 0)),
                  pl.BlockSpec((d, tn), lambda j: (0, j)),
                  pl.BlockSpec((1, tn), lambda j: (0, j))],
        out_specs=pl.BlockSpec((bsz, tn), lambda j: (0, j)),
        out_shape=jax.ShapeDtypeStruct((bsz, n), F32),
        compiler_params=_params(("arbitrary",)),
        name="adaln",
    )(c, w, b.reshape(1, n))


def _inproj_kernel(tiles_per_seq, x_ref, sc_ref, sh_ref, g_ref, w_ref, wconv_ref, shift_ref,
                   qkva_ref, z_ref, ba_ref, *rest):
    qkvb_refs = rest[:len(DILATED_PAIRS)]
    win_scr, pb_scr, pb_next_scr = rest[len(DILATED_PAIRS):]
    i = pl.program_id(0)
    tm = x_ref.shape[0]
    x = x_ref[...]
    h = x * lax.rsqrt(jnp.mean(x * x, axis=-1, keepdims=True) + EPS) * g_ref[...]
    h = h * (1.0 + sc_ref[0]) + sh_ref[0]
    hb = h.astype(BF16)

    @pl.when(i % tiles_per_seq == 0)
    def _():
        win_scr[0:HEAD_ROWS, :] = jnp.zeros((HEAD_ROWS, GDN_CONV_W), F32)

    pa = _dot(hb, w_ref[:, 0:GDN_CONV_W])
    pa16 = pa.astype(BF16)
    for b in range(tm // CONV_BLK):
        rs = slice(b * CONV_BLK, (b + 1) * CONV_BLK)
        acc = pa[rs] * wconv_ref[CONV_K - 1:CONV_K, :]
        for j in range(CONV_K - 1):
            if b == 0:
                shifted = _dot(shift_ref[j, :, CONV_BLK:], pa16[rs])
            else:
                shifted = _dot(shift_ref[j], pa16[(b - 1) * CONV_BLK:(b + 1) * CONV_BLK])
            acc = acc + shifted * wconv_ref[j:j + 1, :]
        qkva_ref[rs, :] = (acc * _sigmoid(acc)).astype(qkva_ref.dtype)
    win_scr[HEAD_ROWS:2 * HEAD_ROWS, :] = pa[0:HEAD_ROWS]
    acc = pa[0:HEAD_ROWS] * wconv_ref[CONV_K - 1:CONV_K, :]
    for j in range(CONV_K - 1):
        acc = acc + win_scr[pl.ds(HEAD_ROWS - (CONV_K - 1) + j, HEAD_ROWS), :] * wconv_ref[j:j + 1, :]
    qkva_ref[0:HEAD_ROWS, :] = (acc * _sigmoid(acc)).astype(qkva_ref.dtype)
    win_scr[0:HEAD_ROWS, :] = pa[tm - HEAD_ROWS:tm]

    c0 = GDN_CONV_W
    z_ref[...] = _dot(hb, w_ref[:, c0:c0 + GDN_V_W]).astype(z_ref.dtype)
    c0 += GDN_V_W
    ba_ref[...] = _dot(hb, w_ref[:, c0:c0 + SMALL_W])
    c0 += SMALL_W
    pb = _dot(hb, w_ref[:, c0:c0 + 3 * ATTN_W])
    ncol = 3 * ATTN_W // LANES
    for j in range(ncol):
        pb_scr[j] = pb[:, j * LANES:(j + 1) * LANES]
    prev_dil, prev_scr, next_scr = 1, pb_scr, pb_next_scr
    for bi, ((_, dil), ref) in enumerate(zip(DILATED_PAIRS, qkvb_refs)):
        if dil == 1:
            ref[...] = pb.astype(ref.dtype)
            continue
        f = dil // prev_dil
        n = tm // dil
        keep = bi + 1 < len(DILATED_PAIRS)
        for r in range(prev_dil):
            for q in range(f):
                r_new = r + prev_dil * q
                for j in range(ncol):
                    rows = prev_scr[j, pl.ds(r * (tm // prev_dil) + q, n, stride=f), :]
                    c1 = r_new * 3 * ATTN_W + j * LANES
                    ref[:, c1:c1 + LANES] = rows.astype(ref.dtype)
                    if keep:
                        next_scr[j, r_new * n:(r_new + 1) * n, :] = rows
        prev_dil, prev_scr, next_scr = dil, next_scr, prev_scr


def _inproj(x2, sc, sh, g, wcat, wconv, seq):
    t, d = x2.shape
    tm = TM_PROJ
    tps = seq // tm
    mod_spec = pl.BlockSpec((1, 1, d), lambda i: (i // tps, 0, 0))
    return pl.pallas_call(
        functools.partial(_inproj_kernel, tps),
        grid=(t // tm,),
        in_specs=[pl.BlockSpec((tm, d), lambda i: (i, 0)),
                  mod_spec, mod_spec,
                  pl.BlockSpec((1, d), lambda i: (0, 0)),
                  pl.BlockSpec((d, W_COLS), lambda i: (0, 0)),
                  pl.BlockSpec((CONV_K, GDN_CONV_W), lambda i: (0, 0)),
                  pl.BlockSpec((CONV_K - 1, CONV_BLK, 2 * CONV_BLK), lambda i: (0, 0, 0))],
        out_specs=[pl.BlockSpec((tm, GDN_CONV_W), lambda i: (i, 0)),
                   pl.BlockSpec((tm, GDN_V_W), lambda i: (i, 0)),
                   pl.BlockSpec((tm, SMALL_W), lambda i: (i, 0))]
                  + [pl.BlockSpec((tm // dil, dil * 3 * ATTN_W), lambda i: (i, 0)) for _, dil in DILATED_PAIRS],
        out_shape=[jax.ShapeDtypeStruct((t, GDN_CONV_W), BF16),
                   jax.ShapeDtypeStruct((t, GDN_V_W), BF16),
                   jax.ShapeDtypeStruct((t, SMALL_W), F32)]
                  + [jax.ShapeDtypeStruct((t // dil, dil * 3 * ATTN_W), BF16) for _, dil in DILATED_PAIRS],
        scratch_shapes=[pltpu.VMEM((2 * HEAD_ROWS, GDN_CONV_W), F32),
                        pltpu.VMEM((3 * ATTN_W // LANES, tm, LANES), F32),
                        pltpu.VMEM((3 * ATTN_W // LANES, tm, LANES), F32)],
        compiler_params=_params(("arbitrary",)),
        name="inproj",
    )(x2, sc, sh, g, wcat, wconv, _shift_bands())


def _shift_bands():
    t = np.arange(CONV_BLK)[:, None]
    c = np.arange(2 * CONV_BLK)[None, :]
    bands = [(c == CONV_BLK + t - (CONV_K - 1 - j)) for j in range(CONV_K - 1)]
    return jnp.asarray(np.stack(bands).astype(np.float32), dtype=BF16)


def _softplus(x):
    return jnp.maximum(x, 0.0) + jnp.log(1.0 + jnp.exp(-jnp.abs(x)))


def _unit_lower_inverses(n_list):
    c = n_list[0].shape[0]
    row = lax.broadcasted_iota(jnp.int32, (c, c), 0)
    col = lax.broadcasted_iota(jnp.int32, (c, c), 1)
    eye = jnp.where(row == col, 1.0, 0.0)
    same_base = (row // INV_BASE) == (col // INV_BASE)
    n0 = [jnp.where(same_base, n, 0.0) for n in n_list]
    p = [eye - x for x in n0]
    m = [x.astype(BF16) for x in n0]
    for j in range(int(math.log2(INV_BASE)) - 1):
        m = [_dot(x, x).astype(BF16) for x in m]
        p = [x + _dot(x.astype(BF16), y) for x, y in zip(p, m)]
    size = INV_BASE
    while size < c:
        sibling = jnp.logical_and((row // size) % 2 == 1, (col // size) == (row // size) - 1)
        cb = [jnp.where(sibling, n, 0.0).astype(BF16) for n in n_list]
        pb = [x.astype(BF16) for x in p]
        pc = [_dot(x, y).astype(BF16) for x, y in zip(pb, cb)]
        p = [x - _dot(y, z) for x, y, z in zip(p, pc, pb)]
        size *= 2
    return p


def _split3_bf16(x):
    h1 = x.astype(BF16)
    r1 = x - h1.astype(F32)
    h2 = r1.astype(BF16)
    h3 = (r1 - h2.astype(F32)).astype(BF16)
    return h1, h2, h3


def _gdn_kernel(steps_per_seq, qkv_ref, ba_ref, bat_ref, gp_ref, gpt_ref, lblk_ref, ublk_ref,
                o_ref, state_scr):
    i = pl.program_id(0)

    @pl.when(i % steps_per_seq == 0)
    def _():
        state_scr[...] = jnp.zeros_like(state_scr)

    nchunk = GDN_TOK // CHUNK
    ba = ba_ref[...]
    bat = bat_ref[...]
    a_vec = -jnp.exp(gp_ref[0:1, :])
    g_tile = a_vec * _softplus(ba + gp_ref[1:2, :])
    lblk = lblk_ref[...]
    g_cum = sum(_dot(lblk, part) for part in _split3_bf16(g_tile))
    a_col = -jnp.exp(gpt_ref[:, 0:1])
    gt_tile = a_col * _softplus(bat + gpt_ref[:, 1:2])
    ublk = ublk_ref[...]
    gt_cum = sum(_dot(part, ublk) for part in _split3_bf16(gt_tile))
    beta_tile = _sigmoid(ba)

    row = lax.broadcasted_iota(jnp.int32, (CHUNK, CHUNK), 0)
    col = lax.broadcasted_iota(jnp.int32, (CHUNK, CHUNK), 1)
    incl = row >= col
    strict = row > col

    for h0 in range(0, GDN_HEADS, GDN_GROUP):
        heads = range(h0, h0 + GDN_GROUP)
        kb_l, k_l, q_l, rhs_l, qg_l, kg_l, gl_l, inc_l, str_l = ([] for _ in range(9))
        for h in heads:
            q_raw = qkv_ref[:, h * GDN_DK:(h + 1) * GDN_DK].astype(F32)
            k_raw = qkv_ref[:, GDN_QK_W + h * GDN_DK:GDN_QK_W + (h + 1) * GDN_DK].astype(F32)
            v = qkv_ref[:, 2 * GDN_QK_W + h * GDN_DV:2 * GDN_QK_W + (h + 1) * GDN_DV].astype(F32)
            qn = q_raw * lax.rsqrt(jnp.sum(q_raw * q_raw, axis=-1, keepdims=True) + EPS) * (GDN_DK ** -0.5)
            kn = k_raw * lax.rsqrt(jnp.sum(k_raw * k_raw, axis=-1, keepdims=True) + EPS)
            beta = beta_tile[:, h:h + 1]
            gc_all = g_cum[:, GDN_HEADS + h:GDN_HEADS + h + 1]
            eg = jnp.exp(gc_all)
            kb = kn * beta
            rhs = jnp.concatenate([v * beta, kb * eg], axis=1).astype(BF16)
            qg = qn * eg
            kb16, k16, q16 = kb.astype(BF16), kn.astype(BF16), qn.astype(BF16)
            for c in range(nchunk):
                sl = slice(c * CHUNK, (c + 1) * CHUNK)
                g_c = gc_all[sl]
                g_r = gt_cum[GDN_HEADS + h:GDN_HEADS + h + 1, sl]
                dec = jnp.exp(g_c - g_r)
                inc_l.append(jnp.where(incl, dec, 0.0))
                str_l.append(jnp.where(strict, dec, 0.0))
                g_last = g_c[CHUNK - 1:CHUNK, :]
                kg_l.append((kn[sl] * jnp.exp(g_last - g_c)).astype(BF16))
                gl_l.append(jnp.exp(g_last))
                kb_l.append(kb16[sl])
                k_l.append(k16[sl])
                q_l.append(q16[sl])
                rhs_l.append(rhs[sl])
                qg_l.append(qg[sl])
        nprob = len(k_l)
        n_l = [_dot_nt(kb_l[j], k_l[j]) * str_l[j] for j in range(nprob)]
        aqk_l = [(_dot_nt(q_l[j], k_l[j]) * inc_l[j]).astype(BF16) for j in range(nprob)]
        tinv_l = _unit_lower_inverses(n_l)
        sol_l = [_dot(tinv_l[j].astype(BF16), rhs_l[j]).astype(BF16) for j in range(nprob)]
        a2_l = [_dot(aqk_l[j], sol_l[j]) for j in range(nprob)]
        k2_l = [_dot_tn(kg_l[j], sol_l[j]) for j in range(nprob)]
        states = [state_scr[h] for h in heads]
        for c in range(nchunk):
            sl = slice(c * CHUNK, (c + 1) * CHUNK)
            for hi, h in enumerate(heads):
                j = hi * nchunk + c
                sb = states[hi].astype(BF16)
                qeff = (qg_l[j] - a2_l[j][:, GDN_DV:]).astype(BF16)
                o = _dot(qeff, sb) + a2_l[j][:, :GDN_DV]
                states[hi] = gl_l[j] * states[hi] + k2_l[j][:, :GDN_DV] - _dot(k2_l[j][:, GDN_DV:].astype(BF16), sb)
                o_ref[sl, h * GDN_DV:(h + 1) * GDN_DV] = o.astype(o_ref.dtype)
        for hi, h in enumerate(heads):
            state_scr[h] = states[hi]


def _gdn(qkva, ba, bat, gp, gpt, lblk, ublk, seq):
    t = qkva.shape[0]
    tok = GDN_TOK
    sps = seq // tok
    const = lambda i: (0, 0)
    return pl.pallas_call(
        functools.partial(_gdn_kernel, sps),
        grid=(t // tok,),
        in_specs=[pl.BlockSpec((tok, GDN_CONV_W), lambda i: (i, 0)),
                  pl.BlockSpec((tok, SMALL_W), lambda i: (i, 0)),
                  pl.BlockSpec((8, tok), lambda i: (0, i)),
                  pl.BlockSpec((8, LANES), const),
                  pl.BlockSpec((8, LANES), const),
                  pl.BlockSpec((tok, tok), const),
                  pl.BlockSpec((tok, tok), const)],
        out_specs=pl.BlockSpec((tok, GDN_V_W), lambda i: (i, 0)),
        out_shape=jax.ShapeDtypeStruct((t, GDN_V_W), F32),
        scratch_shapes=[pltpu.VMEM((GDN_HEADS, GDN_DK, GDN_DV), F32)],
        compiler_params=_params(("arbitrary",)),
        name="gdn",
    )(qkva, ba, bat, gp, gpt, lblk, ublk)


def _attn_kernel(q_ref, kp_ref, kc_ref, vp_ref, vc_ref, bucket_ref, rb_ref, o_ref, lse_ref, bias_ref):
    n = pl.program_id(2)

    @pl.when(jnp.logical_and(jnp.logical_and(pl.program_id(0) == 0, pl.program_id(1) == 0), n == 0))
    def _():
        bucket = bucket_ref[...]
        col = lax.broadcasted_iota(jnp.int32, (QBLK, 2 * QBLK), 1)
        for h in range(ATTN_HEADS):
            acc = jnp.full((QBLK, 2 * QBLK), NEG_INF, F32)
            for b in range(NUM_BUCKETS):
                acc = jnp.where(bucket == b, rb_ref[b, h], acc)
            bias_ref[0, h] = acc
            bias_ref[1, h] = jnp.where(col < QBLK, NEG_INF, acc)

    nsub = q_ref.shape[1] // QBLK
    first = jnp.where(n == 0, 1, 0)
    q_all = q_ref[0]
    k_all = jnp.concatenate([kp_ref[0], kc_ref[0]], axis=0)
    v_all = jnp.concatenate([vp_ref[0], vc_ref[0]], axis=0)
    lane = lax.broadcasted_iota(jnp.int32, (QBLK, LANES), 1)
    ones = jnp.ones((2 * QBLK, LANES), BF16)
    per_group = LANES // ATTN_DH
    ngrp = ATTN_HEADS // per_group
    scale = ATTN_DH ** -0.5
    mine = [jnp.logical_and(lane >= j * ATTN_DH, lane < (j + 1) * ATTN_DH) for j in range(per_group)]
    probs = [(u, h) for u in range(nsub) for h in range(ATTN_HEADS)]
    s_l = []
    for u, h in probs:
        grp, j = divmod(h, per_group)
        gs = slice(grp * LANES, (grp + 1) * LANES)
        qg = q_all[u * QBLK:(u + 1) * QBLK, gs] * scale
        table = first if u == 0 else 0
        s_l.append(_dot_nt(jnp.where(mine[j], qg, jnp.zeros_like(qg)), k_all[u * QBLK:(u + 2) * QBLK, gs])
                   + bias_ref[table, h])
    m_l = [jnp.max(s, axis=-1, keepdims=True) for s in s_l]
    p_l = [jnp.exp(s - m).astype(BF16) for s, m in zip(s_l, m_l)]
    v_ext = {(u, grp): jnp.concatenate([v_all[u * QBLK:(u + 2) * QBLK, grp * LANES:(grp + 1) * LANES], ones],
                                       axis=1)
             for u in range(nsub) for grp in range(ngrp)}
    pv_l = [_dot(p, v_ext[(u, h // per_group)]) for p, (u, h) in zip(p_l, probs)]
    for u in range(nsub):
        outs = []
        lse_tile = jnp.zeros((QBLK, LANES), F32)
        for grp in range(ngrp):
            o_grp = jnp.zeros((QBLK, LANES), F32)
            for j in range(per_group):
                h = grp * per_group + j
                idx = u * ATTN_HEADS + h
                den = pv_l[idx][:, LANES:]
                o_grp = jnp.where(mine[j], pv_l[idx][:, :LANES] / den, o_grp)
                lse_tile = jnp.where(lane == h, m_l[idx] + jnp.log(den[:, 0:1]), lse_tile)
            outs.append(o_grp)
        o_ref[0, u * QBLK:(u + 1) * QBLK, :] = jnp.concatenate(outs, axis=1).astype(o_ref.dtype)
        lse_ref[0, u * QBLK:(u + 1) * QBLK, :] = lse_tile


def _attn_branch(qkvb, bucket, rel_bias, bsz, seq, dil):
    ln = seq // dil
    rows = ATTN_SUB * QBLK
    assert seq % dil == 0 and ln % rows == 0, (seq, dil, rows)
    nb = ln // rows
    xv = qkvb.reshape(bsz, ln, dil * 3 * ATTN_W)
    w = ATTN_W
    cur = lambda part: (lambda b, r, n: (b, n, r * 3 + part))
    prev = lambda part: (lambda b, r, n: (b, jnp.maximum(n * ATTN_SUB - 1, 0), r * 3 + part))
    o, lse = pl.pallas_call(
        _attn_kernel,
        grid=(bsz, dil, nb),
        in_specs=[pl.BlockSpec((1, rows, w), cur(0)),
                  pl.BlockSpec((1, QBLK, w), prev(1)),
                  pl.BlockSpec((1, rows, w), cur(1)),
                  pl.BlockSpec((1, QBLK, w), prev(2)),
                  pl.BlockSpec((1, rows, w), cur(2)),
                  pl.BlockSpec((QBLK, 2 * QBLK), lambda b, r, n: (0, 0)),
                  pl.BlockSpec(memory_space=pltpu.SMEM)],
        out_specs=[pl.BlockSpec((1, rows, w), lambda b, r, n: (b, n, r)),
                   pl.BlockSpec((1, rows, LANES), lambda b, r, n: (b, n, r))],
        out_shape=[jax.ShapeDtypeStruct((bsz, ln, dil * w), BF16),
                   jax.ShapeDtypeStruct((bsz, ln, dil * LANES), F32)],
        scratch_shapes=[pltpu.VMEM((2, ATTN_HEADS, QBLK, 2 * QBLK), F32)],
        compiler_params=_params(("arbitrary", "arbitrary", "arbitrary")),
        name=f"attn_d{dil}",
    )(xv, xv, xv, xv, xv, bucket, rel_bias)
    return o.reshape(bsz * ln, dil * w), lse.reshape(bsz * ln, dil * LANES)


def _bucket_table(window, dil):
    steps = window // dil
    qi = np.arange(QBLK)[:, None]
    kj = np.arange(2 * QBLK)[None, :]
    delta = qi + QBLK - kj
    dist = np.maximum(delta, 0) * dil
    max_exact = NUM_BUCKETS // 2
    d_f = np.maximum(dist, 1).astype(np.float32)
    large = max_exact + (np.log(d_f / max_exact) / math.log(REL_MAX_DIST / max_exact)
                         * (NUM_BUCKETS - max_exact)).astype(np.int32)
    bucket = np.where(dist < max_exact, dist, np.minimum(large, NUM_BUCKETS - 1))
    return np.where((delta >= 0) & (delta <= steps), bucket, -1).astype(np.int32)


def _split_bf16(x):
    hi = x.astype(BF16)
    lo = (x - hi.astype(F32)).astype(BF16)
    return hi, lo


def _post_kernel(og_ref, z_ref, o1_ref, o2_ref, o3_ref, l1_ref, l2_ref, l3_ref, x_ref,
                 gta_ref, scf_ref, shf_ref, gnorm_ref, anorm_ref, wout_ref, nffn_ref,
                 wrh_ref, wrl_ref, exp_ref, ls_ref,
                 x1_ref, h2_ref, route_ref, cnt_ref, carry_scr, *tok_scr):
    i = pl.program_id(0)
    tm = x_ref.shape[0]

    @pl.when(i == 0)
    def _():
        carry_scr[...] = jnp.zeros_like(carry_scr)

    nbr = len(DILATED_PAIRS)
    ob_scr, lse_scr = tok_scr[:nbr], tok_scr[nbr:]
    for (_, dil), o_ref, l_ref, o_s, l_s in zip(DILATED_PAIRS, (o1_ref, o2_ref, o3_ref),
                                                 (l1_ref, l2_ref, l3_ref), ob_scr, lse_scr):
        for r in range(dil):
            rows = pl.ds(r, tm // dil, stride=dil) if dil > 1 else slice(None)
            for j in range(ATTN_W // LANES):
                c1 = r * ATTN_W + j * LANES
                o_s[j, rows, :] = o_ref[:, c1:c1 + LANES].astype(F32)
            l_s[rows, :] = l_ref[:, r * LANES:(r + 1) * LANES]

    heads = []
    for h in range(GDN_HEADS):
        hs = slice(h * GDN_DV, (h + 1) * GDN_DV)
        seg = og_ref[:, hs]
        nrm = seg * lax.rsqrt(jnp.mean(seg * seg, axis=-1, keepdims=True) + EPS) * gnorm_ref[...]
        zz = z_ref[:, hs].astype(F32)
        heads.append((nrm * (zz * _sigmoid(zz))).astype(BF16))
    oa = jnp.concatenate(heads, axis=1)

    l1, l2, l3 = (l_s[...] for l_s in lse_scr)
    m = jnp.maximum(jnp.maximum(l1, l2), l3)
    e1, e2, e3 = jnp.exp(l1 - m), jnp.exp(l2 - m), jnp.exp(l3 - m)
    inv = 1.0 / (e1 + e2 + e3)
    ob = jnp.zeros((tm, ATTN_W), F32)
    for e, o_s in zip((e1, e2, e3), ob_scr):
        hi, lo = _split_bf16(e * inv)
        wexp = _dot(hi, exp_ref[...]) + _dot(lo, exp_ref[...])
        ob = ob + wexp * jnp.concatenate([o_s[j] for j in range(ATTN_W // LANES)], axis=1)
    ob = ob * lax.rsqrt(jnp.mean(ob * ob, axis=-1, keepdims=True) + EPS) * anorm_ref[...]

    mix = _dot(oa, wout_ref[0:GDN_V_W, :]) + _dot(ob.astype(BF16), wout_ref[GDN_V_W:, :])
    x1 = x_ref[...] + gta_ref[0] * mix
    x1_ref[...] = x1
    h2 = x1 * lax.rsqrt(jnp.mean(x1 * x1, axis=-1, keepdims=True) + EPS) * nffn_ref[...]
    h2 = h2 * (1.0 + scf_ref[0]) + shf_ref[0]
    h2_ref[...] = h2

    hh, hl = _split_bf16(h2)
    logits = _dot(hh, wrh_ref[...]) + _dot(hh, wrl_ref[...]) + _dot(hl, wrh_ref[...])
    tm = logits.shape[0]
    lane = lax.broadcasted_iota(jnp.int32, (tm, LANES), 1).astype(F32)
    big = float(LANES)
    gmask = lane < N_GROUPS
    glog = jnp.where(gmask, logits, NEG_INF)
    gmax = jnp.max(glog, axis=-1, keepdims=True)
    gidx = jnp.min(jnp.where(jnp.logical_and(gmask, glog == gmax), lane, big), axis=-1, keepdims=True)
    gprob = 1.0 / jnp.sum(jnp.where(gmask, jnp.exp(glog - gmax), 0.0), axis=-1, keepdims=True)
    lo_lane = N_GROUPS + EXPERTS_PER_GROUP * gidx
    emask = jnp.logical_and(lane >= lo_lane, lane < lo_lane + EXPERTS_PER_GROUP)
    elog = jnp.where(emask, logits, NEG_INF)
    m1 = jnp.max(elog, axis=-1, keepdims=True)
    i1 = jnp.min(jnp.where(jnp.logical_and(emask, elog == m1), lane, big), axis=-1, keepdims=True)
    emask2 = jnp.logical_and(emask, lane != i1)
    elog2 = jnp.where(emask2, logits, NEG_INF)
    m2 = jnp.max(elog2, axis=-1, keepdims=True)
    i2 = jnp.min(jnp.where(jnp.logical_and(emask2, elog2 == m2), lane, big), axis=-1, keepdims=True)
    r = jnp.exp(m2 - m1)
    gate1 = gprob / (1.0 + r)
    gate2 = gprob * r / (1.0 + r)
    ex1 = i1 - N_GROUPS
    ex2 = i2 - N_GROUPS

    hit1 = lane == ex1
    hit2 = lane == ex2
    onehot = jnp.where(jnp.logical_or(hit1, hit2), 1.0, 0.0)
    pref = _dot(ls_ref[...], onehot.astype(BF16)) + carry_scr[...]
    rank1 = jnp.sum(jnp.where(hit1, pref, 0.0), axis=-1, keepdims=True)
    rank2 = jnp.sum(jnp.where(hit2, pref, 0.0), axis=-1, keepdims=True)
    carry = carry_scr[...] + jnp.sum(onehot, axis=0, keepdims=True)
    carry_scr[...] = carry
    cnt_ref[...] = jnp.broadcast_to(carry, cnt_ref.shape)

    route = jnp.zeros((tm, LANES), F32)
    for idx, val in enumerate((ex1, ex2, gate1, gate2, rank1, rank2)):
        route = jnp.where(lane == idx, val, route)
    route_ref[...] = route


def _post(og, z, os_, ls_, x2, gta, scf, shf, gnorm, anorm, wout, nffn, wrh, wrl, expand, lstrict, seq):
    t, d = x2.shape
    tm = TM_POST
    tps = seq // tm
    tile = lambda w: pl.BlockSpec((tm, w), lambda i: (i, 0))
    const2 = lambda a: pl.BlockSpec(a.shape, lambda i: (0, 0))
    mod_spec = pl.BlockSpec((1, 1, d), lambda i: (i // tps, 0, 0))
    return pl.pallas_call(
        _post_kernel,
        grid=(t // tm,),
        in_specs=[tile(GDN_V_W), tile(GDN_V_W)]
                 + [pl.BlockSpec((tm // dil, dil * ATTN_W), lambda i: (i, 0)) for _, dil in DILATED_PAIRS]
                 + [pl.BlockSpec((tm // dil, dil * LANES), lambda i: (i, 0)) for _, dil in DILATED_PAIRS]
                 + [tile(d),
                  mod_spec, mod_spec, mod_spec,
                  const2(gnorm), const2(anorm), const2(wout), const2(nffn),
                  const2(wrh), const2(wrl), const2(expand), const2(lstrict)],
        out_specs=[tile(d), tile(d), tile(LANES), pl.BlockSpec((8, LANES), lambda i: (0, 0))],
        out_shape=[jax.ShapeDtypeStruct((t, d), F32),
                   jax.ShapeDtypeStruct((t, d), F32),
                   jax.ShapeDtypeStruct((t, LANES), F32),
                   jax.ShapeDtypeStruct((8, LANES), F32)],
        scratch_shapes=[pltpu.VMEM((1, LANES), F32)]
                       + [pltpu.VMEM((ATTN_W // LANES, tm, LANES), F32) for _ in DILATED_PAIRS]
                       + [pltpu.VMEM((tm, LANES), F32) for _ in DILATED_PAIRS],
        compiler_params=_params(("arbitrary",)),
        name="post",
    )(og, z, *os_, *ls_, x2, gta, scf, shf, gnorm, anorm, wout, nffn, wrh, wrl, expand, lstrict)


def _dispatch_kernel(dest_ref, fill_ref, pad_ref, nused_ref, h2_ref, xs_ref, zero_scr, sem, fill_sem):
    i = pl.program_id(0)
    tm = h2_ref.shape[0]

    @pl.when(i == 0)
    def _():
        zero_scr[...] = jnp.zeros_like(zero_scr)

        def pieces(e, act):
            base, pad = fill_ref[e], pad_ref[e]
            head = pad & (SUBLANES - 1)
            for j in range(SUBLANES - 1):
                @pl.when(j < head)
                def _():
                    act(pltpu.make_async_copy(zero_scr.at[pl.ds(0, 1)], xs_ref.at[pl.ds(base + j, 1)],
                                              fill_sem))
            off = base + head
            for bit in range(SUBLANES.bit_length() - 1, EXPERT_BLK.bit_length() - 1):
                size = 1 << bit

                @pl.when((pad >> bit) & 1 == 1)
                def _():
                    act(pltpu.make_async_copy(zero_scr.at[pl.ds(0, size)],
                                              xs_ref.at[pl.ds(pl.multiple_of(off, SUBLANES), size)], fill_sem))
                off = off + (pad & size)

        def fill(e, carry):
            pieces(e, lambda cp: cp.start())
            return carry

        def drain(e, carry):
            pieces(e, lambda cp: cp.wait())
            return carry

        lax.fori_loop(0, N_EXPERTS, fill, 0)
        lax.fori_loop(0, N_EXPERTS, drain, 0)

        def tail(b, act):
            for half in range(EXPERT_BLK // zero_scr.shape[0]):
                row0 = pl.multiple_of(b * EXPERT_BLK + half * zero_scr.shape[0], SUBLANES)
                act(pltpu.make_async_copy(zero_scr, xs_ref.at[pl.ds(row0, zero_scr.shape[0])], fill_sem))

        def tail_fill(b, carry):
            tail(b, lambda cp: cp.start())
            return carry

        def tail_drain(b, carry):
            tail(b, lambda cp: cp.wait())
            return carry

        nblk = xs_ref.shape[0] // EXPERT_BLK
        lax.fori_loop(nused_ref[0], nblk, tail_fill, 0)
        lax.fori_loop(nused_ref[0], nblk, tail_drain, 0)

    def start(r, carry):
        for k in range(TOP_K):
            d = dest_ref[(i * tm + r) * TOP_K + k]
            pltpu.make_async_copy(h2_ref.at[pl.ds(r, 1)], xs_ref.at[pl.ds(d, 1)], sem).start()
        return carry

    lax.fori_loop(0, tm, start, 0, unroll=ROW_UNROLL)
    for k in range(TOP_K):
        pltpu.make_async_copy(h2_ref, xs_ref.at[pl.ds(0, tm)], sem).wait()


def _dispatch(dest, fill_start, pad_rows, nused, h2, p):
    t, d = h2.shape
    tm = TM_ROWS
    grid_spec = pltpu.PrefetchScalarGridSpec(
        num_scalar_prefetch=4,
        grid=(t // tm,),
        in_specs=[pl.BlockSpec((tm, d), lambda i, *_: (i, 0))],
        out_specs=pl.BlockSpec(memory_space=pl.ANY),
        scratch_shapes=[pltpu.VMEM((EXPERT_BLK // 2, d), F32), pltpu.SemaphoreType.DMA,
                        pltpu.SemaphoreType.DMA],
    )
    return pl.pallas_call(
        _dispatch_kernel,
        grid_spec=grid_spec,
        out_shape=jax.ShapeDtypeStruct((p, d), F32),
        compiler_params=_params(("arbitrary",)),
        name="dispatch",
    )(dest, fill_start, pad_rows, nused, h2)


def _expert_kernel(blk_e_ref, nused_ref, xs_ref, wg_ref, wu_ref, wd_ref, ys_ref):
    b = pl.program_id(0)

    @pl.when(b < nused_ref[0])
    def _():
        x = xs_ref[...].astype(BF16)
        g = _dot(x, wg_ref[0])
        u = _dot(x, wu_ref[0])
        hid = (g * _sigmoid(g)) * u
        ys_ref[...] = _dot(hid.astype(BF16), wd_ref[0])

    @pl.when(b >= nused_ref[0])
    def _():
        ys_ref[...] = jnp.zeros_like(ys_ref)


def _experts(blk_e, nused, xs, wg, wu, wd):
    p, d = xs.shape
    blk = EXPERT_BLK
    row_map = lambda b, be, nu: (jnp.minimum(b, nu[0] - 1), 0)
    w_map = lambda b, be, nu: (be[b], 0, 0)
    grid_spec = pltpu.PrefetchScalarGridSpec(
        num_scalar_prefetch=2,
        grid=(p // blk,),
        in_specs=[pl.BlockSpec((blk, d), row_map),
                  pl.BlockSpec((1, d, D_EXPERT), w_map),
                  pl.BlockSpec((1, d, D_EXPERT), w_map),
                  pl.BlockSpec((1, D_EXPERT, d), w_map)],
        out_specs=pl.BlockSpec((blk, d), lambda b, be, nu: (b, 0)),
    )
    return pl.pallas_call(
        _expert_kernel,
        grid_spec=grid_spec,
        out_shape=jax.ShapeDtypeStruct((p, d), F32),
        compiler_params=_params(("arbitrary",)),
        name="experts",
    )(blk_e, nused, xs, wg, wu, wd)


def _combine_kernel(final_norm, dest_ref, ys_ref, x1_ref, route_ref, gtf_ref, nf_ref, o_ref, ybuf, sem):
    i = pl.program_id(0)
    tm = x1_ref.shape[0]
    slot = i % 2

    def gather_tile(tile, slot_):
        def start(r, carry):
            for k in range(TOP_K):
                d = dest_ref[(tile * tm + r) * TOP_K + k]
                pltpu.make_async_copy(ys_ref.at[pl.ds(d, 1)], ybuf.at[slot_, k, pl.ds(r, 1)],
                                      sem.at[slot_]).start()
            return carry
        lax.fori_loop(0, tm, start, 0, unroll=ROW_UNROLL)

    @pl.when(i == 0)
    def _():
        gather_tile(0, 0)

    @pl.when(i + 1 < pl.num_programs(0))
    def _():
        gather_tile(i + 1, 1 - slot)

    for k in range(TOP_K):
        pltpu.make_async_copy(ys_ref.at[pl.ds(0, tm)], ybuf.at[slot, k], sem.at[slot]).wait()
    route = route_ref[...]
    moe = ybuf[slot, 0] * route[:, 2:3] + ybuf[slot, 1] * route[:, 3:4]
    x2 = x1_ref[...] + gtf_ref[0] * moe
    if final_norm:
        x2 = x2 * lax.rsqrt(jnp.mean(x2 * x2, axis=-1, keepdims=True) + EPS) * nf_ref[...]
    o_ref[...] = x2


def _combine(dest, ys, x1, route, gtf, nf, seq, final_norm):
    t, d = x1.shape
    tm = TM_ROWS
    tps = seq // tm
    grid_spec = pltpu.PrefetchScalarGridSpec(
        num_scalar_prefetch=1,
        grid=(t // tm,),
        in_specs=[pl.BlockSpec(memory_space=pl.ANY),
                  pl.BlockSpec((tm, d), lambda i, dest: (i, 0)),
                  pl.BlockSpec((tm, LANES), lambda i, dest: (i, 0)),
                  pl.BlockSpec((1, 1, d), lambda i, dest: (i // tps, 0, 0)),
                  pl.BlockSpec((1, d), lambda i, dest: (0, 0))],
        out_specs=pl.BlockSpec((tm, d), lambda i, dest: (i, 0)),
        scratch_shapes=[pltpu.VMEM((2, TOP_K, tm, d), F32), pltpu.SemaphoreType.DMA((2,))],
    )
    return pl.pallas_call(
        functools.partial(_combine_kernel, final_norm),
        grid_spec=grid_spec,
        out_shape=jax.ShapeDtypeStruct((t, d), F32),
        compiler_params=_params(("arbitrary",)),
        name="combine",
    )(dest, ys, x1, route, gtf, nf)


def _block_tri(n, chunk, lower):
    r = jnp.arange(n)[:, None]
    c = jnp.arange(n)[None, :]
    same = (r // chunk) == (c // chunk)
    tri = (r >= c) if lower else (r <= c)
    return jnp.where(same & tri, 1.0, 0.0).astype(BF16)


def _layer(x2, c, bsz, seq, w_ada, b_ada, norm_mix, w_in, w_conv, a_log, dt_bias, gdn_norm,
           attn_norm, w_out, rel_bias, norm_ffn, w_rg, w_re, w_gate, w_up, w_down):
    t, d = x2.shape
    mod = _adaln(c, w_ada, b_ada)
    sh_a, sc_a, gt_a, sh_f, sc_f, gt_f = [m.reshape(bsz, 1, d) for m in jnp.split(mod, 6, axis=-1)]

    s1 = GDN_CONV_W
    s2 = s1 + GDN_V_W
    s4 = s2 + 2 * GDN_HEADS
    small = jnp.pad(w_in[:, s2:s4], ((0, 0), (0, SMALL_W - 2 * GDN_HEADS)))
    wcat = jnp.concatenate([w_in[:, :s2], small, w_in[:, s4:]], axis=1).astype(BF16)
    qkva, z, ba, *qkvb = _inproj(x2, sc_a, sh_a, norm_mix.reshape(1, d), wcat, w_conv, seq)

    pad4 = lambda v: jnp.pad(v.astype(F32), (GDN_HEADS, LANES - 2 * GDN_HEADS))
    gp = jnp.zeros((8, LANES), F32).at[0].set(pad4(a_log)).at[1].set(pad4(dt_bias))
    og = _gdn(qkva, ba, ba[:, :8].T, gp, _gp_cols(a_log, dt_bias),
              _block_tri(GDN_TOK, CHUNK, True), _block_tri(GDN_TOK, CHUNK, False), seq)

    outs, lses = [], []
    for (window, dil), qkvb_d in zip(DILATED_PAIRS, qkvb):
        o_i, lse_i = _attn_branch(qkvb_d, jnp.asarray(_bucket_table(window, dil)), rel_bias.astype(F32),
                                  bsz, seq, dil)
        outs.append(o_i)
        lses.append(lse_i)

    expand = jnp.where((jnp.arange(LANES)[:, None] == jnp.arange(ATTN_W)[None, :] // ATTN_DH), 1.0, 0.0).astype(BF16)
    lstrict = jnp.where(jnp.arange(TM_POST)[:, None] > jnp.arange(TM_POST)[None, :], 1.0, 0.0).astype(BF16)
    wr = jnp.pad(jnp.concatenate([w_rg, w_re], axis=1).astype(F32), ((0, 0), (0, LANES - N_GROUPS - N_EXPERTS)))
    wrh = wr.astype(BF16)
    wrl = (wr - wrh.astype(F32)).astype(BF16)
    x1, h2, route, cnt = _post(og, z, outs, lses, x2, gt_a, sc_f, sh_f,
                               gdn_norm.reshape(1, GDN_DV), attn_norm.reshape(1, ATTN_W),
                               w_out.astype(BF16), norm_ffn.reshape(1, d), wrh, wrl, expand, lstrict, seq)

    blk = EXPERT_BLK
    counts = cnt[0, :N_EXPERTS].astype(jnp.int32)
    padded = (counts + blk - 1) // blk * blk
    pends = jnp.cumsum(padded)
    pstarts = pends - padded
    eids = route[:, 0:TOP_K].astype(jnp.int32)
    ranks = route[:, 4:4 + TOP_K].astype(jnp.int32)
    expert_ids = jnp.arange(N_EXPERTS, dtype=jnp.int32)
    seg_start = jnp.sum(jnp.where(eids[..., None] == expert_ids, pstarts, 0), axis=-1)
    dest = (seg_start + ranks).reshape(t * TOP_K)
    a = t * TOP_K
    p = -(-a // blk) * blk + N_EXPERTS * blk
    nblk = p // blk
    blk_start = jnp.arange(nblk, dtype=jnp.int32) * blk
    blk_e = jnp.minimum(jnp.sum((pends[None, :] <= blk_start[:, None]).astype(jnp.int32), axis=1),
                        N_EXPERTS - 1)
    nused = (pends[-1] // blk).astype(jnp.int32).reshape(1)

    xs = _dispatch(dest, pstarts + counts, padded - counts, nused, h2, p)
    ys = _experts(blk_e, nused, xs, w_gate.astype(BF16), w_up.astype(BF16), w_down.astype(BF16))
    return ys, dest, x1, route, gt_f


def _gp_cols(a_log, dt_bias):
    z = jnp.zeros((8, LANES), F32)
    z = z.at[GDN_HEADS:2 * GDN_HEADS, 0].set(a_log.astype(F32))
    z = z.at[GDN_HEADS:2 * GDN_HEADS, 1].set(dt_bias.astype(F32))
    return z


def kernel(x, c, w_ada, b_ada, norm_mix, w_in, w_conv, a_log, dt_bias, gdn_norm, attn_norm, w_out,
           rel_bias, norm_ffn, w_router_group, w_router_expert, w_gate, w_up, w_down, norm_final):
    bsz, seq, d = x.shape
    depth = w_ada.shape[0]
    x2 = x.reshape(bsz * seq, d)
    for l in range(depth):
        ys, dest, x1, route, gt_f = _layer(
            x2, c, bsz, seq, w_ada[l], b_ada[l], norm_mix[l], w_in[l], w_conv[l], a_log[l], dt_bias[l],
            gdn_norm[l], attn_norm[l], w_out[l], rel_bias, norm_ffn[l], w_router_group[l],
            w_router_expert[l], w_gate[l], w_up[l], w_down[l])
        x2 = _combine(dest, ys, x1, route, gt_f, norm_final.reshape(1, d), seq, l == depth - 1)
    return x2.reshape(bsz, seq, d)
```

```python
import functools
import math

import jax
import jax.numpy as jnp
import numpy as np
from jax import lax
from jax.experimental import pallas as pl
from jax.experimental.pallas import tpu as pltpu

D_MODEL = 1024
GDN_HEADS = 4
GDN_DK = 128
GDN_DV = 128
CONV_K = 4
CHUNK = 64
ATTN_HEADS = 8
ATTN_DH = 64
DILATED_PAIRS = ((128, 1), (512, 4), (2048, 16))
QBLK = 128
NUM_BUCKETS = 32
REL_MAX_DIST = 2048
N_GROUPS = 4
EXPERTS_PER_GROUP = 8
N_EXPERTS = N_GROUPS * EXPERTS_PER_GROUP
TOP_K = 2
D_EXPERT = 256
EPS = 1e-6
NEG_INF = -1e30

GDN_QK_W = GDN_HEADS * GDN_DK
GDN_V_W = GDN_HEADS * GDN_DV
ATTN_W = ATTN_HEADS * ATTN_DH
GDN_CONV_W = 2 * GDN_QK_W + GDN_V_W
LANES = 128
SUBLANES = 8
SMALL_W = LANES
W_COLS = GDN_CONV_W + GDN_V_W + SMALL_W + 3 * ATTN_W

TM_PROJ = 512
CONV_BLK = 128
HEAD_ROWS = 16
TM_POST = 256
TM_ROWS = 256
GDN_TOK = 512
GDN_GROUP = 4
INV_BASE = 8
ATTN_SUB = 2
EXPERT_BLK = 256
ROW_UNROLL = 8
VMEM_LIMIT = 56 * 1024 * 1024

F32 = jnp.float32
BF16 = jnp.bfloat16
HIGHEST = lax.Precision.HIGHEST


def _sigmoid(x):
    return 1.0 / (1.0 + jnp.exp(-x))


def _dot(a, b, precision=None):
    return jnp.dot(a, b, preferred_element_type=F32, precision=precision)


def _dot_nt(a, b, precision=None):
    return lax.dot_general(a, b, (((1,), (1,)), ((), ())), preferred_element_type=F32,
                           precision=precision)


def _dot_tn(a, b, precision=None):
    return lax.dot_general(a, b, (((0,), (0,)), ((), ())), preferred_element_type=F32,
                           precision=precision)


def _params(sem):
    return pltpu.CompilerParams(dimension_semantics=sem, vmem_limit_bytes=VMEM_LIMIT)


def _adaln_kernel(c_ref, w_ref, b_ref, o_ref):
    o_ref[...] = _dot(c_ref[...], w_ref[...], HIGHEST) + b_ref[...]


def _adaln(c, w, b):
    bsz, d = c.shape
    n = w.shape[1]
    tn = 1024
    return pl.pallas_call(
        _adaln_kernel,
        grid=(n // tn,),
        in_specs=[pl.BlockSpec((bsz, d), lambda j: (0, 0)),
                  pl.BlockSpec((d, tn), lambda j: (0, j)),
                  pl.BlockSpec((1, tn), lambda j: (0, j))],
        out_specs=pl.BlockSpec((bsz, tn), lambda j: (0, j)),
        out_shape=jax.ShapeDtypeStruct((bsz, n), F32),
        compiler_params=_params(("arbitrary",)),
        name="adaln",
    )(c, w, b.reshape(1, n))


def _inproj_kernel(tiles_per_seq, x_ref, sc_ref, sh_ref, g_ref, w_ref, wconv_ref, shift_ref,
                   qkva_ref, z_ref, ba_ref, *rest):
    qkvb_refs = rest[:len(DILATED_PAIRS)]
    win_scr, pb_scr, pb_next_scr = rest[len(DILATED_PAIRS):]
    i = pl.program_id(0)
    tm = x_ref.shape[0]
    x = x_ref[...]
    h = x * lax.rsqrt(jnp.mean(x * x, axis=-1, keepdims=True) + EPS) * g_ref[...]
    h = h * (1.0 + sc_ref[0]) + sh_ref[0]
    hb = h.astype(BF16)

    @pl.when(i % tiles_per_seq == 0)
    def _():
        win_scr[0:HEAD_ROWS, :] = jnp.zeros((HEAD_ROWS, GDN_CONV_W), F32)

    pa = _dot(hb, w_ref[:, 0:GDN_CONV_W])
    pa16 = pa.astype(BF16)
    for b in range(tm // CONV_BLK):
        rs = slice(b * CONV_BLK, (b + 1) * CONV_BLK)
        acc = pa[rs] * wconv_ref[CONV_K - 1:CONV_K, :]
        for j in range(CONV_K - 1):
            if b == 0:
                shifted = _dot(shift_ref[j, :, CONV_BLK:], pa16[rs])
            else:
                shifted = _dot(shift_ref[j], pa16[(b - 1) * CONV_BLK:(b + 1) * CONV_BLK])
            acc = acc + shifted * wconv_ref[j:j + 1, :]
        qkva_ref[rs, :] = (acc * _sigmoid(acc)).astype(qkva_ref.dtype)
    win_scr[HEAD_ROWS:2 * HEAD_ROWS, :] = pa[0:HEAD_ROWS]
    acc = pa[0:HEAD_ROWS] * wconv_ref[CONV_K - 1:CONV_K, :]
    for j in range(CONV_K - 1):
        acc = acc + win_scr[pl.ds(HEAD_ROWS - (CONV_K - 1) + j, HEAD_ROWS), :] * wconv_ref[j:j + 1, :]
    qkva_ref[0:HEAD_ROWS, :] = (acc * _sigmoid(acc)).astype(qkva_ref.dtype)
    win_scr[0:HEAD_ROWS, :] = pa[tm - HEAD_ROWS:tm]

    c0 = GDN_CONV_W
    z_ref[...] = _dot(hb, w_ref[:, c0:c0 + GDN_V_W]).astype(z_ref.dtype)
    c0 += GDN_V_W
    ba_ref[...] = _dot(hb, w_ref[:, c0:c0 + SMALL_W])
    c0 += SMALL_W
    pb = _dot(hb, w_ref[:, c0:c0 + 3 * ATTN_W])
    ncol = 3 * ATTN_W // LANES
    for j in range(ncol):
        pb_scr[j] = pb[:, j * LANES:(j + 1) * LANES]
    prev_dil, prev_scr, next_scr = 1, pb_scr, pb_next_scr
    for bi, ((_, dil), ref) in enumerate(zip(DILATED_PAIRS, qkvb_refs)):
        if dil == 1:
            ref[...] = pb.astype(ref.dtype)
            continue
        f = dil // prev_dil
        n = tm // dil
        keep = bi + 1 < len(DILATED_PAIRS)
        for r in range(prev_dil):
            for q in range(f):
                r_new = r + prev_dil * q
                for j in range(ncol):
                    rows = prev_scr[j, pl.ds(r * (tm // prev_dil) + q, n, stride=f), :]
                    c1 = r_new * 3 * ATTN_W + j * LANES
                    ref[:, c1:c1 + LANES] = rows.astype(ref.dtype)
                    if keep:
                        next_scr[j, r_new * n:(r_new + 1) * n, :] = rows
        prev_dil, prev_scr, next_scr = dil, next_scr, prev_scr


def _inproj(x2, sc, sh, g, wcat, wconv, seq):
    t, d = x2.shape
    tm = TM_PROJ
    tps = seq // tm
    mod_spec = pl.BlockSpec((1, 1, d), lambda i: (i // tps, 0, 0))
    return pl.pallas_call(
        functools.partial(_inproj_kernel, tps),
        grid=(t // tm,),
        in_specs=[pl.BlockSpec((tm, d), lambda i: (i, 0)),
                  mod_spec, mod_spec,
                  pl.BlockSpec((1, d), lambda i: (0, 0)),
                  pl.BlockSpec((d, W_COLS), lambda i: (0, 0)),
                  pl.BlockSpec((CONV_K, GDN_CONV_W), lambda i: (0, 0)),
                  pl.BlockSpec((CONV_K - 1, CONV_BLK, 2 * CONV_BLK), lambda i: (0, 0, 0))],
        out_specs=[pl.BlockSpec((tm, GDN_CONV_W), lambda i: (i, 0)),
                   pl.BlockSpec((tm, GDN_V_W), lambda i: (i, 0)),
                   pl.BlockSpec((tm, SMALL_W), lambda i: (i, 0))]
                  + [pl.BlockSpec((tm // dil, dil * 3 * ATTN_W), lambda i: (i, 0)) for _, dil in DILATED_PAIRS],
        out_shape=[jax.ShapeDtypeStruct((t, GDN_CONV_W), BF16),
                   jax.ShapeDtypeStruct((t, GDN_V_W), BF16),
                   jax.ShapeDtypeStruct((t, SMALL_W), F32)]
                  + [jax.ShapeDtypeStruct((t // dil, dil * 3 * ATTN_W), BF16) for _, dil in DILATED_PAIRS],
        scratch_shapes=[pltpu.VMEM((2 * HEAD_ROWS, GDN_CONV_W), F32),
                        pltpu.VMEM((3 * ATTN_W // LANES, tm, LANES), F32),
                        pltpu.VMEM((3 * ATTN_W // LANES, tm, LANES), F32)],
        compiler_params=_params(("arbitrary",)),
        name="inproj",
    )(x2, sc, sh, g, wcat, wconv, _shift_bands())


def _shift_bands():
    t = np.arange(CONV_BLK)[:, None]
    c = np.arange(2 * CONV_BLK)[None, :]
    bands = [(c == CONV_BLK + t - (CONV_K - 1 - j)) for j in range(CONV_K - 1)]
    return jnp.asarray(np.stack(bands).astype(np.float32), dtype=BF16)


def _softplus(x):
    return jnp.maximum(x, 0.0) + jnp.log(1.0 + jnp.exp(-jnp.abs(x)))


def _unit_lower_inverses(n_list):
    c = n_list[0].shape[0]
    row = lax.broadcasted_iota(jnp.int32, (c, c), 0)
    col = lax.broadcasted_iota(jnp.int32, (c, c), 1)
    eye = jnp.where(row == col, 1.0, 0.0)
    same_base = (row // INV_BASE) == (col // INV_BASE)
    n0 = [jnp.where(same_base, n, 0.0) for n in n_list]
    p = [eye - x for x in n0]
    m = [x.astype(BF16) for x in n0]
    for j in range(int(math.log2(INV_BASE)) - 1):
        m = [_dot(x, x).astype(BF16) for x in m]
        p = [x + _dot(x.astype(BF16), y) for x, y in zip(p, m)]
    size = INV_BASE
    while size < c:
        sibling = jnp.logical_and((row // size) % 2 == 1, (col // size) == (row // size) - 1)
        cb = [jnp.where(sibling, n, 0.0).astype(BF16) for n in n_list]
        pb = [x.astype(BF16) for x in p]
        pc = [_dot(x, y).astype(BF16) for x, y in zip(pb, cb)]
        p = [x - _dot(y, z) for x, y, z in zip(p, pc, pb)]
        size *= 2
    return p


def _split3_bf16(x):
    h1 = x.astype(BF16)
    r1 = x - h1.astype(F32)
    h2 = r1.astype(BF16)
    h3 = (r1 - h2.astype(F32)).astype(BF16)
    return h1, h2, h3


def _gdn_kernel(steps_per_seq, qkv_ref, ba_ref, bat_ref, gp_ref, gpt_ref, lblk_ref, ublk_ref,
                o_ref, state_scr):
    i = pl.program_id(0)

    @pl.when(i % steps_per_seq == 0)
    def _():
        state_scr[...] = jnp.zeros_like(state_scr)

    nchunk = GDN_TOK // CHUNK
    ba = ba_ref[...]
    bat = bat_ref[...]
    a_vec = -jnp.exp(gp_ref[0:1, :])
    g_tile = a_vec * _softplus(ba + gp_ref[1:2, :])
    lblk = lblk_ref[...]
    g_cum = sum(_dot(lblk, part) for part in _split3_bf16(g_tile))
    a_col = -jnp.exp(gpt_ref[:, 0:1])
    gt_tile = a_col * _softplus(bat + gpt_ref[:, 1:2])
    ublk = ublk_ref[...]
    gt_cum = sum(_dot(part, ublk) for part in _split3_bf16(gt_tile))
    beta_tile = _sigmoid(ba)

    row = lax.broadcasted_iota(jnp.int32, (CHUNK, CHUNK), 0)
    col = lax.broadcasted_iota(jnp.int32, (CHUNK, CHUNK), 1)
    incl = row >= col
    strict = row > col

    for h0 in range(0, GDN_HEADS, GDN_GROUP):
        heads = range(h0, h0 + GDN_GROUP)
        kb_l, k_l, q_l, rhs_l, qg_l, kg_l, gl_l, inc_l, str_l = ([] for _ in range(9))
        for h in heads:
            q_raw = qkv_ref[:, h * GDN_DK:(h + 1) * GDN_DK].astype(F32)
            k_raw = qkv_ref[:, GDN_QK_W + h * GDN_DK:GDN_QK_W + (h + 1) * GDN_DK].astype(F32)
            v = qkv_ref[:, 2 * GDN_QK_W + h * GDN_DV:2 * GDN_QK_W + (h + 1) * GDN_DV].astype(F32)
            qn = q_raw * lax.rsqrt(jnp.sum(q_raw * q_raw, axis=-1, keepdims=True) + EPS) * (GDN_DK ** -0.5)
            kn = k_raw * lax.rsqrt(jnp.sum(k_raw * k_raw, axis=-1, keepdims=True) + EPS)
            beta = beta_tile[:, h:h + 1]
            gc_all = g_cum[:, GDN_HEADS + h:GDN_HEADS + h + 1]
            eg = jnp.exp(gc_all)
            kb = kn * beta
            rhs = jnp.concatenate([v * beta, kb * eg], axis=1).astype(BF16)
            qg = qn * eg
            kb16, k16, q16 = kb.astype(BF16), kn.astype(BF16), qn.astype(BF16)
            for c in range(nchunk):
                sl = slice(c * CHUNK, (c + 1) * CHUNK)
                g_c = gc_all[sl]
                g_r = gt_cum[GDN_HEADS + h:GDN_HEADS + h + 1, sl]
                dec = jnp.exp(g_c - g_r)
                inc_l.append(jnp.where(incl, dec, 0.0))
                str_l.append(jnp.where(strict, dec, 0.0))
                g_last = g_c[CHUNK - 1:CHUNK, :]
                kg_l.append((kn[sl] * jnp.exp(g_last - g_c)).astype(BF16))
                gl_l.append(jnp.exp(g_last))
                kb_l.append(kb16[sl])
                k_l.append(k16[sl])
                q_l.append(q16[sl])
                rhs_l.append(rhs[sl])
                qg_l.append(qg[sl])
        nprob = len(k_l)
        n_l = [_dot_nt(kb_l[j], k_l[j]) * str_l[j] for j in range(nprob)]
        aqk_l = [(_dot_nt(q_l[j], k_l[j]) * inc_l[j]).astype(BF16) for j in range(nprob)]
        tinv_l = _unit_lower_inverses(n_l)
        sol_l = [_dot(tinv_l[j].astype(BF16), rhs_l[j]).astype(BF16) for j in range(nprob)]
        a2_l = [_dot(aqk_l[j], sol_l[j]) for j in range(nprob)]
        k2_l = [_dot_tn(kg_l[j], sol_l[j]) for j in range(nprob)]
        states = [state_scr[h] for h in heads]
        for c in range(nchunk):
            sl = slice(c * CHUNK, (c + 1) * CHUNK)
            for hi, h in enumerate(heads):
                j = hi * nchunk + c
                sb = states[hi].astype(BF16)
                qeff = (qg_l[j] - a2_l[j][:, GDN_DV:]).astype(BF16)
                o = _dot(qeff, sb) + a2_l[j][:, :GDN_DV]
                states[hi] = gl_l[j] * states[hi] + k2_l[j][:, :GDN_DV] - _dot(k2_l[j][:, GDN_DV:].astype(BF16), sb)
                o_ref[sl, h * GDN_DV:(h + 1) * GDN_DV] = o.astype(o_ref.dtype)
        for hi, h in enumerate(heads):
            state_scr[h] = states[hi]


def _gdn(qkva, ba, bat, gp, gpt, lblk, ublk, seq):
    t = qkva.shape[0]
    tok = GDN_TOK
    sps = seq // tok
    const = lambda i: (0, 0)
    return pl.pallas_call(
        functools.partial(_gdn_kernel, sps),
        grid=(t // tok,),
        in_specs=[pl.BlockSpec((tok, GDN_CONV_W), lambda i: (i, 0)),
                  pl.BlockSpec((tok, SMALL_W), lambda i: (i, 0)),
                  pl.BlockSpec((8, tok), lambda i: (0, i)),
                  pl.BlockSpec((8, LANES), const),
                  pl.BlockSpec((8, LANES), const),
                  pl.BlockSpec((tok, tok), const),
                  pl.BlockSpec((tok, tok), const)],
        out_specs=pl.BlockSpec((tok, GDN_V_W), lambda i: (i, 0)),
        out_shape=jax.ShapeDtypeStruct((t, GDN_V_W), F32),
        scratch_shapes=[pltpu.VMEM((GDN_HEADS, GDN_DK, GDN_DV), F32)],
        compiler_params=_params(("arbitrary",)),
        name="gdn",
    )(qkva, ba, bat, gp, gpt, lblk, ublk)


def _attn_kernel(q_ref, kp_ref, kc_ref, vp_ref, vc_ref, bucket_ref, rb_ref, o_ref, lse_ref, bias_ref):
    n = pl.program_id(2)

    @pl.when(jnp.logical_and(jnp.logical_and(pl.program_id(0) == 0, pl.program_id(1) == 0), n == 0))
    def _():
        bucket = bucket_ref[...]
        col = lax.broadcasted_iota(jnp.int32, (QBLK, 2 * QBLK), 1)
        for h in range(ATTN_HEADS):
            acc = jnp.full((QBLK, 2 * QBLK), NEG_INF, F32)
            for b in range(NUM_BUCKETS):
                acc = jnp.where(bucket == b, rb_ref[b, h], acc)
            bias_ref[0, h] = acc
            bias_ref[1, h] = jnp.where(col < QBLK, NEG_INF, acc)

    nsub = q_ref.shape[1] // QBLK
    first = jnp.where(n == 0, 1, 0)
    q_all = q_ref[0]
    k_all = jnp.concatenate([kp_ref[0], kc_ref[0]], axis=0)
    v_all = jnp.concatenate([vp_ref[0], vc_ref[0]], axis=0)
    lane = lax.broadcasted_iota(jnp.int32, (QBLK, LANES), 1)
    ones = jnp.ones((2 * QBLK, LANES), BF16)
    per_group = LANES // ATTN_DH
    ngrp = ATTN_HEADS // per_group
    scale = ATTN_DH ** -0.5
    mine = [jnp.logical_and(lane >= j * ATTN_DH, lane < (j + 1) * ATTN_DH) for j in range(per_group)]
    probs = [(u, h) for u in range(nsub) for h in range(ATTN_HEADS)]
    s_l = []
    for u, h in probs:
        grp, j = divmod(h, per_group)
        gs = slice(grp * LANES, (grp + 1) * LANES)
        qg = q_all[u * QBLK:(u + 1) * QBLK, gs] * scale
        table = first if u == 0 else 0
        s_l.append(_dot_nt(jnp.where(mine[j], qg, jnp.zeros_like(qg)), k_all[u * QBLK:(u + 2) * QBLK, gs])
                   + bias_ref[table, h])
    m_l = [jnp.max(s, axis=-1, keepdims=True) for s in s_l]
    p_l = [jnp.exp(s - m).astype(BF16) for s, m in zip(s_l, m_l)]
    v_ext = {(u, grp): jnp.concatenate([v_all[u * QBLK:(u + 2) * QBLK, grp * LANES:(grp + 1) * LANES], ones],
                                       axis=1)
             for u in range(nsub) for grp in range(ngrp)}
    pv_l = [_dot(p, v_ext[(u, h // per_group)]) for p, (u, h) in zip(p_l, probs)]
    for u in range(nsub):
        outs = []
        lse_tile = jnp.zeros((QBLK, LANES), F32)
        for grp in range(ngrp):
            o_grp = jnp.zeros((QBLK, LANES), F32)
            for j in range(per_group):
                h = grp * per_group + j
                idx = u * ATTN_HEADS + h
                den = pv_l[idx][:, LANES:]
                o_grp = jnp.where(mine[j], pv_l[idx][:, :LANES] / den, o_grp)
                lse_tile = jnp.where(lane == h, m_l[idx] + jnp.log(den[:, 0:1]), lse_tile)
            outs.append(o_grp)
        o_ref[0, u * QBLK:(u + 1) * QBLK, :] = jnp.concatenate(outs, axis=1).astype(o_ref.dtype)
        lse_ref[0, u * QBLK:(u + 1) * QBLK, :] = lse_tile


def _attn_branch(qkvb, bucket, rel_bias, bsz, seq, dil):
    ln = seq // dil
    rows = ATTN_SUB * QBLK
    assert seq % dil == 0 and ln % rows == 0, (seq, dil, rows)
    nb = ln // rows
    xv = qkvb.reshape(bsz, ln, dil * 3 * ATTN_W)
    w = ATTN_W
    cur = lambda part: (lambda b, r, n: (b, n, r * 3 + part))
    prev = lambda part: (lambda b, r, n: (b, jnp.maximum(n * ATTN_SUB - 1, 0), r * 3 + part))
    o, lse = pl.pallas_call(
        _attn_kernel,
        grid=(bsz, dil, nb),
        in_specs=[pl.BlockSpec((1, rows, w), cur(0)),
                  pl.BlockSpec((1, QBLK, w), prev(1)),
                  pl.BlockSpec((1, rows, w), cur(1)),
                  pl.BlockSpec((1, QBLK, w), prev(2)),
                  pl.BlockSpec((1, rows, w), cur(2)),
                  pl.BlockSpec((QBLK, 2 * QBLK), lambda b, r, n: (0, 0)),
                  pl.BlockSpec(memory_space=pltpu.SMEM)],
        out_specs=[pl.BlockSpec((1, rows, w), lambda b, r, n: (b, n, r)),
                   pl.BlockSpec((1, rows, LANES), lambda b, r, n: (b, n, r))],
        out_shape=[jax.ShapeDtypeStruct((bsz, ln, dil * w), BF16),
                   jax.ShapeDtypeStruct((bsz, ln, dil * LANES), F32)],
        scratch_shapes=[pltpu.VMEM((2, ATTN_HEADS, QBLK, 2 * QBLK), F32)],
        compiler_params=_params(("arbitrary", "arbitrary", "arbitrary")),
        name=f"attn_d{dil}",
    )(xv, xv, xv, xv, xv, bucket, rel_bias)
    return o.reshape(bsz * ln, dil * w), lse.reshape(bsz * ln, dil * LANES)


def _bucket_table(window, dil):
    steps = window // dil
    qi = np.arange(QBLK)[:, None]
    kj = np.arange(2 * QBLK)[None, :]
    delta = qi + QBLK - kj
    dist = np.maximum(delta, 0) * dil
    max_exact = NUM_BUCKETS // 2
    d_f = np.maximum(dist, 1).astype(np.float32)
    large = max_exact + (np.log(d_f / max_exact) / math.log(REL_MAX_DIST / max_exact)
                         * (NUM_BUCKETS - max_exact)).astype(np.int32)
    bucket = np.where(dist < max_exact, dist, np.minimum(large, NUM_BUCKETS - 1))
    return np.where((delta >= 0) & (delta <= steps), bucket, -1).astype(np.int32)


def _split_bf16(x):
    hi = x.astype(BF16)
    lo = (x - hi.astype(F32)).astype(BF16)
    return hi, lo


def _post_kernel(og_ref, z_ref, o1_ref, o2_ref, o3_ref, l1_ref, l2_ref, l3_ref, x_ref,
                 gta_ref, scf_ref, shf_ref, gnorm_ref, anorm_ref, wout_ref, nffn_ref,
                 wrh_ref, exp_ref, ls_ref,
                 x1_ref, h2_ref, route_ref, cnt_ref, carry_scr, *tok_scr):
    i = pl.program_id(0)
    tm = x_ref.shape[0]

    @pl.when(i == 0)
    def _():
        carry_scr[...] = jnp.zeros_like(carry_scr)

    nbr = len(DILATED_PAIRS)
    ob_scr, lse_scr = tok_scr[:nbr], tok_scr[nbr:]
    for (_, dil), o_ref, l_ref, o_s, l_s in zip(DILATED_PAIRS, (o1_ref, o2_ref, o3_ref),
                                                 (l1_ref, l2_ref, l3_ref), ob_scr, lse_scr):
        for r in range(dil):
            rows = pl.ds(r, tm // dil, stride=dil) if dil > 1 else slice(None)
            for j in range(ATTN_W // LANES):
                c1 = r * ATTN_W + j * LANES
                o_s[j, rows, :] = o_ref[:, c1:c1 + LANES].astype(F32)
            l_s[rows, :] = l_ref[:, r * LANES:(r + 1) * LANES]

    heads = []
    for h in range(GDN_HEADS):
        hs = slice(h * GDN_DV, (h + 1) * GDN_DV)
        seg = og_ref[:, hs]
        nrm = seg * lax.rsqrt(jnp.mean(seg * seg, axis=-1, keepdims=True) + EPS) * gnorm_ref[...]
        zz = z_ref[:, hs].astype(F32)
        heads.append((nrm * (zz * _sigmoid(zz))).astype(BF16))
    oa = jnp.concatenate(heads, axis=1)

    l1, l2, l3 = (l_s[...] for l_s in lse_scr)
    m = jnp.maximum(jnp.maximum(l1, l2), l3)
    e1, e2, e3 = jnp.exp(l1 - m), jnp.exp(l2 - m), jnp.exp(l3 - m)
    inv = 1.0 / (e1 + e2 + e3)
    ob = jnp.zeros((tm, ATTN_W), F32)
    for e, o_s in zip((e1, e2, e3), ob_scr):
        hi, lo = _split_bf16(e * inv)
        wexp = _dot(hi, exp_ref[...]) + _dot(lo, exp_ref[...])
        ob = ob + wexp * jnp.concatenate([o_s[j] for j in range(ATTN_W // LANES)], axis=1)
    ob = ob * lax.rsqrt(jnp.mean(ob * ob, axis=-1, keepdims=True) + EPS) * anorm_ref[...]

    mix = _dot(oa, wout_ref[0:GDN_V_W, :]) + _dot(ob.astype(BF16), wout_ref[GDN_V_W:, :])
    x1 = x_ref[...] + gta_ref[0] * mix
    x1_ref[...] = x1
    h2 = x1 * lax.rsqrt(jnp.mean(x1 * x1, axis=-1, keepdims=True) + EPS) * nffn_ref[...]
    h2 = h2 * (1.0 + scf_ref[0]) + shf_ref[0]
    h2_ref[...] = h2

    hh, hl = _split_bf16(h2)
    both = _dot(hh, wrh_ref[...])
    logits = both[:, :LANES] + both[:, LANES:] + _dot(hl, wrh_ref[:, :LANES])
    tm = logits.shape[0]
    lane = lax.broadcasted_iota(jnp.int32, (tm, LANES), 1).astype(F32)
    big = float(LANES)
    gmask = lane < N_GROUPS
    glog = jnp.where(gmask, logits, NEG_INF)
    gmax = jnp.max(glog, axis=-1, keepdims=True)
    gidx = jnp.min(jnp.where(jnp.logical_and(gmask, glog == gmax), lane, big), axis=-1, keepdims=True)
    gprob = 1.0 / jnp.sum(jnp.where(gmask, jnp.exp(glog - gmax), 0.0), axis=-1, keepdims=True)
    lo_lane = N_GROUPS + EXPERTS_PER_GROUP * gidx
    emask = jnp.logical_and(lane >= lo_lane, lane < lo_lane + EXPERTS_PER_GROUP)
    elog = jnp.where(emask, logits, NEG_INF)
    m1 = jnp.max(elog, axis=-1, keepdims=True)
    i1 = jnp.min(jnp.where(jnp.logical_and(emask, elog == m1), lane, big), axis=-1, keepdims=True)
    emask2 = jnp.logical_and(emask, lane != i1)
    elog2 = jnp.where(emask2, logits, NEG_INF)
    m2 = jnp.max(elog2, axis=-1, keepdims=True)
    i2 = jnp.min(jnp.where(jnp.logical_and(emask2, elog2 == m2), lane, big), axis=-1, keepdims=True)
    r = jnp.exp(m2 - m1)
    gate1 = gprob / (1.0 + r)
    gate2 = gprob * r / (1.0 + r)
    ex1 = i1 - N_GROUPS
    ex2 = i2 - N_GROUPS

    hit1 = lane == ex1
    hit2 = lane == ex2
    onehot = jnp.where(jnp.logical_or(hit1, hit2), 1.0, 0.0)
    pref = _dot(ls_ref[...], onehot.astype(BF16)) + carry_scr[...]
    rank1 = jnp.sum(jnp.where(hit1, pref, 0.0), axis=-1, keepdims=True)
    rank2 = jnp.sum(jnp.where(hit2, pref, 0.0), axis=-1, keepdims=True)
    carry = carry_scr[...] + jnp.sum(onehot, axis=0, keepdims=True)
    carry_scr[...] = carry
    cnt_ref[...] = jnp.broadcast_to(carry, cnt_ref.shape)

    route = jnp.zeros((tm, LANES), F32)
    for idx, val in enumerate((ex1, ex2, gate1, gate2, rank1, rank2)):
        route = jnp.where(lane == idx, val, route)
    route_ref[...] = route


def _post(og, z, os_, ls_, x2, gta, scf, shf, gnorm, anorm, wout, nffn, wrh, expand, lstrict, seq):
    t, d = x2.shape
    tm = TM_POST
    tps = seq // tm
    tile = lambda w: pl.BlockSpec((tm, w), lambda i: (i, 0))
    const2 = lambda a: pl.BlockSpec(a.shape, lambda i: (0, 0))
    mod_spec = pl.BlockSpec((1, 1, d), lambda i: (i // tps, 0, 0))
    return pl.pallas_call(
        _post_kernel,
        grid=(t // tm,),
        in_specs=[tile(GDN_V_W), tile(GDN_V_W)]
                 + [pl.BlockSpec((tm // dil, dil * ATTN_W), lambda i: (i, 0)) for _, dil in DILATED_PAIRS]
                 + [pl.BlockSpec((tm // dil, dil * LANES), lambda i: (i, 0)) for _, dil in DILATED_PAIRS]
                 + [tile(d),
                  mod_spec, mod_spec, mod_spec,
                  const2(gnorm), const2(anorm), const2(wout), const2(nffn),
                  const2(wrh), const2(expand), const2(lstrict)],
        out_specs=[tile(d), tile(d), tile(LANES), pl.BlockSpec((8, LANES), lambda i: (0, 0))],
        out_shape=[jax.ShapeDtypeStruct((t, d), F32),
                   jax.ShapeDtypeStruct((t, d), F32),
                   jax.ShapeDtypeStruct((t, LANES), F32),
                   jax.ShapeDtypeStruct((8, LANES), F32)],
        scratch_shapes=[pltpu.VMEM((1, LANES), F32)]
                       + [pltpu.VMEM((ATTN_W // LANES, tm, LANES), F32) for _ in DILATED_PAIRS]
                       + [pltpu.VMEM((tm, LANES), F32) for _ in DILATED_PAIRS],
        compiler_params=_params(("arbitrary",)),
        name="post",
    )(og, z, *os_, *ls_, x2, gta, scf, shf, gnorm, anorm, wout, nffn, wrh, expand, lstrict)


def _dispatch_kernel(dest_ref, fill_ref, pad_ref, nused_ref, h2_ref, xs_ref, zero_scr, sem, fill_sem):
    i = pl.program_id(0)
    tm = h2_ref.shape[0]

    @pl.when(i == 0)
    def _():
        zero_scr[...] = jnp.zeros_like(zero_scr)

        def pieces(e, act):
            base, pad = fill_ref[e], pad_ref[e]
            head = pad & (SUBLANES - 1)
            for j in range(SUBLANES - 1):
                @pl.when(j < head)
                def _():
                    act(pltpu.make_async_copy(zero_scr.at[pl.ds(0, 1)], xs_ref.at[pl.ds(base + j, 1)],
                                              fill_sem))
            off = base + head
            for bit in range(SUBLANES.bit_length() - 1, EXPERT_BLK.bit_length() - 1):
                size = 1 << bit

                @pl.when((pad >> bit) & 1 == 1)
                def _():
                    act(pltpu.make_async_copy(zero_scr.at[pl.ds(0, size)],
                                              xs_ref.at[pl.ds(pl.multiple_of(off, SUBLANES), size)], fill_sem))
                off = off + (pad & size)

        def fill(e, carry):
            pieces(e, lambda cp: cp.start())
            return carry

        def drain(e, carry):
            pieces(e, lambda cp: cp.wait())
            return carry

        lax.fori_loop(0, N_EXPERTS, fill, 0)
        lax.fori_loop(0, N_EXPERTS, drain, 0)

        def tail(b, act):
            for half in range(EXPERT_BLK // zero_scr.shape[0]):
                row0 = pl.multiple_of(b * EXPERT_BLK + half * zero_scr.shape[0], SUBLANES)
                act(pltpu.make_async_copy(zero_scr, xs_ref.at[pl.ds(row0, zero_scr.shape[0])], fill_sem))

        def tail_fill(b, carry):
            tail(b, lambda cp: cp.start())
            return carry

        def tail_drain(b, carry):
            tail(b, lambda cp: cp.wait())
            return carry

        nblk = xs_ref.shape[0] // EXPERT_BLK
        lax.fori_loop(nused_ref[0], nblk, tail_fill, 0)
        lax.fori_loop(nused_ref[0], nblk, tail_drain, 0)

    def start(r, carry):
        for k in range(TOP_K):
            d = dest_ref[(i * tm + r) * TOP_K + k]
            pltpu.make_async_copy(h2_ref.at[pl.ds(r, 1)], xs_ref.at[pl.ds(d, 1)], sem).start()
        return carry

    lax.fori_loop(0, tm, start, 0, unroll=ROW_UNROLL)
    for k in range(TOP_K):
        pltpu.make_async_copy(h2_ref, xs_ref.at[pl.ds(0, tm)], sem).wait()


def _dispatch(dest, fill_start, pad_rows, nused, h2, p):
    t, d = h2.shape
    tm = TM_ROWS
    grid_spec = pltpu.PrefetchScalarGridSpec(
        num_scalar_prefetch=4,
        grid=(t // tm,),
        in_specs=[pl.BlockSpec((tm, d), lambda i, *_: (i, 0))],
        out_specs=pl.BlockSpec(memory_space=pl.ANY),
        scratch_shapes=[pltpu.VMEM((EXPERT_BLK // 2, d), F32), pltpu.SemaphoreType.DMA,
                        pltpu.SemaphoreType.DMA],
    )
    return pl.pallas_call(
        _dispatch_kernel,
        grid_spec=grid_spec,
        out_shape=jax.ShapeDtypeStruct((p, d), F32),
        compiler_params=_params(("arbitrary",)),
        name="dispatch",
    )(dest, fill_start, pad_rows, nused, h2)


def _expert_kernel(blk_e_ref, nused_ref, xs_ref, wg_ref, wu_ref, wd_ref, ys_ref, wg16, wu16, wd16):
    b = pl.program_id(0)
    used = b < nused_ref[0]

    new_expert = jnp.logical_or(b == 0, blk_e_ref[b] != blk_e_ref[jnp.maximum(b - 1, 0)])

    @pl.when(jnp.logical_and(used, new_expert))
    def _():
        wg16[...] = wg_ref[0].astype(BF16)
        wu16[...] = wu_ref[0].astype(BF16)
        wd16[...] = wd_ref[0].astype(BF16)

    @pl.when(used)
    def _():
        x = xs_ref[...].astype(BF16)
        g = _dot(x, wg16[...])
        u = _dot(x, wu16[...])
        hid = (g * _sigmoid(g)) * u
        ys_ref[...] = _dot(hid.astype(BF16), wd16[...])

    @pl.when(b >= nused_ref[0])
    def _():
        ys_ref[...] = jnp.zeros_like(ys_ref)


def _experts(blk_e, nused, xs, wg, wu, wd):
    p, d = xs.shape
    blk = EXPERT_BLK
    row_map = lambda b, be, nu: (jnp.minimum(b, nu[0] - 1), 0)
    w_map = lambda b, be, nu: (be[b], 0, 0)
    grid_spec = pltpu.PrefetchScalarGridSpec(
        num_scalar_prefetch=2,
        grid=(p // blk,),
        in_specs=[pl.BlockSpec((blk, d), row_map),
                  pl.BlockSpec((1, d, D_EXPERT), w_map),
                  pl.BlockSpec((1, d, D_EXPERT), w_map),
                  pl.BlockSpec((1, D_EXPERT, d), w_map)],
        out_specs=pl.BlockSpec((blk, d), lambda b, be, nu: (b, 0)),
        scratch_shapes=[pltpu.VMEM((d, D_EXPERT), BF16), pltpu.VMEM((d, D_EXPERT), BF16),
                        pltpu.VMEM((D_EXPERT, d), BF16)],
    )
    return pl.pallas_call(
        _expert_kernel,
        grid_spec=grid_spec,
        out_shape=jax.ShapeDtypeStruct((p, d), F32),
        compiler_params=_params(("arbitrary",)),
        name="experts",
    )(blk_e, nused, xs, wg, wu, wd)


def _combine_kernel(final_norm, dest_ref, ys_ref, x1_ref, route_ref, gtf_ref, nf_ref, o_ref, ybuf, sem):
    i = pl.program_id(0)
    tm = x1_ref.shape[0]
    slot = i % 2

    def gather_tile(tile, slot_):
        def start(r, carry):
            for k in range(TOP_K):
                d = dest_ref[(tile * tm + r) * TOP_K + k]
                pltpu.make_async_copy(ys_ref.at[pl.ds(d, 1)], ybuf.at[slot_, k, pl.ds(r, 1)],
                                      sem.at[slot_]).start()
            return carry
        lax.fori_loop(0, tm, start, 0, unroll=ROW_UNROLL)

    @pl.when(i == 0)
    def _():
        gather_tile(0, 0)

    @pl.when(i + 1 < pl.num_programs(0))
    def _():
        gather_tile(i + 1, 1 - slot)

    for k in range(TOP_K):
        pltpu.make_async_copy(ys_ref.at[pl.ds(0, tm)], ybuf.at[slot, k], sem.at[slot]).wait()
    route = route_ref[...]
    moe = ybuf[slot, 0] * route[:, 2:3] + ybuf[slot, 1] * route[:, 3:4]
    x2 = x1_ref[...] + gtf_ref[0] * moe
    if final_norm:
        x2 = x2 * lax.rsqrt(jnp.mean(x2 * x2, axis=-1, keepdims=True) + EPS) * nf_ref[...]
    o_ref[...] = x2


def _combine(dest, ys, x1, route, gtf, nf, seq, final_norm):
    t, d = x1.shape
    tm = TM_ROWS
    tps = seq // tm
    grid_spec = pltpu.PrefetchScalarGridSpec(
        num_scalar_prefetch=1,
        grid=(t // tm,),
        in_specs=[pl.BlockSpec(memory_space=pl.ANY),
                  pl.BlockSpec((tm, d), lambda i, dest: (i, 0)),
                  pl.BlockSpec((tm, LANES), lambda i, dest: (i, 0)),
                  pl.BlockSpec((1, 1, d), lambda i, dest: (i // tps, 0, 0)),
                  pl.BlockSpec((1, d), lambda i, dest: (0, 0))],
        out_specs=pl.BlockSpec((tm, d), lambda i, dest: (i, 0)),
        scratch_shapes=[pltpu.VMEM((2, TOP_K, tm, d), F32), pltpu.SemaphoreType.DMA((2,))],
    )
    return pl.pallas_call(
        functools.partial(_combine_kernel, final_norm),
        grid_spec=grid_spec,
        out_shape=jax.ShapeDtypeStruct((t, d), F32),
        compiler_params=_params(("arbitrary",)),
        name="combine",
    )(dest, ys, x1, route, gtf, nf)


def _block_tri(n, chunk, lower):
    r = jnp.arange(n)[:, None]
    c = jnp.arange(n)[None, :]
    same = (r // chunk) == (c // chunk)
    tri = (r >= c) if lower else (r <= c)
    return jnp.where(same & tri, 1.0, 0.0).astype(BF16)


def _layer(x2, c, bsz, seq, w_ada, b_ada, norm_mix, w_in, w_conv, a_log, dt_bias, gdn_norm,
           attn_norm, w_out, rel_bias, norm_ffn, w_rg, w_re, w_gate, w_up, w_down):
    t, d = x2.shape
    mod = _adaln(c, w_ada, b_ada)
    sh_a, sc_a, gt_a, sh_f, sc_f, gt_f = [m.reshape(bsz, 1, d) for m in jnp.split(mod, 6, axis=-1)]

    s1 = GDN_CONV_W
    s2 = s1 + GDN_V_W
    s4 = s2 + 2 * GDN_HEADS
    small = jnp.pad(w_in[:, s2:s4], ((0, 0), (0, SMALL_W - 2 * GDN_HEADS)))
    wcat = jnp.concatenate([w_in[:, :s2], small, w_in[:, s4:]], axis=1).astype(BF16)
    qkva, z, ba, *qkvb = _inproj(x2, sc_a, sh_a, norm_mix.reshape(1, d), wcat, w_conv, seq)

    pad4 = lambda v: jnp.pad(v.astype(F32), (GDN_HEADS, LANES - 2 * GDN_HEADS))
    gp = jnp.zeros((8, LANES), F32).at[0].set(pad4(a_log)).at[1].set(pad4(dt_bias))
    og = _gdn(qkva, ba, ba[:, :8].T, gp, _gp_cols(a_log, dt_bias),
              _block_tri(GDN_TOK, CHUNK, True), _block_tri(GDN_TOK, CHUNK, False), seq)

    outs, lses = [], []
    for (window, dil), qkvb_d in zip(DILATED_PAIRS, qkvb):
        o_i, lse_i = _attn_branch(qkvb_d, jnp.asarray(_bucket_table(window, dil)), rel_bias.astype(F32),
                                  bsz, seq, dil)
        outs.append(o_i)
        lses.append(lse_i)

    expand = jnp.where((jnp.arange(LANES)[:, None] == jnp.arange(ATTN_W)[None, :] // ATTN_DH), 1.0, 0.0).astype(BF16)
    lstrict = jnp.where(jnp.arange(TM_POST)[:, None] > jnp.arange(TM_POST)[None, :], 1.0, 0.0).astype(BF16)
    wr = jnp.pad(jnp.concatenate([w_rg, w_re], axis=1).astype(F32), ((0, 0), (0, LANES - N_GROUPS - N_EXPERTS)))
    wrh = wr.astype(BF16)
    wr_split = jnp.concatenate([wrh, (wr - wrh.astype(F32)).astype(BF16)], axis=1)
    x1, h2, route, cnt = _post(og, z, outs, lses, x2, gt_a, sc_f, sh_f,
                               gdn_norm.reshape(1, GDN_DV), attn_norm.reshape(1, ATTN_W),
                               w_out.astype(BF16), norm_ffn.reshape(1, d), wr_split, expand, lstrict, seq)

    blk = EXPERT_BLK
    counts = cnt[0, :N_EXPERTS].astype(jnp.int32)
    padded = (counts + blk - 1) // blk * blk
    pends = jnp.cumsum(padded)
    pstarts = pends - padded
    eids = route[:, 0:TOP_K].astype(jnp.int32)
    ranks = route[:, 4:4 + TOP_K].astype(jnp.int32)
    expert_ids = jnp.arange(N_EXPERTS, dtype=jnp.int32)
    seg_start = jnp.sum(jnp.where(eids[..., None] == expert_ids, pstarts, 0), axis=-1)
    dest = (seg_start + ranks).reshape(t * TOP_K)
    a = t * TOP_K
    p = -(-a // blk) * blk + N_EXPERTS * blk
    nblk = p // blk
    blk_start = jnp.arange(nblk, dtype=jnp.int32) * blk
    blk_e = jnp.minimum(jnp.sum((pends[None, :] <= blk_start[:, None]).astype(jnp.int32), axis=1),
                        N_EXPERTS - 1)
    nused = (pends[-1] // blk).astype(jnp.int32).reshape(1)

    xs = _dispatch(dest, pstarts + counts, padded - counts, nused, h2, p)
    ys = _experts(blk_e, nused, xs, w_gate, w_up, w_down)
    return ys, dest, x1, route, gt_f


def _gp_cols(a_log, dt_bias):
    z = jnp.zeros((8, LANES), F32)
    z = z.at[GDN_HEADS:2 * GDN_HEADS, 0].set(a_log.astype(F32))
    z = z.at[GDN_HEADS:2 * GDN_HEADS, 1].set(dt_bias.astype(F32))
    return z


def kernel(x, c, w_ada, b_ada, norm_mix, w_in, w_conv, a_log, dt_bias, gdn_norm, attn_norm, w_out,
           rel_bias, norm_ffn, w_router_group, w_router_expert, w_gate, w_up, w_down, norm_final):
    bsz, seq, d = x.shape
    depth = w_ada.shape[0]
    x2 = x.reshape(bsz * seq, d)
    for l in range(depth):
        ys, dest, x1, route, gt_f = _layer(
            x2, c, bsz, seq, w_ada[l], b_ada[l], norm_mix[l], w_in[l], w_conv[l], a_log[l], dt_bias[l],
            gdn_norm[l], attn_norm[l], w_out[l], rel_bias, norm_ffn[l], w_router_group[l],
            w_router_expert[l], w_gate[l], w_up[l], w_down[l])
        x2 = _combine(dest, ys, x1, route, gt_f, norm_final.reshape(1, d), seq, l == depth - 1)
    return x2.reshape(bsz, seq, d)
```

```python
import functools
import math

import jax
import jax.numpy as jnp
import numpy as np
from jax import lax
from jax.experimental import pallas as pl
from jax.experimental.pallas import tpu as pltpu

D_MODEL = 1024
GDN_HEADS = 4
GDN_DK = 128
GDN_DV = 128
CONV_K = 4
CHUNK = 64
ATTN_HEADS = 8
ATTN_DH = 64
DILATED_PAIRS = ((128, 1), (512, 4), (2048, 16))
QBLK = 128
NUM_BUCKETS = 32
REL_MAX_DIST = 2048
N_GROUPS = 4
EXPERTS_PER_GROUP = 8
N_EXPERTS = N_GROUPS * EXPERTS_PER_GROUP
TOP_K = 2
D_EXPERT = 256
EPS = 1e-6
NEG_INF = -1e30

GDN_QK_W = GDN_HEADS * GDN_DK
GDN_V_W = GDN_HEADS * GDN_DV
ATTN_W = ATTN_HEADS * ATTN_DH
GDN_CONV_W = 2 * GDN_QK_W + GDN_V_W
LANES = 128
SUBLANES = 8
SMALL_W = LANES
W_COLS = GDN_CONV_W + GDN_V_W + SMALL_W + 3 * ATTN_W

TM_PROJ = 512
CONV_BLK = 128
HEAD_ROWS = 16
TM_POST = 256
TM_ROWS = 256
GDN_TOK = 512
GDN_GROUP = 4
INV_BASE = 8
ATTN_SUB = 2
EXPERT_BLK = 512
ROW_UNROLL = 8
VMEM_LIMIT = 56 * 1024 * 1024

F32 = jnp.float32
BF16 = jnp.bfloat16
HIGHEST = lax.Precision.HIGHEST


def _sigmoid(x):
    return 1.0 / (1.0 + jnp.exp(-x))


def _dot(a, b, precision=None):
    return jnp.dot(a, b, preferred_element_type=F32, precision=precision)


def _dot_nt(a, b, precision=None):
    return lax.dot_general(a, b, (((1,), (1,)), ((), ())), preferred_element_type=F32,
                           precision=precision)


def _dot_tn(a, b, precision=None):
    return lax.dot_general(a, b, (((0,), (0,)), ((), ())), preferred_element_type=F32,
                           precision=precision)


def _params(sem):
    return pltpu.CompilerParams(dimension_semantics=sem, vmem_limit_bytes=VMEM_LIMIT)


def _adaln_kernel(c_ref, w_ref, b_ref, o_ref):
    o_ref[...] = _dot(c_ref[...], w_ref[...], HIGHEST) + b_ref[...]


def _adaln(c, w, b):
    bsz, d = c.shape
    n = w.shape[1]
    tn = 1024
    return pl.pallas_call(
        _adaln_kernel,
        grid=(n // tn,),
        in_specs=[pl.BlockSpec((bsz, d), lambda j: (0, 0)),
                  pl.BlockSpec((d, tn), lambda j: (0, j)),
                  pl.BlockSpec((1, tn), lambda j: (0, j))],
        out_specs=pl.BlockSpec((bsz, tn), lambda j: (0, j)),
        out_shape=jax.ShapeDtypeStruct((bsz, n), F32),
        compiler_params=_params(("arbitrary",)),
        name="adaln",
    )(c, w, b.reshape(1, n))


def _inproj_kernel(tiles_per_seq, x_ref, sc_ref, sh_ref, g_ref, w_ref, wconv_ref, shift_ref,
                   qkva_ref, z_ref, ba_ref, *rest):
    qkvb_refs = rest[:len(DILATED_PAIRS)]
    win_scr, pb_scr, pb_next_scr = rest[len(DILATED_PAIRS):]
    i = pl.program_id(0)
    tm = x_ref.shape[0]
    x = x_ref[...]
    h = x * lax.rsqrt(jnp.mean(x * x, axis=-1, keepdims=True) + EPS) * g_ref[...]
    h = h * (1.0 + sc_ref[0]) + sh_ref[0]
    hb = h.astype(BF16)

    @pl.when(i % tiles_per_seq == 0)
    def _():
        win_scr[0:HEAD_ROWS, :] = jnp.zeros((HEAD_ROWS, GDN_CONV_W), F32)

    pa = _dot(hb, w_ref[:, 0:GDN_CONV_W])
    pa16 = pa.astype(BF16)
    for b in range(tm // CONV_BLK):
        rs = slice(b * CONV_BLK, (b + 1) * CONV_BLK)
        acc = pa[rs] * wconv_ref[CONV_K - 1:CONV_K, :]
        for j in range(CONV_K - 1):
            if b == 0:
                shifted = _dot(shift_ref[j, :, CONV_BLK:], pa16[rs])
            else:
                shifted = _dot(shift_ref[j], pa16[(b - 1) * CONV_BLK:(b + 1) * CONV_BLK])
            acc = acc + shifted * wconv_ref[j:j + 1, :]
        qkva_ref[rs, :] = (acc * _sigmoid(acc)).astype(qkva_ref.dtype)
    win_scr[HEAD_ROWS:2 * HEAD_ROWS, :] = pa[0:HEAD_ROWS]
    acc = pa[0:HEAD_ROWS] * wconv_ref[CONV_K - 1:CONV_K, :]
    for j in range(CONV_K - 1):
        acc = acc + win_scr[pl.ds(HEAD_ROWS - (CONV_K - 1) + j, HEAD_ROWS), :] * wconv_ref[j:j + 1, :]
    qkva_ref[0:HEAD_ROWS, :] = (acc * _sigmoid(acc)).astype(qkva_ref.dtype)
    win_scr[0:HEAD_ROWS, :] = pa[tm - HEAD_ROWS:tm]

    c0 = GDN_CONV_W
    z_ref[...] = _dot(hb, w_ref[:, c0:c0 + GDN_V_W]).astype(z_ref.dtype)
    c0 += GDN_V_W
    ba_ref[...] = _dot(hb, w_ref[:, c0:c0 + SMALL_W])
    c0 += SMALL_W
    pb = _dot(hb, w_ref[:, c0:c0 + 3 * ATTN_W])
    ncol = 3 * ATTN_W // LANES
    for j in range(ncol):
        pb_scr[j] = pb[:, j * LANES:(j + 1) * LANES]
    prev_dil, prev_scr, next_scr = 1, pb_scr, pb_next_scr
    for bi, ((_, dil), ref) in enumerate(zip(DILATED_PAIRS, qkvb_refs)):
        if dil == 1:
            ref[...] = pb.astype(ref.dtype)
            continue
        f = dil // prev_dil
        n = tm // dil
        keep = bi + 1 < len(DILATED_PAIRS)
        for r in range(prev_dil):
            for q in range(f):
                r_new = r + prev_dil * q
                for j in range(ncol):
                    rows = prev_scr[j, pl.ds(r * (tm // prev_dil) + q, n, stride=f), :]
                    c1 = r_new * 3 * ATTN_W + j * LANES
                    ref[:, c1:c1 + LANES] = rows.astype(ref.dtype)
                    if keep:
                        next_scr[j, r_new * n:(r_new + 1) * n, :] = rows
        prev_dil, prev_scr, next_scr = dil, next_scr, prev_scr


def _inproj(x2, sc, sh, g, wcat, wconv, seq):
    t, d = x2.shape
    tm = TM_PROJ
    tps = seq // tm
    mod_spec = pl.BlockSpec((1, 1, d), lambda i: (i // tps, 0, 0))
    return pl.pallas_call(
        functools.partial(_inproj_kernel, tps),
        grid=(t // tm,),
        in_specs=[pl.BlockSpec((tm, d), lambda i: (i, 0)),
                  mod_spec, mod_spec,
                  pl.BlockSpec((1, d), lambda i: (0, 0)),
                  pl.BlockSpec((d, W_COLS), lambda i: (0, 0)),
                  pl.BlockSpec((CONV_K, GDN_CONV_W), lambda i: (0, 0)),
                  pl.BlockSpec((CONV_K - 1, CONV_BLK, 2 * CONV_BLK), lambda i: (0, 0, 0))],
        out_specs=[pl.BlockSpec((tm, GDN_CONV_W), lambda i: (i, 0)),
                   pl.BlockSpec((tm, GDN_V_W), lambda i: (i, 0)),
                   pl.BlockSpec((tm, SMALL_W), lambda i: (i, 0))]
                  + [pl.BlockSpec((tm // dil, dil * 3 * ATTN_W), lambda i: (i, 0)) for _, dil in DILATED_PAIRS],
        out_shape=[jax.ShapeDtypeStruct((t, GDN_CONV_W), BF16),
                   jax.ShapeDtypeStruct((t, GDN_V_W), BF16),
                   jax.ShapeDtypeStruct((t, SMALL_W), F32)]
                  + [jax.ShapeDtypeStruct((t // dil, dil * 3 * ATTN_W), BF16) for _, dil in DILATED_PAIRS],
        scratch_shapes=[pltpu.VMEM((2 * HEAD_ROWS, GDN_CONV_W), F32),
                        pltpu.VMEM((3 * ATTN_W // LANES, tm, LANES), F32),
                        pltpu.VMEM((3 * ATTN_W // LANES, tm, LANES), F32)],
        compiler_params=_params(("arbitrary",)),
        name="inproj",
    )(x2, sc, sh, g, wcat, wconv, _shift_bands())


def _shift_bands():
    t = np.arange(CONV_BLK)[:, None]
    c = np.arange(2 * CONV_BLK)[None, :]
    bands = [(c == CONV_BLK + t - (CONV_K - 1 - j)) for j in range(CONV_K - 1)]
    return jnp.asarray(np.stack(bands).astype(np.float32), dtype=BF16)


def _softplus(x):
    return jnp.maximum(x, 0.0) + jnp.log(1.0 + jnp.exp(-jnp.abs(x)))


def _unit_lower_inverses(n_list):
    c = n_list[0].shape[0]
    row = lax.broadcasted_iota(jnp.int32, (c, c), 0)
    col = lax.broadcasted_iota(jnp.int32, (c, c), 1)
    eye = jnp.where(row == col, 1.0, 0.0)
    same_base = (row // INV_BASE) == (col // INV_BASE)
    n0 = [jnp.where(same_base, n, 0.0) for n in n_list]
    p = [eye - x for x in n0]
    m = [x.astype(BF16) for x in n0]
    for j in range(int(math.log2(INV_BASE)) - 1):
        m = [_dot(x, x).astype(BF16) for x in m]
        p = [x + _dot(x.astype(BF16), y) for x, y in zip(p, m)]
    size = INV_BASE
    while size < c:
        sibling = jnp.logical_and((row // size) % 2 == 1, (col // size) == (row // size) - 1)
        cb = [jnp.where(sibling, n, 0.0).astype(BF16) for n in n_list]
        pb = [x.astype(BF16) for x in p]
        pc = [_dot(x, y).astype(BF16) for x, y in zip(pb, cb)]
        p = [x - _dot(y, z) for x, y, z in zip(p, pc, pb)]
        size *= 2
    return p


def _split3_bf16(x):
    h1 = x.astype(BF16)
    r1 = x - h1.astype(F32)
    h2 = r1.astype(BF16)
    h3 = (r1 - h2.astype(F32)).astype(BF16)
    return h1, h2, h3


def _gdn_kernel(steps_per_seq, qkv_ref, ba_ref, bat_ref, gp_ref, gpt_ref, lblk_ref, ublk_ref,
                o_ref, state_scr):
    i = pl.program_id(0)

    @pl.when(i % steps_per_seq == 0)
    def _():
        state_scr[...] = jnp.zeros_like(state_scr)

    nchunk = GDN_TOK // CHUNK
    ba = ba_ref[...]
    bat = bat_ref[...]
    a_vec = -jnp.exp(gp_ref[0:1, :])
    g_tile = a_vec * _softplus(ba + gp_ref[1:2, :])
    lblk = lblk_ref[...]
    g_cum = sum(_dot(lblk, part) for part in _split3_bf16(g_tile))
    a_col = -jnp.exp(gpt_ref[:, 0:1])
    gt_tile = a_col * _softplus(bat + gpt_ref[:, 1:2])
    ublk = ublk_ref[...]
    gt_cum = sum(_dot(part, ublk) for part in _split3_bf16(gt_tile))
    beta_tile = _sigmoid(ba)

    row = lax.broadcasted_iota(jnp.int32, (CHUNK, CHUNK), 0)
    col = lax.broadcasted_iota(jnp.int32, (CHUNK, CHUNK), 1)
    incl = row >= col
    strict = row > col

    for h0 in range(0, GDN_HEADS, GDN_GROUP):
        heads = range(h0, h0 + GDN_GROUP)
        kb_l, k_l, q_l, rhs_l, qg_l, kg_l, gl_l, inc_l, str_l = ([] for _ in range(9))
        for h in heads:
            q_raw = qkv_ref[:, h * GDN_DK:(h + 1) * GDN_DK].astype(F32)
            k_raw = qkv_ref[:, GDN_QK_W + h * GDN_DK:GDN_QK_W + (h + 1) * GDN_DK].astype(F32)
            v = qkv_ref[:, 2 * GDN_QK_W + h * GDN_DV:2 * GDN_QK_W + (h + 1) * GDN_DV].astype(F32)
            qn = q_raw * lax.rsqrt(jnp.sum(q_raw * q_raw, axis=-1, keepdims=True) + EPS) * (GDN_DK ** -0.5)
            kn = k_raw * lax.rsqrt(jnp.sum(k_raw * k_raw, axis=-1, keepdims=True) + EPS)
            beta = beta_tile[:, h:h + 1]
            gc_all = g_cum[:, GDN_HEADS + h:GDN_HEADS + h + 1]
            eg = jnp.exp(gc_all)
            kb = kn * beta
            rhs = jnp.concatenate([v * beta, kb * eg], axis=1).astype(BF16)
            qg = qn * eg
            kb16, k16, q16 = kb.astype(BF16), kn.astype(BF16), qn.astype(BF16)
            for c in range(nchunk):
                sl = slice(c * CHUNK, (c + 1) * CHUNK)
                g_c = gc_all[sl]
                g_r = gt_cum[GDN_HEADS + h:GDN_HEADS + h + 1, sl]
                dec = jnp.exp(g_c - g_r)
                inc_l.append(jnp.where(incl, dec, 0.0))
                str_l.append(jnp.where(strict, dec, 0.0))
                g_last = g_c[CHUNK - 1:CHUNK, :]
                kg_l.append((kn[sl] * jnp.exp(g_last - g_c)).astype(BF16))
                gl_l.append(jnp.exp(g_last))
                kb_l.append(kb16[sl])
                k_l.append(k16[sl])
                q_l.append(q16[sl])
                rhs_l.append(rhs[sl])
                qg_l.append(qg[sl])
        nprob = len(k_l)
        n_l = [_dot_nt(kb_l[j], k_l[j]) * str_l[j] for j in range(nprob)]
        aqk_l = [(_dot_nt(q_l[j], k_l[j]) * inc_l[j]).astype(BF16) for j in range(nprob)]
        tinv_l = _unit_lower_inverses(n_l)
        sol_l = [_dot(tinv_l[j].astype(BF16), rhs_l[j]).astype(BF16) for j in range(nprob)]
        a2_l = [_dot(aqk_l[j], sol_l[j]) for j in range(nprob)]
        k2_l = [_dot_tn(kg_l[j], sol_l[j]) for j in range(nprob)]
        states = [state_scr[h] for h in heads]
        for c in range(nchunk):
            sl = slice(c * CHUNK, (c + 1) * CHUNK)
            for hi, h in enumerate(heads):
                j = hi * nchunk + c
                sb = states[hi].astype(BF16)
                qeff = (qg_l[j] - a2_l[j][:, GDN_DV:]).astype(BF16)
                o = _dot(qeff, sb) + a2_l[j][:, :GDN_DV]
                states[hi] = gl_l[j] * states[hi] + k2_l[j][:, :GDN_DV] - _dot(k2_l[j][:, GDN_DV:].astype(BF16), sb)
                o_ref[sl, h * GDN_DV:(h + 1) * GDN_DV] = o.astype(o_ref.dtype)
        for hi, h in enumerate(heads):
            state_scr[h] = states[hi]


def _gdn(qkva, ba, bat, gp, gpt, lblk, ublk, seq):
    t = qkva.shape[0]
    tok = GDN_TOK
    sps = seq // tok
    const = lambda i: (0, 0)
    return pl.pallas_call(
        functools.partial(_gdn_kernel, sps),
        grid=(t // tok,),
        in_specs=[pl.BlockSpec((tok, GDN_CONV_W), lambda i: (i, 0)),
                  pl.BlockSpec((tok, SMALL_W), lambda i: (i, 0)),
                  pl.BlockSpec((8, tok), lambda i: (0, i)),
                  pl.BlockSpec((8, LANES), const),
                  pl.BlockSpec((8, LANES), const),
                  pl.BlockSpec((tok, tok), const),
                  pl.BlockSpec((tok, tok), const)],
        out_specs=pl.BlockSpec((tok, GDN_V_W), lambda i: (i, 0)),
        out_shape=jax.ShapeDtypeStruct((t, GDN_V_W), F32),
        scratch_shapes=[pltpu.VMEM((GDN_HEADS, GDN_DK, GDN_DV), F32)],
        compiler_params=_params(("arbitrary",)),
        name="gdn",
    )(qkva, ba, bat, gp, gpt, lblk, ublk)


def _attn_kernel(q_ref, kp_ref, kc_ref, vp_ref, vc_ref, bucket_ref, rb_ref, o_ref, lse_ref, bias_ref):
    n = pl.program_id(2)

    @pl.when(jnp.logical_and(jnp.logical_and(pl.program_id(0) == 0, pl.program_id(1) == 0), n == 0))
    def _():
        bucket = bucket_ref[...]
        col = lax.broadcasted_iota(jnp.int32, (QBLK, 2 * QBLK), 1)
        for h in range(ATTN_HEADS):
            acc = jnp.full((QBLK, 2 * QBLK), NEG_INF, F32)
            for b in range(NUM_BUCKETS):
                acc = jnp.where(bucket == b, rb_ref[b, h], acc)
            bias_ref[0, h] = acc
            bias_ref[1, h] = jnp.where(col < QBLK, NEG_INF, acc)

    nsub = q_ref.shape[1] // QBLK
    first = jnp.where(n == 0, 1, 0)
    q_all = q_ref[0]
    k_all = jnp.concatenate([kp_ref[0], kc_ref[0]], axis=0)
    v_all = jnp.concatenate([vp_ref[0], vc_ref[0]], axis=0)
    lane = lax.broadcasted_iota(jnp.int32, (QBLK, LANES), 1)
    ones = jnp.ones((2 * QBLK, LANES), BF16)
    per_group = LANES // ATTN_DH
    ngrp = ATTN_HEADS // per_group
    scale = ATTN_DH ** -0.5
    mine = [jnp.logical_and(lane >= j * ATTN_DH, lane < (j + 1) * ATTN_DH) for j in range(per_group)]
    probs = [(u, h) for u in range(nsub) for h in range(ATTN_HEADS)]
    s_l = []
    for u, h in probs:
        grp, j = divmod(h, per_group)
        gs = slice(grp * LANES, (grp + 1) * LANES)
        qg = q_all[u * QBLK:(u + 1) * QBLK, gs] * scale
        table = first if u == 0 else 0
        s_l.append(_dot_nt(jnp.where(mine[j], qg, jnp.zeros_like(qg)), k_all[u * QBLK:(u + 2) * QBLK, gs])
                   + bias_ref[table, h])
    m_l = [jnp.max(s, axis=-1, keepdims=True) for s in s_l]
    p_l = [jnp.exp(s - m).astype(BF16) for s, m in zip(s_l, m_l)]
    v_ext = {(u, grp): jnp.concatenate([v_all[u * QBLK:(u + 2) * QBLK, grp * LANES:(grp + 1) * LANES], ones],
                                       axis=1)
             for u in range(nsub) for grp in range(ngrp)}
    pv_l = [_dot(p, v_ext[(u, h // per_group)]) for p, (u, h) in zip(p_l, probs)]
    for u in range(nsub):
        outs = []
        lse_tile = jnp.zeros((QBLK, LANES), F32)
        for grp in range(ngrp):
            o_grp = jnp.zeros((QBLK, LANES), F32)
            for j in range(per_group):
                h = grp * per_group + j
                idx = u * ATTN_HEADS + h
                den = pv_l[idx][:, LANES:]
                o_grp = jnp.where(mine[j], pv_l[idx][:, :LANES] / den, o_grp)
                lse_tile = jnp.where(lane == h, m_l[idx] + jnp.log(den[:, 0:1]), lse_tile)
            outs.append(o_grp)
        o_ref[0, u * QBLK:(u + 1) * QBLK, :] = jnp.concatenate(outs, axis=1).astype(o_ref.dtype)
        lse_ref[0, u * QBLK:(u + 1) * QBLK, :] = lse_tile


def _attn_branch(qkvb, bucket, rel_bias, bsz, seq, dil):
    ln = seq // dil
    rows = ATTN_SUB * QBLK
    assert seq % dil == 0 and ln % rows == 0, (seq, dil, rows)
    nb = ln // rows
    xv = qkvb.reshape(bsz, ln, dil * 3 * ATTN_W)
    w = ATTN_W
    cur = lambda part: (lambda b, r, n: (b, n, r * 3 + part))
    prev = lambda part: (lambda b, r, n: (b, jnp.maximum(n * ATTN_SUB - 1, 0), r * 3 + part))
    o, lse = pl.pallas_call(
        _attn_kernel,
        grid=(bsz, dil, nb),
        in_specs=[pl.BlockSpec((1, rows, w), cur(0)),
                  pl.BlockSpec((1, QBLK, w), prev(1)),
                  pl.BlockSpec((1, rows, w), cur(1)),
                  pl.BlockSpec((1, QBLK, w), prev(2)),
                  pl.BlockSpec((1, rows, w), cur(2)),
                  pl.BlockSpec((QBLK, 2 * QBLK), lambda b, r, n: (0, 0)),
                  pl.BlockSpec(memory_space=pltpu.SMEM)],
        out_specs=[pl.BlockSpec((1, rows, w), lambda b, r, n: (b, n, r)),
                   pl.BlockSpec((1, rows, LANES), lambda b, r, n: (b, n, r))],
        out_shape=[jax.ShapeDtypeStruct((bsz, ln, dil * w), BF16),
                   jax.ShapeDtypeStruct((bsz, ln, dil * LANES), F32)],
        scratch_shapes=[pltpu.VMEM((2, ATTN_HEADS, QBLK, 2 * QBLK), F32)],
        compiler_params=_params(("arbitrary", "arbitrary", "arbitrary")),
        name=f"attn_d{dil}",
    )(xv, xv, xv, xv, xv, bucket, rel_bias)
    return o.reshape(bsz * ln, dil * w), lse.reshape(bsz * ln, dil * LANES)


def _bucket_table(window, dil):
    steps = window // dil
    qi = np.arange(QBLK)[:, None]
    kj = np.arange(2 * QBLK)[None, :]
    delta = qi + QBLK - kj
    dist = np.maximum(delta, 0) * dil
    max_exact = NUM_BUCKETS // 2
    d_f = np.maximum(dist, 1).astype(np.float32)
    large = max_exact + (np.log(d_f / max_exact) / math.log(REL_MAX_DIST / max_exact)
                         * (NUM_BUCKETS - max_exact)).astype(np.int32)
    bucket = np.where(dist < max_exact, dist, np.minimum(large, NUM_BUCKETS - 1))
    return np.where((delta >= 0) & (delta <= steps), bucket, -1).astype(np.int32)


def _split_bf16(x):
    hi = x.astype(BF16)
    lo = (x - hi.astype(F32)).astype(BF16)
    return hi, lo


def _post_kernel(og_ref, z_ref, o1_ref, o2_ref, o3_ref, l1_ref, l2_ref, l3_ref, x_ref,
                 gta_ref, scf_ref, shf_ref, gnorm_ref, anorm_ref, wout_ref, nffn_ref,
                 wrh_ref, exp_ref, ls_ref,
                 x1_ref, h2_ref, route_ref, cnt_ref, carry_scr, *tok_scr):
    i = pl.program_id(0)
    tm = x_ref.shape[0]

    @pl.when(i == 0)
    def _():
        carry_scr[...] = jnp.zeros_like(carry_scr)

    nbr = len(DILATED_PAIRS)
    ob_scr, lse_scr = tok_scr[:nbr], tok_scr[nbr:]
    for (_, dil), o_ref, l_ref, o_s, l_s in zip(DILATED_PAIRS, (o1_ref, o2_ref, o3_ref),
                                                 (l1_ref, l2_ref, l3_ref), ob_scr, lse_scr):
        for r in range(dil):
            rows = pl.ds(r, tm // dil, stride=dil) if dil > 1 else slice(None)
            for j in range(ATTN_W // LANES):
                c1 = r * ATTN_W + j * LANES
                o_s[j, rows, :] = o_ref[:, c1:c1 + LANES].astype(F32)
            l_s[rows, :] = l_ref[:, r * LANES:(r + 1) * LANES]

    heads = []
    for h in range(GDN_HEADS):
        hs = slice(h * GDN_DV, (h + 1) * GDN_DV)
        seg = og_ref[:, hs]
        nrm = seg * lax.rsqrt(jnp.mean(seg * seg, axis=-1, keepdims=True) + EPS) * gnorm_ref[...]
        zz = z_ref[:, hs].astype(F32)
        heads.append((nrm * (zz * _sigmoid(zz))).astype(BF16))
    oa = jnp.concatenate(heads, axis=1)

    l1, l2, l3 = (l_s[...] for l_s in lse_scr)
    m = jnp.maximum(jnp.maximum(l1, l2), l3)
    e1, e2, e3 = jnp.exp(l1 - m), jnp.exp(l2 - m), jnp.exp(l3 - m)
    inv = 1.0 / (e1 + e2 + e3)
    ob = jnp.zeros((tm, ATTN_W), F32)
    for e, o_s in zip((e1, e2, e3), ob_scr):
        hi, lo = _split_bf16(e * inv)
        wexp = _dot(hi, exp_ref[...]) + _dot(lo, exp_ref[...])
        ob = ob + wexp * jnp.concatenate([o_s[j] for j in range(ATTN_W // LANES)], axis=1)
    ob = ob * lax.rsqrt(jnp.mean(ob * ob, axis=-1, keepdims=True) + EPS) * anorm_ref[...]

    mix = _dot(oa, wout_ref[0:GDN_V_W, :]) + _dot(ob.astype(BF16), wout_ref[GDN_V_W:, :])
    x1 = x_ref[...] + gta_ref[0] * mix
    x1_ref[...] = x1
    h2 = x1 * lax.rsqrt(jnp.mean(x1 * x1, axis=-1, keepdims=True) + EPS) * nffn_ref[...]
    h2 = h2 * (1.0 + scf_ref[0]) + shf_ref[0]
    h2_ref[...] = h2

    hh, hl = _split_bf16(h2)
    both = _dot(hh, wrh_ref[...])
    logits = both[:, :LANES] + both[:, LANES:] + _dot(hl, wrh_ref[:, :LANES])
    tm = logits.shape[0]
    lane = lax.broadcasted_iota(jnp.int32, (tm, LANES), 1).astype(F32)
    big = float(LANES)
    gmask = lane < N_GROUPS
    glog = jnp.where(gmask, logits, NEG_INF)
    gmax = jnp.max(glog, axis=-1, keepdims=True)
    gidx = jnp.min(jnp.where(jnp.logical_and(gmask, glog == gmax), lane, big), axis=-1, keepdims=True)
    gprob = 1.0 / jnp.sum(jnp.where(gmask, jnp.exp(glog - gmax), 0.0), axis=-1, keepdims=True)
    lo_lane = N_GROUPS + EXPERTS_PER_GROUP * gidx
    emask = jnp.logical_and(lane >= lo_lane, lane < lo_lane + EXPERTS_PER_GROUP)
    elog = jnp.where(emask, logits, NEG_INF)
    m1 = jnp.max(elog, axis=-1, keepdims=True)
    i1 = jnp.min(jnp.where(jnp.logical_and(emask, elog == m1), lane, big), axis=-1, keepdims=True)
    emask2 = jnp.logical_and(emask, lane != i1)
    elog2 = jnp.where(emask2, logits, NEG_INF)
    m2 = jnp.max(elog2, axis=-1, keepdims=True)
    i2 = jnp.min(jnp.where(jnp.logical_and(emask2, elog2 == m2), lane, big), axis=-1, keepdims=True)
    r = jnp.exp(m2 - m1)
    gate1 = gprob / (1.0 + r)
    gate2 = gprob * r / (1.0 + r)
    ex1 = i1 - N_GROUPS
    ex2 = i2 - N_GROUPS

    hit1 = lane == ex1
    hit2 = lane == ex2
    onehot = jnp.where(jnp.logical_or(hit1, hit2), 1.0, 0.0)
    pref = _dot(ls_ref[...], onehot.astype(BF16)) + carry_scr[...]
    rank1 = jnp.sum(jnp.where(hit1, pref, 0.0), axis=-1, keepdims=True)
    rank2 = jnp.sum(jnp.where(hit2, pref, 0.0), axis=-1, keepdims=True)
    carry = carry_scr[...] + jnp.sum(onehot, axis=0, keepdims=True)
    carry_scr[...] = carry
    cnt_ref[...] = jnp.broadcast_to(carry, cnt_ref.shape)

    route = jnp.zeros((tm, LANES), F32)
    for idx, val in enumerate((ex1, ex2, gate1, gate2, rank1, rank2)):
        route = jnp.where(lane == idx, val, route)
    route_ref[...] = route


def _post(og, z, os_, ls_, x2, gta, scf, shf, gnorm, anorm, wout, nffn, wrh, expand, lstrict, seq):
    t, d = x2.shape
    tm = TM_POST
    tps = seq // tm
    tile = lambda w: pl.BlockSpec((tm, w), lambda i: (i, 0))
    const2 = lambda a: pl.BlockSpec(a.shape, lambda i: (0, 0))
    mod_spec = pl.BlockSpec((1, 1, d), lambda i: (i // tps, 0, 0))
    return pl.pallas_call(
        _post_kernel,
        grid=(t // tm,),
        in_specs=[tile(GDN_V_W), tile(GDN_V_W)]
                 + [pl.BlockSpec((tm // dil, dil * ATTN_W), lambda i: (i, 0)) for _, dil in DILATED_PAIRS]
                 + [pl.BlockSpec((tm // dil, dil * LANES), lambda i: (i, 0)) for _, dil in DILATED_PAIRS]
                 + [tile(d),
                  mod_spec, mod_spec, mod_spec,
                  const2(gnorm), const2(anorm), const2(wout), const2(nffn),
                  const2(wrh), const2(expand), const2(lstrict)],
        out_specs=[tile(d), tile(d), tile(LANES), pl.BlockSpec((8, LANES), lambda i: (0, 0))],
        out_shape=[jax.ShapeDtypeStruct((t, d), F32),
                   jax.ShapeDtypeStruct((t, d), F32),
                   jax.ShapeDtypeStruct((t, LANES), F32),
                   jax.ShapeDtypeStruct((8, LANES), F32)],
        scratch_shapes=[pltpu.VMEM((1, LANES), F32)]
                       + [pltpu.VMEM((ATTN_W // LANES, tm, LANES), F32) for _ in DILATED_PAIRS]
                       + [pltpu.VMEM((tm, LANES), F32) for _ in DILATED_PAIRS],
        compiler_params=_params(("arbitrary",)),
        name="post",
    )(og, z, *os_, *ls_, x2, gta, scf, shf, gnorm, anorm, wout, nffn, wrh, expand, lstrict)


def _dispatch_kernel(dest_ref, fill_ref, pad_ref, nused_ref, h2_ref, xs_ref, zero_scr, sem, fill_sem):
    i = pl.program_id(0)
    tm = h2_ref.shape[0]

    @pl.when(i == 0)
    def _():
        zero_scr[...] = jnp.zeros_like(zero_scr)

        def pieces(e, act):
            base, pad = fill_ref[e], pad_ref[e]
            head = pad & (SUBLANES - 1)
            for j in range(SUBLANES - 1):
                @pl.when(j < head)
                def _():
                    act(pltpu.make_async_copy(zero_scr.at[pl.ds(0, 1)], xs_ref.at[pl.ds(base + j, 1)],
                                              fill_sem))
            off = base + head
            for bit in range(SUBLANES.bit_length() - 1, EXPERT_BLK.bit_length() - 1):
                size = 1 << bit

                @pl.when((pad >> bit) & 1 == 1)
                def _():
                    act(pltpu.make_async_copy(zero_scr.at[pl.ds(0, size)],
                                              xs_ref.at[pl.ds(pl.multiple_of(off, SUBLANES), size)], fill_sem))
                off = off + (pad & size)

        def fill(e, carry):
            pieces(e, lambda cp: cp.start())
            return carry

        def drain(e, carry):
            pieces(e, lambda cp: cp.wait())
            return carry

        lax.fori_loop(0, N_EXPERTS, fill, 0)
        lax.fori_loop(0, N_EXPERTS, drain, 0)

        def tail(b, act):
            for half in range(EXPERT_BLK // zero_scr.shape[0]):
                row0 = pl.multiple_of(b * EXPERT_BLK + half * zero_scr.shape[0], SUBLANES)
                act(pltpu.make_async_copy(zero_scr, xs_ref.at[pl.ds(row0, zero_scr.shape[0])], fill_sem))

        def tail_fill(b, carry):
            tail(b, lambda cp: cp.start())
            return carry

        def tail_drain(b, carry):
            tail(b, lambda cp: cp.wait())
            return carry

        nblk = xs_ref.shape[0] // EXPERT_BLK
        lax.fori_loop(nused_ref[0], nblk, tail_fill, 0)
        lax.fori_loop(nused_ref[0], nblk, tail_drain, 0)

    def start(r, carry):
        for k in range(TOP_K):
            d = dest_ref[(i * tm + r) * TOP_K + k]
            pltpu.make_async_copy(h2_ref.at[pl.ds(r, 1)], xs_ref.at[pl.ds(d, 1)], sem).start()
        return carry

    lax.fori_loop(0, tm, start, 0, unroll=ROW_UNROLL)
    for k in range(TOP_K):
        pltpu.make_async_copy(h2_ref, xs_ref.at[pl.ds(0, tm)], sem).wait()


def _dispatch(dest, fill_start, pad_rows, nused, h2, p):
    t, d = h2.shape
    tm = TM_ROWS
    grid_spec = pltpu.PrefetchScalarGridSpec(
        num_scalar_prefetch=4,
        grid=(t // tm,),
        in_specs=[pl.BlockSpec((tm, d), lambda i, *_: (i, 0))],
        out_specs=pl.BlockSpec(memory_space=pl.ANY),
        scratch_shapes=[pltpu.VMEM((EXPERT_BLK // 2, d), F32), pltpu.SemaphoreType.DMA,
                        pltpu.SemaphoreType.DMA],
    )
    return pl.pallas_call(
        _dispatch_kernel,
        grid_spec=grid_spec,
        out_shape=jax.ShapeDtypeStruct((p, d), F32),
        compiler_params=_params(("arbitrary",)),
        name="dispatch",
    )(dest, fill_start, pad_rows, nused, h2)


def _expert_kernel(blk_e_ref, nused_ref, xs_ref, wg_ref, wu_ref, wd_ref, ys_ref, wg16, wu16, wd16):
    b = pl.program_id(0)
    used = b < nused_ref[0]

    new_expert = jnp.logical_or(b == 0, blk_e_ref[b] != blk_e_ref[jnp.maximum(b - 1, 0)])

    @pl.when(jnp.logical_and(used, new_expert))
    def _():
        wg16[...] = wg_ref[0].astype(BF16)
        wu16[...] = wu_ref[0].astype(BF16)
        wd16[...] = wd_ref[0].astype(BF16)

    @pl.when(used)
    def _():
        x = xs_ref[...].astype(BF16)
        g = _dot(x, wg16[...])
        u = _dot(x, wu16[...])
        hid = (g * _sigmoid(g)) * u
        ys_ref[...] = _dot(hid.astype(BF16), wd16[...])

    @pl.when(b >= nused_ref[0])
    def _():
        ys_ref[...] = jnp.zeros_like(ys_ref)


def _experts(blk_e, nused, xs, wg, wu, wd):
    p, d = xs.shape
    blk = EXPERT_BLK
    row_map = lambda b, be, nu: (jnp.minimum(b, nu[0] - 1), 0)
    w_map = lambda b, be, nu: (be[b], 0, 0)
    grid_spec = pltpu.PrefetchScalarGridSpec(
        num_scalar_prefetch=2,
        grid=(p // blk,),
        in_specs=[pl.BlockSpec((blk, d), row_map),
                  pl.BlockSpec((1, d, D_EXPERT), w_map),
                  pl.BlockSpec((1, d, D_EXPERT), w_map),
                  pl.BlockSpec((1, D_EXPERT, d), w_map)],
        out_specs=pl.BlockSpec((blk, d), lambda b, be, nu: (b, 0)),
        scratch_shapes=[pltpu.VMEM((d, D_EXPERT), BF16), pltpu.VMEM((d, D_EXPERT), BF16),
                        pltpu.VMEM((D_EXPERT, d), BF16)],
    )
    return pl.pallas_call(
        _expert_kernel,
        grid_spec=grid_spec,
        out_shape=jax.ShapeDtypeStruct((p, d), F32),
        compiler_params=_params(("arbitrary",)),
        name="experts",
    )(blk_e, nused, xs, wg, wu, wd)


def _combine_kernel(final_norm, dest_ref, ys_ref, x1_ref, route_ref, gtf_ref, nf_ref, o_ref, ybuf, sem):
    i = pl.program_id(0)
    tm = x1_ref.shape[0]
    slot = i % 2

    def gather_tile(tile, slot_):
        def start(r, carry):
            for k in range(TOP_K):
                d = dest_ref[(tile * tm + r) * TOP_K + k]
                pltpu.make_async_copy(ys_ref.at[pl.ds(d, 1)], ybuf.at[slot_, k, pl.ds(r, 1)],
                                      sem.at[slot_]).start()
            return carry
        lax.fori_loop(0, tm, start, 0, unroll=ROW_UNROLL)

    @pl.when(i == 0)
    def _():
        gather_tile(0, 0)

    @pl.when(i + 1 < pl.num_programs(0))
    def _():
        gather_tile(i + 1, 1 - slot)

    for k in range(TOP_K):
        pltpu.make_async_copy(ys_ref.at[pl.ds(0, tm)], ybuf.at[slot, k], sem.at[slot]).wait()
    route = route_ref[...]
    moe = ybuf[slot, 0] * route[:, 2:3] + ybuf[slot, 1] * route[:, 3:4]
    x2 = x1_ref[...] + gtf_ref[0] * moe
    if final_norm:
        x2 = x2 * lax.rsqrt(jnp.mean(x2 * x2, axis=-1, keepdims=True) + EPS) * nf_ref[...]
    o_ref[...] = x2


def _combine(dest, ys, x1, route, gtf, nf, seq, final_norm):
    t, d = x1.shape
    tm = TM_ROWS
    tps = seq // tm
    grid_spec = pltpu.PrefetchScalarGridSpec(
        num_scalar_prefetch=1,
        grid=(t // tm,),
        in_specs=[pl.BlockSpec(memory_space=pl.ANY),
                  pl.BlockSpec((tm, d), lambda i, dest: (i, 0)),
                  pl.BlockSpec((tm, LANES), lambda i, dest: (i, 0)),
                  pl.BlockSpec((1, 1, d), lambda i, dest: (i // tps, 0, 0)),
                  pl.BlockSpec((1, d), lambda i, dest: (0, 0))],
        out_specs=pl.BlockSpec((tm, d), lambda i, dest: (i, 0)),
        scratch_shapes=[pltpu.VMEM((2, TOP_K, tm, d), F32), pltpu.SemaphoreType.DMA((2,))],
    )
    return pl.pallas_call(
        functools.partial(_combine_kernel, final_norm),
        grid_spec=grid_spec,
        out_shape=jax.ShapeDtypeStruct((t, d), F32),
        compiler_params=_params(("arbitrary",)),
        name="combine",
    )(dest, ys, x1, route, gtf, nf)


def _block_tri(n, chunk, lower):
    r = jnp.arange(n)[:, None]
    c = jnp.arange(n)[None, :]
    same = (r // chunk) == (c // chunk)
    tri = (r >= c) if lower else (r <= c)
    return jnp.where(same & tri, 1.0, 0.0).astype(BF16)


def _layer(x2, c, bsz, seq, w_ada, b_ada, norm_mix, w_in, w_conv, a_log, dt_bias, gdn_norm,
           attn_norm, w_out, rel_bias, norm_ffn, w_rg, w_re, w_gate, w_up, w_down):
    t, d = x2.shape
    mod = _adaln(c, w_ada, b_ada)
    sh_a, sc_a, gt_a, sh_f, sc_f, gt_f = [m.reshape(bsz, 1, d) for m in jnp.split(mod, 6, axis=-1)]

    s1 = GDN_CONV_W
    s2 = s1 + GDN_V_W
    s4 = s2 + 2 * GDN_HEADS
    small = jnp.pad(w_in[:, s2:s4], ((0, 0), (0, SMALL_W - 2 * GDN_HEADS)))
    wcat = jnp.concatenate([w_in[:, :s2], small, w_in[:, s4:]], axis=1).astype(BF16)
    qkva, z, ba, *qkvb = _inproj(x2, sc_a, sh_a, norm_mix.reshape(1, d), wcat, w_conv, seq)

    pad4 = lambda v: jnp.pad(v.astype(F32), (GDN_HEADS, LANES - 2 * GDN_HEADS))
    gp = jnp.zeros((8, LANES), F32).at[0].set(pad4(a_log)).at[1].set(pad4(dt_bias))
    og = _gdn(qkva, ba, ba[:, :8].T, gp, _gp_cols(a_log, dt_bias),
              _block_tri(GDN_TOK, CHUNK, True), _block_tri(GDN_TOK, CHUNK, False), seq)

    outs, lses = [], []
    for (window, dil), qkvb_d in zip(DILATED_PAIRS, qkvb):
        o_i, lse_i = _attn_branch(qkvb_d, jnp.asarray(_bucket_table(window, dil)), rel_bias.astype(F32),
                                  bsz, seq, dil)
        outs.append(o_i)
        lses.append(lse_i)

    expand = jnp.where((jnp.arange(LANES)[:, None] == jnp.arange(ATTN_W)[None, :] // ATTN_DH), 1.0, 0.0).astype(BF16)
    lstrict = jnp.where(jnp.arange(TM_POST)[:, None] > jnp.arange(TM_POST)[None, :], 1.0, 0.0).astype(BF16)
    wr = jnp.pad(jnp.concatenate([w_rg, w_re], axis=1).astype(F32), ((0, 0), (0, LANES - N_GROUPS - N_EXPERTS)))
    wrh = wr.astype(BF16)
    wr_split = jnp.concatenate([wrh, (wr - wrh.astype(F32)).astype(BF16)], axis=1)
    x1, h2, route, cnt = _post(og, z, outs, lses, x2, gt_a, sc_f, sh_f,
                               gdn_norm.reshape(1, GDN_DV), attn_norm.reshape(1, ATTN_W),
                               w_out.astype(BF16), norm_ffn.reshape(1, d), wr_split, expand, lstrict, seq)

    blk = EXPERT_BLK
    counts = cnt[0, :N_EXPERTS].astype(jnp.int32)
    padded = (counts + blk - 1) // blk * blk
    pends = jnp.cumsum(padded)
    pstarts = pends - padded
    eids = route[:, 0:TOP_K].astype(jnp.int32)
    ranks = route[:, 4:4 + TOP_K].astype(jnp.int32)
    expert_ids = jnp.arange(N_EXPERTS, dtype=jnp.int32)
    seg_start = jnp.sum(jnp.where(eids[..., None] == expert_ids, pstarts, 0), axis=-1)
    dest = (seg_start + ranks).reshape(t * TOP_K)
    a = t * TOP_K
    p = -(-a // blk) * blk + N_EXPERTS * blk
    nblk = p // blk
    blk_start = jnp.arange(nblk, dtype=jnp.int32) * blk
    blk_e = jnp.minimum(jnp.sum((pends[None, :] <= blk_start[:, None]).astype(jnp.int32), axis=1),
                        N_EXPERTS - 1)
    nused = (pends[-1] // blk).astype(jnp.int32).reshape(1)

    xs = _dispatch(dest, pstarts + counts, padded - counts, nused, h2, p)
    ys = _experts(blk_e, nused, xs, w_gate, w_up, w_down)
    return ys, dest, x1, route, gt_f


def _gp_cols(a_log, dt_bias):
    z = jnp.zeros((8, LANES), F32)
    z = z.at[GDN_HEADS:2 * GDN_HEADS, 0].set(a_log.astype(F32))
    z = z.at[GDN_HEADS:2 * GDN_HEADS, 1].set(dt_bias.astype(F32))
    return z


def kernel(x, c, w_ada, b_ada, norm_mix, w_in, w_conv, a_log, dt_bias, gdn_norm, attn_norm, w_out,
           rel_bias, norm_ffn, w_router_group, w_router_expert, w_gate, w_up, w_down, norm_final):
    bsz, seq, d = x.shape
    depth = w_ada.shape[0]
    x2 = x.reshape(bsz * seq, d)
    for l in range(depth):
        ys, dest, x1, route, gt_f = _layer(
            x2, c, bsz, seq, w_ada[l], b_ada[l], norm_mix[l], w_in[l], w_conv[l], a_log[l], dt_bias[l],
            gdn_norm[l], attn_norm[l], w_out[l], rel_bias, norm_ffn[l], w_router_group[l],
            w_router_expert[l], w_gate[l], w_up[l], w_down[l])
        x2 = _combine(dest, ys, x1, route, gt_f, norm_final.reshape(1, d), seq, l == depth - 1)
    return x2.reshape(bsz, seq, d)
```

```python
import functools
import math

import jax
import jax.numpy as jnp
import numpy as np
from jax import lax
from jax.experimental import pallas as pl
from jax.experimental.pallas import tpu as pltpu

D_MODEL = 1024
GDN_HEADS = 4
GDN_DK = 128
GDN_DV = 128
CONV_K = 4
CHUNK = 64
ATTN_HEADS = 8
ATTN_DH = 64
DILATED_PAIRS = ((128, 1), (512, 4), (2048, 16))
QBLK = 128
NUM_BUCKETS = 32
REL_MAX_DIST = 2048
N_GROUPS = 4
EXPERTS_PER_GROUP = 8
N_EXPERTS = N_GROUPS * EXPERTS_PER_GROUP
TOP_K = 2
D_EXPERT = 256
EPS = 1e-6
NEG_INF = -1e30

GDN_QK_W = GDN_HEADS * GDN_DK
GDN_V_W = GDN_HEADS * GDN_DV
ATTN_W = ATTN_HEADS * ATTN_DH
GDN_CONV_W = 2 * GDN_QK_W + GDN_V_W
LANES = 128
SUBLANES = 8
SMALL_W = LANES
W_COLS = GDN_CONV_W + GDN_V_W + SMALL_W + 3 * ATTN_W

TM_PROJ = 512
CONV_BLK = 128
HEAD_ROWS = 16
TM_POST = 512
TM_ROWS = 256
GDN_TOK = 512
GDN_GROUP = 4
INV_BASE = 8
ATTN_SUB = 4
EXPERT_BLK = 512
ROW_UNROLL = 8
VMEM_LIMIT = 56 * 1024 * 1024

F32 = jnp.float32
BF16 = jnp.bfloat16
HIGHEST = lax.Precision.HIGHEST


def _sigmoid(x):
    return 1.0 / (1.0 + jnp.exp(-x))


def _dot(a, b, precision=None):
    return jnp.dot(a, b, preferred_element_type=F32, precision=precision)


def _dot_nt(a, b, precision=None):
    return lax.dot_general(a, b, (((1,), (1,)), ((), ())), preferred_element_type=F32,
                           precision=precision)


def _dot_tn(a, b, precision=None):
    return lax.dot_general(a, b, (((0,), (0,)), ((), ())), preferred_element_type=F32,
                           precision=precision)


def _params(sem):
    return pltpu.CompilerParams(dimension_semantics=sem, vmem_limit_bytes=VMEM_LIMIT)


def _adaln_kernel(c_ref, w_ref, b_ref, o_ref):
    o_ref[...] = _dot(c_ref[...], w_ref[...], HIGHEST) + b_ref[...]


def _adaln(c, w, b):
    bsz, d = c.shape
    n = w.shape[1]
    tn = 1024
    return pl.pallas_call(
        _adaln_kernel,
        grid=(n // tn,),
        in_specs=[pl.BlockSpec((bsz, d), lambda j: (0, 0)),
                  pl.BlockSpec((d, tn), lambda j: (0, j)),
                  pl.BlockSpec((1, tn), lambda j: (0, j))],
        out_specs=pl.BlockSpec((bsz, tn), lambda j: (0, j)),
        out_shape=jax.ShapeDtypeStruct((bsz, n), F32),
        compiler_params=_params(("arbitrary",)),
        name="adaln",
    )(c, w, b.reshape(1, n))


def _inproj_kernel(tiles_per_seq, x_ref, sc_ref, sh_ref, g_ref, w_ref, wconv_ref, shift_ref,
                   qkva_ref, z_ref, ba_ref, *rest):
    qkvb_refs = rest[:len(DILATED_PAIRS)]
    win_scr, pb_scr, pb_next_scr = rest[len(DILATED_PAIRS):]
    i = pl.program_id(0)
    tm = x_ref.shape[0]
    x = x_ref[...]
    h = x * lax.rsqrt(jnp.mean(x * x, axis=-1, keepdims=True) + EPS) * g_ref[...]
    h = h * (1.0 + sc_ref[0]) + sh_ref[0]
    hb = h.astype(BF16)

    @pl.when(i % tiles_per_seq == 0)
    def _():
        win_scr[0:HEAD_ROWS, :] = jnp.zeros((HEAD_ROWS, GDN_CONV_W), F32)

    pa = _dot(hb, w_ref[:, 0:GDN_CONV_W])
    pa16 = pa.astype(BF16)
    for b in range(tm // CONV_BLK):
        rs = slice(b * CONV_BLK, (b + 1) * CONV_BLK)
        acc = pa[rs] * wconv_ref[CONV_K - 1:CONV_K, :]
        for j in range(CONV_K - 1):
            if b == 0:
                shifted = _dot(shift_ref[j, :, CONV_BLK:], pa16[rs])
            else:
                shifted = _dot(shift_ref[j], pa16[(b - 1) * CONV_BLK:(b + 1) * CONV_BLK])
            acc = acc + shifted * wconv_ref[j:j + 1, :]
        qkva_ref[rs, :] = (acc * _sigmoid(acc)).astype(qkva_ref.dtype)
    win_scr[HEAD_ROWS:2 * HEAD_ROWS, :] = pa[0:HEAD_ROWS]
    acc = pa[0:HEAD_ROWS] * wconv_ref[CONV_K - 1:CONV_K, :]
    for j in range(CONV_K - 1):
        acc = acc + win_scr[pl.ds(HEAD_ROWS - (CONV_K - 1) + j, HEAD_ROWS), :] * wconv_ref[j:j + 1, :]
    qkva_ref[0:HEAD_ROWS, :] = (acc * _sigmoid(acc)).astype(qkva_ref.dtype)
    win_scr[0:HEAD_ROWS, :] = pa[tm - HEAD_ROWS:tm]

    c0 = GDN_CONV_W
    z_ref[...] = _dot(hb, w_ref[:, c0:c0 + GDN_V_W]).astype(z_ref.dtype)
    c0 += GDN_V_W
    ba_ref[...] = _dot(hb, w_ref[:, c0:c0 + SMALL_W])
    c0 += SMALL_W
    pb = _dot(hb, w_ref[:, c0:c0 + 3 * ATTN_W])
    ncol = 3 * ATTN_W // LANES
    for j in range(ncol):
        pb_scr[j] = pb[:, j * LANES:(j + 1) * LANES]
    prev_dil, prev_scr, next_scr = 1, pb_scr, pb_next_scr
    for bi, ((_, dil), ref) in enumerate(zip(DILATED_PAIRS, qkvb_refs)):
        if dil == 1:
            ref[...] = pb.astype(ref.dtype)
            continue
        f = dil // prev_dil
        n = tm // dil
        keep = bi + 1 < len(DILATED_PAIRS)
        for r in range(prev_dil):
            for q in range(f):
                r_new = r + prev_dil * q
                for j in range(ncol):
                    rows = prev_scr[j, pl.ds(r * (tm // prev_dil) + q, n, stride=f), :]
                    c1 = r_new * 3 * ATTN_W + j * LANES
                    ref[:, c1:c1 + LANES] = rows.astype(ref.dtype)
                    if keep:
                        next_scr[j, r_new * n:(r_new + 1) * n, :] = rows
        prev_dil, prev_scr, next_scr = dil, next_scr, prev_scr


def _inproj(x2, sc, sh, g, wcat, wconv, seq):
    t, d = x2.shape
    tm = TM_PROJ
    tps = seq // tm
    mod_spec = pl.BlockSpec((1, 1, d), lambda i: (i // tps, 0, 0))
    return pl.pallas_call(
        functools.partial(_inproj_kernel, tps),
        grid=(t // tm,),
        in_specs=[pl.BlockSpec((tm, d), lambda i: (i, 0)),
                  mod_spec, mod_spec,
                  pl.BlockSpec((1, d), lambda i: (0, 0)),
                  pl.BlockSpec((d, W_COLS), lambda i: (0, 0)),
                  pl.BlockSpec((CONV_K, GDN_CONV_W), lambda i: (0, 0)),
                  pl.BlockSpec((CONV_K - 1, CONV_BLK, 2 * CONV_BLK), lambda i: (0, 0, 0))],
        out_specs=[pl.BlockSpec((tm, GDN_CONV_W), lambda i: (i, 0)),
                   pl.BlockSpec((tm, GDN_V_W), lambda i: (i, 0)),
                   pl.BlockSpec((tm, SMALL_W), lambda i: (i, 0))]
                  + [pl.BlockSpec((tm // dil, dil * 3 * ATTN_W), lambda i: (i, 0)) for _, dil in DILATED_PAIRS],
        out_shape=[jax.ShapeDtypeStruct((t, GDN_CONV_W), BF16),
                   jax.ShapeDtypeStruct((t, GDN_V_W), BF16),
                   jax.ShapeDtypeStruct((t, SMALL_W), F32)]
                  + [jax.ShapeDtypeStruct((t // dil, dil * 3 * ATTN_W), BF16) for _, dil in DILATED_PAIRS],
        scratch_shapes=[pltpu.VMEM((2 * HEAD_ROWS, GDN_CONV_W), F32),
                        pltpu.VMEM((3 * ATTN_W // LANES, tm, LANES), F32),
                        pltpu.VMEM((3 * ATTN_W // LANES, tm, LANES), F32)],
        compiler_params=_params(("arbitrary",)),
        name="inproj",
    )(x2, sc, sh, g, wcat, wconv, _shift_bands())


def _shift_bands():
    t = np.arange(CONV_BLK)[:, None]
    c = np.arange(2 * CONV_BLK)[None, :]
    bands = [(c == CONV_BLK + t - (CONV_K - 1 - j)) for j in range(CONV_K - 1)]
    return jnp.asarray(np.stack(bands).astype(np.float32), dtype=BF16)


def _softplus(x):
    return jnp.maximum(x, 0.0) + jnp.log(1.0 + jnp.exp(-jnp.abs(x)))


def _unit_lower_inverses(n_list):
    c = n_list[0].shape[0]
    row = lax.broadcasted_iota(jnp.int32, (c, c), 0)
    col = lax.broadcasted_iota(jnp.int32, (c, c), 1)
    eye = jnp.where(row == col, 1.0, 0.0)
    same_base = (row // INV_BASE) == (col // INV_BASE)
    n0 = [jnp.where(same_base, n, 0.0) for n in n_list]
    p = [eye - x for x in n0]
    m = [x.astype(BF16) for x in n0]
    for j in range(int(math.log2(INV_BASE)) - 1):
        m = [_dot(x, x).astype(BF16) for x in m]
        p = [x + _dot(x.astype(BF16), y) for x, y in zip(p, m)]
    size = INV_BASE
    while size < c:
        sibling = jnp.logical_and((row // size) % 2 == 1, (col // size) == (row // size) - 1)
        cb = [jnp.where(sibling, n, 0.0).astype(BF16) for n in n_list]
        pb = [x.astype(BF16) for x in p]
        pc = [_dot(x, y).astype(BF16) for x, y in zip(pb, cb)]
        p = [x - _dot(y, z) for x, y, z in zip(p, pc, pb)]
        size *= 2
    return p


def _split3_bf16(x):
    h1 = x.astype(BF16)
    r1 = x - h1.astype(F32)
    h2 = r1.astype(BF16)
    h3 = (r1 - h2.astype(F32)).astype(BF16)
    return h1, h2, h3


def _gdn_kernel(steps_per_seq, qkv_ref, ba_ref, bat_ref, gp_ref, gpt_ref, lblk_ref, ublk_ref,
                o_ref, state_scr):
    i = pl.program_id(0)

    @pl.when(i % steps_per_seq == 0)
    def _():
        state_scr[...] = jnp.zeros_like(state_scr)

    nchunk = GDN_TOK // CHUNK
    ba = ba_ref[...]
    bat = bat_ref[...]
    a_vec = -jnp.exp(gp_ref[0:1, :])
    g_tile = a_vec * _softplus(ba + gp_ref[1:2, :])
    lblk = lblk_ref[...]
    g_cum = sum(_dot(lblk, part) for part in _split3_bf16(g_tile))
    a_col = -jnp.exp(gpt_ref[:, 0:1])
    gt_tile = a_col * _softplus(bat + gpt_ref[:, 1:2])
    ublk = ublk_ref[...]
    gt_cum = sum(_dot(part, ublk) for part in _split3_bf16(gt_tile))
    beta_tile = _sigmoid(ba)

    row = lax.broadcasted_iota(jnp.int32, (CHUNK, CHUNK), 0)
    col = lax.broadcasted_iota(jnp.int32, (CHUNK, CHUNK), 1)
    incl = row >= col
    strict = row > col

    for h0 in range(0, GDN_HEADS, GDN_GROUP):
        heads = range(h0, h0 + GDN_GROUP)
        kb_l, k_l, q_l, rhs_l, qg_l, kg_l, gl_l, inc_l, str_l = ([] for _ in range(9))
        for h in heads:
            q_raw = qkv_ref[:, h * GDN_DK:(h + 1) * GDN_DK].astype(F32)
            k_raw = qkv_ref[:, GDN_QK_W + h * GDN_DK:GDN_QK_W + (h + 1) * GDN_DK].astype(F32)
            v = qkv_ref[:, 2 * GDN_QK_W + h * GDN_DV:2 * GDN_QK_W + (h + 1) * GDN_DV].astype(F32)
            qn = q_raw * lax.rsqrt(jnp.sum(q_raw * q_raw, axis=-1, keepdims=True) + EPS) * (GDN_DK ** -0.5)
            kn = k_raw * lax.rsqrt(jnp.sum(k_raw * k_raw, axis=-1, keepdims=True) + EPS)
            beta = beta_tile[:, h:h + 1]
            gc_all = g_cum[:, GDN_HEADS + h:GDN_HEADS + h + 1]
            eg = jnp.exp(gc_all)
            kb = kn * beta
            rhs = jnp.concatenate([v * beta, kb * eg], axis=1).astype(BF16)
            qg = qn * eg
            kb16, k16, q16 = kb.astype(BF16), kn.astype(BF16), qn.astype(BF16)
            for c in range(nchunk):
                sl = slice(c * CHUNK, (c + 1) * CHUNK)
                g_c = gc_all[sl]
                g_r = gt_cum[GDN_HEADS + h:GDN_HEADS + h + 1, sl]
                dec = jnp.exp(g_c - g_r)
                inc_l.append(jnp.where(incl, dec, 0.0))
                str_l.append(jnp.where(strict, dec, 0.0))
                g_last = g_c[CHUNK - 1:CHUNK, :]
                kg_l.append((kn[sl] * jnp.exp(g_last - g_c)).astype(BF16))
                gl_l.append(jnp.exp(g_last))
                kb_l.append(kb16[sl])
                k_l.append(k16[sl])
                q_l.append(q16[sl])
                rhs_l.append(rhs[sl])
                qg_l.append(qg[sl])
        nprob = len(k_l)
        n_l = [_dot_nt(kb_l[j], k_l[j]) * str_l[j] for j in range(nprob)]
        aqk_l = [(_dot_nt(q_l[j], k_l[j]) * inc_l[j]).astype(BF16) for j in range(nprob)]
        tinv_l = _unit_lower_inverses(n_l)
        sol_l = [_dot(tinv_l[j].astype(BF16), rhs_l[j]).astype(BF16) for j in range(nprob)]
        a2_l = [_dot(aqk_l[j], sol_l[j]) for j in range(nprob)]
        k2_l = [_dot_tn(kg_l[j], sol_l[j]) for j in range(nprob)]
        states = [state_scr[h] for h in heads]
        for c in range(nchunk):
            sl = slice(c * CHUNK, (c + 1) * CHUNK)
            for hi, h in enumerate(heads):
                j = hi * nchunk + c
                sb = states[hi].astype(BF16)
                qeff = (qg_l[j] - a2_l[j][:, GDN_DV:]).astype(BF16)
                o = _dot(qeff, sb) + a2_l[j][:, :GDN_DV]
                states[hi] = gl_l[j] * states[hi] + k2_l[j][:, :GDN_DV] - _dot(k2_l[j][:, GDN_DV:].astype(BF16), sb)
                o_ref[sl, h * GDN_DV:(h + 1) * GDN_DV] = o.astype(o_ref.dtype)
        for hi, h in enumerate(heads):
            state_scr[h] = states[hi]


def _gdn(qkva, ba, bat, gp, gpt, lblk, ublk, seq):
    t = qkva.shape[0]
    tok = GDN_TOK
    sps = seq // tok
    const = lambda i: (0, 0)
    return pl.pallas_call(
        functools.partial(_gdn_kernel, sps),
        grid=(t // tok,),
        in_specs=[pl.BlockSpec((tok, GDN_CONV_W), lambda i: (i, 0)),
                  pl.BlockSpec((tok, SMALL_W), lambda i: (i, 0)),
                  pl.BlockSpec((8, tok), lambda i: (0, i)),
                  pl.BlockSpec((8, LANES), const),
                  pl.BlockSpec((8, LANES), const),
                  pl.BlockSpec((tok, tok), const),
                  pl.BlockSpec((tok, tok), const)],
        out_specs=pl.BlockSpec((tok, GDN_V_W), lambda i: (i, 0)),
        out_shape=jax.ShapeDtypeStruct((t, GDN_V_W), F32),
        scratch_shapes=[pltpu.VMEM((GDN_HEADS, GDN_DK, GDN_DV), F32)],
        compiler_params=_params(("arbitrary",)),
        name="gdn",
    )(qkva, ba, bat, gp, gpt, lblk, ublk)


def _attn_kernel(q_ref, kp_ref, kc_ref, vp_ref, vc_ref, bucket_ref, rb_ref, o_ref, lse_ref, bias_ref):
    n = pl.program_id(2)

    @pl.when(jnp.logical_and(jnp.logical_and(pl.program_id(0) == 0, pl.program_id(1) == 0), n == 0))
    def _():
        bucket = bucket_ref[...]
        col = lax.broadcasted_iota(jnp.int32, (QBLK, 2 * QBLK), 1)
        for h in range(ATTN_HEADS):
            acc = jnp.full((QBLK, 2 * QBLK), NEG_INF, F32)
            for b in range(NUM_BUCKETS):
                acc = jnp.where(bucket == b, rb_ref[b, h], acc)
            bias_ref[0, h] = acc
            bias_ref[1, h] = jnp.where(col < QBLK, NEG_INF, acc)

    nsub = q_ref.shape[1] // QBLK
    first = jnp.where(n == 0, 1, 0)
    q_all = q_ref[0]
    k_all = jnp.concatenate([kp_ref[0], kc_ref[0]], axis=0)
    v_all = jnp.concatenate([vp_ref[0], vc_ref[0]], axis=0)
    lane = lax.broadcasted_iota(jnp.int32, (QBLK, LANES), 1)
    ones = jnp.ones((2 * QBLK, LANES), BF16)
    per_group = LANES // ATTN_DH
    ngrp = ATTN_HEADS // per_group
    scale = ATTN_DH ** -0.5
    mine = [jnp.logical_and(lane >= j * ATTN_DH, lane < (j + 1) * ATTN_DH) for j in range(per_group)]
    probs = [(u, h) for u in range(nsub) for h in range(ATTN_HEADS)]
    s_l = []
    for u, h in probs:
        grp, j = divmod(h, per_group)
        gs = slice(grp * LANES, (grp + 1) * LANES)
        qg = q_all[u * QBLK:(u + 1) * QBLK, gs] * scale
        table = first if u == 0 else 0
        s_l.append(_dot_nt(jnp.where(mine[j], qg, jnp.zeros_like(qg)), k_all[u * QBLK:(u + 2) * QBLK, gs])
                   + bias_ref[table, h])
    m_l = [jnp.max(s, axis=-1, keepdims=True) for s in s_l]
    p_l = [jnp.exp(s - m).astype(BF16) for s, m in zip(s_l, m_l)]
    v_ext = {(u, grp): jnp.concatenate([v_all[u * QBLK:(u + 2) * QBLK, grp * LANES:(grp + 1) * LANES], ones],
                                       axis=1)
             for u in range(nsub) for grp in range(ngrp)}
    pv_l = [_dot(p, v_ext[(u, h // per_group)]) for p, (u, h) in zip(p_l, probs)]
    for u in range(nsub):
        outs = []
        lse_tile = jnp.zeros((QBLK, LANES), F32)
        for grp in range(ngrp):
            o_grp = jnp.zeros((QBLK, LANES), F32)
            for j in range(per_group):
                h = grp * per_group + j
                idx = u * ATTN_HEADS + h
                den = pv_l[idx][:, LANES:]
                o_grp = jnp.where(mine[j], pv_l[idx][:, :LANES] / den, o_grp)
                lse_tile = jnp.where(lane == h, m_l[idx] + jnp.log(den[:, 0:1]), lse_tile)
            outs.append(o_grp)
        o_ref[0, u * QBLK:(u + 1) * QBLK, :] = jnp.concatenate(outs, axis=1).astype(o_ref.dtype)
        lse_ref[0, u * QBLK:(u + 1) * QBLK, :] = lse_tile


def _attn_branch(qkvb, bucket, rel_bias, bsz, seq, dil):
    ln = seq // dil
    sub = min(ATTN_SUB, ln // QBLK)
    rows = sub * QBLK
    assert seq % dil == 0 and ln % rows == 0, (seq, dil, rows)
    nb = ln // rows
    xv = qkvb.reshape(bsz, ln, dil * 3 * ATTN_W)
    w = ATTN_W
    cur = lambda part: (lambda b, r, n: (b, n, r * 3 + part))
    prev = lambda part: (lambda b, r, n: (b, jnp.maximum(n * sub - 1, 0), r * 3 + part))
    o, lse = pl.pallas_call(
        _attn_kernel,
        grid=(bsz, dil, nb),
        in_specs=[pl.BlockSpec((1, rows, w), cur(0)),
                  pl.BlockSpec((1, QBLK, w), prev(1)),
                  pl.BlockSpec((1, rows, w), cur(1)),
                  pl.BlockSpec((1, QBLK, w), prev(2)),
                  pl.BlockSpec((1, rows, w), cur(2)),
                  pl.BlockSpec((QBLK, 2 * QBLK), lambda b, r, n: (0, 0)),
                  pl.BlockSpec(memory_space=pltpu.SMEM)],
        out_specs=[pl.BlockSpec((1, rows, w), lambda b, r, n: (b, n, r)),
                   pl.BlockSpec((1, rows, LANES), lambda b, r, n: (b, n, r))],
        out_shape=[jax.ShapeDtypeStruct((bsz, ln, dil * w), BF16),
                   jax.ShapeDtypeStruct((bsz, ln, dil * LANES), F32)],
        scratch_shapes=[pltpu.VMEM((2, ATTN_HEADS, QBLK, 2 * QBLK), F32)],
        compiler_params=_params(("arbitrary", "arbitrary", "arbitrary")),
        name=f"attn_d{dil}",
    )(xv, xv, xv, xv, xv, bucket, rel_bias)
    return o.reshape(bsz * ln, dil * w), lse.reshape(bsz * ln, dil * LANES)


def _bucket_table(window, dil):
    steps = window // dil
    qi = np.arange(QBLK)[:, None]
    kj = np.arange(2 * QBLK)[None, :]
    delta = qi + QBLK - kj
    dist = np.maximum(delta, 0) * dil
    max_exact = NUM_BUCKETS // 2
    d_f = np.maximum(dist, 1).astype(np.float32)
    large = max_exact + (np.log(d_f / max_exact) / math.log(REL_MAX_DIST / max_exact)
                         * (NUM_BUCKETS - max_exact)).astype(np.int32)
    bucket = np.where(dist < max_exact, dist, np.minimum(large, NUM_BUCKETS - 1))
    return np.where((delta >= 0) & (delta <= steps), bucket, -1).astype(np.int32)


def _split_bf16(x):
    hi = x.astype(BF16)
    lo = (x - hi.astype(F32)).astype(BF16)
    return hi, lo


def _post_kernel(og_ref, z_ref, o1_ref, o2_ref, o3_ref, l1_ref, l2_ref, l3_ref, x_ref,
                 gta_ref, scf_ref, shf_ref, gnorm_ref, anorm_ref, wout_ref, nffn_ref,
                 wrh_ref, exp_ref, ls_ref,
                 x1_ref, h2_ref, route_ref, cnt_ref, carry_scr, *tok_scr):
    i = pl.program_id(0)
    tm = x_ref.shape[0]

    @pl.when(i == 0)
    def _():
        carry_scr[...] = jnp.zeros_like(carry_scr)

    nbr = len(DILATED_PAIRS)
    ob_scr, lse_scr = tok_scr[:nbr], tok_scr[nbr:]
    for (_, dil), o_ref, l_ref, o_s, l_s in zip(DILATED_PAIRS, (o1_ref, o2_ref, o3_ref),
                                                 (l1_ref, l2_ref, l3_ref), ob_scr, lse_scr):
        for r in range(dil):
            rows = pl.ds(r, tm // dil, stride=dil) if dil > 1 else slice(None)
            for j in range(ATTN_W // LANES):
                c1 = r * ATTN_W + j * LANES
                o_s[j, rows, :] = o_ref[:, c1:c1 + LANES].astype(F32)
            l_s[rows, :] = l_ref[:, r * LANES:(r + 1) * LANES]

    heads = []
    for h in range(GDN_HEADS):
        hs = slice(h * GDN_DV, (h + 1) * GDN_DV)
        seg = og_ref[:, hs]
        nrm = seg * lax.rsqrt(jnp.mean(seg * seg, axis=-1, keepdims=True) + EPS) * gnorm_ref[...]
        zz = z_ref[:, hs].astype(F32)
        heads.append((nrm * (zz * _sigmoid(zz))).astype(BF16))
    oa = jnp.concatenate(heads, axis=1)

    l1, l2, l3 = (l_s[...] for l_s in lse_scr)
    m = jnp.maximum(jnp.maximum(l1, l2), l3)
    e1, e2, e3 = jnp.exp(l1 - m), jnp.exp(l2 - m), jnp.exp(l3 - m)
    inv = 1.0 / (e1 + e2 + e3)
    ob = jnp.zeros((tm, ATTN_W), F32)
    for e, o_s in zip((e1, e2, e3), ob_scr):
        hi, lo = _split_bf16(e * inv)
        wexp = _dot(hi, exp_ref[...]) + _dot(lo, exp_ref[...])
        ob = ob + wexp * jnp.concatenate([o_s[j] for j in range(ATTN_W // LANES)], axis=1)
    ob = ob * lax.rsqrt(jnp.mean(ob * ob, axis=-1, keepdims=True) + EPS) * anorm_ref[...]

    mix = _dot(oa, wout_ref[0:GDN_V_W, :]) + _dot(ob.astype(BF16), wout_ref[GDN_V_W:, :])
    x1 = x_ref[...] + gta_ref[0] * mix
    x1_ref[...] = x1
    h2 = x1 * lax.rsqrt(jnp.mean(x1 * x1, axis=-1, keepdims=True) + EPS) * nffn_ref[...]
    h2 = h2 * (1.0 + scf_ref[0]) + shf_ref[0]
    h2_ref[...] = h2

    hh, hl = _split_bf16(h2)
    both = _dot(hh, wrh_ref[...])
    logits = both[:, :LANES] + both[:, LANES:] + _dot(hl, wrh_ref[:, :LANES])
    tm = logits.shape[0]
    lane = lax.broadcasted_iota(jnp.int32, (tm, LANES), 1).astype(F32)
    big = float(LANES)
    gmask = lane < N_GROUPS
    glog = jnp.where(gmask, logits, NEG_INF)
    gmax = jnp.max(glog, axis=-1, keepdims=True)
    gidx = jnp.min(jnp.where(jnp.logical_and(gmask, glog == gmax), lane, big), axis=-1, keepdims=True)
    gprob = 1.0 / jnp.sum(jnp.where(gmask, jnp.exp(glog - gmax), 0.0), axis=-1, keepdims=True)
    lo_lane = N_GROUPS + EXPERTS_PER_GROUP * gidx
    emask = jnp.logical_and(lane >= lo_lane, lane < lo_lane + EXPERTS_PER_GROUP)
    elog = jnp.where(emask, logits, NEG_INF)
    m1 = jnp.max(elog, axis=-1, keepdims=True)
    i1 = jnp.min(jnp.where(jnp.logical_and(emask, elog == m1), lane, big), axis=-1, keepdims=True)
    emask2 = jnp.logical_and(emask, lane != i1)
    elog2 = jnp.where(emask2, logits, NEG_INF)
    m2 = jnp.max(elog2, axis=-1, keepdims=True)
    i2 = jnp.min(jnp.where(jnp.logical_and(emask2, elog2 == m2), lane, big), axis=-1, keepdims=True)
    r = jnp.exp(m2 - m1)
    gate1 = gprob / (1.0 + r)
    gate2 = gprob * r / (1.0 + r)
    ex1 = i1 - N_GROUPS
    ex2 = i2 - N_GROUPS

    hit1 = lane == ex1
    hit2 = lane == ex2
    onehot = jnp.where(jnp.logical_or(hit1, hit2), 1.0, 0.0)
    pref = _dot(ls_ref[...], onehot.astype(BF16)) + carry_scr[...]
    rank1 = jnp.sum(jnp.where(hit1, pref, 0.0), axis=-1, keepdims=True)
    rank2 = jnp.sum(jnp.where(hit2, pref, 0.0), axis=-1, keepdims=True)
    carry = carry_scr[...] + jnp.sum(onehot, axis=0, keepdims=True)
    carry_scr[...] = carry
    cnt_ref[...] = jnp.broadcast_to(carry, cnt_ref.shape)

    route = jnp.zeros((tm, LANES), F32)
    for idx, val in enumerate((ex1, ex2, gate1, gate2, rank1, rank2)):
        route = jnp.where(lane == idx, val, route)
    route_ref[...] = route


def _post(og, z, os_, ls_, x2, gta, scf, shf, gnorm, anorm, wout, nffn, wrh, expand, lstrict, seq):
    t, d = x2.shape
    tm = TM_POST
    tps = seq // tm
    tile = lambda w: pl.BlockSpec((tm, w), lambda i: (i, 0))
    const2 = lambda a: pl.BlockSpec(a.shape, lambda i: (0, 0))
    mod_spec = pl.BlockSpec((1, 1, d), lambda i: (i // tps, 0, 0))
    return pl.pallas_call(
        _post_kernel,
        grid=(t // tm,),
        in_specs=[tile(GDN_V_W), tile(GDN_V_W)]
                 + [pl.BlockSpec((tm // dil, dil * ATTN_W), lambda i: (i, 0)) for _, dil in DILATED_PAIRS]
                 + [pl.BlockSpec((tm // dil, dil * LANES), lambda i: (i, 0)) for _, dil in DILATED_PAIRS]
                 + [tile(d),
                  mod_spec, mod_spec, mod_spec,
                  const2(gnorm), const2(anorm), const2(wout), const2(nffn),
                  const2(wrh), const2(expand), const2(lstrict)],
        out_specs=[tile(d), tile(d), tile(LANES), pl.BlockSpec((8, LANES), lambda i: (0, 0))],
        out_shape=[jax.ShapeDtypeStruct((t, d), F32),
                   jax.ShapeDtypeStruct((t, d), F32),
                   jax.ShapeDtypeStruct((t, LANES), F32),
                   jax.ShapeDtypeStruct((8, LANES), F32)],
        scratch_shapes=[pltpu.VMEM((1, LANES), F32)]
                       + [pltpu.VMEM((ATTN_W // LANES, tm, LANES), F32) for _ in DILATED_PAIRS]
                       + [pltpu.VMEM((tm, LANES), F32) for _ in DILATED_PAIRS],
        compiler_params=_params(("arbitrary",)),
        name="post",
    )(og, z, *os_, *ls_, x2, gta, scf, shf, gnorm, anorm, wout, nffn, wrh, expand, lstrict)


def _dispatch_kernel(dest_ref, fill_ref, pad_ref, nused_ref, h2_ref, xs_ref, zero_scr, sem, fill_sem):
    i = pl.program_id(0)
    tm = h2_ref.shape[0]

    @pl.when(i == 0)
    def _():
        zero_scr[...] = jnp.zeros_like(zero_scr)

        def pieces(e, act):
            base, pad = fill_ref[e], pad_ref[e]
            head = pad & (SUBLANES - 1)
            for j in range(SUBLANES - 1):
                @pl.when(j < head)
                def _():
                    act(pltpu.make_async_copy(zero_scr.at[pl.ds(0, 1)], xs_ref.at[pl.ds(base + j, 1)],
                                              fill_sem))
            off = base + head
            for bit in range(SUBLANES.bit_length() - 1, EXPERT_BLK.bit_length() - 1):
                size = 1 << bit

                @pl.when((pad >> bit) & 1 == 1)
                def _():
                    act(pltpu.make_async_copy(zero_scr.at[pl.ds(0, size)],
                                              xs_ref.at[pl.ds(pl.multiple_of(off, SUBLANES), size)], fill_sem))
                off = off + (pad & size)

        def fill(e, carry):
            pieces(e, lambda cp: cp.start())
            return carry

        def drain(e, carry):
            pieces(e, lambda cp: cp.wait())
            return carry

        lax.fori_loop(0, N_EXPERTS, fill, 0)
        lax.fori_loop(0, N_EXPERTS, drain, 0)

        def tail(b, act):
            for half in range(EXPERT_BLK // zero_scr.shape[0]):
                row0 = pl.multiple_of(b * EXPERT_BLK + half * zero_scr.shape[0], SUBLANES)
                act(pltpu.make_async_copy(zero_scr, xs_ref.at[pl.ds(row0, zero_scr.shape[0])], fill_sem))

        def tail_fill(b, carry):
            tail(b, lambda cp: cp.start())
            return carry

        def tail_drain(b, carry):
            tail(b, lambda cp: cp.wait())
            return carry

        nblk = xs_ref.shape[0] // EXPERT_BLK
        lax.fori_loop(nused_ref[0], nblk, tail_fill, 0)
        lax.fori_loop(nused_ref[0], nblk, tail_drain, 0)

    def start(r, carry):
        for k in range(TOP_K):
            d = dest_ref[(i * tm + r) * TOP_K + k]
            pltpu.make_async_copy(h2_ref.at[pl.ds(r, 1)], xs_ref.at[pl.ds(d, 1)], sem).start()
        return carry

    lax.fori_loop(0, tm, start, 0, unroll=ROW_UNROLL)
    for k in range(TOP_K):
        pltpu.make_async_copy(h2_ref, xs_ref.at[pl.ds(0, tm)], sem).wait()


def _dispatch(dest, fill_start, pad_rows, nused, h2, p):
    t, d = h2.shape
    tm = TM_ROWS
    grid_spec = pltpu.PrefetchScalarGridSpec(
        num_scalar_prefetch=4,
        grid=(t // tm,),
        in_specs=[pl.BlockSpec((tm, d), lambda i, *_: (i, 0))],
        out_specs=pl.BlockSpec(memory_space=pl.ANY),
        scratch_shapes=[pltpu.VMEM((EXPERT_BLK // 2, d), F32), pltpu.SemaphoreType.DMA,
                        pltpu.SemaphoreType.DMA],
    )
    return pl.pallas_call(
        _dispatch_kernel,
        grid_spec=grid_spec,
        out_shape=jax.ShapeDtypeStruct((p, d), F32),
        compiler_params=_params(("arbitrary",)),
        name="dispatch",
    )(dest, fill_start, pad_rows, nused, h2)


def _expert_kernel(blk_e_ref, nused_ref, xs_ref, wg_ref, wu_ref, wd_ref, ys_ref, wg16, wu16, wd16):
    b = pl.program_id(0)
    used = b < nused_ref[0]

    new_expert = jnp.logical_or(b == 0, blk_e_ref[b] != blk_e_ref[jnp.maximum(b - 1, 0)])

    @pl.when(jnp.logical_and(used, new_expert))
    def _():
        wg16[...] = wg_ref[0].astype(BF16)
        wu16[...] = wu_ref[0].astype(BF16)
        wd16[...] = wd_ref[0].astype(BF16)

    @pl.when(used)
    def _():
        x = xs_ref[...].astype(BF16)
        g = _dot(x, wg16[...])
        u = _dot(x, wu16[...])
        hid = (g * _sigmoid(g)) * u
        ys_ref[...] = _dot(hid.astype(BF16), wd16[...])

    @pl.when(b >= nused_ref[0])
    def _():
        ys_ref[...] = jnp.zeros_like(ys_ref)


def _experts(blk_e, nused, xs, wg, wu, wd):
    p, d = xs.shape
    blk = EXPERT_BLK
    row_map = lambda b, be, nu: (jnp.minimum(b, nu[0] - 1), 0)
    w_map = lambda b, be, nu: (be[b], 0, 0)
    grid_spec = pltpu.PrefetchScalarGridSpec(
        num_scalar_prefetch=2,
        grid=(p // blk,),
        in_specs=[pl.BlockSpec((blk, d), row_map),
                  pl.BlockSpec((1, d, D_EXPERT), w_map),
                  pl.BlockSpec((1, d, D_EXPERT), w_map),
                  pl.BlockSpec((1, D_EXPERT, d), w_map)],
        out_specs=pl.BlockSpec((blk, d), lambda b, be, nu: (b, 0)),
        scratch_shapes=[pltpu.VMEM((d, D_EXPERT), BF16), pltpu.VMEM((d, D_EXPERT), BF16),
                        pltpu.VMEM((D_EXPERT, d), BF16)],
    )
    return pl.pallas_call(
        _expert_kernel,
        grid_spec=grid_spec,
        out_shape=jax.ShapeDtypeStruct((p, d), F32),
        compiler_params=_params(("arbitrary",)),
        name="experts",
    )(blk_e, nused, xs, wg, wu, wd)


def _combine_kernel(final_norm, dest_ref, ys_ref, x1_ref, route_ref, gtf_ref, nf_ref, o_ref, ybuf, sem):
    i = pl.program_id(0)
    tm = x1_ref.shape[0]
    slot = i % 2

    def gather_tile(tile, slot_):
        def start(r, carry):
            for k in range(TOP_K):
                d = dest_ref[(tile * tm + r) * TOP_K + k]
                pltpu.make_async_copy(ys_ref.at[pl.ds(d, 1)], ybuf.at[slot_, k, pl.ds(r, 1)],
                                      sem.at[slot_]).start()
            return carry
        lax.fori_loop(0, tm, start, 0, unroll=ROW_UNROLL)

    @pl.when(i == 0)
    def _():
        gather_tile(0, 0)

    @pl.when(i + 1 < pl.num_programs(0))
    def _():
        gather_tile(i + 1, 1 - slot)

    for k in range(TOP_K):
        pltpu.make_async_copy(ys_ref.at[pl.ds(0, tm)], ybuf.at[slot, k], sem.at[slot]).wait()
    route = route_ref[...]
    moe = ybuf[slot, 0] * route[:, 2:3] + ybuf[slot, 1] * route[:, 3:4]
    x2 = x1_ref[...] + gtf_ref[0] * moe
    if final_norm:
        x2 = x2 * lax.rsqrt(jnp.mean(x2 * x2, axis=-1, keepdims=True) + EPS) * nf_ref[...]
    o_ref[...] = x2


def _combine(dest, ys, x1, route, gtf, nf, seq, final_norm):
    t, d = x1.shape
    tm = TM_ROWS
    tps = seq // tm
    grid_spec = pltpu.PrefetchScalarGridSpec(
        num_scalar_prefetch=1,
        grid=(t // tm,),
        in_specs=[pl.BlockSpec(memory_space=pl.ANY),
                  pl.BlockSpec((tm, d), lambda i, dest: (i, 0)),
                  pl.BlockSpec((tm, LANES), lambda i, dest: (i, 0)),
                  pl.BlockSpec((1, 1, d), lambda i, dest: (i // tps, 0, 0)),
                  pl.BlockSpec((1, d), lambda i, dest: (0, 0))],
        out_specs=pl.BlockSpec((tm, d), lambda i, dest: (i, 0)),
        scratch_shapes=[pltpu.VMEM((2, TOP_K, tm, d), F32), pltpu.SemaphoreType.DMA((2,))],
    )
    return pl.pallas_call(
        functools.partial(_combine_kernel, final_norm),
        grid_spec=grid_spec,
        out_shape=jax.ShapeDtypeStruct((t, d), F32),
        compiler_params=_params(("arbitrary",)),
        name="combine",
    )(dest, ys, x1, route, gtf, nf)


def _block_tri(n, chunk, lower):
    r = jnp.arange(n)[:, None]
    c = jnp.arange(n)[None, :]
    same = (r // chunk) == (c // chunk)
    tri = (r >= c) if lower else (r <= c)
    return jnp.where(same & tri, 1.0, 0.0).astype(BF16)


def _layer(x2, c, bsz, seq, w_ada, b_ada, norm_mix, w_in, w_conv, a_log, dt_bias, gdn_norm,
           attn_norm, w_out, rel_bias, norm_ffn, w_rg, w_re, w_gate, w_up, w_down):
    t, d = x2.shape
    mod = _adaln(c, w_ada, b_ada)
    sh_a, sc_a, gt_a, sh_f, sc_f, gt_f = [m.reshape(bsz, 1, d) for m in jnp.split(mod, 6, axis=-1)]

    s1 = GDN_CONV_W
    s2 = s1 + GDN_V_W
    s4 = s2 + 2 * GDN_HEADS
    small = jnp.pad(w_in[:, s2:s4], ((0, 0), (0, SMALL_W - 2 * GDN_HEADS)))
    wcat = jnp.concatenate([w_in[:, :s2], small, w_in[:, s4:]], axis=1).astype(BF16)
    qkva, z, ba, *qkvb = _inproj(x2, sc_a, sh_a, norm_mix.reshape(1, d), wcat, w_conv, seq)

    pad4 = lambda v: jnp.pad(v.astype(F32), (GDN_HEADS, LANES - 2 * GDN_HEADS))
    gp = jnp.zeros((8, LANES), F32).at[0].set(pad4(a_log)).at[1].set(pad4(dt_bias))
    og = _gdn(qkva, ba, ba[:, :8].T, gp, _gp_cols(a_log, dt_bias),
              _block_tri(GDN_TOK, CHUNK, True), _block_tri(GDN_TOK, CHUNK, False), seq)

    outs, lses = [], []
    for (window, dil), qkvb_d in zip(DILATED_PAIRS, qkvb):
        o_i, lse_i = _attn_branch(qkvb_d, jnp.asarray(_bucket_table(window, dil)), rel_bias.astype(F32),
                                  bsz, seq, dil)
        outs.append(o_i)
        lses.append(lse_i)

    expand = jnp.where((jnp.arange(LANES)[:, None] == jnp.arange(ATTN_W)[None, :] // ATTN_DH), 1.0, 0.0).astype(BF16)
    lstrict = jnp.where(jnp.arange(TM_POST)[:, None] > jnp.arange(TM_POST)[None, :], 1.0, 0.0).astype(BF16)
    wr = jnp.pad(jnp.concatenate([w_rg, w_re], axis=1).astype(F32), ((0, 0), (0, LANES - N_GROUPS - N_EXPERTS)))
    wrh = wr.astype(BF16)
    wr_split = jnp.concatenate([wrh, (wr - wrh.astype(F32)).astype(BF16)], axis=1)
    x1, h2, route, cnt = _post(og, z, outs, lses, x2, gt_a, sc_f, sh_f,
                               gdn_norm.reshape(1, GDN_DV), attn_norm.reshape(1, ATTN_W),
                               w_out.astype(BF16), norm_ffn.reshape(1, d), wr_split, expand, lstrict, seq)

    blk = EXPERT_BLK
    counts = cnt[0, :N_EXPERTS].astype(jnp.int32)
    padded = (counts + blk - 1) // blk * blk
    pends = jnp.cumsum(padded)
    pstarts = pends - padded
    eids = route[:, 0:TOP_K].astype(jnp.int32)
    ranks = route[:, 4:4 + TOP_K].astype(jnp.int32)
    expert_ids = jnp.arange(N_EXPERTS, dtype=jnp.int32)
    seg_start = jnp.sum(jnp.where(eids[..., None] == expert_ids, pstarts, 0), axis=-1)
    dest = (seg_start + ranks).reshape(t * TOP_K)
    a = t * TOP_K
    p = -(-a // blk) * blk + N_EXPERTS * blk
    nblk = p // blk
    blk_start = jnp.arange(nblk, dtype=jnp.int32) * blk
    blk_e = jnp.minimum(jnp.sum((pends[None, :] <= blk_start[:, None]).astype(jnp.int32), axis=1),
                        N_EXPERTS - 1)
    nused = (pends[-1] // blk).astype(jnp.int32).reshape(1)

    xs = _dispatch(dest, pstarts + counts, padded - counts, nused, h2, p)
    ys = _experts(blk_e, nused, xs, w_gate, w_up, w_down)
    return ys, dest, x1, route, gt_f


def _gp_cols(a_log, dt_bias):
    z = jnp.zeros((8, LANES), F32)
    z = z.at[GDN_HEADS:2 * GDN_HEADS, 0].set(a_log.astype(F32))
    z = z.at[GDN_HEADS:2 * GDN_HEADS, 1].set(dt_bias.astype(F32))
    return z


def kernel(x, c, w_ada, b_ada, norm_mix, w_in, w_conv, a_log, dt_bias, gdn_norm, attn_norm, w_out,
           rel_bias, norm_ffn, w_router_group, w_router_expert, w_gate, w_up, w_down, norm_final):
    bsz, seq, d = x.shape
    depth = w_ada.shape[0]
    x2 = x.reshape(bsz * seq, d)
    for l in range(depth):
        ys, dest, x1, route, gt_f = _layer(
            x2, c, bsz, seq, w_ada[l], b_ada[l], norm_mix[l], w_in[l], w_conv[l], a_log[l], dt_bias[l],
            gdn_norm[l], attn_norm[l], w_out[l], rel_bias, norm_ffn[l], w_router_group[l],
            w_router_expert[l], w_gate[l], w_up[l], w_down[l])
        x2 = _combine(dest, ys, x1, route, gt_f, norm_final.reshape(1, d), seq, l == depth - 1)
    return x2.reshape(bsz, seq, d)
```

```python
import functools
import math

import jax
import jax.numpy as jnp
import numpy as np
from jax import lax
from jax.experimental import pallas as pl
from jax.experimental.pallas import tpu as pltpu

D_MODEL = 1024
GDN_HEADS = 4
GDN_DK = 128
GDN_DV = 128
CONV_K = 4
CHUNK = 64
ATTN_HEADS = 8
ATTN_DH = 64
DILATED_PAIRS = ((128, 1), (512, 4), (2048, 16))
QBLK = 128
NUM_BUCKETS = 32
REL_MAX_DIST = 2048
N_GROUPS = 4
EXPERTS_PER_GROUP = 8
N_EXPERTS = N_GROUPS * EXPERTS_PER_GROUP
TOP_K = 2
D_EXPERT = 256
EPS = 1e-6
NEG_INF = -1e30

GDN_QK_W = GDN_HEADS * GDN_DK
GDN_V_W = GDN_HEADS * GDN_DV
ATTN_W = ATTN_HEADS * ATTN_DH
GDN_CONV_W = 2 * GDN_QK_W + GDN_V_W
LANES = 128
SUBLANES = 8
SMALL_W = LANES
W_COLS = GDN_CONV_W + GDN_V_W + SMALL_W + 3 * ATTN_W

TM_PROJ = 512
CONV_BLK = 128
HEAD_ROWS = 16
TM_POST = 512
TM_ROWS = 512
GDN_TOK = 512
GDN_GROUP = 4
INV_BASE = 8
ATTN_SUB = 4
EXPERT_BLK = 512
ROW_UNROLL = 8
VMEM_LIMIT = 56 * 1024 * 1024

F32 = jnp.float32
BF16 = jnp.bfloat16
U32 = jnp.uint32
HIGHEST = lax.Precision.HIGHEST


def _sigmoid(x):
    return 1.0 / (1.0 + jnp.exp(-x))


def _dot(a, b, precision=None):
    return jnp.dot(a, b, preferred_element_type=F32, precision=precision)


def _dot_nt(a, b, precision=None):
    return lax.dot_general(a, b, (((1,), (1,)), ((), ())), preferred_element_type=F32,
                           precision=precision)


def _dot_tn(a, b, precision=None):
    return lax.dot_general(a, b, (((0,), (0,)), ((), ())), preferred_element_type=F32,
                           precision=precision)


def _params(sem):
    return pltpu.CompilerParams(dimension_semantics=sem, vmem_limit_bytes=VMEM_LIMIT)


def _adaln_kernel(c_ref, w_ref, b_ref, o_ref):
    o_ref[...] = _dot(c_ref[...], w_ref[...], HIGHEST) + b_ref[...]


def _adaln(c, w, b):
    bsz, d = c.shape
    n = w.shape[1]
    tn = 1024
    return pl.pallas_call(
        _adaln_kernel,
        grid=(n // tn,),
        in_specs=[pl.BlockSpec((bsz, d), lambda j: (0, 0)),
                  pl.BlockSpec((d, tn), lambda j: (0, j)),
                  pl.BlockSpec((1, tn), lambda j: (0, j))],
        out_specs=pl.BlockSpec((bsz, tn), lambda j: (0, j)),
        out_shape=jax.ShapeDtypeStruct((bsz, n), F32),
        compiler_params=_params(("arbitrary",)),
        name="adaln",
    )(c, w, b.reshape(1, n))


def _inproj_kernel(tiles_per_seq, x_ref, sc_ref, sh_ref, g_ref, w_ref, wconv_ref, shift_ref,
                   qkva_ref, z_ref, ba_ref, *rest):
    qkvb_refs = rest[:len(DILATED_PAIRS)]
    win_scr, pb_scr, pb_next_scr = rest[len(DILATED_PAIRS):]
    i = pl.program_id(0)
    tm = x_ref.shape[0]
    x = x_ref[...]
    h = x * lax.rsqrt(jnp.mean(x * x, axis=-1, keepdims=True) + EPS) * g_ref[...]
    h = h * (1.0 + sc_ref[0]) + sh_ref[0]
    hb = h.astype(BF16)

    @pl.when(i % tiles_per_seq == 0)
    def _():
        win_scr[0:HEAD_ROWS, :] = jnp.zeros((HEAD_ROWS, GDN_CONV_W), F32)

    pa = _dot(hb, w_ref[:, 0:GDN_CONV_W])
    pa16 = pa.astype(BF16)
    for b in range(tm // CONV_BLK):
        rs = slice(b * CONV_BLK, (b + 1) * CONV_BLK)
        acc = pa[rs] * wconv_ref[CONV_K - 1:CONV_K, :]
        for j in range(CONV_K - 1):
            if b == 0:
                shifted = _dot(shift_ref[j, :, CONV_BLK:], pa16[rs])
            else:
                shifted = _dot(shift_ref[j], pa16[(b - 1) * CONV_BLK:(b + 1) * CONV_BLK])
            acc = acc + shifted * wconv_ref[j:j + 1, :]
        qkva_ref[rs, :] = (acc * _sigmoid(acc)).astype(qkva_ref.dtype)
    win_scr[HEAD_ROWS:2 * HEAD_ROWS, :] = pa[0:HEAD_ROWS]
    acc = pa[0:HEAD_ROWS] * wconv_ref[CONV_K - 1:CONV_K, :]
    for j in range(CONV_K - 1):
        acc = acc + win_scr[pl.ds(HEAD_ROWS - (CONV_K - 1) + j, HEAD_ROWS), :] * wconv_ref[j:j + 1, :]
    qkva_ref[0:HEAD_ROWS, :] = (acc * _sigmoid(acc)).astype(qkva_ref.dtype)
    win_scr[0:HEAD_ROWS, :] = pa[tm - HEAD_ROWS:tm]

    c0 = GDN_CONV_W
    z_ref[...] = _dot(hb, w_ref[:, c0:c0 + GDN_V_W]).astype(z_ref.dtype)
    c0 += GDN_V_W
    ba_ref[...] = _dot(hb, w_ref[:, c0:c0 + SMALL_W])
    c0 += SMALL_W
    pb = _dot(hb, w_ref[:, c0:c0 + 3 * ATTN_W])
    ncol = 3 * ATTN_W // LANES
    for j in range(ncol):
        pb_scr[j] = pb[:, j * LANES:(j + 1) * LANES]
    prev_dil, prev_scr, next_scr = 1, pb_scr, pb_next_scr
    for bi, ((_, dil), ref) in enumerate(zip(DILATED_PAIRS, qkvb_refs)):
        if dil == 1:
            ref[...] = pb.astype(ref.dtype)
            continue
        f = dil // prev_dil
        n = tm // dil
        keep = bi + 1 < len(DILATED_PAIRS)
        for r in range(prev_dil):
            for q in range(f):
                r_new = r + prev_dil * q
                for j in range(ncol):
                    rows = prev_scr[j, pl.ds(r * (tm // prev_dil) + q, n, stride=f), :]
                    c1 = r_new * 3 * ATTN_W + j * LANES
                    ref[:, c1:c1 + LANES] = rows.astype(ref.dtype)
                    if keep:
                        next_scr[j, r_new * n:(r_new + 1) * n, :] = rows
        prev_dil, prev_scr, next_scr = dil, next_scr, prev_scr


def _inproj(x2, sc, sh, g, wcat, wconv, seq):
    t, d = x2.shape
    tm = TM_PROJ
    tps = seq // tm
    mod_spec = pl.BlockSpec((1, 1, d), lambda i: (i // tps, 0, 0))
    return pl.pallas_call(
        functools.partial(_inproj_kernel, tps),
        grid=(t // tm,),
        in_specs=[pl.BlockSpec((tm, d), lambda i: (i, 0)),
                  mod_spec, mod_spec,
                  pl.BlockSpec((1, d), lambda i: (0, 0)),
                  pl.BlockSpec((d, W_COLS), lambda i: (0, 0)),
                  pl.BlockSpec((CONV_K, GDN_CONV_W), lambda i: (0, 0)),
                  pl.BlockSpec((CONV_K - 1, CONV_BLK, 2 * CONV_BLK), lambda i: (0, 0, 0))],
        out_specs=[pl.BlockSpec((tm, GDN_CONV_W), lambda i: (i, 0)),
                   pl.BlockSpec((tm, GDN_V_W), lambda i: (i, 0)),
                   pl.BlockSpec((tm, SMALL_W), lambda i: (i, 0))]
                  + [pl.BlockSpec((tm // dil, dil * 3 * ATTN_W), lambda i: (i, 0)) for _, dil in DILATED_PAIRS],
        out_shape=[jax.ShapeDtypeStruct((t, GDN_CONV_W), BF16),
                   jax.ShapeDtypeStruct((t, GDN_V_W), BF16),
                   jax.ShapeDtypeStruct((t, SMALL_W), F32)]
                  + [jax.ShapeDtypeStruct((t // dil, dil * 3 * ATTN_W), BF16) for _, dil in DILATED_PAIRS],
        scratch_shapes=[pltpu.VMEM((2 * HEAD_ROWS, GDN_CONV_W), F32),
                        pltpu.VMEM((3 * ATTN_W // LANES, tm, LANES), F32),
                        pltpu.VMEM((3 * ATTN_W // LANES, tm, LANES), F32)],
        compiler_params=_params(("arbitrary",)),
        name="inproj",
    )(x2, sc, sh, g, wcat, wconv, _shift_bands())


def _shift_bands():
    t = np.arange(CONV_BLK)[:, None]
    c = np.arange(2 * CONV_BLK)[None, :]
    bands = [(c == CONV_BLK + t - (CONV_K - 1 - j)) for j in range(CONV_K - 1)]
    return jnp.asarray(np.stack(bands).astype(np.float32), dtype=BF16)


def _softplus(x):
    return jnp.maximum(x, 0.0) + jnp.log(1.0 + jnp.exp(-jnp.abs(x)))


def _unit_lower_inverses(n_list):
    c = n_list[0].shape[0]
    row = lax.broadcasted_iota(jnp.int32, (c, c), 0)
    col = lax.broadcasted_iota(jnp.int32, (c, c), 1)
    eye = jnp.where(row == col, 1.0, 0.0)
    same_base = (row // INV_BASE) == (col // INV_BASE)
    n0 = [jnp.where(same_base, n, 0.0) for n in n_list]
    p = [eye - x for x in n0]
    m = [x.astype(BF16) for x in n0]
    for j in range(int(math.log2(INV_BASE)) - 1):
        m = [_dot(x, x).astype(BF16) for x in m]
        p = [x + _dot(x.astype(BF16), y) for x, y in zip(p, m)]
    size = INV_BASE
    while size < c:
        sibling = jnp.logical_and((row // size) % 2 == 1, (col // size) == (row // size) - 1)
        cb = [jnp.where(sibling, n, 0.0).astype(BF16) for n in n_list]
        pb = [x.astype(BF16) for x in p]
        pc = [_dot(x, y).astype(BF16) for x, y in zip(pb, cb)]
        p = [x - _dot(y, z) for x, y, z in zip(p, pc, pb)]
        size *= 2
    return p


def _split3_bf16(x):
    h1 = x.astype(BF16)
    r1 = x - h1.astype(F32)
    h2 = r1.astype(BF16)
    h3 = (r1 - h2.astype(F32)).astype(BF16)
    return h1, h2, h3


def _gdn_kernel(steps_per_seq, qkv_ref, ba_ref, bat_ref, gp_ref, gpt_ref, lblk_ref, ublk_ref,
                o_ref, state_scr):
    i = pl.program_id(0)

    @pl.when(i % steps_per_seq == 0)
    def _():
        state_scr[...] = jnp.zeros_like(state_scr)

    nchunk = GDN_TOK // CHUNK
    ba = ba_ref[...]
    bat = bat_ref[...]
    a_vec = -jnp.exp(gp_ref[0:1, :])
    g_tile = a_vec * _softplus(ba + gp_ref[1:2, :])
    lblk = lblk_ref[...]
    g_cum = sum(_dot(lblk, part) for part in _split3_bf16(g_tile))
    a_col = -jnp.exp(gpt_ref[:, 0:1])
    gt_tile = a_col * _softplus(bat + gpt_ref[:, 1:2])
    ublk = ublk_ref[...]
    gt_cum = sum(_dot(part, ublk) for part in _split3_bf16(gt_tile))
    beta_tile = _sigmoid(ba)

    row = lax.broadcasted_iota(jnp.int32, (CHUNK, CHUNK), 0)
    col = lax.broadcasted_iota(jnp.int32, (CHUNK, CHUNK), 1)
    incl = row >= col
    strict = row > col

    for h0 in range(0, GDN_HEADS, GDN_GROUP):
        heads = range(h0, h0 + GDN_GROUP)
        kb_l, k_l, q_l, rhs_l, qg_l, kg_l, gl_l, inc_l, str_l = ([] for _ in range(9))
        for h in heads:
            q_raw = qkv_ref[:, h * GDN_DK:(h + 1) * GDN_DK].astype(F32)
            k_raw = qkv_ref[:, GDN_QK_W + h * GDN_DK:GDN_QK_W + (h + 1) * GDN_DK].astype(F32)
            v = qkv_ref[:, 2 * GDN_QK_W + h * GDN_DV:2 * GDN_QK_W + (h + 1) * GDN_DV].astype(F32)
            qn = q_raw * lax.rsqrt(jnp.sum(q_raw * q_raw, axis=-1, keepdims=True) + EPS) * (GDN_DK ** -0.5)
            kn = k_raw * lax.rsqrt(jnp.sum(k_raw * k_raw, axis=-1, keepdims=True) + EPS)
            beta = beta_tile[:, h:h + 1]
            gc_all = g_cum[:, GDN_HEADS + h:GDN_HEADS + h + 1]
            eg = jnp.exp(gc_all)
            kb = kn * beta
            rhs = jnp.concatenate([v * beta, kb * eg], axis=1).astype(BF16)
            qg = qn * eg
            kb16, k16, q16 = kb.astype(BF16), kn.astype(BF16), qn.astype(BF16)
            for c in range(nchunk):
                sl = slice(c * CHUNK, (c + 1) * CHUNK)
                g_c = gc_all[sl]
                g_r = gt_cum[GDN_HEADS + h:GDN_HEADS + h + 1, sl]
                dec = jnp.exp(g_c - g_r)
                inc_l.append(jnp.where(incl, dec, 0.0))
                str_l.append(jnp.where(strict, dec, 0.0))
                g_last = g_c[CHUNK - 1:CHUNK, :]
                kg_l.append((kn[sl] * jnp.exp(g_last - g_c)).astype(BF16))
                gl_l.append(jnp.exp(g_last))
                kb_l.append(kb16[sl])
                k_l.append(k16[sl])
                q_l.append(q16[sl])
                rhs_l.append(rhs[sl])
                qg_l.append(qg[sl])
        nprob = len(k_l)
        n_l = [_dot_nt(kb_l[j], k_l[j]) * str_l[j] for j in range(nprob)]
        aqk_l = [(_dot_nt(q_l[j], k_l[j]) * inc_l[j]).astype(BF16) for j in range(nprob)]
        tinv_l = _unit_lower_inverses(n_l)
        sol_l = [_dot(tinv_l[j].astype(BF16), rhs_l[j]).astype(BF16) for j in range(nprob)]
        a2_l = [_dot(aqk_l[j], sol_l[j]) for j in range(nprob)]
        k2_l = [_dot_tn(kg_l[j], sol_l[j]) for j in range(nprob)]
        states = [state_scr[h] for h in heads]
        for c in range(nchunk):
            sl = slice(c * CHUNK, (c + 1) * CHUNK)
            for hi, h in enumerate(heads):
                j = hi * nchunk + c
                sb = states[hi].astype(BF16)
                qeff = (qg_l[j] - a2_l[j][:, GDN_DV:]).astype(BF16)
                o = _dot(qeff, sb) + a2_l[j][:, :GDN_DV]
                states[hi] = gl_l[j] * states[hi] + k2_l[j][:, :GDN_DV] - _dot(k2_l[j][:, GDN_DV:].astype(BF16), sb)
                o_ref[sl, h * GDN_DV:(h + 1) * GDN_DV] = o.astype(o_ref.dtype)
        for hi, h in enumerate(heads):
            state_scr[h] = states[hi]


def _gdn(qkva, ba, bat, gp, gpt, lblk, ublk, seq):
    t = qkva.shape[0]
    tok = GDN_TOK
    sps = seq // tok
    const = lambda i: (0, 0)
    return pl.pallas_call(
        functools.partial(_gdn_kernel, sps),
        grid=(t // tok,),
        in_specs=[pl.BlockSpec((tok, GDN_CONV_W), lambda i: (i, 0)),
                  pl.BlockSpec((tok, SMALL_W), lambda i: (i, 0)),
                  pl.BlockSpec((8, tok), lambda i: (0, i)),
                  pl.BlockSpec((8, LANES), const),
                  pl.BlockSpec((8, LANES), const),
                  pl.BlockSpec((tok, tok), const),
                  pl.BlockSpec((tok, tok), const)],
        out_specs=pl.BlockSpec((tok, GDN_V_W), lambda i: (i, 0)),
        out_shape=jax.ShapeDtypeStruct((t, GDN_V_W), F32),
        scratch_shapes=[pltpu.VMEM((GDN_HEADS, GDN_DK, GDN_DV), F32)],
        compiler_params=_params(("arbitrary",)),
        name="gdn",
    )(qkva, ba, bat, gp, gpt, lblk, ublk)


def _attn_kernel(q_ref, kp_ref, kc_ref, vp_ref, vc_ref, bucket_ref, rb_ref, o_ref, lse_ref, bias_ref):
    n = pl.program_id(2)

    @pl.when(jnp.logical_and(jnp.logical_and(pl.program_id(0) == 0, pl.program_id(1) == 0), n == 0))
    def _():
        bucket = bucket_ref[...]
        col = lax.broadcasted_iota(jnp.int32, (QBLK, 2 * QBLK), 1)
        for h in range(ATTN_HEADS):
            acc = jnp.full((QBLK, 2 * QBLK), NEG_INF, F32)
            for b in range(NUM_BUCKETS):
                acc = jnp.where(bucket == b, rb_ref[b, h], acc)
            bias_ref[0, h] = acc
            bias_ref[1, h] = jnp.where(col < QBLK, NEG_INF, acc)

    nsub = q_ref.shape[1] // QBLK
    first = jnp.where(n == 0, 1, 0)
    q_all = q_ref[0]
    k_all = jnp.concatenate([kp_ref[0], kc_ref[0]], axis=0)
    v_all = jnp.concatenate([vp_ref[0], vc_ref[0]], axis=0)
    lane = lax.broadcasted_iota(jnp.int32, (QBLK, LANES), 1)
    ones = jnp.ones((2 * QBLK, LANES), BF16)
    per_group = LANES // ATTN_DH
    ngrp = ATTN_HEADS // per_group
    scale = ATTN_DH ** -0.5
    mine = [jnp.logical_and(lane >= j * ATTN_DH, lane < (j + 1) * ATTN_DH) for j in range(per_group)]
    probs = [(u, h) for u in range(nsub) for h in range(ATTN_HEADS)]
    s_l = []
    for u, h in probs:
        grp, j = divmod(h, per_group)
        gs = slice(grp * LANES, (grp + 1) * LANES)
        qg = q_all[u * QBLK:(u + 1) * QBLK, gs] * scale
        table = first if u == 0 else 0
        s_l.append(_dot_nt(jnp.where(mine[j], qg, jnp.zeros_like(qg)), k_all[u * QBLK:(u + 2) * QBLK, gs])
                   + bias_ref[table, h])
    m_l = [jnp.max(s, axis=-1, keepdims=True) for s in s_l]
    p_l = [jnp.exp(s - m).astype(BF16) for s, m in zip(s_l, m_l)]
    v_ext = {(u, grp): jnp.concatenate([v_all[u * QBLK:(u + 2) * QBLK, grp * LANES:(grp + 1) * LANES], ones],
                                       axis=1)
             for u in range(nsub) for grp in range(ngrp)}
    pv_l = [_dot(p, v_ext[(u, h // per_group)]) for p, (u, h) in zip(p_l, probs)]
    for u in range(nsub):
        outs = []
        lse_tile = jnp.zeros((QBLK, LANES), F32)
        for grp in range(ngrp):
            o_grp = jnp.zeros((QBLK, LANES), F32)
            for j in range(per_group):
                h = grp * per_group + j
                idx = u * ATTN_HEADS + h
                den = pv_l[idx][:, LANES:]
                o_grp = jnp.where(mine[j], pv_l[idx][:, :LANES] / den, o_grp)
                lse_tile = jnp.where(lane == h, m_l[idx] + jnp.log(den[:, 0:1]), lse_tile)
            outs.append(o_grp)
        o_ref[0, u * QBLK:(u + 1) * QBLK, :] = jnp.concatenate(outs, axis=1).astype(o_ref.dtype)
        lse_ref[0, u * QBLK:(u + 1) * QBLK, :] = lse_tile


def _attn_branch(qkvb, bucket, rel_bias, bsz, seq, dil):
    ln = seq // dil
    sub = min(ATTN_SUB, ln // QBLK)
    rows = sub * QBLK
    assert seq % dil == 0 and ln % rows == 0, (seq, dil, rows)
    nb = ln // rows
    xv = qkvb.reshape(bsz, ln, dil * 3 * ATTN_W)
    w = ATTN_W
    cur = lambda part: (lambda b, r, n: (b, n, r * 3 + part))
    prev = lambda part: (lambda b, r, n: (b, jnp.maximum(n * sub - 1, 0), r * 3 + part))
    o, lse = pl.pallas_call(
        _attn_kernel,
        grid=(bsz, dil, nb),
        in_specs=[pl.BlockSpec((1, rows, w), cur(0)),
                  pl.BlockSpec((1, QBLK, w), prev(1)),
                  pl.BlockSpec((1, rows, w), cur(1)),
                  pl.BlockSpec((1, QBLK, w), prev(2)),
                  pl.BlockSpec((1, rows, w), cur(2)),
                  pl.BlockSpec((QBLK, 2 * QBLK), lambda b, r, n: (0, 0)),
                  pl.BlockSpec(memory_space=pltpu.SMEM)],
        out_specs=[pl.BlockSpec((1, rows, w), lambda b, r, n: (b, n, r)),
                   pl.BlockSpec((1, rows, LANES), lambda b, r, n: (b, n, r))],
        out_shape=[jax.ShapeDtypeStruct((bsz, ln, dil * w), BF16),
                   jax.ShapeDtypeStruct((bsz, ln, dil * LANES), F32)],
        scratch_shapes=[pltpu.VMEM((2, ATTN_HEADS, QBLK, 2 * QBLK), F32)],
        compiler_params=_params(("arbitrary", "arbitrary", "arbitrary")),
        name=f"attn_d{dil}",
    )(xv, xv, xv, xv, xv, bucket, rel_bias)
    return o.reshape(bsz * ln, dil * w), lse.reshape(bsz * ln, dil * LANES)


def _bucket_table(window, dil):
    steps = window // dil
    qi = np.arange(QBLK)[:, None]
    kj = np.arange(2 * QBLK)[None, :]
    delta = qi + QBLK - kj
    dist = np.maximum(delta, 0) * dil
    max_exact = NUM_BUCKETS // 2
    d_f = np.maximum(dist, 1).astype(np.float32)
    large = max_exact + (np.log(d_f / max_exact) / math.log(REL_MAX_DIST / max_exact)
                         * (NUM_BUCKETS - max_exact)).astype(np.int32)
    bucket = np.where(dist < max_exact, dist, np.minimum(large, NUM_BUCKETS - 1))
    return np.where((delta >= 0) & (delta <= steps), bucket, -1).astype(np.int32)


def _split_bf16(x):
    hi = x.astype(BF16)
    lo = (x - hi.astype(F32)).astype(BF16)
    return hi, lo


def _pack_halves(x):
    w = x.shape[1] // 2
    lo = lax.bitcast_convert_type(x[:, :w].astype(BF16).astype(F32), U32)
    hi = lax.bitcast_convert_type(x[:, w:].astype(BF16).astype(F32), U32)
    return (hi & jnp.uint32(0xFFFF0000)) | (lo >> 16)


def _unpack_halves(u):
    lo = lax.bitcast_convert_type(u << 16, F32)
    hi = lax.bitcast_convert_type(u & jnp.uint32(0xFFFF0000), F32)
    return lo, hi


def _post_kernel(og_ref, z_ref, o1_ref, o2_ref, o3_ref, l1_ref, l2_ref, l3_ref, x_ref,
                 gta_ref, scf_ref, shf_ref, gnorm_ref, anorm_ref, wout_ref, nffn_ref,
                 wrh_ref, exp_ref, ls_ref,
                 x1_ref, h2_ref, route_ref, cnt_ref, carry_scr, *tok_scr):
    i = pl.program_id(0)
    tm = x_ref.shape[0]

    @pl.when(i == 0)
    def _():
        carry_scr[...] = jnp.zeros_like(carry_scr)

    nbr = len(DILATED_PAIRS)
    ob_scr, lse_scr = tok_scr[:nbr], tok_scr[nbr:]
    for (_, dil), o_ref, l_ref, o_s, l_s in zip(DILATED_PAIRS, (o1_ref, o2_ref, o3_ref),
                                                 (l1_ref, l2_ref, l3_ref), ob_scr, lse_scr):
        for r in range(dil):
            rows = pl.ds(r, tm // dil, stride=dil) if dil > 1 else slice(None)
            for j in range(ATTN_W // LANES):
                c1 = r * ATTN_W + j * LANES
                o_s[j, rows, :] = o_ref[:, c1:c1 + LANES].astype(F32)
            l_s[rows, :] = l_ref[:, r * LANES:(r + 1) * LANES]

    heads = []
    for h in range(GDN_HEADS):
        hs = slice(h * GDN_DV, (h + 1) * GDN_DV)
        seg = og_ref[:, hs]
        nrm = seg * lax.rsqrt(jnp.mean(seg * seg, axis=-1, keepdims=True) + EPS) * gnorm_ref[...]
        zz = z_ref[:, hs].astype(F32)
        heads.append((nrm * (zz * _sigmoid(zz))).astype(BF16))
    oa = jnp.concatenate(heads, axis=1)

    l1, l2, l3 = (l_s[...] for l_s in lse_scr)
    m = jnp.maximum(jnp.maximum(l1, l2), l3)
    e1, e2, e3 = jnp.exp(l1 - m), jnp.exp(l2 - m), jnp.exp(l3 - m)
    inv = 1.0 / (e1 + e2 + e3)
    ob = jnp.zeros((tm, ATTN_W), F32)
    for e, o_s in zip((e1, e2, e3), ob_scr):
        hi, lo = _split_bf16(e * inv)
        wexp = _dot(hi, exp_ref[...]) + _dot(lo, exp_ref[...])
        ob = ob + wexp * jnp.concatenate([o_s[j] for j in range(ATTN_W // LANES)], axis=1)
    ob = ob * lax.rsqrt(jnp.mean(ob * ob, axis=-1, keepdims=True) + EPS) * anorm_ref[...]

    mix = _dot(oa, wout_ref[0:GDN_V_W, :]) + _dot(ob.astype(BF16), wout_ref[GDN_V_W:, :])
    x1 = x_ref[...] + gta_ref[0] * mix
    x1_ref[...] = x1
    h2 = x1 * lax.rsqrt(jnp.mean(x1 * x1, axis=-1, keepdims=True) + EPS) * nffn_ref[...]
    h2 = h2 * (1.0 + scf_ref[0]) + shf_ref[0]
    h2_ref[...] = _pack_halves(h2)

    hh, hl = _split_bf16(h2)
    both = _dot(hh, wrh_ref[...])
    logits = both[:, :LANES] + both[:, LANES:] + _dot(hl, wrh_ref[:, :LANES])
    tm = logits.shape[0]
    lane = lax.broadcasted_iota(jnp.int32, (tm, LANES), 1).astype(F32)
    big = float(LANES)
    gmask = lane < N_GROUPS
    glog = jnp.where(gmask, logits, NEG_INF)
    gmax = jnp.max(glog, axis=-1, keepdims=True)
    gidx = jnp.min(jnp.where(jnp.logical_and(gmask, glog == gmax), lane, big), axis=-1, keepdims=True)
    gprob = 1.0 / jnp.sum(jnp.where(gmask, jnp.exp(glog - gmax), 0.0), axis=-1, keepdims=True)
    lo_lane = N_GROUPS + EXPERTS_PER_GROUP * gidx
    emask = jnp.logical_and(lane >= lo_lane, lane < lo_lane + EXPERTS_PER_GROUP)
    elog = jnp.where(emask, logits, NEG_INF)
    m1 = jnp.max(elog, axis=-1, keepdims=True)
    i1 = jnp.min(jnp.where(jnp.logical_and(emask, elog == m1), lane, big), axis=-1, keepdims=True)
    emask2 = jnp.logical_and(emask, lane != i1)
    elog2 = jnp.where(emask2, logits, NEG_INF)
    m2 = jnp.max(elog2, axis=-1, keepdims=True)
    i2 = jnp.min(jnp.where(jnp.logical_and(emask2, elog2 == m2), lane, big), axis=-1, keepdims=True)
    r = jnp.exp(m2 - m1)
    gate1 = gprob / (1.0 + r)
    gate2 = gprob * r / (1.0 + r)
    ex1 = i1 - N_GROUPS
    ex2 = i2 - N_GROUPS

    hit1 = lane == ex1
    hit2 = lane == ex2
    onehot = jnp.where(jnp.logical_or(hit1, hit2), 1.0, 0.0)
    pref = _dot(ls_ref[...], onehot.astype(BF16)) + carry_scr[...]
    rank1 = jnp.sum(jnp.where(hit1, pref, 0.0), axis=-1, keepdims=True)
    rank2 = jnp.sum(jnp.where(hit2, pref, 0.0), axis=-1, keepdims=True)
    carry = carry_scr[...] + jnp.sum(onehot, axis=0, keepdims=True)
    carry_scr[...] = carry
    cnt_ref[...] = jnp.broadcast_to(carry, cnt_ref.shape)

    route = jnp.zeros((tm, LANES), F32)
    for idx, val in enumerate((ex1, ex2, gate1, gate2, rank1, rank2)):
        route = jnp.where(lane == idx, val, route)
    route_ref[...] = route


def _post(og, z, os_, ls_, x2, gta, scf, shf, gnorm, anorm, wout, nffn, wrh, expand, lstrict, seq):
    t, d = x2.shape
    tm = TM_POST
    tps = seq // tm
    tile = lambda w: pl.BlockSpec((tm, w), lambda i: (i, 0))
    const2 = lambda a: pl.BlockSpec(a.shape, lambda i: (0, 0))
    mod_spec = pl.BlockSpec((1, 1, d), lambda i: (i // tps, 0, 0))
    return pl.pallas_call(
        _post_kernel,
        grid=(t // tm,),
        in_specs=[tile(GDN_V_W), tile(GDN_V_W)]
                 + [pl.BlockSpec((tm // dil, dil * ATTN_W), lambda i: (i, 0)) for _, dil in DILATED_PAIRS]
                 + [pl.BlockSpec((tm // dil, dil * LANES), lambda i: (i, 0)) for _, dil in DILATED_PAIRS]
                 + [tile(d),
                  mod_spec, mod_spec, mod_spec,
                  const2(gnorm), const2(anorm), const2(wout), const2(nffn),
                  const2(wrh), const2(expand), const2(lstrict)],
        out_specs=[tile(d), tile(d // 2), tile(LANES), pl.BlockSpec((8, LANES), lambda i: (0, 0))],
        out_shape=[jax.ShapeDtypeStruct((t, d), F32),
                   jax.ShapeDtypeStruct((t, d // 2), U32),
                   jax.ShapeDtypeStruct((t, LANES), F32),
                   jax.ShapeDtypeStruct((8, LANES), F32)],
        scratch_shapes=[pltpu.VMEM((1, LANES), F32)]
                       + [pltpu.VMEM((ATTN_W // LANES, tm, LANES), F32) for _ in DILATED_PAIRS]
                       + [pltpu.VMEM((tm, LANES), F32) for _ in DILATED_PAIRS],
        compiler_params=_params(("arbitrary",)),
        name="post",
    )(og, z, *os_, *ls_, x2, gta, scf, shf, gnorm, anorm, wout, nffn, wrh, expand, lstrict)


def _dispatch_kernel(dest_ref, fill_ref, pad_ref, nused_ref, h2_ref, xs_ref, zero_scr, sem, fill_sem):
    i = pl.program_id(0)
    tm = h2_ref.shape[0]

    @pl.when(i == 0)
    def _():
        zero_scr[...] = jnp.zeros_like(zero_scr)

        def pieces(e, act):
            base, pad = fill_ref[e], pad_ref[e]
            head = pad & (SUBLANES - 1)
            for j in range(SUBLANES - 1):
                @pl.when(j < head)
                def _():
                    act(pltpu.make_async_copy(zero_scr.at[pl.ds(0, 1)], xs_ref.at[pl.ds(base + j, 1)],
                                              fill_sem))
            off = base + head
            for bit in range(SUBLANES.bit_length() - 1, EXPERT_BLK.bit_length() - 1):
                size = 1 << bit

                @pl.when((pad >> bit) & 1 == 1)
                def _():
                    act(pltpu.make_async_copy(zero_scr.at[pl.ds(0, size)],
                                              xs_ref.at[pl.ds(pl.multiple_of(off, SUBLANES), size)], fill_sem))
                off = off + (pad & size)

        def fill(e, carry):
            pieces(e, lambda cp: cp.start())
            return carry

        def drain(e, carry):
            pieces(e, lambda cp: cp.wait())
            return carry

        lax.fori_loop(0, N_EXPERTS, fill, 0)
        lax.fori_loop(0, N_EXPERTS, drain, 0)

        def tail(b, act):
            for half in range(EXPERT_BLK // zero_scr.shape[0]):
                row0 = pl.multiple_of(b * EXPERT_BLK + half * zero_scr.shape[0], SUBLANES)
                act(pltpu.make_async_copy(zero_scr, xs_ref.at[pl.ds(row0, zero_scr.shape[0])], fill_sem))

        def tail_fill(b, carry):
            tail(b, lambda cp: cp.start())
            return carry

        def tail_drain(b, carry):
            tail(b, lambda cp: cp.wait())
            return carry

        nblk = xs_ref.shape[0] // EXPERT_BLK
        lax.fori_loop(nused_ref[0], nblk, tail_fill, 0)
        lax.fori_loop(nused_ref[0], nblk, tail_drain, 0)

    def start(r, carry):
        for k in range(TOP_K):
            d = dest_ref[(i * tm + r) * TOP_K + k]
            pltpu.make_async_copy(h2_ref.at[pl.ds(r, 1)], xs_ref.at[pl.ds(d, 1)], sem).start()
        return carry

    lax.fori_loop(0, tm, start, 0, unroll=ROW_UNROLL)
    for k in range(TOP_K):
        pltpu.make_async_copy(h2_ref, xs_ref.at[pl.ds(0, tm)], sem).wait()


def _dispatch(dest, fill_start, pad_rows, nused, h2, p):
    t, d = h2.shape
    tm = TM_ROWS
    grid_spec = pltpu.PrefetchScalarGridSpec(
        num_scalar_prefetch=4,
        grid=(t // tm,),
        in_specs=[pl.BlockSpec((tm, d), lambda i, *_: (i, 0))],
        out_specs=pl.BlockSpec(memory_space=pl.ANY),
        scratch_shapes=[pltpu.VMEM((EXPERT_BLK // 2, d), h2.dtype), pltpu.SemaphoreType.DMA,
                        pltpu.SemaphoreType.DMA],
    )
    return pl.pallas_call(
        _dispatch_kernel,
        grid_spec=grid_spec,
        out_shape=jax.ShapeDtypeStruct((p, d), h2.dtype),
        compiler_params=_params(("arbitrary",)),
        name="dispatch",
    )(dest, fill_start, pad_rows, nused, h2)


def _expert_kernel(blk_e_ref, nused_ref, xs_ref, wg_ref, wu_ref, wd_ref, ys_ref, wg16, wu16, wd16):
    b = pl.program_id(0)
    used = b < nused_ref[0]

    new_expert = jnp.logical_or(b == 0, blk_e_ref[b] != blk_e_ref[jnp.maximum(b - 1, 0)])

    @pl.when(jnp.logical_and(used, new_expert))
    def _():
        wg16[...] = wg_ref[0].astype(BF16)
        wu16[...] = wu_ref[0].astype(BF16)
        wd16[...] = wd_ref[0].astype(BF16)

    @pl.when(used)
    def _():
        half = xs_ref.shape[1]
        x_lo, x_hi = (v.astype(BF16) for v in _unpack_halves(xs_ref[...]))
        g = _dot(x_lo, wg16[0:half, :]) + _dot(x_hi, wg16[half:, :])
        u = _dot(x_lo, wu16[0:half, :]) + _dot(x_hi, wu16[half:, :])
        hid = (g * _sigmoid(g)) * u
        ys_ref[...] = _pack_halves(_dot(hid.astype(BF16), wd16[...]))

    @pl.when(b >= nused_ref[0])
    def _():
        ys_ref[...] = jnp.zeros_like(ys_ref)


def _experts(blk_e, nused, xs, wg, wu, wd):
    p, dh = xs.shape
    d = wg.shape[1]
    assert d == 2 * dh, (d, dh)
    blk = EXPERT_BLK
    row_map = lambda b, be, nu: (jnp.minimum(b, nu[0] - 1), 0)
    w_map = lambda b, be, nu: (be[b], 0, 0)
    grid_spec = pltpu.PrefetchScalarGridSpec(
        num_scalar_prefetch=2,
        grid=(p // blk,),
        in_specs=[pl.BlockSpec((blk, dh), row_map),
                  pl.BlockSpec((1, d, D_EXPERT), w_map),
                  pl.BlockSpec((1, d, D_EXPERT), w_map),
                  pl.BlockSpec((1, D_EXPERT, d), w_map)],
        out_specs=pl.BlockSpec((blk, dh), lambda b, be, nu: (b, 0)),
        scratch_shapes=[pltpu.VMEM((d, D_EXPERT), BF16), pltpu.VMEM((d, D_EXPERT), BF16),
                        pltpu.VMEM((D_EXPERT, d), BF16)],
    )
    return pl.pallas_call(
        _expert_kernel,
        grid_spec=grid_spec,
        out_shape=jax.ShapeDtypeStruct((p, dh), U32),
        compiler_params=_params(("arbitrary",)),
        name="experts",
    )(blk_e, nused, xs, wg, wu, wd)


def _combine_kernel(final_norm, dest_ref, ys_ref, x1_ref, route_ref, gtf_ref, nf_ref, o_ref, ybuf, sem):
    i = pl.program_id(0)
    tm = x1_ref.shape[0]
    slot = i % 2

    def gather_tile(tile, slot_):
        def start(r, carry):
            for k in range(TOP_K):
                d = dest_ref[(tile * tm + r) * TOP_K + k]
                pltpu.make_async_copy(ys_ref.at[pl.ds(d, 1)], ybuf.at[slot_, k, pl.ds(r, 1)],
                                      sem.at[slot_]).start()
            return carry
        lax.fori_loop(0, tm, start, 0, unroll=ROW_UNROLL)

    @pl.when(i == 0)
    def _():
        gather_tile(0, 0)

    @pl.when(i + 1 < pl.num_programs(0))
    def _():
        gather_tile(i + 1, 1 - slot)

    for k in range(TOP_K):
        pltpu.make_async_copy(ys_ref.at[pl.ds(0, tm)], ybuf.at[slot, k], sem.at[slot]).wait()
    route = route_ref[...]
    lo0, hi0 = _unpack_halves(ybuf[slot, 0])
    lo1, hi1 = _unpack_halves(ybuf[slot, 1])
    g0, g1 = route[:, 2:3], route[:, 3:4]
    moe = jnp.concatenate([lo0 * g0 + lo1 * g1, hi0 * g0 + hi1 * g1], axis=1)
    x2 = x1_ref[...] + gtf_ref[0] * moe
    if final_norm:
        x2 = x2 * lax.rsqrt(jnp.mean(x2 * x2, axis=-1, keepdims=True) + EPS) * nf_ref[...]
    o_ref[...] = x2


def _combine(dest, ys, x1, route, gtf, nf, seq, final_norm):
    t, d = x1.shape
    tm = TM_ROWS
    tps = seq // tm
    grid_spec = pltpu.PrefetchScalarGridSpec(
        num_scalar_prefetch=1,
        grid=(t // tm,),
        in_specs=[pl.BlockSpec(memory_space=pl.ANY),
                  pl.BlockSpec((tm, d), lambda i, dest: (i, 0)),
                  pl.BlockSpec((tm, LANES), lambda i, dest: (i, 0)),
                  pl.BlockSpec((1, 1, d), lambda i, dest: (i // tps, 0, 0)),
                  pl.BlockSpec((1, d), lambda i, dest: (0, 0))],
        out_specs=pl.BlockSpec((tm, d), lambda i, dest: (i, 0)),
        scratch_shapes=[pltpu.VMEM((2, TOP_K, tm, ys.shape[1]), ys.dtype), pltpu.SemaphoreType.DMA((2,))],
    )
    return pl.pallas_call(
        functools.partial(_combine_kernel, final_norm),
        grid_spec=grid_spec,
        out_shape=jax.ShapeDtypeStruct((t, d), F32),
        compiler_params=_params(("arbitrary",)),
        name="combine",
    )(dest, ys, x1, route, gtf, nf)


def _block_tri(n, chunk, lower):
    r = jnp.arange(n)[:, None]
    c = jnp.arange(n)[None, :]
    same = (r // chunk) == (c // chunk)
    tri = (r >= c) if lower else (r <= c)
    return jnp.where(same & tri, 1.0, 0.0).astype(BF16)


def _layer(x2, c, bsz, seq, w_ada, b_ada, norm_mix, w_in, w_conv, a_log, dt_bias, gdn_norm,
           attn_norm, w_out, rel_bias, norm_ffn, w_rg, w_re, w_gate, w_up, w_down):
    t, d = x2.shape
    mod = _adaln(c, w_ada, b_ada)
    sh_a, sc_a, gt_a, sh_f, sc_f, gt_f = [m.reshape(bsz, 1, d) for m in jnp.split(mod, 6, axis=-1)]

    s1 = GDN_CONV_W
    s2 = s1 + GDN_V_W
    s4 = s2 + 2 * GDN_HEADS
    small = jnp.pad(w_in[:, s2:s4], ((0, 0), (0, SMALL_W - 2 * GDN_HEADS)))
    wcat = jnp.concatenate([w_in[:, :s2], small, w_in[:, s4:]], axis=1).astype(BF16)
    qkva, z, ba, *qkvb = _inproj(x2, sc_a, sh_a, norm_mix.reshape(1, d), wcat, w_conv, seq)

    pad4 = lambda v: jnp.pad(v.astype(F32), (GDN_HEADS, LANES - 2 * GDN_HEADS))
    gp = jnp.zeros((8, LANES), F32).at[0].set(pad4(a_log)).at[1].set(pad4(dt_bias))
    og = _gdn(qkva, ba, ba[:, :8].T, gp, _gp_cols(a_log, dt_bias),
              _block_tri(GDN_TOK, CHUNK, True), _block_tri(GDN_TOK, CHUNK, False), seq)

    outs, lses = [], []
    for (window, dil), qkvb_d in zip(DILATED_PAIRS, qkvb):
        o_i, lse_i = _attn_branch(qkvb_d, jnp.asarray(_bucket_table(window, dil)), rel_bias.astype(F32),
                                  bsz, seq, dil)
        outs.append(o_i)
        lses.append(lse_i)

    expand = jnp.where((jnp.arange(LANES)[:, None] == jnp.arange(ATTN_W)[None, :] // ATTN_DH), 1.0, 0.0).astype(BF16)
    lstrict = jnp.where(jnp.arange(TM_POST)[:, None] > jnp.arange(TM_POST)[None, :], 1.0, 0.0).astype(BF16)
    wr = jnp.pad(jnp.concatenate([w_rg, w_re], axis=1).astype(F32), ((0, 0), (0, LANES - N_GROUPS - N_EXPERTS)))
    wrh = wr.astype(BF16)
    wr_split = jnp.concatenate([wrh, (wr - wrh.astype(F32)).astype(BF16)], axis=1)
    x1, h2, route, cnt = _post(og, z, outs, lses, x2, gt_a, sc_f, sh_f,
                               gdn_norm.reshape(1, GDN_DV), attn_norm.reshape(1, ATTN_W),
                               w_out.astype(BF16), norm_ffn.reshape(1, d), wr_split, expand, lstrict, seq)

    blk = EXPERT_BLK
    counts = cnt[0, :N_EXPERTS].astype(jnp.int32)
    padded = (counts + blk - 1) // blk * blk
    pends = jnp.cumsum(padded)
    pstarts = pends - padded
    eids = route[:, 0:TOP_K].astype(jnp.int32)
    ranks = route[:, 4:4 + TOP_K].astype(jnp.int32)
    expert_ids = jnp.arange(N_EXPERTS, dtype=jnp.int32)
    seg_start = jnp.sum(jnp.where(eids[..., None] == expert_ids, pstarts, 0), axis=-1)
    dest = (seg_start + ranks).reshape(t * TOP_K)
    a = t * TOP_K
    p = -(-a // blk) * blk + N_EXPERTS * blk
    nblk = p // blk
    blk_start = jnp.arange(nblk, dtype=jnp.int32) * blk
    blk_e = jnp.minimum(jnp.sum((pends[None, :] <= blk_start[:, None]).astype(jnp.int32), axis=1),
                        N_EXPERTS - 1)
    nused = (pends[-1] // blk).astype(jnp.int32).reshape(1)

    xs = _dispatch(dest, pstarts + counts, padded - counts, nused, h2, p)
    ys = _experts(blk_e, nused, xs, w_gate, w_up, w_down)
    return ys, dest, x1, route, gt_f


def _gp_cols(a_log, dt_bias):
    z = jnp.zeros((8, LANES), F32)
    z = z.at[GDN_HEADS:2 * GDN_HEADS, 0].set(a_log.astype(F32))
    z = z.at[GDN_HEADS:2 * GDN_HEADS, 1].set(dt_bias.astype(F32))
    return z


def kernel(x, c, w_ada, b_ada, norm_mix, w_in, w_conv, a_log, dt_bias, gdn_norm, attn_norm, w_out,
           rel_bias, norm_ffn, w_router_group, w_router_expert, w_gate, w_up, w_down, norm_final):
    bsz, seq, d = x.shape
    depth = w_ada.shape[0]
    x2 = x.reshape(bsz * seq, d)
    for l in range(depth):
        ys, dest, x1, route, gt_f = _layer(
            x2, c, bsz, seq, w_ada[l], b_ada[l], norm_mix[l], w_in[l], w_conv[l], a_log[l], dt_bias[l],
            gdn_norm[l], attn_norm[l], w_out[l], rel_bias, norm_ffn[l], w_router_group[l],
            w_router_expert[l], w_gate[l], w_up[l], w_down[l])
        x2 = _combine(dest, ys, x1, route, gt_f, norm_final.reshape(1, d), seq, l == depth - 1)
    return x2.reshape(bsz, seq, d)
```

```python
import functools
import math

import jax
import jax.numpy as jnp
import numpy as np
from jax import lax
from jax.experimental import pallas as pl
from jax.experimental.pallas import tpu as pltpu

D_MODEL = 1024
GDN_HEADS = 4
GDN_DK = 128
GDN_DV = 128
CONV_K = 4
CHUNK = 64
ATTN_HEADS = 8
ATTN_DH = 64
DILATED_PAIRS = ((128, 1), (512, 4), (2048, 16))
QBLK = 128
NUM_BUCKETS = 32
REL_MAX_DIST = 2048
N_GROUPS = 4
EXPERTS_PER_GROUP = 8
N_EXPERTS = N_GROUPS * EXPERTS_PER_GROUP
TOP_K = 2
D_EXPERT = 256
EPS = 1e-6
NEG_INF = -1e30

GDN_QK_W = GDN_HEADS * GDN_DK
GDN_V_W = GDN_HEADS * GDN_DV
ATTN_W = ATTN_HEADS * ATTN_DH
GDN_CONV_W = 2 * GDN_QK_W + GDN_V_W
LANES = 128
SUBLANES = 8
SMALL_W = LANES
W_COLS = GDN_CONV_W + GDN_V_W + SMALL_W + 3 * ATTN_W

TM_PROJ = 512
CONV_BLK = 128
HEAD_ROWS = 16
TM_POST = 512
TM_ROWS = 512
GDN_TOK = 512
GDN_GROUP = 4
INV_BASE = 8
ATTN_SUB = 4
EXPERT_BLK = 512
ROW_UNROLL = 8
VMEM_LIMIT = 56 * 1024 * 1024

F32 = jnp.float32
BF16 = jnp.bfloat16
U32 = jnp.uint32
HIGHEST = lax.Precision.HIGHEST


def _sigmoid(x):
    return 1.0 / (1.0 + jnp.exp(-x))


def _dot(a, b, precision=None):
    return jnp.dot(a, b, preferred_element_type=F32, precision=precision)


def _dot_nt(a, b, precision=None):
    return lax.dot_general(a, b, (((1,), (1,)), ((), ())), preferred_element_type=F32,
                           precision=precision)


def _dot_tn(a, b, precision=None):
    return lax.dot_general(a, b, (((0,), (0,)), ((), ())), preferred_element_type=F32,
                           precision=precision)


def _params(sem):
    return pltpu.CompilerParams(dimension_semantics=sem, vmem_limit_bytes=VMEM_LIMIT)


def _adaln_kernel(c_ref, w_ref, b_ref, o_ref):
    o_ref[...] = _dot(c_ref[...], w_ref[...], HIGHEST) + b_ref[...]


def _adaln(c, w, b):
    bsz, d = c.shape
    n = w.shape[1]
    tn = 1024
    return pl.pallas_call(
        _adaln_kernel,
        grid=(n // tn,),
        in_specs=[pl.BlockSpec((bsz, d), lambda j: (0, 0)),
                  pl.BlockSpec((d, tn), lambda j: (0, j)),
                  pl.BlockSpec((1, tn), lambda j: (0, j))],
        out_specs=pl.BlockSpec((bsz, tn), lambda j: (0, j)),
        out_shape=jax.ShapeDtypeStruct((bsz, n), F32),
        compiler_params=_params(("arbitrary",)),
        name="adaln",
    )(c, w, b.reshape(1, n))


def _inproj_kernel(tiles_per_seq, x_ref, sc_ref, sh_ref, g_ref, w_ref, wconv_ref, shift_ref,
                   qkva_ref, z_ref, ba_ref, *rest):
    qkvb_refs = rest[:len(DILATED_PAIRS)]
    win_scr, pb_scr, pb_next_scr = rest[len(DILATED_PAIRS):]
    i = pl.program_id(0)
    tm = x_ref.shape[0]
    x = x_ref[...]
    h = x * lax.rsqrt(jnp.mean(x * x, axis=-1, keepdims=True) + EPS) * g_ref[...]
    h = h * (1.0 + sc_ref[0]) + sh_ref[0]
    hb = h.astype(BF16)

    @pl.when(i % tiles_per_seq == 0)
    def _():
        win_scr[0:HEAD_ROWS, :] = jnp.zeros((HEAD_ROWS, GDN_CONV_W), F32)

    pa = _dot(hb, w_ref[:, 0:GDN_CONV_W])
    pa16 = pa.astype(BF16)
    for b in range(tm // CONV_BLK):
        rs = slice(b * CONV_BLK, (b + 1) * CONV_BLK)
        acc = pa[rs] * wconv_ref[CONV_K - 1:CONV_K, :]
        for j in range(CONV_K - 1):
            if b == 0:
                shifted = _dot(shift_ref[j, :, CONV_BLK:], pa16[rs])
            else:
                shifted = _dot(shift_ref[j], pa16[(b - 1) * CONV_BLK:(b + 1) * CONV_BLK])
            acc = acc + shifted * wconv_ref[j:j + 1, :]
        qkva_ref[rs, :] = (acc * _sigmoid(acc)).astype(qkva_ref.dtype)
    win_scr[HEAD_ROWS:2 * HEAD_ROWS, :] = pa[0:HEAD_ROWS]
    acc = pa[0:HEAD_ROWS] * wconv_ref[CONV_K - 1:CONV_K, :]
    for j in range(CONV_K - 1):
        acc = acc + win_scr[pl.ds(HEAD_ROWS - (CONV_K - 1) + j, HEAD_ROWS), :] * wconv_ref[j:j + 1, :]
    qkva_ref[0:HEAD_ROWS, :] = (acc * _sigmoid(acc)).astype(qkva_ref.dtype)
    win_scr[0:HEAD_ROWS, :] = pa[tm - HEAD_ROWS:tm]

    c0 = GDN_CONV_W
    z_ref[...] = _dot(hb, w_ref[:, c0:c0 + GDN_V_W]).astype(z_ref.dtype)
    c0 += GDN_V_W
    ba_ref[...] = _dot(hb, w_ref[:, c0:c0 + SMALL_W])
    c0 += SMALL_W
    pb = _dot(hb, w_ref[:, c0:c0 + 3 * ATTN_W])
    ncol = 3 * ATTN_W // LANES
    for j in range(ncol):
        pb_scr[j] = pb[:, j * LANES:(j + 1) * LANES]
    prev_dil, prev_scr, next_scr = 1, pb_scr, pb_next_scr
    for bi, ((_, dil), ref) in enumerate(zip(DILATED_PAIRS, qkvb_refs)):
        if dil == 1:
            ref[...] = pb.astype(ref.dtype)
            continue
        f = dil // prev_dil
        n = tm // dil
        keep = bi + 1 < len(DILATED_PAIRS)
        for r in range(prev_dil):
            for q in range(f):
                r_new = r + prev_dil * q
                for j in range(ncol):
                    rows = prev_scr[j, pl.ds(r * (tm // prev_dil) + q, n, stride=f), :]
                    c1 = r_new * 3 * ATTN_W + j * LANES
                    ref[:, c1:c1 + LANES] = rows.astype(ref.dtype)
                    if keep:
                        next_scr[j, r_new * n:(r_new + 1) * n, :] = rows
        prev_dil, prev_scr, next_scr = dil, next_scr, prev_scr


def _inproj(x2, sc, sh, g, wcat, wconv, seq):
    t, d = x2.shape
    tm = TM_PROJ
    tps = seq // tm
    mod_spec = pl.BlockSpec((1, 1, d), lambda i: (i // tps, 0, 0))
    return pl.pallas_call(
        functools.partial(_inproj_kernel, tps),
        grid=(t // tm,),
        in_specs=[pl.BlockSpec((tm, d), lambda i: (i, 0)),
                  mod_spec, mod_spec,
                  pl.BlockSpec((1, d), lambda i: (0, 0)),
                  pl.BlockSpec((d, W_COLS), lambda i: (0, 0)),
                  pl.BlockSpec((CONV_K, GDN_CONV_W), lambda i: (0, 0)),
                  pl.BlockSpec((CONV_K - 1, CONV_BLK, 2 * CONV_BLK), lambda i: (0, 0, 0))],
        out_specs=[pl.BlockSpec((tm, GDN_CONV_W), lambda i: (i, 0)),
                   pl.BlockSpec((tm, GDN_V_W), lambda i: (i, 0)),
                   pl.BlockSpec((tm, SMALL_W), lambda i: (i, 0))]
                  + [pl.BlockSpec((tm // dil, dil * 3 * ATTN_W), lambda i: (i, 0)) for _, dil in DILATED_PAIRS],
        out_shape=[jax.ShapeDtypeStruct((t, GDN_CONV_W), BF16),
                   jax.ShapeDtypeStruct((t, GDN_V_W), BF16),
                   jax.ShapeDtypeStruct((t, SMALL_W), F32)]
                  + [jax.ShapeDtypeStruct((t // dil, dil * 3 * ATTN_W), BF16) for _, dil in DILATED_PAIRS],
        scratch_shapes=[pltpu.VMEM((2 * HEAD_ROWS, GDN_CONV_W), F32),
                        pltpu.VMEM((3 * ATTN_W // LANES, tm, LANES), F32),
                        pltpu.VMEM((3 * ATTN_W // LANES, tm, LANES), F32)],
        compiler_params=_params(("arbitrary",)),
        name="inproj",
    )(x2, sc, sh, g, wcat, wconv, _shift_bands())


def _shift_bands():
    t = np.arange(CONV_BLK)[:, None]
    c = np.arange(2 * CONV_BLK)[None, :]
    bands = [(c == CONV_BLK + t - (CONV_K - 1 - j)) for j in range(CONV_K - 1)]
    return jnp.asarray(np.stack(bands).astype(np.float32), dtype=BF16)


def _softplus(x):
    return jnp.maximum(x, 0.0) + jnp.log(1.0 + jnp.exp(-jnp.abs(x)))


def _unit_lower_inverses(n_list):
    c = n_list[0].shape[0]
    row = lax.broadcasted_iota(jnp.int32, (c, c), 0)
    col = lax.broadcasted_iota(jnp.int32, (c, c), 1)
    eye = jnp.where(row == col, 1.0, 0.0)
    same_base = (row // INV_BASE) == (col // INV_BASE)
    n0 = [jnp.where(same_base, n, 0.0) for n in n_list]
    p = [eye - x for x in n0]
    m = [x.astype(BF16) for x in n0]
    for j in range(int(math.log2(INV_BASE)) - 1):
        m = [_dot(x, x).astype(BF16) for x in m]
        p = [x + _dot(x.astype(BF16), y) for x, y in zip(p, m)]
    size = INV_BASE
    while size < c:
        sibling = jnp.logical_and((row // size) % 2 == 1, (col // size) == (row // size) - 1)
        cb = [jnp.where(sibling, n, 0.0).astype(BF16) for n in n_list]
        pb = [x.astype(BF16) for x in p]
        pc = [_dot(x, y).astype(BF16) for x, y in zip(pb, cb)]
        p = [x - _dot(y, z) for x, y, z in zip(p, pc, pb)]
        size *= 2
    return p


def _split3_bf16(x):
    h1 = x.astype(BF16)
    r1 = x - h1.astype(F32)
    h2 = r1.astype(BF16)
    h3 = (r1 - h2.astype(F32)).astype(BF16)
    return h1, h2, h3


def _gdn_kernel(steps_per_seq, qkv_ref, ba_ref, bat_ref, gp_ref, gpt_ref, lblk_ref, ublk_ref,
                o_ref, state_scr):
    i = pl.program_id(0)

    @pl.when(i % steps_per_seq == 0)
    def _():
        state_scr[...] = jnp.zeros_like(state_scr)

    nchunk = GDN_TOK // CHUNK
    ba = ba_ref[...]
    bat = bat_ref[...]
    a_vec = -jnp.exp(gp_ref[0:1, :])
    g_tile = a_vec * _softplus(ba + gp_ref[1:2, :])
    lblk = lblk_ref[...]
    g_cum = sum(_dot(lblk, part) for part in _split3_bf16(g_tile))
    a_col = -jnp.exp(gpt_ref[:, 0:1])
    gt_tile = a_col * _softplus(bat + gpt_ref[:, 1:2])
    ublk = ublk_ref[...]
    gt_cum = sum(_dot(part, ublk) for part in _split3_bf16(gt_tile))
    beta_tile = _sigmoid(ba)

    row = lax.broadcasted_iota(jnp.int32, (CHUNK, CHUNK), 0)
    col = lax.broadcasted_iota(jnp.int32, (CHUNK, CHUNK), 1)
    incl = row >= col
    strict = row > col

    for h0 in range(0, GDN_HEADS, GDN_GROUP):
        heads = range(h0, h0 + GDN_GROUP)
        kb_l, k_l, q_l, rhs_l, qg_l, kg_l, gl_l, inc_l, str_l = ([] for _ in range(9))
        for h in heads:
            q_raw = qkv_ref[:, h * GDN_DK:(h + 1) * GDN_DK].astype(F32)
            k_raw = qkv_ref[:, GDN_QK_W + h * GDN_DK:GDN_QK_W + (h + 1) * GDN_DK].astype(F32)
            v = qkv_ref[:, 2 * GDN_QK_W + h * GDN_DV:2 * GDN_QK_W + (h + 1) * GDN_DV].astype(F32)
            qn = q_raw * lax.rsqrt(jnp.sum(q_raw * q_raw, axis=-1, keepdims=True) + EPS) * (GDN_DK ** -0.5)
            kn = k_raw * lax.rsqrt(jnp.sum(k_raw * k_raw, axis=-1, keepdims=True) + EPS)
            beta = beta_tile[:, h:h + 1]
            gc_all = g_cum[:, GDN_HEADS + h:GDN_HEADS + h + 1]
            eg = jnp.exp(gc_all)
            kb = kn * beta
            rhs = jnp.concatenate([v * beta, kb * eg], axis=1).astype(BF16)
            qg = qn * eg
            kb16, k16, q16 = kb.astype(BF16), kn.astype(BF16), qn.astype(BF16)
            for c in range(nchunk):
                sl = slice(c * CHUNK, (c + 1) * CHUNK)
                g_c = gc_all[sl]
                g_r = gt_cum[GDN_HEADS + h:GDN_HEADS + h + 1, sl]
                dec = jnp.exp(g_c - g_r)
                inc_l.append(jnp.where(incl, dec, 0.0))
                str_l.append(jnp.where(strict, dec, 0.0))
                g_last = g_c[CHUNK - 1:CHUNK, :]
                kg_l.append((kn[sl] * jnp.exp(g_last - g_c)).astype(BF16))
                gl_l.append(jnp.exp(g_last))
                kb_l.append(kb16[sl])
                k_l.append(k16[sl])
                q_l.append(q16[sl])
                rhs_l.append(rhs[sl])
                qg_l.append(qg[sl])
        nprob = len(k_l)
        n_l = [_dot_nt(kb_l[j], k_l[j]) * str_l[j] for j in range(nprob)]
        aqk_l = [(_dot_nt(q_l[j], k_l[j]) * inc_l[j]).astype(BF16) for j in range(nprob)]
        tinv_l = _unit_lower_inverses(n_l)
        sol_l = [_dot(tinv_l[j].astype(BF16), rhs_l[j]).astype(BF16) for j in range(nprob)]
        a2_l = [_dot(aqk_l[j], sol_l[j]) for j in range(nprob)]
        k2_l = [_dot_tn(kg_l[j], sol_l[j]) for j in range(nprob)]
        states = [state_scr[h] for h in heads]
        for c in range(nchunk):
            sl = slice(c * CHUNK, (c + 1) * CHUNK)
            for hi, h in enumerate(heads):
                j = hi * nchunk + c
                sb = states[hi].astype(BF16)
                qeff = (qg_l[j] - a2_l[j][:, GDN_DV:]).astype(BF16)
                o = _dot(qeff, sb) + a2_l[j][:, :GDN_DV]
                states[hi] = gl_l[j] * states[hi] + k2_l[j][:, :GDN_DV] - _dot(k2_l[j][:, GDN_DV:].astype(BF16), sb)
                o_ref[sl, h * GDN_DV:(h + 1) * GDN_DV] = o.astype(o_ref.dtype)
        for hi, h in enumerate(heads):
            state_scr[h] = states[hi]


def _gdn(qkva, ba, bat, gp, gpt, lblk, ublk, seq):
    t = qkva.shape[0]
    tok = GDN_TOK
    sps = seq // tok
    const = lambda i: (0, 0)
    return pl.pallas_call(
        functools.partial(_gdn_kernel, sps),
        grid=(t // tok,),
        in_specs=[pl.BlockSpec((tok, GDN_CONV_W), lambda i: (i, 0)),
                  pl.BlockSpec((tok, SMALL_W), lambda i: (i, 0)),
                  pl.BlockSpec((8, tok), lambda i: (0, i)),
                  pl.BlockSpec((8, LANES), const),
                  pl.BlockSpec((8, LANES), const),
                  pl.BlockSpec((tok, tok), const),
                  pl.BlockSpec((tok, tok), const)],
        out_specs=pl.BlockSpec((tok, GDN_V_W), lambda i: (i, 0)),
        out_shape=jax.ShapeDtypeStruct((t, GDN_V_W), F32),
        scratch_shapes=[pltpu.VMEM((GDN_HEADS, GDN_DK, GDN_DV), F32)],
        compiler_params=_params(("arbitrary",)),
        name="gdn",
    )(qkva, ba, bat, gp, gpt, lblk, ublk)


def _attn_kernel(cur_ref, prev_ref, bucket_ref, rb_ref, o_ref, lse_ref, bias_ref):
    n = pl.program_id(2)

    @pl.when(jnp.logical_and(jnp.logical_and(pl.program_id(0) == 0, pl.program_id(1) == 0), n == 0))
    def _():
        bucket = bucket_ref[...]
        col = lax.broadcasted_iota(jnp.int32, (QBLK, 2 * QBLK), 1)
        for h in range(ATTN_HEADS):
            acc = jnp.full((QBLK, 2 * QBLK), NEG_INF, F32)
            for b in range(NUM_BUCKETS):
                acc = jnp.where(bucket == b, rb_ref[b, h], acc)
            bias_ref[0, h] = acc
            bias_ref[1, h] = jnp.where(col < QBLK, NEG_INF, acc)

    nsub = cur_ref.shape[1] // QBLK
    nres = cur_ref.shape[2] // (3 * ATTN_W)
    first = jnp.where(n == 0, 1, 0)
    w = ATTN_W
    q_all = [cur_ref[0, :, c * 3 * w:c * 3 * w + w] for c in range(nres)]
    k_all = [jnp.concatenate([prev_ref[0, :, c * 3 * w + w:c * 3 * w + 2 * w],
                              cur_ref[0, :, c * 3 * w + w:c * 3 * w + 2 * w]], axis=0) for c in range(nres)]
    v_all = [jnp.concatenate([prev_ref[0, :, c * 3 * w + 2 * w:(c + 1) * 3 * w],
                              cur_ref[0, :, c * 3 * w + 2 * w:(c + 1) * 3 * w]], axis=0) for c in range(nres)]
    lane = lax.broadcasted_iota(jnp.int32, (QBLK, LANES), 1)
    ones = jnp.ones((2 * QBLK, LANES), BF16)
    per_group = LANES // ATTN_DH
    ngrp = ATTN_HEADS // per_group
    scale = ATTN_DH ** -0.5
    mine = [jnp.logical_and(lane >= j * ATTN_DH, lane < (j + 1) * ATTN_DH) for j in range(per_group)]
    probs = [(c, u, h) for c in range(nres) for u in range(nsub) for h in range(ATTN_HEADS)]
    s_l = []
    for c, u, h in probs:
        grp, j = divmod(h, per_group)
        gs = slice(grp * LANES, (grp + 1) * LANES)
        qg = q_all[c][u * QBLK:(u + 1) * QBLK, gs] * scale
        table = first if u == 0 else 0
        s_l.append(_dot_nt(jnp.where(mine[j], qg, jnp.zeros_like(qg)), k_all[c][u * QBLK:(u + 2) * QBLK, gs])
                   + bias_ref[table, h])
    m_l = [jnp.max(s, axis=-1, keepdims=True) for s in s_l]
    p_l = [jnp.exp(s - m).astype(BF16) for s, m in zip(s_l, m_l)]
    v_ext = {(c, u, grp): jnp.concatenate(
                 [v_all[c][u * QBLK:(u + 2) * QBLK, grp * LANES:(grp + 1) * LANES], ones], axis=1)
             for c in range(nres) for u in range(nsub) for grp in range(ngrp)}
    pv_l = [_dot(p, v_ext[(c, u, h // per_group)]) for p, (c, u, h) in zip(p_l, probs)]
    for c in range(nres):
        for u in range(nsub):
            outs = []
            lse_tile = jnp.zeros((QBLK, LANES), F32)
            for grp in range(ngrp):
                o_grp = jnp.zeros((QBLK, LANES), F32)
                for j in range(per_group):
                    h = grp * per_group + j
                    idx = (c * nsub + u) * ATTN_HEADS + h
                    den = pv_l[idx][:, LANES:]
                    o_grp = jnp.where(mine[j], pv_l[idx][:, :LANES] / den, o_grp)
                    lse_tile = jnp.where(lane == h, m_l[idx] + jnp.log(den[:, 0:1]), lse_tile)
                outs.append(o_grp)
            rs = slice(u * QBLK, (u + 1) * QBLK)
            o_ref[0, rs, c * w:(c + 1) * w] = jnp.concatenate(outs, axis=1).astype(o_ref.dtype)
            lse_ref[0, rs, c * LANES:(c + 1) * LANES] = lse_tile


def _attn_branch(qkvb, bucket, rel_bias, bsz, seq, dil):
    ln = seq // dil
    sub = min(ATTN_SUB, ln // QBLK)
    rows = sub * QBLK
    assert seq % dil == 0 and ln % rows == 0, (seq, dil, rows)
    nb = ln // rows
    nres = min(max(ATTN_SUB // sub, 1), dil)
    assert dil % nres == 0, (dil, nres)
    xv = qkvb.reshape(bsz, ln, dil * 3 * ATTN_W)
    w = ATTN_W
    o, lse = pl.pallas_call(
        _attn_kernel,
        grid=(bsz, dil // nres, nb),
        in_specs=[pl.BlockSpec((1, rows, nres * 3 * w), lambda b, r, n: (b, n, r)),
                  pl.BlockSpec((1, QBLK, nres * 3 * w), lambda b, r, n: (b, jnp.maximum(n * sub - 1, 0), r)),
                  pl.BlockSpec((QBLK, 2 * QBLK), lambda b, r, n: (0, 0)),
                  pl.BlockSpec(memory_space=pltpu.SMEM)],
        out_specs=[pl.BlockSpec((1, rows, nres * w), lambda b, r, n: (b, n, r)),
                   pl.BlockSpec((1, rows, nres * LANES), lambda b, r, n: (b, n, r))],
        out_shape=[jax.ShapeDtypeStruct((bsz, ln, dil * w), BF16),
                   jax.ShapeDtypeStruct((bsz, ln, dil * LANES), F32)],
        scratch_shapes=[pltpu.VMEM((2, ATTN_HEADS, QBLK, 2 * QBLK), F32)],
        compiler_params=_params(("arbitrary", "arbitrary", "arbitrary")),
        name=f"attn_d{dil}",
    )(xv, xv, bucket, rel_bias)
    return o.reshape(bsz * ln, dil * w), lse.reshape(bsz * ln, dil * LANES)


def _bucket_table(window, dil):
    steps = window // dil
    qi = np.arange(QBLK)[:, None]
    kj = np.arange(2 * QBLK)[None, :]
    delta = qi + QBLK - kj
    dist = np.maximum(delta, 0) * dil
    max_exact = NUM_BUCKETS // 2
    d_f = np.maximum(dist, 1).astype(np.float32)
    large = max_exact + (np.log(d_f / max_exact) / math.log(REL_MAX_DIST / max_exact)
                         * (NUM_BUCKETS - max_exact)).astype(np.int32)
    bucket = np.where(dist < max_exact, dist, np.minimum(large, NUM_BUCKETS - 1))
    return np.where((delta >= 0) & (delta <= steps), bucket, -1).astype(np.int32)


def _split_bf16(x):
    hi = x.astype(BF16)
    lo = (x - hi.astype(F32)).astype(BF16)
    return hi, lo


def _pack_halves(x):
    w = x.shape[1] // 2
    lo = lax.bitcast_convert_type(x[:, :w].astype(BF16).astype(F32), U32)
    hi = lax.bitcast_convert_type(x[:, w:].astype(BF16).astype(F32), U32)
    return (hi & jnp.uint32(0xFFFF0000)) | (lo >> 16)


def _unpack_halves(u):
    lo = lax.bitcast_convert_type(u << 16, F32)
    hi = lax.bitcast_convert_type(u & jnp.uint32(0xFFFF0000), F32)
    return lo, hi


def _post_kernel(og_ref, z_ref, o1_ref, o2_ref, o3_ref, l1_ref, l2_ref, l3_ref, x_ref,
                 gta_ref, scf_ref, shf_ref, gnorm_ref, anorm_ref, wout_ref, nffn_ref,
                 wrh_ref, exp_ref, ls_ref,
                 x1_ref, h2_ref, route_ref, cnt_ref, carry_scr, *tok_scr):
    i = pl.program_id(0)
    tm = x_ref.shape[0]

    @pl.when(i == 0)
    def _():
        carry_scr[...] = jnp.zeros_like(carry_scr)

    nbr = len(DILATED_PAIRS)
    ob_scr, lse_scr = tok_scr[:nbr], tok_scr[nbr:]
    for (_, dil), o_ref, l_ref, o_s, l_s in zip(DILATED_PAIRS, (o1_ref, o2_ref, o3_ref),
                                                 (l1_ref, l2_ref, l3_ref), ob_scr, lse_scr):
        for r in range(dil):
            rows = pl.ds(r, tm // dil, stride=dil) if dil > 1 else slice(None)
            for j in range(ATTN_W // LANES):
                c1 = r * ATTN_W + j * LANES
                o_s[j, rows, :] = o_ref[:, c1:c1 + LANES].astype(F32)
            l_s[rows, :] = l_ref[:, r * LANES:(r + 1) * LANES]

    heads = []
    for h in range(GDN_HEADS):
        hs = slice(h * GDN_DV, (h + 1) * GDN_DV)
        seg = og_ref[:, hs]
        nrm = seg * lax.rsqrt(jnp.mean(seg * seg, axis=-1, keepdims=True) + EPS) * gnorm_ref[...]
        zz = z_ref[:, hs].astype(F32)
        heads.append((nrm * (zz * _sigmoid(zz))).astype(BF16))
    oa = jnp.concatenate(heads, axis=1)

    l1, l2, l3 = (l_s[...] for l_s in lse_scr)
    m = jnp.maximum(jnp.maximum(l1, l2), l3)
    e1, e2, e3 = jnp.exp(l1 - m), jnp.exp(l2 - m), jnp.exp(l3 - m)
    inv = 1.0 / (e1 + e2 + e3)
    ob = jnp.zeros((tm, ATTN_W), F32)
    for e, o_s in zip((e1, e2, e3), ob_scr):
        hi, lo = _split_bf16(e * inv)
        wexp = _dot(hi, exp_ref[...]) + _dot(lo, exp_ref[...])
        ob = ob + wexp * jnp.concatenate([o_s[j] for j in range(ATTN_W // LANES)], axis=1)
    ob = ob * lax.rsqrt(jnp.mean(ob * ob, axis=-1, keepdims=True) + EPS) * anorm_ref[...]

    mix = _dot(oa, wout_ref[0:GDN_V_W, :]) + _dot(ob.astype(BF16), wout_ref[GDN_V_W:, :])
    x1 = x_ref[...] + gta_ref[0] * mix
    x1_ref[...] = x1
    h2 = x1 * lax.rsqrt(jnp.mean(x1 * x1, axis=-1, keepdims=True) + EPS) * nffn_ref[...]
    h2 = h2 * (1.0 + scf_ref[0]) + shf_ref[0]
    h2_ref[...] = _pack_halves(h2)

    hh, hl = _split_bf16(h2)
    both = _dot(hh, wrh_ref[...])
    logits = both[:, :LANES] + both[:, LANES:] + _dot(hl, wrh_ref[:, :LANES])
    tm = logits.shape[0]
    lane = lax.broadcasted_iota(jnp.int32, (tm, LANES), 1).astype(F32)
    big = float(LANES)
    gmask = lane < N_GROUPS
    glog = jnp.where(gmask, logits, NEG_INF)
    gmax = jnp.max(glog, axis=-1, keepdims=True)
    gidx = jnp.min(jnp.where(jnp.logical_and(gmask, glog == gmax), lane, big), axis=-1, keepdims=True)
    gprob = 1.0 / jnp.sum(jnp.where(gmask, jnp.exp(glog - gmax), 0.0), axis=-1, keepdims=True)
    lo_lane = N_GROUPS + EXPERTS_PER_GROUP * gidx
    emask = jnp.logical_and(lane >= lo_lane, lane < lo_lane + EXPERTS_PER_GROUP)
    elog = jnp.where(emask, logits, NEG_INF)
    m1 = jnp.max(elog, axis=-1, keepdims=True)
    i1 = jnp.min(jnp.where(jnp.logical_and(emask, elog == m1), lane, big), axis=-1, keepdims=True)
    emask2 = jnp.logical_and(emask, lane != i1)
    elog2 = jnp.where(emask2, logits, NEG_INF)
    m2 = jnp.max(elog2, axis=-1, keepdims=True)
    i2 = jnp.min(jnp.where(jnp.logical_and(emask2, elog2 == m2), lane, big), axis=-1, keepdims=True)
    r = jnp.exp(m2 - m1)
    gate1 = gprob / (1.0 + r)
    gate2 = gprob * r / (1.0 + r)
    ex1 = i1 - N_GROUPS
    ex2 = i2 - N_GROUPS

    hit1 = lane == ex1
    hit2 = lane == ex2
    onehot = jnp.where(jnp.logical_or(hit1, hit2), 1.0, 0.0)
    pref = _dot(ls_ref[...], onehot.astype(BF16)) + carry_scr[...]
    rank1 = jnp.sum(jnp.where(hit1, pref, 0.0), axis=-1, keepdims=True)
    rank2 = jnp.sum(jnp.where(hit2, pref, 0.0), axis=-1, keepdims=True)
    carry = carry_scr[...] + jnp.sum(onehot, axis=0, keepdims=True)
    carry_scr[...] = carry
    cnt_ref[...] = jnp.broadcast_to(carry, cnt_ref.shape)

    route = jnp.zeros((tm, LANES), F32)
    for idx, val in enumerate((ex1, ex2, gate1, gate2, rank1, rank2)):
        route = jnp.where(lane == idx, val, route)
    route_ref[...] = route


def _post(og, z, os_, ls_, x2, gta, scf, shf, gnorm, anorm, wout, nffn, wrh, expand, lstrict, seq):
    t, d = x2.shape
    tm = TM_POST
    tps = seq // tm
    tile = lambda w: pl.BlockSpec((tm, w), lambda i: (i, 0))
    const2 = lambda a: pl.BlockSpec(a.shape, lambda i: (0, 0))
    mod_spec = pl.BlockSpec((1, 1, d), lambda i: (i // tps, 0, 0))
    return pl.pallas_call(
        _post_kernel,
        grid=(t // tm,),
        in_specs=[tile(GDN_V_W), tile(GDN_V_W)]
                 + [pl.BlockSpec((tm // dil, dil * ATTN_W), lambda i: (i, 0)) for _, dil in DILATED_PAIRS]
                 + [pl.BlockSpec((tm // dil, dil * LANES), lambda i: (i, 0)) for _, dil in DILATED_PAIRS]
                 + [tile(d),
                  mod_spec, mod_spec, mod_spec,
                  const2(gnorm), const2(anorm), const2(wout), const2(nffn),
                  const2(wrh), const2(expand), const2(lstrict)],
        out_specs=[tile(d), tile(d // 2), tile(LANES), pl.BlockSpec((8, LANES), lambda i: (0, 0))],
        out_shape=[jax.ShapeDtypeStruct((t, d), F32),
                   jax.ShapeDtypeStruct((t, d // 2), U32),
                   jax.ShapeDtypeStruct((t, LANES), F32),
                   jax.ShapeDtypeStruct((8, LANES), F32)],
        scratch_shapes=[pltpu.VMEM((1, LANES), F32)]
                       + [pltpu.VMEM((ATTN_W // LANES, tm, LANES), F32) for _ in DILATED_PAIRS]
                       + [pltpu.VMEM((tm, LANES), F32) for _ in DILATED_PAIRS],
        compiler_params=_params(("arbitrary",)),
        name="post",
    )(og, z, *os_, *ls_, x2, gta, scf, shf, gnorm, anorm, wout, nffn, wrh, expand, lstrict)


def _dispatch_kernel(dest_ref, fill_ref, pad_ref, nused_ref, h2_ref, xs_ref, zero_scr, sem, fill_sem):
    i = pl.program_id(0)
    tm = h2_ref.shape[0]

    @pl.when(i == 0)
    def _():
        zero_scr[...] = jnp.zeros_like(zero_scr)

        def pieces(e, act):
            base, pad = fill_ref[e], pad_ref[e]
            head = pad & (SUBLANES - 1)
            for j in range(SUBLANES - 1):
                @pl.when(j < head)
                def _():
                    act(pltpu.make_async_copy(zero_scr.at[pl.ds(0, 1)], xs_ref.at[pl.ds(base + j, 1)],
                                              fill_sem))
            off = base + head
            for bit in range(SUBLANES.bit_length() - 1, EXPERT_BLK.bit_length() - 1):
                size = 1 << bit

                @pl.when((pad >> bit) & 1 == 1)
                def _():
                    act(pltpu.make_async_copy(zero_scr.at[pl.ds(0, size)],
                                              xs_ref.at[pl.ds(pl.multiple_of(off, SUBLANES), size)], fill_sem))
                off = off + (pad & size)

        def fill(e, carry):
            pieces(e, lambda cp: cp.start())
            return carry

        def drain(e, carry):
            pieces(e, lambda cp: cp.wait())
            return carry

        lax.fori_loop(0, N_EXPERTS, fill, 0)
        lax.fori_loop(0, N_EXPERTS, drain, 0)

        def tail(b, act):
            for half in range(EXPERT_BLK // zero_scr.shape[0]):
                row0 = pl.multiple_of(b * EXPERT_BLK + half * zero_scr.shape[0], SUBLANES)
                act(pltpu.make_async_copy(zero_scr, xs_ref.at[pl.ds(row0, zero_scr.shape[0])], fill_sem))

        def tail_fill(b, carry):
            tail(b, lambda cp: cp.start())
            return carry

        def tail_drain(b, carry):
            tail(b, lambda cp: cp.wait())
            return carry

        nblk = xs_ref.shape[0] // EXPERT_BLK
        lax.fori_loop(nused_ref[0], nblk, tail_fill, 0)
        lax.fori_loop(nused_ref[0], nblk, tail_drain, 0)

    def start(r, carry):
        for k in range(TOP_K):
            d = dest_ref[(i * tm + r) * TOP_K + k]
            pltpu.make_async_copy(h2_ref.at[pl.ds(r, 1)], xs_ref.at[pl.ds(d, 1)], sem).start()
        return carry

    lax.fori_loop(0, tm, start, 0, unroll=ROW_UNROLL)
    for k in range(TOP_K):
        pltpu.make_async_copy(h2_ref, xs_ref.at[pl.ds(0, tm)], sem).wait()


def _dispatch(dest, fill_start, pad_rows, nused, h2, p):
    t, d = h2.shape
    tm = TM_ROWS
    grid_spec = pltpu.PrefetchScalarGridSpec(
        num_scalar_prefetch=4,
        grid=(t // tm,),
        in_specs=[pl.BlockSpec((tm, d), lambda i, *_: (i, 0))],
        out_specs=pl.BlockSpec(memory_space=pl.ANY),
        scratch_shapes=[pltpu.VMEM((EXPERT_BLK // 2, d), h2.dtype), pltpu.SemaphoreType.DMA,
                        pltpu.SemaphoreType.DMA],
    )
    return pl.pallas_call(
        _dispatch_kernel,
        grid_spec=grid_spec,
        out_shape=jax.ShapeDtypeStruct((p, d), h2.dtype),
        compiler_params=_params(("arbitrary",)),
        name="dispatch",
    )(dest, fill_start, pad_rows, nused, h2)


def _expert_kernel(blk_e_ref, nused_ref, xs_ref, wg_ref, wu_ref, wd_ref, ys_ref, wg16, wu16, wd16):
    b = pl.program_id(0)
    used = b < nused_ref[0]

    new_expert = jnp.logical_or(b == 0, blk_e_ref[b] != blk_e_ref[jnp.maximum(b - 1, 0)])

    @pl.when(jnp.logical_and(used, new_expert))
    def _():
        wg16[...] = wg_ref[0].astype(BF16)
        wu16[...] = wu_ref[0].astype(BF16)
        wd16[...] = wd_ref[0].astype(BF16)

    @pl.when(used)
    def _():
        half = xs_ref.shape[1]
        x_lo, x_hi = (v.astype(BF16) for v in _unpack_halves(xs_ref[...]))
        g = _dot(x_lo, wg16[0:half, :]) + _dot(x_hi, wg16[half:, :])
        u = _dot(x_lo, wu16[0:half, :]) + _dot(x_hi, wu16[half:, :])
        hid = (g * _sigmoid(g)) * u
        ys_ref[...] = _pack_halves(_dot(hid.astype(BF16), wd16[...]))

    @pl.when(b >= nused_ref[0])
    def _():
        ys_ref[...] = jnp.zeros_like(ys_ref)


def _experts(blk_e, nused, xs, wg, wu, wd):
    p, dh = xs.shape
    d = wg.shape[1]
    assert d == 2 * dh, (d, dh)
    blk = EXPERT_BLK
    row_map = lambda b, be, nu: (jnp.minimum(b, nu[0] - 1), 0)
    w_map = lambda b, be, nu: (be[b], 0, 0)
    grid_spec = pltpu.PrefetchScalarGridSpec(
        num_scalar_prefetch=2,
        grid=(p // blk,),
        in_specs=[pl.BlockSpec((blk, dh), row_map),
                  pl.BlockSpec((1, d, D_EXPERT), w_map),
                  pl.BlockSpec((1, d, D_EXPERT), w_map),
                  pl.BlockSpec((1, D_EXPERT, d), w_map)],
        out_specs=pl.BlockSpec((blk, dh), lambda b, be, nu: (b, 0)),
        scratch_shapes=[pltpu.VMEM((d, D_EXPERT), BF16), pltpu.VMEM((d, D_EXPERT), BF16),
                        pltpu.VMEM((D_EXPERT, d), BF16)],
    )
    return pl.pallas_call(
        _expert_kernel,
        grid_spec=grid_spec,
        out_shape=jax.ShapeDtypeStruct((p, dh), U32),
        compiler_params=_params(("arbitrary",)),
        name="experts",
    )(blk_e, nused, xs, wg, wu, wd)


def _combine_kernel(final_norm, dest_ref, ys_ref, x1_ref, route_ref, gtf_ref, nf_ref, o_ref, ybuf, sem):
    i = pl.program_id(0)
    tm = x1_ref.shape[0]
    slot = i % 2

    def gather_tile(tile, slot_):
        def start(r, carry):
            for k in range(TOP_K):
                d = dest_ref[(tile * tm + r) * TOP_K + k]
                pltpu.make_async_copy(ys_ref.at[pl.ds(d, 1)], ybuf.at[slot_, k, pl.ds(r, 1)],
                                      sem.at[slot_]).start()
            return carry
        lax.fori_loop(0, tm, start, 0, unroll=ROW_UNROLL)

    @pl.when(i == 0)
    def _():
        gather_tile(0, 0)

    @pl.when(i + 1 < pl.num_programs(0))
    def _():
        gather_tile(i + 1, 1 - slot)

    for k in range(TOP_K):
        pltpu.make_async_copy(ys_ref.at[pl.ds(0, tm)], ybuf.at[slot, k], sem.at[slot]).wait()
    route = route_ref[...]
    lo0, hi0 = _unpack_halves(ybuf[slot, 0])
    lo1, hi1 = _unpack_halves(ybuf[slot, 1])
    g0, g1 = route[:, 2:3], route[:, 3:4]
    moe = jnp.concatenate([lo0 * g0 + lo1 * g1, hi0 * g0 + hi1 * g1], axis=1)
    x2 = x1_ref[...] + gtf_ref[0] * moe
    if final_norm:
        x2 = x2 * lax.rsqrt(jnp.mean(x2 * x2, axis=-1, keepdims=True) + EPS) * nf_ref[...]
    o_ref[...] = x2


def _combine(dest, ys, x1, route, gtf, nf, seq, final_norm):
    t, d = x1.shape
    tm = TM_ROWS
    tps = seq // tm
    grid_spec = pltpu.PrefetchScalarGridSpec(
        num_scalar_prefetch=1,
        grid=(t // tm,),
        in_specs=[pl.BlockSpec(memory_space=pl.ANY),
                  pl.BlockSpec((tm, d), lambda i, dest: (i, 0)),
                  pl.BlockSpec((tm, LANES), lambda i, dest: (i, 0)),
                  pl.BlockSpec((1, 1, d), lambda i, dest: (i // tps, 0, 0)),
                  pl.BlockSpec((1, d), lambda i, dest: (0, 0))],
        out_specs=pl.BlockSpec((tm, d), lambda i, dest: (i, 0)),
        scratch_shapes=[pltpu.VMEM((2, TOP_K, tm, ys.shape[1]), ys.dtype), pltpu.SemaphoreType.DMA((2,))],
    )
    return pl.pallas_call(
        functools.partial(_combine_kernel, final_norm),
        grid_spec=grid_spec,
        out_shape=jax.ShapeDtypeStruct((t, d), F32),
        compiler_params=_params(("arbitrary",)),
        name="combine",
    )(dest, ys, x1, route, gtf, nf)


def _block_tri(n, chunk, lower):
    r = jnp.arange(n)[:, None]
    c = jnp.arange(n)[None, :]
    same = (r // chunk) == (c // chunk)
    tri = (r >= c) if lower else (r <= c)
    return jnp.where(same & tri, 1.0, 0.0).astype(BF16)


def _layer(x2, c, bsz, seq, w_ada, b_ada, norm_mix, w_in, w_conv, a_log, dt_bias, gdn_norm,
           attn_norm, w_out, rel_bias, norm_ffn, w_rg, w_re, w_gate, w_up, w_down):
    t, d = x2.shape
    mod = _adaln(c, w_ada, b_ada)
    sh_a, sc_a, gt_a, sh_f, sc_f, gt_f = [m.reshape(bsz, 1, d) for m in jnp.split(mod, 6, axis=-1)]

    s1 = GDN_CONV_W
    s2 = s1 + GDN_V_W
    s4 = s2 + 2 * GDN_HEADS
    small = jnp.pad(w_in[:, s2:s4], ((0, 0), (0, SMALL_W - 2 * GDN_HEADS)))
    wcat = jnp.concatenate([w_in[:, :s2], small, w_in[:, s4:]], axis=1).astype(BF16)
    qkva, z, ba, *qkvb = _inproj(x2, sc_a, sh_a, norm_mix.reshape(1, d), wcat, w_conv, seq)

    pad4 = lambda v: jnp.pad(v.astype(F32), (GDN_HEADS, LANES - 2 * GDN_HEADS))
    gp = jnp.zeros((8, LANES), F32).at[0].set(pad4(a_log)).at[1].set(pad4(dt_bias))
    og = _gdn(qkva, ba, ba[:, :8].T, gp, _gp_cols(a_log, dt_bias),
              _block_tri(GDN_TOK, CHUNK, True), _block_tri(GDN_TOK, CHUNK, False), seq)

    outs, lses = [], []
    for (window, dil), qkvb_d in zip(DILATED_PAIRS, qkvb):
        o_i, lse_i = _attn_branch(qkvb_d, jnp.asarray(_bucket_table(window, dil)), rel_bias.astype(F32),
                                  bsz, seq, dil)
        outs.append(o_i)
        lses.append(lse_i)

    expand = jnp.where((jnp.arange(LANES)[:, None] == jnp.arange(ATTN_W)[None, :] // ATTN_DH), 1.0, 0.0).astype(BF16)
    lstrict = jnp.where(jnp.arange(TM_POST)[:, None] > jnp.arange(TM_POST)[None, :], 1.0, 0.0).astype(BF16)
    wr = jnp.pad(jnp.concatenate([w_rg, w_re], axis=1).astype(F32), ((0, 0), (0, LANES - N_GROUPS - N_EXPERTS)))
    wrh = wr.astype(BF16)
    wr_split = jnp.concatenate([wrh, (wr - wrh.astype(F32)).astype(BF16)], axis=1)
    x1, h2, route, cnt = _post(og, z, outs, lses, x2, gt_a, sc_f, sh_f,
                               gdn_norm.reshape(1, GDN_DV), attn_norm.reshape(1, ATTN_W),
                               w_out.astype(BF16), norm_ffn.reshape(1, d), wr_split, expand, lstrict, seq)

    blk = EXPERT_BLK
    counts = cnt[0, :N_EXPERTS].astype(jnp.int32)
    padded = (counts + blk - 1) // blk * blk
    pends = jnp.cumsum(padded)
    pstarts = pends - padded
    eids = route[:, 0:TOP_K].astype(jnp.int32)
    ranks = route[:, 4:4 + TOP_K].astype(jnp.int32)
    expert_ids = jnp.arange(N_EXPERTS, dtype=jnp.int32)
    seg_start = jnp.sum(jnp.where(eids[..., None] == expert_ids, pstarts, 0), axis=-1)
    dest = (seg_start + ranks).reshape(t * TOP_K)
    a = t * TOP_K
    p = -(-a // blk) * blk + N_EXPERTS * blk
    nblk = p // blk
    blk_start = jnp.arange(nblk, dtype=jnp.int32) * blk
    blk_e = jnp.minimum(jnp.sum((pends[None, :] <= blk_start[:, None]).astype(jnp.int32), axis=1),
                        N_EXPERTS - 1)
    nused = (pends[-1] // blk).astype(jnp.int32).reshape(1)

    xs = _dispatch(dest, pstarts + counts, padded - counts, nused, h2, p)
    ys = _experts(blk_e, nused, xs, w_gate, w_up, w_down)
    return ys, dest, x1, route, gt_f


def _gp_cols(a_log, dt_bias):
    z = jnp.zeros((8, LANES), F32)
    z = z.at[GDN_HEADS:2 * GDN_HEADS, 0].set(a_log.astype(F32))
    z = z.at[GDN_HEADS:2 * GDN_HEADS, 1].set(dt_bias.astype(F32))
    return z


def kernel(x, c, w_ada, b_ada, norm_mix, w_in, w_conv, a_log, dt_bias, gdn_norm, attn_norm, w_out,
           rel_bias, norm_ffn, w_router_group, w_router_expert, w_gate, w_up, w_down, norm_final):
    bsz, seq, d = x.shape
    depth = w_ada.shape[0]
    x2 = x.reshape(bsz * seq, d)
    for l in range(depth):
        ys, dest, x1, route, gt_f = _layer(
            x2, c, bsz, seq, w_ada[l], b_ada[l], norm_mix[l], w_in[l], w_conv[l], a_log[l], dt_bias[l],
            gdn_norm[l], attn_norm[l], w_out[l], rel_bias, norm_ffn[l], w_router_group[l],
            w_router_expert[l], w_gate[l], w_up[l], w_down[l])
        x2 = _combine(dest, ys, x1, route, gt_f, norm_final.reshape(1, d), seq, l == depth - 1)
    return x2.reshape(bsz, seq, d)
```

```python
import functools
import math

import jax
import jax.numpy as jnp
import numpy as np
from jax import lax
from jax.experimental import pallas as pl
from jax.experimental.pallas import tpu as pltpu

D_MODEL = 1024
GDN_HEADS = 4
GDN_DK = 128
GDN_DV = 128
CONV_K = 4
CHUNK = 64
ATTN_HEADS = 8
ATTN_DH = 64
DILATED_PAIRS = ((128, 1), (512, 4), (2048, 16))
QBLK = 128
NUM_BUCKETS = 32
REL_MAX_DIST = 2048
N_GROUPS = 4
EXPERTS_PER_GROUP = 8
N_EXPERTS = N_GROUPS * EXPERTS_PER_GROUP
TOP_K = 2
D_EXPERT = 256
EPS = 1e-6
NEG_INF = -1e30

GDN_QK_W = GDN_HEADS * GDN_DK
GDN_V_W = GDN_HEADS * GDN_DV
ATTN_W = ATTN_HEADS * ATTN_DH
GDN_CONV_W = 2 * GDN_QK_W + GDN_V_W
LANES = 128
SUBLANES = 8
SMALL_W = LANES
W_COLS = GDN_CONV_W + GDN_V_W + SMALL_W + 3 * ATTN_W

TM_PROJ = 512
CONV_BLK = 128
HEAD_ROWS = 16
TM_POST = 512
TM_ROWS = 512
GDN_TOK = 512
GDN_GROUP = 4
INV_BASE = 8
ATTN_SUB = 8
EXPERT_BLK = 512
ROW_UNROLL = 8
VMEM_LIMIT = 56 * 1024 * 1024

F32 = jnp.float32
BF16 = jnp.bfloat16
U32 = jnp.uint32
HIGHEST = lax.Precision.HIGHEST


def _sigmoid(x):
    return 1.0 / (1.0 + jnp.exp(-x))


def _dot(a, b, precision=None):
    return jnp.dot(a, b, preferred_element_type=F32, precision=precision)


def _dot_nt(a, b, precision=None):
    return lax.dot_general(a, b, (((1,), (1,)), ((), ())), preferred_element_type=F32,
                           precision=precision)


def _dot_tn(a, b, precision=None):
    return lax.dot_general(a, b, (((0,), (0,)), ((), ())), preferred_element_type=F32,
                           precision=precision)


def _params(sem):
    return pltpu.CompilerParams(dimension_semantics=sem, vmem_limit_bytes=VMEM_LIMIT)


def _adaln_kernel(c_ref, w_ref, b_ref, o_ref):
    o_ref[...] = _dot(c_ref[...], w_ref[...], HIGHEST) + b_ref[...]


def _adaln(c, w, b):
    bsz, d = c.shape
    n = w.shape[1]
    tn = 1024
    return pl.pallas_call(
        _adaln_kernel,
        grid=(n // tn,),
        in_specs=[pl.BlockSpec((bsz, d), lambda j: (0, 0)),
                  pl.BlockSpec((d, tn), lambda j: (0, j)),
                  pl.BlockSpec((1, tn), lambda j: (0, j))],
        out_specs=pl.BlockSpec((bsz, tn), lambda j: (0, j)),
        out_shape=jax.ShapeDtypeStruct((bsz, n), F32),
        compiler_params=_params(("arbitrary",)),
        name="adaln",
    )(c, w, b.reshape(1, n))


def _inproj_kernel(tiles_per_seq, x_ref, sc_ref, sh_ref, g_ref, w_ref, wconv_ref, shift_ref,
                   qkva_ref, z_ref, ba_ref, *rest):
    qkvb_refs = rest[:len(DILATED_PAIRS)]
    win_scr, pb_scr, pb_next_scr = rest[len(DILATED_PAIRS):]
    i = pl.program_id(0)
    tm = x_ref.shape[0]
    x = x_ref[...]
    h = x * lax.rsqrt(jnp.mean(x * x, axis=-1, keepdims=True) + EPS) * g_ref[...]
    h = h * (1.0 + sc_ref[0]) + sh_ref[0]
    hb = h.astype(BF16)

    @pl.when(i % tiles_per_seq == 0)
    def _():
        win_scr[0:HEAD_ROWS, :] = jnp.zeros((HEAD_ROWS, GDN_CONV_W), F32)

    pa = _dot(hb, w_ref[:, 0:GDN_CONV_W])
    pa16 = pa.astype(BF16)
    for b in range(tm // CONV_BLK):
        rs = slice(b * CONV_BLK, (b + 1) * CONV_BLK)
        acc = pa[rs] * wconv_ref[CONV_K - 1:CONV_K, :]
        for j in range(CONV_K - 1):
            if b == 0:
                shifted = _dot(shift_ref[j, :, CONV_BLK:], pa16[rs])
            else:
                shifted = _dot(shift_ref[j], pa16[(b - 1) * CONV_BLK:(b + 1) * CONV_BLK])
            acc = acc + shifted * wconv_ref[j:j + 1, :]
        qkva_ref[rs, :] = (acc * _sigmoid(acc)).astype(qkva_ref.dtype)
    win_scr[HEAD_ROWS:2 * HEAD_ROWS, :] = pa[0:HEAD_ROWS]
    acc = pa[0:HEAD_ROWS] * wconv_ref[CONV_K - 1:CONV_K, :]
    for j in range(CONV_K - 1):
        acc = acc + win_scr[pl.ds(HEAD_ROWS - (CONV_K - 1) + j, HEAD_ROWS), :] * wconv_ref[j:j + 1, :]
    qkva_ref[0:HEAD_ROWS, :] = (acc * _sigmoid(acc)).astype(qkva_ref.dtype)
    win_scr[0:HEAD_ROWS, :] = pa[tm - HEAD_ROWS:tm]

    c0 = GDN_CONV_W
    z_ref[...] = _dot(hb, w_ref[:, c0:c0 + GDN_V_W]).astype(z_ref.dtype)
    c0 += GDN_V_W
    ba_ref[...] = _dot(hb, w_ref[:, c0:c0 + SMALL_W])
    c0 += SMALL_W
    pb = _dot(hb, w_ref[:, c0:c0 + 3 * ATTN_W])
    ncol = 3 * ATTN_W // LANES
    for j in range(ncol):
        pb_scr[j] = pb[:, j * LANES:(j + 1) * LANES]
    prev_dil, prev_scr, next_scr = 1, pb_scr, pb_next_scr
    for bi, ((_, dil), ref) in enumerate(zip(DILATED_PAIRS, qkvb_refs)):
        if dil == 1:
            ref[...] = pb.astype(ref.dtype)
            continue
        f = dil // prev_dil
        n = tm // dil
        keep = bi + 1 < len(DILATED_PAIRS)
        for r in range(prev_dil):
            for q in range(f):
                r_new = r + prev_dil * q
                for j in range(ncol):
                    rows = prev_scr[j, pl.ds(r * (tm // prev_dil) + q, n, stride=f), :]
                    c1 = r_new * 3 * ATTN_W + j * LANES
                    ref[:, c1:c1 + LANES] = rows.astype(ref.dtype)
                    if keep:
                        next_scr[j, r_new * n:(r_new + 1) * n, :] = rows
        prev_dil, prev_scr, next_scr = dil, next_scr, prev_scr


def _inproj(x2, sc, sh, g, wcat, wconv, seq):
    t, d = x2.shape
    tm = TM_PROJ
    tps = seq // tm
    mod_spec = pl.BlockSpec((1, 1, d), lambda i: (i // tps, 0, 0))
    return pl.pallas_call(
        functools.partial(_inproj_kernel, tps),
        grid=(t // tm,),
        in_specs=[pl.BlockSpec((tm, d), lambda i: (i, 0)),
                  mod_spec, mod_spec,
                  pl.BlockSpec((1, d), lambda i: (0, 0)),
                  pl.BlockSpec((d, W_COLS), lambda i: (0, 0)),
                  pl.BlockSpec((CONV_K, GDN_CONV_W), lambda i: (0, 0)),
                  pl.BlockSpec((CONV_K - 1, CONV_BLK, 2 * CONV_BLK), lambda i: (0, 0, 0))],
        out_specs=[pl.BlockSpec((tm, GDN_CONV_W), lambda i: (i, 0)),
                   pl.BlockSpec((tm, GDN_V_W), lambda i: (i, 0)),
                   pl.BlockSpec((tm, SMALL_W), lambda i: (i, 0))]
                  + [pl.BlockSpec((tm // dil, dil * 3 * ATTN_W), lambda i: (i, 0)) for _, dil in DILATED_PAIRS],
        out_shape=[jax.ShapeDtypeStruct((t, GDN_CONV_W), BF16),
                   jax.ShapeDtypeStruct((t, GDN_V_W), BF16),
                   jax.ShapeDtypeStruct((t, SMALL_W), F32)]
                  + [jax.ShapeDtypeStruct((t // dil, dil * 3 * ATTN_W), BF16) for _, dil in DILATED_PAIRS],
        scratch_shapes=[pltpu.VMEM((2 * HEAD_ROWS, GDN_CONV_W), F32),
                        pltpu.VMEM((3 * ATTN_W // LANES, tm, LANES), F32),
                        pltpu.VMEM((3 * ATTN_W // LANES, tm, LANES), F32)],
        compiler_params=_params(("arbitrary",)),
        name="inproj",
    )(x2, sc, sh, g, wcat, wconv, _shift_bands())


def _shift_bands():
    t = np.arange(CONV_BLK)[:, None]
    c = np.arange(2 * CONV_BLK)[None, :]
    bands = [(c == CONV_BLK + t - (CONV_K - 1 - j)) for j in range(CONV_K - 1)]
    return jnp.asarray(np.stack(bands).astype(np.float32), dtype=BF16)


def _softplus(x):
    return jnp.maximum(x, 0.0) + jnp.log(1.0 + jnp.exp(-jnp.abs(x)))


def _unit_lower_inverses(n_list):
    c = n_list[0].shape[0]
    row = lax.broadcasted_iota(jnp.int32, (c, c), 0)
    col = lax.broadcasted_iota(jnp.int32, (c, c), 1)
    eye = jnp.where(row == col, 1.0, 0.0)
    same_base = (row // INV_BASE) == (col // INV_BASE)
    n0 = [jnp.where(same_base, n, 0.0) for n in n_list]
    p = [eye - x for x in n0]
    m = [x.astype(BF16) for x in n0]
    for j in range(int(math.log2(INV_BASE)) - 1):
        m = [_dot(x, x).astype(BF16) for x in m]
        p = [x + _dot(x.astype(BF16), y) for x, y in zip(p, m)]
    size = INV_BASE
    while size < c:
        sibling = jnp.logical_and((row // size) % 2 == 1, (col // size) == (row // size) - 1)
        cb = [jnp.where(sibling, n, 0.0).astype(BF16) for n in n_list]
        pb = [x.astype(BF16) for x in p]
        pc = [_dot(x, y).astype(BF16) for x, y in zip(pb, cb)]
        p = [x - _dot(y, z) for x, y, z in zip(p, pc, pb)]
        size *= 2
    return p


def _split3_bf16(x):
    h1 = x.astype(BF16)
    r1 = x - h1.astype(F32)
    h2 = r1.astype(BF16)
    h3 = (r1 - h2.astype(F32)).astype(BF16)
    return h1, h2, h3


def _gdn_kernel(steps_per_seq, qkv_ref, ba_ref, bat_ref, gp_ref, gpt_ref, lblk_ref, ublk_ref,
                o_ref, state_scr):
    i = pl.program_id(0)

    @pl.when(i % steps_per_seq == 0)
    def _():
        state_scr[...] = jnp.zeros_like(state_scr)

    nchunk = GDN_TOK // CHUNK
    ba = ba_ref[...]
    bat = bat_ref[...]
    a_vec = -jnp.exp(gp_ref[0:1, :])
    g_tile = a_vec * _softplus(ba + gp_ref[1:2, :])
    lblk = lblk_ref[...]
    g_cum = sum(_dot(lblk, part) for part in _split3_bf16(g_tile))
    a_col = -jnp.exp(gpt_ref[:, 0:1])
    gt_tile = a_col * _softplus(bat + gpt_ref[:, 1:2])
    ublk = ublk_ref[...]
    gt_cum = sum(_dot(part, ublk) for part in _split3_bf16(gt_tile))
    beta_tile = _sigmoid(ba)

    row = lax.broadcasted_iota(jnp.int32, (CHUNK, CHUNK), 0)
    col = lax.broadcasted_iota(jnp.int32, (CHUNK, CHUNK), 1)
    incl = row >= col
    strict = row > col

    for h0 in range(0, GDN_HEADS, GDN_GROUP):
        heads = range(h0, h0 + GDN_GROUP)
        kb_l, k_l, q_l, rhs_l, qg_l, kg_l, gl_l, inc_l, str_l = ([] for _ in range(9))
        for h in heads:
            q_raw = qkv_ref[:, h * GDN_DK:(h + 1) * GDN_DK].astype(F32)
            k_raw = qkv_ref[:, GDN_QK_W + h * GDN_DK:GDN_QK_W + (h + 1) * GDN_DK].astype(F32)
            v = qkv_ref[:, 2 * GDN_QK_W + h * GDN_DV:2 * GDN_QK_W + (h + 1) * GDN_DV].astype(F32)
            qn = q_raw * lax.rsqrt(jnp.sum(q_raw * q_raw, axis=-1, keepdims=True) + EPS) * (GDN_DK ** -0.5)
            kn = k_raw * lax.rsqrt(jnp.sum(k_raw * k_raw, axis=-1, keepdims=True) + EPS)
            beta = beta_tile[:, h:h + 1]
            gc_all = g_cum[:, GDN_HEADS + h:GDN_HEADS + h + 1]
            eg = jnp.exp(gc_all)
            kb = kn * beta
            rhs = jnp.concatenate([v * beta, kb * eg], axis=1).astype(BF16)
            qg = qn * eg
            kb16, k16, q16 = kb.astype(BF16), kn.astype(BF16), qn.astype(BF16)
            for c in range(nchunk):
                sl = slice(c * CHUNK, (c + 1) * CHUNK)
                g_c = gc_all[sl]
                g_r = gt_cum[GDN_HEADS + h:GDN_HEADS + h + 1, sl]
                dec = jnp.exp(g_c - g_r)
                inc_l.append(jnp.where(incl, dec, 0.0))
                str_l.append(jnp.where(strict, dec, 0.0))
                g_last = g_c[CHUNK - 1:CHUNK, :]
                kg_l.append((kn[sl] * jnp.exp(g_last - g_c)).astype(BF16))
                gl_l.append(jnp.exp(g_last))
                kb_l.append(kb16[sl])
                k_l.append(k16[sl])
                q_l.append(q16[sl])
                rhs_l.append(rhs[sl])
                qg_l.append(qg[sl])
        nprob = len(k_l)
        n_l = [_dot_nt(kb_l[j], k_l[j]) * str_l[j] for j in range(nprob)]
        aqk_l = [(_dot_nt(q_l[j], k_l[j]) * inc_l[j]).astype(BF16) for j in range(nprob)]
        tinv_l = _unit_lower_inverses(n_l)
        sol_l = [_dot(tinv_l[j].astype(BF16), rhs_l[j]).astype(BF16) for j in range(nprob)]
        a2_l = [_dot(aqk_l[j], sol_l[j]) for j in range(nprob)]
        k2_l = [_dot_tn(kg_l[j], sol_l[j]) for j in range(nprob)]
        states = [state_scr[h] for h in heads]
        for c in range(nchunk):
            sl = slice(c * CHUNK, (c + 1) * CHUNK)
            for hi, h in enumerate(heads):
                j = hi * nchunk + c
                sb = states[hi].astype(BF16)
                qeff = (qg_l[j] - a2_l[j][:, GDN_DV:]).astype(BF16)
                o = _dot(qeff, sb) + a2_l[j][:, :GDN_DV]
                states[hi] = gl_l[j] * states[hi] + k2_l[j][:, :GDN_DV] - _dot(k2_l[j][:, GDN_DV:].astype(BF16), sb)
                o_ref[sl, h * GDN_DV:(h + 1) * GDN_DV] = o.astype(o_ref.dtype)
        for hi, h in enumerate(heads):
            state_scr[h] = states[hi]


def _gdn(qkva, ba, bat, gp, gpt, lblk, ublk, seq):
    t = qkva.shape[0]
    tok = GDN_TOK
    sps = seq // tok
    const = lambda i: (0, 0)
    return pl.pallas_call(
        functools.partial(_gdn_kernel, sps),
        grid=(t // tok,),
        in_specs=[pl.BlockSpec((tok, GDN_CONV_W), lambda i: (i, 0)),
                  pl.BlockSpec((tok, SMALL_W), lambda i: (i, 0)),
                  pl.BlockSpec((8, tok), lambda i: (0, i)),
                  pl.BlockSpec((8, LANES), const),
                  pl.BlockSpec((8, LANES), const),
                  pl.BlockSpec((tok, tok), const),
                  pl.BlockSpec((tok, tok), const)],
        out_specs=pl.BlockSpec((tok, GDN_V_W), lambda i: (i, 0)),
        out_shape=jax.ShapeDtypeStruct((t, GDN_V_W), F32),
        scratch_shapes=[pltpu.VMEM((GDN_HEADS, GDN_DK, GDN_DV), F32)],
        compiler_params=_params(("arbitrary",)),
        name="gdn",
    )(qkva, ba, bat, gp, gpt, lblk, ublk)


def _attn_kernel(cur_ref, prev_ref, bucket_ref, rb_ref, o_ref, lse_ref, bias_ref):
    n = pl.program_id(2)

    @pl.when(jnp.logical_and(jnp.logical_and(pl.program_id(0) == 0, pl.program_id(1) == 0), n == 0))
    def _():
        bucket = bucket_ref[...]
        col = lax.broadcasted_iota(jnp.int32, (QBLK, 2 * QBLK), 1)
        for h in range(ATTN_HEADS):
            acc = jnp.full((QBLK, 2 * QBLK), NEG_INF, F32)
            for b in range(NUM_BUCKETS):
                acc = jnp.where(bucket == b, rb_ref[b, h], acc)
            bias_ref[0, h] = acc
            bias_ref[1, h] = jnp.where(col < QBLK, NEG_INF, acc)

    nsub = cur_ref.shape[1] // QBLK
    nres = cur_ref.shape[2] // (3 * ATTN_W)
    first = jnp.where(n == 0, 1, 0)
    w = ATTN_W
    q_all = [cur_ref[0, :, c * 3 * w:c * 3 * w + w] for c in range(nres)]
    k_all = [jnp.concatenate([prev_ref[0, :, c * 3 * w + w:c * 3 * w + 2 * w],
                              cur_ref[0, :, c * 3 * w + w:c * 3 * w + 2 * w]], axis=0) for c in range(nres)]
    v_all = [jnp.concatenate([prev_ref[0, :, c * 3 * w + 2 * w:(c + 1) * 3 * w],
                              cur_ref[0, :, c * 3 * w + 2 * w:(c + 1) * 3 * w]], axis=0) for c in range(nres)]
    lane = lax.broadcasted_iota(jnp.int32, (QBLK, LANES), 1)
    ones = jnp.ones((2 * QBLK, LANES), BF16)
    per_group = LANES // ATTN_DH
    ngrp = ATTN_HEADS // per_group
    scale = ATTN_DH ** -0.5
    mine = [jnp.logical_and(lane >= j * ATTN_DH, lane < (j + 1) * ATTN_DH) for j in range(per_group)]
    probs = [(c, u, h) for c in range(nres) for u in range(nsub) for h in range(ATTN_HEADS)]
    s_l = []
    for c, u, h in probs:
        grp, j = divmod(h, per_group)
        gs = slice(grp * LANES, (grp + 1) * LANES)
        qg = q_all[c][u * QBLK:(u + 1) * QBLK, gs] * scale
        table = first if u == 0 else 0
        s_l.append(_dot_nt(jnp.where(mine[j], qg, jnp.zeros_like(qg)), k_all[c][u * QBLK:(u + 2) * QBLK, gs])
                   + bias_ref[table, h])
    m_l = [jnp.max(s, axis=-1, keepdims=True) for s in s_l]
    p_l = [jnp.exp(s - m).astype(BF16) for s, m in zip(s_l, m_l)]
    v_ext = {(c, u, grp): jnp.concatenate(
                 [v_all[c][u * QBLK:(u + 2) * QBLK, grp * LANES:(grp + 1) * LANES], ones], axis=1)
             for c in range(nres) for u in range(nsub) for grp in range(ngrp)}
    pv_l = [_dot(p, v_ext[(c, u, h // per_group)]) for p, (c, u, h) in zip(p_l, probs)]
    for c in range(nres):
        for u in range(nsub):
            outs = []
            lse_tile = jnp.zeros((QBLK, LANES), F32)
            for grp in range(ngrp):
                o_grp = jnp.zeros((QBLK, LANES), F32)
                for j in range(per_group):
                    h = grp * per_group + j
                    idx = (c * nsub + u) * ATTN_HEADS + h
                    den = pv_l[idx][:, LANES:]
                    o_grp = jnp.where(mine[j], pv_l[idx][:, :LANES] / den, o_grp)
                    lse_tile = jnp.where(lane == h, m_l[idx] + jnp.log(den[:, 0:1]), lse_tile)
                outs.append(o_grp)
            rs = slice(u * QBLK, (u + 1) * QBLK)
            o_ref[0, rs, c * w:(c + 1) * w] = jnp.concatenate(outs, axis=1).astype(o_ref.dtype)
            lse_ref[0, rs, c * LANES:(c + 1) * LANES] = lse_tile


def _attn_branch(qkvb, bucket, rel_bias, bsz, seq, dil):
    ln = seq // dil
    sub = min(ATTN_SUB, ln // QBLK)
    rows = sub * QBLK
    assert seq % dil == 0 and ln % rows == 0, (seq, dil, rows)
    nb = ln // rows
    nres = min(max(ATTN_SUB // sub, 1), dil)
    assert dil % nres == 0, (dil, nres)
    xv = qkvb.reshape(bsz, ln, dil * 3 * ATTN_W)
    w = ATTN_W
    o, lse = pl.pallas_call(
        _attn_kernel,
        grid=(bsz, dil // nres, nb),
        in_specs=[pl.BlockSpec((1, rows, nres * 3 * w), lambda b, r, n: (b, n, r)),
                  pl.BlockSpec((1, QBLK, nres * 3 * w), lambda b, r, n: (b, jnp.maximum(n * sub - 1, 0), r)),
                  pl.BlockSpec((QBLK, 2 * QBLK), lambda b, r, n: (0, 0)),
                  pl.BlockSpec(memory_space=pltpu.SMEM)],
        out_specs=[pl.BlockSpec((1, rows, nres * w), lambda b, r, n: (b, n, r)),
                   pl.BlockSpec((1, rows, nres * LANES), lambda b, r, n: (b, n, r))],
        out_shape=[jax.ShapeDtypeStruct((bsz, ln, dil * w), BF16),
                   jax.ShapeDtypeStruct((bsz, ln, dil * LANES), F32)],
        scratch_shapes=[pltpu.VMEM((2, ATTN_HEADS, QBLK, 2 * QBLK), F32)],
        compiler_params=_params(("arbitrary", "arbitrary", "arbitrary")),
        name=f"attn_d{dil}",
    )(xv, xv, bucket, rel_bias)
    return o.reshape(bsz * ln, dil * w), lse.reshape(bsz * ln, dil * LANES)


def _bucket_table(window, dil):
    steps = window // dil
    qi = np.arange(QBLK)[:, None]
    kj = np.arange(2 * QBLK)[None, :]
    delta = qi + QBLK - kj
    dist = np.maximum(delta, 0) * dil
    max_exact = NUM_BUCKETS // 2
    d_f = np.maximum(dist, 1).astype(np.float32)
    large = max_exact + (np.log(d_f / max_exact) / math.log(REL_MAX_DIST / max_exact)
                         * (NUM_BUCKETS - max_exact)).astype(np.int32)
    bucket = np.where(dist < max_exact, dist, np.minimum(large, NUM_BUCKETS - 1))
    return np.where((delta >= 0) & (delta <= steps), bucket, -1).astype(np.int32)


def _split_bf16(x):
    hi = x.astype(BF16)
    lo = (x - hi.astype(F32)).astype(BF16)
    return hi, lo


def _pack_halves(x):
    w = x.shape[1] // 2
    lo = lax.bitcast_convert_type(x[:, :w].astype(BF16).astype(F32), U32)
    hi = lax.bitcast_convert_type(x[:, w:].astype(BF16).astype(F32), U32)
    return (hi & jnp.uint32(0xFFFF0000)) | (lo >> 16)


def _unpack_halves(u):
    lo = lax.bitcast_convert_type(u << 16, F32)
    hi = lax.bitcast_convert_type(u & jnp.uint32(0xFFFF0000), F32)
    return lo, hi


def _post_kernel(og_ref, z_ref, o1_ref, o2_ref, o3_ref, l1_ref, l2_ref, l3_ref, x_ref,
                 gta_ref, scf_ref, shf_ref, gnorm_ref, anorm_ref, wout_ref, nffn_ref,
                 wrh_ref, exp_ref, ls_ref,
                 x1_ref, h2_ref, route_ref, cnt_ref, carry_scr, *tok_scr):
    i = pl.program_id(0)
    tm = x_ref.shape[0]

    @pl.when(i == 0)
    def _():
        carry_scr[...] = jnp.zeros_like(carry_scr)

    nbr = len(DILATED_PAIRS)
    ob_scr, lse_scr = tok_scr[:nbr], tok_scr[nbr:]
    for (_, dil), o_ref, l_ref, o_s, l_s in zip(DILATED_PAIRS, (o1_ref, o2_ref, o3_ref),
                                                 (l1_ref, l2_ref, l3_ref), ob_scr, lse_scr):
        for r in range(dil):
            rows = pl.ds(r, tm // dil, stride=dil) if dil > 1 else slice(None)
            for j in range(ATTN_W // LANES):
                c1 = r * ATTN_W + j * LANES
                o_s[j, rows, :] = o_ref[:, c1:c1 + LANES].astype(F32)
            l_s[rows, :] = l_ref[:, r * LANES:(r + 1) * LANES]

    heads = []
    for h in range(GDN_HEADS):
        hs = slice(h * GDN_DV, (h + 1) * GDN_DV)
        seg = og_ref[:, hs]
        nrm = seg * lax.rsqrt(jnp.mean(seg * seg, axis=-1, keepdims=True) + EPS) * gnorm_ref[...]
        zz = z_ref[:, hs].astype(F32)
        heads.append((nrm * (zz * _sigmoid(zz))).astype(BF16))
    oa = jnp.concatenate(heads, axis=1)

    l1, l2, l3 = (l_s[...] for l_s in lse_scr)
    m = jnp.maximum(jnp.maximum(l1, l2), l3)
    e1, e2, e3 = jnp.exp(l1 - m), jnp.exp(l2 - m), jnp.exp(l3 - m)
    inv = 1.0 / (e1 + e2 + e3)
    ob = jnp.zeros((tm, ATTN_W), F32)
    for e, o_s in zip((e1, e2, e3), ob_scr):
        hi, lo = _split_bf16(e * inv)
        wexp = _dot(hi, exp_ref[...]) + _dot(lo, exp_ref[...])
        ob = ob + wexp * jnp.concatenate([o_s[j] for j in range(ATTN_W // LANES)], axis=1)
    ob = ob * lax.rsqrt(jnp.mean(ob * ob, axis=-1, keepdims=True) + EPS) * anorm_ref[...]

    mix = _dot(oa, wout_ref[0:GDN_V_W, :]) + _dot(ob.astype(BF16), wout_ref[GDN_V_W:, :])
    x1 = x_ref[...] + gta_ref[0] * mix
    x1_ref[...] = x1
    h2 = x1 * lax.rsqrt(jnp.mean(x1 * x1, axis=-1, keepdims=True) + EPS) * nffn_ref[...]
    h2 = h2 * (1.0 + scf_ref[0]) + shf_ref[0]
    h2_ref[...] = _pack_halves(h2)

    hh, hl = _split_bf16(h2)
    both = _dot(hh, wrh_ref[...])
    logits = both[:, :LANES] + both[:, LANES:] + _dot(hl, wrh_ref[:, :LANES])
    tm = logits.shape[0]
    lane = lax.broadcasted_iota(jnp.int32, (tm, LANES), 1).astype(F32)
    big = float(LANES)
    gmask = lane < N_GROUPS
    glog = jnp.where(gmask, logits, NEG_INF)
    gmax = jnp.max(glog, axis=-1, keepdims=True)
    gidx = jnp.min(jnp.where(jnp.logical_and(gmask, glog == gmax), lane, big), axis=-1, keepdims=True)
    gprob = 1.0 / jnp.sum(jnp.where(gmask, jnp.exp(glog - gmax), 0.0), axis=-1, keepdims=True)
    lo_lane = N_GROUPS + EXPERTS_PER_GROUP * gidx
    emask = jnp.logical_and(lane >= lo_lane, lane < lo_lane + EXPERTS_PER_GROUP)
    elog = jnp.where(emask, logits, NEG_INF)
    m1 = jnp.max(elog, axis=-1, keepdims=True)
    i1 = jnp.min(jnp.where(jnp.logical_and(emask, elog == m1), lane, big), axis=-1, keepdims=True)
    emask2 = jnp.logical_and(emask, lane != i1)
    elog2 = jnp.where(emask2, logits, NEG_INF)
    m2 = jnp.max(elog2, axis=-1, keepdims=True)
    i2 = jnp.min(jnp.where(jnp.logical_and(emask2, elog2 == m2), lane, big), axis=-1, keepdims=True)
    r = jnp.exp(m2 - m1)
    gate1 = gprob / (1.0 + r)
    gate2 = gprob * r / (1.0 + r)
    ex1 = i1 - N_GROUPS
    ex2 = i2 - N_GROUPS

    hit1 = lane == ex1
    hit2 = lane == ex2
    onehot = jnp.where(jnp.logical_or(hit1, hit2), 1.0, 0.0)
    pref = _dot(ls_ref[...], onehot.astype(BF16)) + carry_scr[...]
    rank1 = jnp.sum(jnp.where(hit1, pref, 0.0), axis=-1, keepdims=True)
    rank2 = jnp.sum(jnp.where(hit2, pref, 0.0), axis=-1, keepdims=True)
    carry = carry_scr[...] + jnp.sum(onehot, axis=0, keepdims=True)
    carry_scr[...] = carry
    cnt_ref[...] = jnp.broadcast_to(carry, cnt_ref.shape)

    route = jnp.zeros((tm, LANES), F32)
    for idx, val in enumerate((ex1, ex2, gate1, gate2, rank1, rank2)):
        route = jnp.where(lane == idx, val, route)
    route_ref[...] = route


def _post(og, z, os_, ls_, x2, gta, scf, shf, gnorm, anorm, wout, nffn, wrh, expand, lstrict, seq):
    t, d = x2.shape
    tm = TM_POST
    tps = seq // tm
    tile = lambda w: pl.BlockSpec((tm, w), lambda i: (i, 0))
    const2 = lambda a: pl.BlockSpec(a.shape, lambda i: (0, 0))
    mod_spec = pl.BlockSpec((1, 1, d), lambda i: (i // tps, 0, 0))
    return pl.pallas_call(
        _post_kernel,
        grid=(t // tm,),
        in_specs=[tile(GDN_V_W), tile(GDN_V_W)]
                 + [pl.BlockSpec((tm // dil, dil * ATTN_W), lambda i: (i, 0)) for _, dil in DILATED_PAIRS]
                 + [pl.BlockSpec((tm // dil, dil * LANES), lambda i: (i, 0)) for _, dil in DILATED_PAIRS]
                 + [tile(d),
                  mod_spec, mod_spec, mod_spec,
                  const2(gnorm), const2(anorm), const2(wout), const2(nffn),
                  const2(wrh), const2(expand), const2(lstrict)],
        out_specs=[tile(d), tile(d // 2), tile(LANES), pl.BlockSpec((8, LANES), lambda i: (0, 0))],
        out_shape=[jax.ShapeDtypeStruct((t, d), F32),
                   jax.ShapeDtypeStruct((t, d // 2), U32),
                   jax.ShapeDtypeStruct((t, LANES), F32),
                   jax.ShapeDtypeStruct((8, LANES), F32)],
        scratch_shapes=[pltpu.VMEM((1, LANES), F32)]
                       + [pltpu.VMEM((ATTN_W // LANES, tm, LANES), F32) for _ in DILATED_PAIRS]
                       + [pltpu.VMEM((tm, LANES), F32) for _ in DILATED_PAIRS],
        compiler_params=_params(("arbitrary",)),
        name="post",
    )(og, z, *os_, *ls_, x2, gta, scf, shf, gnorm, anorm, wout, nffn, wrh, expand, lstrict)


def _dispatch_kernel(dest_ref, fill_ref, pad_ref, nused_ref, h2_ref, xs_ref, zero_scr, sem, fill_sem):
    i = pl.program_id(0)
    tm = h2_ref.shape[0]

    @pl.when(i == 0)
    def _():
        zero_scr[...] = jnp.zeros_like(zero_scr)

        def pieces(e, act):
            base, pad = fill_ref[e], pad_ref[e]
            head = pad & (SUBLANES - 1)
            for j in range(SUBLANES - 1):
                @pl.when(j < head)
                def _():
                    act(pltpu.make_async_copy(zero_scr.at[pl.ds(0, 1)], xs_ref.at[pl.ds(base + j, 1)],
                                              fill_sem))
            off = base + head
            for bit in range(SUBLANES.bit_length() - 1, EXPERT_BLK.bit_length() - 1):
                size = 1 << bit

                @pl.when((pad >> bit) & 1 == 1)
                def _():
                    act(pltpu.make_async_copy(zero_scr.at[pl.ds(0, size)],
                                              xs_ref.at[pl.ds(pl.multiple_of(off, SUBLANES), size)], fill_sem))
                off = off + (pad & size)

        def fill(e, carry):
            pieces(e, lambda cp: cp.start())
            return carry

        def drain(e, carry):
            pieces(e, lambda cp: cp.wait())
            return carry

        lax.fori_loop(0, N_EXPERTS, fill, 0)
        lax.fori_loop(0, N_EXPERTS, drain, 0)

        def tail(b, act):
            for half in range(EXPERT_BLK // zero_scr.shape[0]):
                row0 = pl.multiple_of(b * EXPERT_BLK + half * zero_scr.shape[0], SUBLANES)
                act(pltpu.make_async_copy(zero_scr, xs_ref.at[pl.ds(row0, zero_scr.shape[0])], fill_sem))

        def tail_fill(b, carry):
            tail(b, lambda cp: cp.start())
            return carry

        def tail_drain(b, carry):
            tail(b, lambda cp: cp.wait())
            return carry

        nblk = xs_ref.shape[0] // EXPERT_BLK
        lax.fori_loop(nused_ref[0], nblk, tail_fill, 0)
        lax.fori_loop(nused_ref[0], nblk, tail_drain, 0)

    def start(r, carry):
        for k in range(TOP_K):
            d = dest_ref[(i * tm + r) * TOP_K + k]
            pltpu.make_async_copy(h2_ref.at[pl.ds(r, 1)], xs_ref.at[pl.ds(d, 1)], sem).start()
        return carry

    lax.fori_loop(0, tm, start, 0, unroll=ROW_UNROLL)
    for k in range(TOP_K):
        pltpu.make_async_copy(h2_ref, xs_ref.at[pl.ds(0, tm)], sem).wait()


def _dispatch(dest, fill_start, pad_rows, nused, h2, p):
    t, d = h2.shape
    tm = TM_ROWS
    grid_spec = pltpu.PrefetchScalarGridSpec(
        num_scalar_prefetch=4,
        grid=(t // tm,),
        in_specs=[pl.BlockSpec((tm, d), lambda i, *_: (i, 0))],
        out_specs=pl.BlockSpec(memory_space=pl.ANY),
        scratch_shapes=[pltpu.VMEM((EXPERT_BLK // 2, d), h2.dtype), pltpu.SemaphoreType.DMA,
                        pltpu.SemaphoreType.DMA],
    )
    return pl.pallas_call(
        _dispatch_kernel,
        grid_spec=grid_spec,
        out_shape=jax.ShapeDtypeStruct((p, d), h2.dtype),
        compiler_params=_params(("arbitrary",)),
        name="dispatch",
    )(dest, fill_start, pad_rows, nused, h2)


def _expert_kernel(blk_e_ref, nused_ref, xs_ref, wg_ref, wu_ref, wd_ref, ys_ref, wg16, wu16, wd16):
    b = pl.program_id(0)
    used = b < nused_ref[0]

    new_expert = jnp.logical_or(b == 0, blk_e_ref[b] != blk_e_ref[jnp.maximum(b - 1, 0)])

    @pl.when(jnp.logical_and(used, new_expert))
    def _():
        wg16[...] = wg_ref[0].astype(BF16)
        wu16[...] = wu_ref[0].astype(BF16)
        wd16[...] = wd_ref[0].astype(BF16)

    @pl.when(used)
    def _():
        half = xs_ref.shape[1]
        x_lo, x_hi = (v.astype(BF16) for v in _unpack_halves(xs_ref[...]))
        g = _dot(x_lo, wg16[0:half, :]) + _dot(x_hi, wg16[half:, :])
        u = _dot(x_lo, wu16[0:half, :]) + _dot(x_hi, wu16[half:, :])
        hid = (g * _sigmoid(g)) * u
        ys_ref[...] = _pack_halves(_dot(hid.astype(BF16), wd16[...]))

    @pl.when(b >= nused_ref[0])
    def _():
        ys_ref[...] = jnp.zeros_like(ys_ref)


def _experts(blk_e, nused, xs, wg, wu, wd):
    p, dh = xs.shape
    d = wg.shape[1]
    assert d == 2 * dh, (d, dh)
    blk = EXPERT_BLK
    row_map = lambda b, be, nu: (jnp.minimum(b, nu[0] - 1), 0)
    w_map = lambda b, be, nu: (be[b], 0, 0)
    grid_spec = pltpu.PrefetchScalarGridSpec(
        num_scalar_prefetch=2,
        grid=(p // blk,),
        in_specs=[pl.BlockSpec((blk, dh), row_map),
                  pl.BlockSpec((1, d, D_EXPERT), w_map),
                  pl.BlockSpec((1, d, D_EXPERT), w_map),
                  pl.BlockSpec((1, D_EXPERT, d), w_map)],
        out_specs=pl.BlockSpec((blk, dh), lambda b, be, nu: (b, 0)),
        scratch_shapes=[pltpu.VMEM((d, D_EXPERT), BF16), pltpu.VMEM((d, D_EXPERT), BF16),
                        pltpu.VMEM((D_EXPERT, d), BF16)],
    )
    return pl.pallas_call(
        _expert_kernel,
        grid_spec=grid_spec,
        out_shape=jax.ShapeDtypeStruct((p, dh), U32),
        compiler_params=_params(("arbitrary",)),
        name="experts",
    )(blk_e, nused, xs, wg, wu, wd)


def _combine_kernel(final_norm, dest_ref, ys_ref, x1_ref, route_ref, gtf_ref, nf_ref, o_ref, ybuf, sem):
    i = pl.program_id(0)
    tm = x1_ref.shape[0]
    slot = i % 2

    def gather_tile(tile, slot_):
        def start(r, carry):
            for k in range(TOP_K):
                d = dest_ref[(tile * tm + r) * TOP_K + k]
                pltpu.make_async_copy(ys_ref.at[pl.ds(d, 1)], ybuf.at[slot_, k, pl.ds(r, 1)],
                                      sem.at[slot_]).start()
            return carry
        lax.fori_loop(0, tm, start, 0, unroll=ROW_UNROLL)

    @pl.when(i == 0)
    def _():
        gather_tile(0, 0)

    @pl.when(i + 1 < pl.num_programs(0))
    def _():
        gather_tile(i + 1, 1 - slot)

    for k in range(TOP_K):
        pltpu.make_async_copy(ys_ref.at[pl.ds(0, tm)], ybuf.at[slot, k], sem.at[slot]).wait()
    route = route_ref[...]
    lo0, hi0 = _unpack_halves(ybuf[slot, 0])
    lo1, hi1 = _unpack_halves(ybuf[slot, 1])
    g0, g1 = route[:, 2:3], route[:, 3:4]
    moe = jnp.concatenate([lo0 * g0 + lo1 * g1, hi0 * g0 + hi1 * g1], axis=1)
    x2 = x1_ref[...] + gtf_ref[0] * moe
    if final_norm:
        x2 = x2 * lax.rsqrt(jnp.mean(x2 * x2, axis=-1, keepdims=True) + EPS) * nf_ref[...]
    o_ref[...] = x2


def _combine(dest, ys, x1, route, gtf, nf, seq, final_norm):
    t, d = x1.shape
    tm = TM_ROWS
    tps = seq // tm
    grid_spec = pltpu.PrefetchScalarGridSpec(
        num_scalar_prefetch=1,
        grid=(t // tm,),
        in_specs=[pl.BlockSpec(memory_space=pl.ANY),
                  pl.BlockSpec((tm, d), lambda i, dest: (i, 0)),
                  pl.BlockSpec((tm, LANES), lambda i, dest: (i, 0)),
                  pl.BlockSpec((1, 1, d), lambda i, dest: (i // tps, 0, 0)),
                  pl.BlockSpec((1, d), lambda i, dest: (0, 0))],
        out_specs=pl.BlockSpec((tm, d), lambda i, dest: (i, 0)),
        scratch_shapes=[pltpu.VMEM((2, TOP_K, tm, ys.shape[1]), ys.dtype), pltpu.SemaphoreType.DMA((2,))],
    )
    return pl.pallas_call(
        functools.partial(_combine_kernel, final_norm),
        grid_spec=grid_spec,
        out_shape=jax.ShapeDtypeStruct((t, d), F32),
        compiler_params=_params(("arbitrary",)),
        name="combine",
    )(dest, ys, x1, route, gtf, nf)


def _block_tri(n, chunk, lower):
    r = jnp.arange(n)[:, None]
    c = jnp.arange(n)[None, :]
    same = (r // chunk) == (c // chunk)
    tri = (r >= c) if lower else (r <= c)
    return jnp.where(same & tri, 1.0, 0.0).astype(BF16)


def _layer(x2, c, bsz, seq, w_ada, b_ada, norm_mix, w_in, w_conv, a_log, dt_bias, gdn_norm,
           attn_norm, w_out, rel_bias, norm_ffn, w_rg, w_re, w_gate, w_up, w_down):
    t, d = x2.shape
    mod = _adaln(c, w_ada, b_ada)
    sh_a, sc_a, gt_a, sh_f, sc_f, gt_f = [m.reshape(bsz, 1, d) for m in jnp.split(mod, 6, axis=-1)]

    s1 = GDN_CONV_W
    s2 = s1 + GDN_V_W
    s4 = s2 + 2 * GDN_HEADS
    small = jnp.pad(w_in[:, s2:s4], ((0, 0), (0, SMALL_W - 2 * GDN_HEADS)))
    wcat = jnp.concatenate([w_in[:, :s2], small, w_in[:, s4:]], axis=1).astype(BF16)
    qkva, z, ba, *qkvb = _inproj(x2, sc_a, sh_a, norm_mix.reshape(1, d), wcat, w_conv, seq)

    pad4 = lambda v: jnp.pad(v.astype(F32), (GDN_HEADS, LANES - 2 * GDN_HEADS))
    gp = jnp.zeros((8, LANES), F32).at[0].set(pad4(a_log)).at[1].set(pad4(dt_bias))
    og = _gdn(qkva, ba, ba[:, :8].T, gp, _gp_cols(a_log, dt_bias),
              _block_tri(GDN_TOK, CHUNK, True), _block_tri(GDN_TOK, CHUNK, False), seq)

    outs, lses = [], []
    for (window, dil), qkvb_d in zip(DILATED_PAIRS, qkvb):
        o_i, lse_i = _attn_branch(qkvb_d, jnp.asarray(_bucket_table(window, dil)), rel_bias.astype(F32),
                                  bsz, seq, dil)
        outs.append(o_i)
        lses.append(lse_i)

    expand = jnp.where((jnp.arange(LANES)[:, None] == jnp.arange(ATTN_W)[None, :] // ATTN_DH), 1.0, 0.0).astype(BF16)
    lstrict = jnp.where(jnp.arange(TM_POST)[:, None] > jnp.arange(TM_POST)[None, :], 1.0, 0.0).astype(BF16)
    wr = jnp.pad(jnp.concatenate([w_rg, w_re], axis=1).astype(F32), ((0, 0), (0, LANES - N_GROUPS - N_EXPERTS)))
    wrh = wr.astype(BF16)
    wr_split = jnp.concatenate([wrh, (wr - wrh.astype(F32)).astype(BF16)], axis=1)
    x1, h2, route, cnt = _post(og, z, outs, lses, x2, gt_a, sc_f, sh_f,
                               gdn_norm.reshape(1, GDN_DV), attn_norm.reshape(1, ATTN_W),
                               w_out.astype(BF16), norm_ffn.reshape(1, d), wr_split, expand, lstrict, seq)

    blk = EXPERT_BLK
    counts = cnt[0, :N_EXPERTS].astype(jnp.int32)
    padded = (counts + blk - 1) // blk * blk
    pends = jnp.cumsum(padded)
    pstarts = pends - padded
    eids = route[:, 0:TOP_K].astype(jnp.int32)
    ranks = route[:, 4:4 + TOP_K].astype(jnp.int32)
    expert_ids = jnp.arange(N_EXPERTS, dtype=jnp.int32)
    seg_start = jnp.sum(jnp.where(eids[..., None] == expert_ids, pstarts, 0), axis=-1)
    dest = (seg_start + ranks).reshape(t * TOP_K)
    a = t * TOP_K
    p = -(-a // blk) * blk + N_EXPERTS * blk
    nblk = p // blk
    blk_start = jnp.arange(nblk, dtype=jnp.int32) * blk
    blk_e = jnp.minimum(jnp.sum((pends[None, :] <= blk_start[:, None]).astype(jnp.int32), axis=1),
                        N_EXPERTS - 1)
    nused = (pends[-1] // blk).astype(jnp.int32).reshape(1)

    xs = _dispatch(dest, pstarts + counts, padded - counts, nused, h2, p)
    ys = _experts(blk_e, nused, xs, w_gate, w_up, w_down)
    return ys, dest, x1, route, gt_f


def _gp_cols(a_log, dt_bias):
    z = jnp.zeros((8, LANES), F32)
    z = z.at[GDN_HEADS:2 * GDN_HEADS, 0].set(a_log.astype(F32))
    z = z.at[GDN_HEADS:2 * GDN_HEADS, 1].set(dt_bias.astype(F32))
    return z


def kernel(x, c, w_ada, b_ada, norm_mix, w_in, w_conv, a_log, dt_bias, gdn_norm, attn_norm, w_out,
           rel_bias, norm_ffn, w_router_group, w_router_expert, w_gate, w_up, w_down, norm_final):
    bsz, seq, d = x.shape
    depth = w_ada.shape[0]
    x2 = x.reshape(bsz * seq, d)
    for l in range(depth):
        ys, dest, x1, route, gt_f = _layer(
            x2, c, bsz, seq, w_ada[l], b_ada[l], norm_mix[l], w_in[l], w_conv[l], a_log[l], dt_bias[l],
            gdn_norm[l], attn_norm[l], w_out[l], rel_bias, norm_ffn[l], w_router_group[l],
            w_router_expert[l], w_gate[l], w_up[l], w_down[l])
        x2 = _combine(dest, ys, x1, route, gt_f, norm_final.reshape(1, d), seq, l == depth - 1)
    return x2.reshape(bsz, seq, d)
```

```python
import functools
import math

import jax
import jax.numpy as jnp
import numpy as np
from jax import lax
from jax.experimental import pallas as pl
from jax.experimental.pallas import tpu as pltpu

D_MODEL = 1024
GDN_HEADS = 4
GDN_DK = 128
GDN_DV = 128
CONV_K = 4
CHUNK = 64
ATTN_HEADS = 8
ATTN_DH = 64
DILATED_PAIRS = ((128, 1), (512, 4), (2048, 16))
QBLK = 128
NUM_BUCKETS = 32
REL_MAX_DIST = 2048
N_GROUPS = 4
EXPERTS_PER_GROUP = 8
N_EXPERTS = N_GROUPS * EXPERTS_PER_GROUP
TOP_K = 2
D_EXPERT = 256
EPS = 1e-6
NEG_INF = -1e30

GDN_QK_W = GDN_HEADS * GDN_DK
GDN_V_W = GDN_HEADS * GDN_DV
ATTN_W = ATTN_HEADS * ATTN_DH
GDN_CONV_W = 2 * GDN_QK_W + GDN_V_W
LANES = 128
SUBLANES = 8
SMALL_W = LANES
W_COLS = GDN_CONV_W + GDN_V_W + SMALL_W + 3 * ATTN_W

TM_PROJ = 512
CONV_BLK = 128
HEAD_ROWS = 16
TM_POST = 512
TM_ROWS = 512
GDN_TOK = 512
GDN_GROUP = 4
INV_BASE = 8
ATTN_SUB = 8
EXPERT_BLK = 512
ROW_UNROLL = 16
VMEM_LIMIT = 56 * 1024 * 1024

F32 = jnp.float32
BF16 = jnp.bfloat16
U32 = jnp.uint32
HIGHEST = lax.Precision.HIGHEST


def _sigmoid(x):
    return 1.0 / (1.0 + jnp.exp(-x))


def _dot(a, b, precision=None):
    return jnp.dot(a, b, preferred_element_type=F32, precision=precision)


def _dot_nt(a, b, precision=None):
    return lax.dot_general(a, b, (((1,), (1,)), ((), ())), preferred_element_type=F32,
                           precision=precision)


def _dot_tn(a, b, precision=None):
    return lax.dot_general(a, b, (((0,), (0,)), ((), ())), preferred_element_type=F32,
                           precision=precision)


def _params(sem):
    return pltpu.CompilerParams(dimension_semantics=sem, vmem_limit_bytes=VMEM_LIMIT)


def _adaln_kernel(c_ref, w_ref, b_ref, o_ref):
    o_ref[...] = _dot(c_ref[...], w_ref[...], HIGHEST) + b_ref[...]


def _adaln(c, w, b):
    bsz, d = c.shape
    n = w.shape[1]
    tn = 1024
    return pl.pallas_call(
        _adaln_kernel,
        grid=(n // tn,),
        in_specs=[pl.BlockSpec((bsz, d), lambda j: (0, 0)),
                  pl.BlockSpec((d, tn), lambda j: (0, j)),
                  pl.BlockSpec((1, tn), lambda j: (0, j))],
        out_specs=pl.BlockSpec((bsz, tn), lambda j: (0, j)),
        out_shape=jax.ShapeDtypeStruct((bsz, n), F32),
        compiler_params=_params(("arbitrary",)),
        name="adaln",
    )(c, w, b.reshape(1, n))


def _inproj_kernel(tiles_per_seq, x_ref, sc_ref, sh_ref, g_ref, w_ref, wconv_ref, shift_ref,
                   qkva_ref, z_ref, ba_ref, *rest):
    qkvb_refs = rest[:len(DILATED_PAIRS)]
    win_scr, pb_scr, pb_next_scr = rest[len(DILATED_PAIRS):]
    i = pl.program_id(0)
    tm = x_ref.shape[0]
    x = x_ref[...]
    h = x * lax.rsqrt(jnp.mean(x * x, axis=-1, keepdims=True) + EPS) * g_ref[...]
    h = h * (1.0 + sc_ref[0]) + sh_ref[0]
    hb = h.astype(BF16)

    @pl.when(i % tiles_per_seq == 0)
    def _():
        win_scr[0:HEAD_ROWS, :] = jnp.zeros((HEAD_ROWS, GDN_CONV_W), F32)

    pa = _dot(hb, w_ref[:, 0:GDN_CONV_W])
    pa16 = pa.astype(BF16)
    for b in range(tm // CONV_BLK):
        rs = slice(b * CONV_BLK, (b + 1) * CONV_BLK)
        acc = pa[rs] * wconv_ref[CONV_K - 1:CONV_K, :]
        for j in range(CONV_K - 1):
            if b == 0:
                shifted = _dot(shift_ref[j, :, CONV_BLK:], pa16[rs])
            else:
                shifted = _dot(shift_ref[j], pa16[(b - 1) * CONV_BLK:(b + 1) * CONV_BLK])
            acc = acc + shifted * wconv_ref[j:j + 1, :]
        qkva_ref[rs, :] = (acc * _sigmoid(acc)).astype(qkva_ref.dtype)
    win_scr[HEAD_ROWS:2 * HEAD_ROWS, :] = pa[0:HEAD_ROWS]
    acc = pa[0:HEAD_ROWS] * wconv_ref[CONV_K - 1:CONV_K, :]
    for j in range(CONV_K - 1):
        acc = acc + win_scr[pl.ds(HEAD_ROWS - (CONV_K - 1) + j, HEAD_ROWS), :] * wconv_ref[j:j + 1, :]
    qkva_ref[0:HEAD_ROWS, :] = (acc * _sigmoid(acc)).astype(qkva_ref.dtype)
    win_scr[0:HEAD_ROWS, :] = pa[tm - HEAD_ROWS:tm]

    c0 = GDN_CONV_W
    z_ref[...] = _dot(hb, w_ref[:, c0:c0 + GDN_V_W]).astype(z_ref.dtype)
    c0 += GDN_V_W
    ba_ref[...] = _dot(hb, w_ref[:, c0:c0 + SMALL_W])
    c0 += SMALL_W
    pb = _dot(hb, w_ref[:, c0:c0 + 3 * ATTN_W])
    ncol = 3 * ATTN_W // LANES
    for j in range(ncol):
        pb_scr[j] = pb[:, j * LANES:(j + 1) * LANES]
    prev_dil, prev_scr, next_scr = 1, pb_scr, pb_next_scr
    for bi, ((_, dil), ref) in enumerate(zip(DILATED_PAIRS, qkvb_refs)):
        if dil == 1:
            ref[...] = pb.astype(ref.dtype)
            continue
        f = dil // prev_dil
        n = tm // dil
        keep = bi + 1 < len(DILATED_PAIRS)
        for r in range(prev_dil):
            for q in range(f):
                r_new = r + prev_dil * q
                for j in range(ncol):
                    rows = prev_scr[j, pl.ds(r * (tm // prev_dil) + q, n, stride=f), :]
                    c1 = r_new * 3 * ATTN_W + j * LANES
                    ref[:, c1:c1 + LANES] = rows.astype(ref.dtype)
                    if keep:
                        next_scr[j, r_new * n:(r_new + 1) * n, :] = rows
        prev_dil, prev_scr, next_scr = dil, next_scr, prev_scr


def _inproj(x2, sc, sh, g, wcat, wconv, seq):
    t, d = x2.shape
    tm = TM_PROJ
    tps = seq // tm
    mod_spec = pl.BlockSpec((1, 1, d), lambda i: (i // tps, 0, 0))
    return pl.pallas_call(
        functools.partial(_inproj_kernel, tps),
        grid=(t // tm,),
        in_specs=[pl.BlockSpec((tm, d), lambda i: (i, 0)),
                  mod_spec, mod_spec,
                  pl.BlockSpec((1, d), lambda i: (0, 0)),
                  pl.BlockSpec((d, W_COLS), lambda i: (0, 0)),
                  pl.BlockSpec((CONV_K, GDN_CONV_W), lambda i: (0, 0)),
                  pl.BlockSpec((CONV_K - 1, CONV_BLK, 2 * CONV_BLK), lambda i: (0, 0, 0))],
        out_specs=[pl.BlockSpec((tm, GDN_CONV_W), lambda i: (i, 0)),
                   pl.BlockSpec((tm, GDN_V_W), lambda i: (i, 0)),
                   pl.BlockSpec((tm, SMALL_W), lambda i: (i, 0))]
                  + [pl.BlockSpec((tm // dil, dil * 3 * ATTN_W), lambda i: (i, 0)) for _, dil in DILATED_PAIRS],
        out_shape=[jax.ShapeDtypeStruct((t, GDN_CONV_W), BF16),
                   jax.ShapeDtypeStruct((t, GDN_V_W), BF16),
                   jax.ShapeDtypeStruct((t, SMALL_W), F32)]
                  + [jax.ShapeDtypeStruct((t // dil, dil * 3 * ATTN_W), BF16) for _, dil in DILATED_PAIRS],
        scratch_shapes=[pltpu.VMEM((2 * HEAD_ROWS, GDN_CONV_W), F32),
                        pltpu.VMEM((3 * ATTN_W // LANES, tm, LANES), F32),
                        pltpu.VMEM((3 * ATTN_W // LANES, tm, LANES), F32)],
        compiler_params=_params(("arbitrary",)),
        name="inproj",
    )(x2, sc, sh, g, wcat, wconv, _shift_bands())


def _shift_bands():
    t = np.arange(CONV_BLK)[:, None]
    c = np.arange(2 * CONV_BLK)[None, :]
    bands = [(c == CONV_BLK + t - (CONV_K - 1 - j)) for j in range(CONV_K - 1)]
    return jnp.asarray(np.stack(bands).astype(np.float32), dtype=BF16)


def _softplus(x):
    return jnp.maximum(x, 0.0) + jnp.log(1.0 + jnp.exp(-jnp.abs(x)))


def _unit_lower_inverses(n_list):
    c = n_list[0].shape[0]
    row = lax.broadcasted_iota(jnp.int32, (c, c), 0)
    col = lax.broadcasted_iota(jnp.int32, (c, c), 1)
    eye = jnp.where(row == col, 1.0, 0.0)
    same_base = (row // INV_BASE) == (col // INV_BASE)
    n0 = [jnp.where(same_base, n, 0.0) for n in n_list]
    p = [eye - x for x in n0]
    m = [x.astype(BF16) for x in n0]
    for j in range(int(math.log2(INV_BASE)) - 1):
        m = [_dot(x, x).astype(BF16) for x in m]
        p = [x + _dot(x.astype(BF16), y) for x, y in zip(p, m)]
    size = INV_BASE
    while size < c:
        sibling = jnp.logical_and((row // size) % 2 == 1, (col // size) == (row // size) - 1)
        cb = [jnp.where(sibling, n, 0.0).astype(BF16) for n in n_list]
        pb = [x.astype(BF16) for x in p]
        pc = [_dot(x, y).astype(BF16) for x, y in zip(pb, cb)]
        p = [x - _dot(y, z) for x, y, z in zip(p, pc, pb)]
        size *= 2
    return p


def _split3_bf16(x):
    h1 = x.astype(BF16)
    r1 = x - h1.astype(F32)
    h2 = r1.astype(BF16)
    h3 = (r1 - h2.astype(F32)).astype(BF16)
    return h1, h2, h3


def _gdn_kernel(steps_per_seq, qkv_ref, ba_ref, bat_ref, gp_ref, gpt_ref, lblk_ref, ublk_ref,
                o_ref, state_scr):
    i = pl.program_id(0)

    @pl.when(i % steps_per_seq == 0)
    def _():
        state_scr[...] = jnp.zeros_like(state_scr)

    nchunk = GDN_TOK // CHUNK
    ba = ba_ref[...]
    bat = bat_ref[...]
    a_vec = -jnp.exp(gp_ref[0:1, :])
    g_tile = a_vec * _softplus(ba + gp_ref[1:2, :])
    lblk = lblk_ref[...]
    g_cum = sum(_dot(lblk, part) for part in _split3_bf16(g_tile))
    a_col = -jnp.exp(gpt_ref[:, 0:1])
    gt_tile = a_col * _softplus(bat + gpt_ref[:, 1:2])
    ublk = ublk_ref[...]
    gt_cum = sum(_dot(part, ublk) for part in _split3_bf16(gt_tile))
    beta_tile = _sigmoid(ba)

    row = lax.broadcasted_iota(jnp.int32, (CHUNK, CHUNK), 0)
    col = lax.broadcasted_iota(jnp.int32, (CHUNK, CHUNK), 1)
    incl = row >= col
    strict = row > col

    for h0 in range(0, GDN_HEADS, GDN_GROUP):
        heads = range(h0, h0 + GDN_GROUP)
        kb_l, k_l, q_l, rhs_l, qg_l, kg_l, gl_l, inc_l, str_l = ([] for _ in range(9))
        for h in heads:
            q_raw = qkv_ref[:, h * GDN_DK:(h + 1) * GDN_DK].astype(F32)
            k_raw = qkv_ref[:, GDN_QK_W + h * GDN_DK:GDN_QK_W + (h + 1) * GDN_DK].astype(F32)
            v = qkv_ref[:, 2 * GDN_QK_W + h * GDN_DV:2 * GDN_QK_W + (h + 1) * GDN_DV].astype(F32)
            qn = q_raw * lax.rsqrt(jnp.sum(q_raw * q_raw, axis=-1, keepdims=True) + EPS) * (GDN_DK ** -0.5)
            kn = k_raw * lax.rsqrt(jnp.sum(k_raw * k_raw, axis=-1, keepdims=True) + EPS)
            beta = beta_tile[:, h:h + 1]
            gc_all = g_cum[:, GDN_HEADS + h:GDN_HEADS + h + 1]
            eg = jnp.exp(gc_all)
            kb = kn * beta
            rhs = jnp.concatenate([v * beta, kb * eg], axis=1).astype(BF16)
            qg = qn * eg
            kb16, k16, q16 = kb.astype(BF16), kn.astype(BF16), qn.astype(BF16)
            for c in range(nchunk):
                sl = slice(c * CHUNK, (c + 1) * CHUNK)
                g_c = gc_all[sl]
                g_r = gt_cum[GDN_HEADS + h:GDN_HEADS + h + 1, sl]
                dec = jnp.exp(g_c - g_r)
                inc_l.append(jnp.where(incl, dec, 0.0))
                str_l.append(jnp.where(strict, dec, 0.0))
                g_last = g_c[CHUNK - 1:CHUNK, :]
                kg_l.append((kn[sl] * jnp.exp(g_last - g_c)).astype(BF16))
                gl_l.append(jnp.exp(g_last))
                kb_l.append(kb16[sl])
                k_l.append(k16[sl])
                q_l.append(q16[sl])
                rhs_l.append(rhs[sl])
                qg_l.append(qg[sl])
        nprob = len(k_l)
        n_l = [_dot_nt(kb_l[j], k_l[j]) * str_l[j] for j in range(nprob)]
        aqk_l = [(_dot_nt(q_l[j], k_l[j]) * inc_l[j]).astype(BF16) for j in range(nprob)]
        tinv_l = _unit_lower_inverses(n_l)
        sol_l = [_dot(tinv_l[j].astype(BF16), rhs_l[j]).astype(BF16) for j in range(nprob)]
        a2_l = [_dot(aqk_l[j], sol_l[j]) for j in range(nprob)]
        k2_l = [_dot_tn(kg_l[j], sol_l[j]) for j in range(nprob)]
        states = [state_scr[h] for h in heads]
        for c in range(nchunk):
            sl = slice(c * CHUNK, (c + 1) * CHUNK)
            for hi, h in enumerate(heads):
                j = hi * nchunk + c
                sb = states[hi].astype(BF16)
                qeff = (qg_l[j] - a2_l[j][:, GDN_DV:]).astype(BF16)
                o = _dot(qeff, sb) + a2_l[j][:, :GDN_DV]
                states[hi] = gl_l[j] * states[hi] + k2_l[j][:, :GDN_DV] - _dot(k2_l[j][:, GDN_DV:].astype(BF16), sb)
                o_ref[sl, h * GDN_DV:(h + 1) * GDN_DV] = o.astype(o_ref.dtype)
        for hi, h in enumerate(heads):
            state_scr[h] = states[hi]


def _gdn(qkva, ba, bat, gp, gpt, lblk, ublk, seq):
    t = qkva.shape[0]
    tok = GDN_TOK
    sps = seq // tok
    const = lambda i: (0, 0)
    return pl.pallas_call(
        functools.partial(_gdn_kernel, sps),
        grid=(t // tok,),
        in_specs=[pl.BlockSpec((tok, GDN_CONV_W), lambda i: (i, 0)),
                  pl.BlockSpec((tok, SMALL_W), lambda i: (i, 0)),
                  pl.BlockSpec((8, tok), lambda i: (0, i)),
                  pl.BlockSpec((8, LANES), const),
                  pl.BlockSpec((8, LANES), const),
                  pl.BlockSpec((tok, tok), const),
                  pl.BlockSpec((tok, tok), const)],
        out_specs=pl.BlockSpec((tok, GDN_V_W), lambda i: (i, 0)),
        out_shape=jax.ShapeDtypeStruct((t, GDN_V_W), F32),
        scratch_shapes=[pltpu.VMEM((GDN_HEADS, GDN_DK, GDN_DV), F32)],
        compiler_params=_params(("arbitrary",)),
        name="gdn",
    )(qkva, ba, bat, gp, gpt, lblk, ublk)


def _attn_kernel(cur_ref, prev_ref, bucket_ref, rb_ref, o_ref, lse_ref, bias_ref):
    n = pl.program_id(2)

    @pl.when(jnp.logical_and(jnp.logical_and(pl.program_id(0) == 0, pl.program_id(1) == 0), n == 0))
    def _():
        bucket = bucket_ref[...]
        col = lax.broadcasted_iota(jnp.int32, (QBLK, 2 * QBLK), 1)
        for h in range(ATTN_HEADS):
            acc = jnp.full((QBLK, 2 * QBLK), NEG_INF, F32)
            for b in range(NUM_BUCKETS):
                acc = jnp.where(bucket == b, rb_ref[b, h], acc)
            bias_ref[0, h] = acc
            bias_ref[1, h] = jnp.where(col < QBLK, NEG_INF, acc)

    nsub = cur_ref.shape[1] // QBLK
    nres = cur_ref.shape[2] // (3 * ATTN_W)
    first = jnp.where(n == 0, 1, 0)
    w = ATTN_W
    q_all = [cur_ref[0, :, c * 3 * w:c * 3 * w + w] for c in range(nres)]
    k_all = [jnp.concatenate([prev_ref[0, :, c * 3 * w + w:c * 3 * w + 2 * w],
                              cur_ref[0, :, c * 3 * w + w:c * 3 * w + 2 * w]], axis=0) for c in range(nres)]
    v_all = [jnp.concatenate([prev_ref[0, :, c * 3 * w + 2 * w:(c + 1) * 3 * w],
                              cur_ref[0, :, c * 3 * w + 2 * w:(c + 1) * 3 * w]], axis=0) for c in range(nres)]
    lane = lax.broadcasted_iota(jnp.int32, (QBLK, LANES), 1)
    ones = jnp.ones((2 * QBLK, LANES), BF16)
    per_group = LANES // ATTN_DH
    ngrp = ATTN_HEADS // per_group
    scale = ATTN_DH ** -0.5
    mine = [jnp.logical_and(lane >= j * ATTN_DH, lane < (j + 1) * ATTN_DH) for j in range(per_group)]
    probs = [(c, u, h) for c in range(nres) for u in range(nsub) for h in range(ATTN_HEADS)]
    s_l = []
    for c, u, h in probs:
        grp, j = divmod(h, per_group)
        gs = slice(grp * LANES, (grp + 1) * LANES)
        qg = q_all[c][u * QBLK:(u + 1) * QBLK, gs] * scale
        table = first if u == 0 else 0
        s_l.append(_dot_nt(jnp.where(mine[j], qg, jnp.zeros_like(qg)), k_all[c][u * QBLK:(u + 2) * QBLK, gs])
                   + bias_ref[table, h])
    m_l = [jnp.max(s, axis=-1, keepdims=True) for s in s_l]
    p_l = [jnp.exp(s - m).astype(BF16) for s, m in zip(s_l, m_l)]
    v_ext = {(c, u, grp): jnp.concatenate(
                 [v_all[c][u * QBLK:(u + 2) * QBLK, grp * LANES:(grp + 1) * LANES], ones], axis=1)
             for c in range(nres) for u in range(nsub) for grp in range(ngrp)}
    pv_l = [_dot(p, v_ext[(c, u, h // per_group)]) for p, (c, u, h) in zip(p_l, probs)]
    for c in range(nres):
        for u in range(nsub):
            outs = []
            lse_tile = jnp.zeros((QBLK, LANES), F32)
            for grp in range(ngrp):
                o_grp = jnp.zeros((QBLK, LANES), F32)
                for j in range(per_group):
                    h = grp * per_group + j
                    idx = (c * nsub + u) * ATTN_HEADS + h
                    den = pv_l[idx][:, LANES:]
                    o_grp = jnp.where(mine[j], pv_l[idx][:, :LANES] / den, o_grp)
                    lse_tile = jnp.where(lane == h, m_l[idx] + jnp.log(den[:, 0:1]), lse_tile)
                outs.append(o_grp)
            rs = slice(u * QBLK, (u + 1) * QBLK)
            o_ref[0, rs, c * w:(c + 1) * w] = jnp.concatenate(outs, axis=1).astype(o_ref.dtype)
            lse_ref[0, rs, c * LANES:(c + 1) * LANES] = lse_tile


def _attn_branch(qkvb, bucket, rel_bias, bsz, seq, dil):
    ln = seq // dil
    sub = min(ATTN_SUB, ln // QBLK)
    rows = sub * QBLK
    assert seq % dil == 0 and ln % rows == 0, (seq, dil, rows)
    nb = ln // rows
    nres = min(max(ATTN_SUB // sub, 1), dil)
    assert dil % nres == 0, (dil, nres)
    xv = qkvb.reshape(bsz, ln, dil * 3 * ATTN_W)
    w = ATTN_W
    o, lse = pl.pallas_call(
        _attn_kernel,
        grid=(bsz, dil // nres, nb),
        in_specs=[pl.BlockSpec((1, rows, nres * 3 * w), lambda b, r, n: (b, n, r)),
                  pl.BlockSpec((1, QBLK, nres * 3 * w), lambda b, r, n: (b, jnp.maximum(n * sub - 1, 0), r)),
                  pl.BlockSpec((QBLK, 2 * QBLK), lambda b, r, n: (0, 0)),
                  pl.BlockSpec(memory_space=pltpu.SMEM)],
        out_specs=[pl.BlockSpec((1, rows, nres * w), lambda b, r, n: (b, n, r)),
                   pl.BlockSpec((1, rows, nres * LANES), lambda b, r, n: (b, n, r))],
        out_shape=[jax.ShapeDtypeStruct((bsz, ln, dil * w), BF16),
                   jax.ShapeDtypeStruct((bsz, ln, dil * LANES), F32)],
        scratch_shapes=[pltpu.VMEM((2, ATTN_HEADS, QBLK, 2 * QBLK), F32)],
        compiler_params=_params(("arbitrary", "arbitrary", "arbitrary")),
        name=f"attn_d{dil}",
    )(xv, xv, bucket, rel_bias)
    return o.reshape(bsz * ln, dil * w), lse.reshape(bsz * ln, dil * LANES)


def _bucket_table(window, dil):
    steps = window // dil
    qi = np.arange(QBLK)[:, None]
    kj = np.arange(2 * QBLK)[None, :]
    delta = qi + QBLK - kj
    dist = np.maximum(delta, 0) * dil
    max_exact = NUM_BUCKETS // 2
    d_f = np.maximum(dist, 1).astype(np.float32)
    large = max_exact + (np.log(d_f / max_exact) / math.log(REL_MAX_DIST / max_exact)
                         * (NUM_BUCKETS - max_exact)).astype(np.int32)
    bucket = np.where(dist < max_exact, dist, np.minimum(large, NUM_BUCKETS - 1))
    return np.where((delta >= 0) & (delta <= steps), bucket, -1).astype(np.int32)


def _split_bf16(x):
    hi = x.astype(BF16)
    lo = (x - hi.astype(F32)).astype(BF16)
    return hi, lo


def _pack_halves(x):
    w = x.shape[1] // 2
    lo = lax.bitcast_convert_type(x[:, :w].astype(BF16).astype(F32), U32)
    hi = lax.bitcast_convert_type(x[:, w:].astype(BF16).astype(F32), U32)
    return (hi & jnp.uint32(0xFFFF0000)) | (lo >> 16)


def _unpack_halves(u):
    lo = lax.bitcast_convert_type(u << 16, F32)
    hi = lax.bitcast_convert_type(u & jnp.uint32(0xFFFF0000), F32)
    return lo, hi


def _post_kernel(og_ref, z_ref, o1_ref, o2_ref, o3_ref, l1_ref, l2_ref, l3_ref, x_ref,
                 gta_ref, scf_ref, shf_ref, gnorm_ref, anorm_ref, wout_ref, nffn_ref,
                 wrh_ref, exp_ref, ls_ref,
                 x1_ref, h2_ref, route_ref, cnt_ref, carry_scr, *tok_scr):
    i = pl.program_id(0)
    tm = x_ref.shape[0]

    @pl.when(i == 0)
    def _():
        carry_scr[...] = jnp.zeros_like(carry_scr)

    nbr = len(DILATED_PAIRS)
    ob_scr, lse_scr = tok_scr[:nbr], tok_scr[nbr:]
    for (_, dil), o_ref, l_ref, o_s, l_s in zip(DILATED_PAIRS, (o1_ref, o2_ref, o3_ref),
                                                 (l1_ref, l2_ref, l3_ref), ob_scr, lse_scr):
        for r in range(dil):
            rows = pl.ds(r, tm // dil, stride=dil) if dil > 1 else slice(None)
            for j in range(ATTN_W // LANES):
                c1 = r * ATTN_W + j * LANES
                o_s[j, rows, :] = o_ref[:, c1:c1 + LANES].astype(F32)
            l_s[rows, :] = l_ref[:, r * LANES:(r + 1) * LANES]

    heads = []
    for h in range(GDN_HEADS):
        hs = slice(h * GDN_DV, (h + 1) * GDN_DV)
        seg = og_ref[:, hs]
        nrm = seg * lax.rsqrt(jnp.mean(seg * seg, axis=-1, keepdims=True) + EPS) * gnorm_ref[...]
        zz = z_ref[:, hs].astype(F32)
        heads.append((nrm * (zz * _sigmoid(zz))).astype(BF16))
    oa = jnp.concatenate(heads, axis=1)

    l1, l2, l3 = (l_s[...] for l_s in lse_scr)
    m = jnp.maximum(jnp.maximum(l1, l2), l3)
    e1, e2, e3 = jnp.exp(l1 - m), jnp.exp(l2 - m), jnp.exp(l3 - m)
    inv = 1.0 / (e1 + e2 + e3)
    ob = jnp.zeros((tm, ATTN_W), F32)
    for e, o_s in zip((e1, e2, e3), ob_scr):
        hi, lo = _split_bf16(e * inv)
        wexp = _dot(hi, exp_ref[...]) + _dot(lo, exp_ref[...])
        ob = ob + wexp * jnp.concatenate([o_s[j] for j in range(ATTN_W // LANES)], axis=1)
    ob = ob * lax.rsqrt(jnp.mean(ob * ob, axis=-1, keepdims=True) + EPS) * anorm_ref[...]

    mix = _dot(oa, wout_ref[0:GDN_V_W, :]) + _dot(ob.astype(BF16), wout_ref[GDN_V_W:, :])
    x1 = x_ref[...] + gta_ref[0] * mix
    x1_ref[...] = x1
    h2 = x1 * lax.rsqrt(jnp.mean(x1 * x1, axis=-1, keepdims=True) + EPS) * nffn_ref[...]
    h2 = h2 * (1.0 + scf_ref[0]) + shf_ref[0]
    h2_ref[...] = _pack_halves(h2)

    hh, hl = _split_bf16(h2)
    both = _dot(hh, wrh_ref[...])
    logits = both[:, :LANES] + both[:, LANES:] + _dot(hl, wrh_ref[:, :LANES])
    tm = logits.shape[0]
    lane = lax.broadcasted_iota(jnp.int32, (tm, LANES), 1).astype(F32)
    big = float(LANES)
    gmask = lane < N_GROUPS
    glog = jnp.where(gmask, logits, NEG_INF)
    gmax = jnp.max(glog, axis=-1, keepdims=True)
    gidx = jnp.min(jnp.where(jnp.logical_and(gmask, glog == gmax), lane, big), axis=-1, keepdims=True)
    gprob = 1.0 / jnp.sum(jnp.where(gmask, jnp.exp(glog - gmax), 0.0), axis=-1, keepdims=True)
    lo_lane = N_GROUPS + EXPERTS_PER_GROUP * gidx
    emask = jnp.logical_and(lane >= lo_lane, lane < lo_lane + EXPERTS_PER_GROUP)
    elog = jnp.where(emask, logits, NEG_INF)
    m1 = jnp.max(elog, axis=-1, keepdims=True)
    i1 = jnp.min(jnp.where(jnp.logical_and(emask, elog == m1), lane, big), axis=-1, keepdims=True)
    emask2 = jnp.logical_and(emask, lane != i1)
    elog2 = jnp.where(emask2, logits, NEG_INF)
    m2 = jnp.max(elog2, axis=-1, keepdims=True)
    i2 = jnp.min(jnp.where(jnp.logical_and(emask2, elog2 == m2), lane, big), axis=-1, keepdims=True)
    r = jnp.exp(m2 - m1)
    gate1 = gprob / (1.0 + r)
    gate2 = gprob * r / (1.0 + r)
    ex1 = i1 - N_GROUPS
    ex2 = i2 - N_GROUPS

    hit1 = lane == ex1
    hit2 = lane == ex2
    onehot = jnp.where(jnp.logical_or(hit1, hit2), 1.0, 0.0)
    pref = _dot(ls_ref[...], onehot.astype(BF16)) + carry_scr[...]
    rank1 = jnp.sum(jnp.where(hit1, pref, 0.0), axis=-1, keepdims=True)
    rank2 = jnp.sum(jnp.where(hit2, pref, 0.0), axis=-1, keepdims=True)
    carry = carry_scr[...] + jnp.sum(onehot, axis=0, keepdims=True)
    carry_scr[...] = carry
    cnt_ref[...] = jnp.broadcast_to(carry, cnt_ref.shape)

    route = jnp.zeros((tm, LANES), F32)
    for idx, val in enumerate((ex1, ex2, gate1, gate2, rank1, rank2)):
        route = jnp.where(lane == idx, val, route)
    route_ref[...] = route


def _post(og, z, os_, ls_, x2, gta, scf, shf, gnorm, anorm, wout, nffn, wrh, expand, lstrict, seq):
    t, d = x2.shape
    tm = TM_POST
    tps = seq // tm
    tile = lambda w: pl.BlockSpec((tm, w), lambda i: (i, 0))
    const2 = lambda a: pl.BlockSpec(a.shape, lambda i: (0, 0))
    mod_spec = pl.BlockSpec((1, 1, d), lambda i: (i // tps, 0, 0))
    return pl.pallas_call(
        _post_kernel,
        grid=(t // tm,),
        in_specs=[tile(GDN_V_W), tile(GDN_V_W)]
                 + [pl.BlockSpec((tm // dil, dil * ATTN_W), lambda i: (i, 0)) for _, dil in DILATED_PAIRS]
                 + [pl.BlockSpec((tm // dil, dil * LANES), lambda i: (i, 0)) for _, dil in DILATED_PAIRS]
                 + [tile(d),
                  mod_spec, mod_spec, mod_spec,
                  const2(gnorm), const2(anorm), const2(wout), const2(nffn),
                  const2(wrh), const2(expand), const2(lstrict)],
        out_specs=[tile(d), tile(d // 2), tile(LANES), pl.BlockSpec((8, LANES), lambda i: (0, 0))],
        out_shape=[jax.ShapeDtypeStruct((t, d), F32),
                   jax.ShapeDtypeStruct((t, d // 2), U32),
                   jax.ShapeDtypeStruct((t, LANES), F32),
                   jax.ShapeDtypeStruct((8, LANES), F32)],
        scratch_shapes=[pltpu.VMEM((1, LANES), F32)]
                       + [pltpu.VMEM((ATTN_W // LANES, tm, LANES), F32) for _ in DILATED_PAIRS]
                       + [pltpu.VMEM((tm, LANES), F32) for _ in DILATED_PAIRS],
        compiler_params=_params(("arbitrary",)),
        name="post",
    )(og, z, *os_, *ls_, x2, gta, scf, shf, gnorm, anorm, wout, nffn, wrh, expand, lstrict)


def _dispatch_kernel(dest_ref, fill_ref, pad_ref, nused_ref, h2_ref, xs_ref, zero_scr, sem, fill_sem):
    i = pl.program_id(0)
    tm = h2_ref.shape[0]

    @pl.when(i == 0)
    def _():
        zero_scr[...] = jnp.zeros_like(zero_scr)

        def pieces(e, act):
            base, pad = fill_ref[e], pad_ref[e]
            head = pad & (SUBLANES - 1)
            for j in range(SUBLANES - 1):
                @pl.when(j < head)
                def _():
                    act(pltpu.make_async_copy(zero_scr.at[pl.ds(0, 1)], xs_ref.at[pl.ds(base + j, 1)],
                                              fill_sem))
            off = base + head
            for bit in range(SUBLANES.bit_length() - 1, EXPERT_BLK.bit_length() - 1):
                size = 1 << bit

                @pl.when((pad >> bit) & 1 == 1)
                def _():
                    act(pltpu.make_async_copy(zero_scr.at[pl.ds(0, size)],
                                              xs_ref.at[pl.ds(pl.multiple_of(off, SUBLANES), size)], fill_sem))
                off = off + (pad & size)

        def fill(e, carry):
            pieces(e, lambda cp: cp.start())
            return carry

        def drain(e, carry):
            pieces(e, lambda cp: cp.wait())
            return carry

        lax.fori_loop(0, N_EXPERTS, fill, 0)
        lax.fori_loop(0, N_EXPERTS, drain, 0)

        def tail(b, act):
            for half in range(EXPERT_BLK // zero_scr.shape[0]):
                row0 = pl.multiple_of(b * EXPERT_BLK + half * zero_scr.shape[0], SUBLANES)
                act(pltpu.make_async_copy(zero_scr, xs_ref.at[pl.ds(row0, zero_scr.shape[0])], fill_sem))

        def tail_fill(b, carry):
            tail(b, lambda cp: cp.start())
            return carry

        def tail_drain(b, carry):
            tail(b, lambda cp: cp.wait())
            return carry

        nblk = xs_ref.shape[0] // EXPERT_BLK
        lax.fori_loop(nused_ref[0], nblk, tail_fill, 0)
        lax.fori_loop(nused_ref[0], nblk, tail_drain, 0)

    def start(r, carry):
        for k in range(TOP_K):
            d = dest_ref[(i * tm + r) * TOP_K + k]
            pltpu.make_async_copy(h2_ref.at[pl.ds(r, 1)], xs_ref.at[pl.ds(d, 1)], sem).start()
        return carry

    lax.fori_loop(0, tm, start, 0, unroll=ROW_UNROLL)
    for k in range(TOP_K):
        pltpu.make_async_copy(h2_ref, xs_ref.at[pl.ds(0, tm)], sem).wait()


def _dispatch(dest, fill_start, pad_rows, nused, h2, p):
    t, d = h2.shape
    tm = TM_ROWS
    grid_spec = pltpu.PrefetchScalarGridSpec(
        num_scalar_prefetch=4,
        grid=(t // tm,),
        in_specs=[pl.BlockSpec((tm, d), lambda i, *_: (i, 0))],
        out_specs=pl.BlockSpec(memory_space=pl.ANY),
        scratch_shapes=[pltpu.VMEM((EXPERT_BLK // 2, d), h2.dtype), pltpu.SemaphoreType.DMA,
                        pltpu.SemaphoreType.DMA],
    )
    return pl.pallas_call(
        _dispatch_kernel,
        grid_spec=grid_spec,
        out_shape=jax.ShapeDtypeStruct((p, d), h2.dtype),
        compiler_params=_params(("arbitrary",)),
        name="dispatch",
    )(dest, fill_start, pad_rows, nused, h2)


def _expert_kernel(blk_e_ref, nused_ref, xs_ref, wg_ref, wu_ref, wd_ref, ys_ref, wg16, wu16, wd16):
    b = pl.program_id(0)
    used = b < nused_ref[0]

    new_expert = jnp.logical_or(b == 0, blk_e_ref[b] != blk_e_ref[jnp.maximum(b - 1, 0)])

    @pl.when(jnp.logical_and(used, new_expert))
    def _():
        wg16[...] = wg_ref[0].astype(BF16)
        wu16[...] = wu_ref[0].astype(BF16)
        wd16[...] = wd_ref[0].astype(BF16)

    @pl.when(used)
    def _():
        half = xs_ref.shape[1]
        x_lo, x_hi = (v.astype(BF16) for v in _unpack_halves(xs_ref[...]))
        g = _dot(x_lo, wg16[0:half, :]) + _dot(x_hi, wg16[half:, :])
        u = _dot(x_lo, wu16[0:half, :]) + _dot(x_hi, wu16[half:, :])
        hid = (g * _sigmoid(g)) * u
        ys_ref[...] = _pack_halves(_dot(hid.astype(BF16), wd16[...]))

    @pl.when(b >= nused_ref[0])
    def _():
        ys_ref[...] = jnp.zeros_like(ys_ref)


def _experts(blk_e, nused, xs, wg, wu, wd):
    p, dh = xs.shape
    d = wg.shape[1]
    assert d == 2 * dh, (d, dh)
    blk = EXPERT_BLK
    row_map = lambda b, be, nu: (jnp.minimum(b, nu[0] - 1), 0)
    w_map = lambda b, be, nu: (be[b], 0, 0)
    grid_spec = pltpu.PrefetchScalarGridSpec(
        num_scalar_prefetch=2,
        grid=(p // blk,),
        in_specs=[pl.BlockSpec((blk, dh), row_map),
                  pl.BlockSpec((1, d, D_EXPERT), w_map),
                  pl.BlockSpec((1, d, D_EXPERT), w_map),
                  pl.BlockSpec((1, D_EXPERT, d), w_map)],
        out_specs=pl.BlockSpec((blk, dh), lambda b, be, nu: (b, 0)),
        scratch_shapes=[pltpu.VMEM((d, D_EXPERT), BF16), pltpu.VMEM((d, D_EXPERT), BF16),
                        pltpu.VMEM((D_EXPERT, d), BF16)],
    )
    return pl.pallas_call(
        _expert_kernel,
        grid_spec=grid_spec,
        out_shape=jax.ShapeDtypeStruct((p, dh), U32),
        compiler_params=_params(("arbitrary",)),
        name="experts",
    )(blk_e, nused, xs, wg, wu, wd)


def _combine_kernel(final_norm, dest_ref, ys_ref, x1_ref, route_ref, gtf_ref, nf_ref, o_ref, ybuf, sem):
    i = pl.program_id(0)
    tm = x1_ref.shape[0]
    slot = i % 2

    def gather_tile(tile, slot_):
        def start(r, carry):
            for k in range(TOP_K):
                d = dest_ref[(tile * tm + r) * TOP_K + k]
                pltpu.make_async_copy(ys_ref.at[pl.ds(d, 1)], ybuf.at[slot_, k, pl.ds(r, 1)],
                                      sem.at[slot_]).start()
            return carry
        lax.fori_loop(0, tm, start, 0, unroll=ROW_UNROLL)

    @pl.when(i == 0)
    def _():
        gather_tile(0, 0)

    @pl.when(i + 1 < pl.num_programs(0))
    def _():
        gather_tile(i + 1, 1 - slot)

    for k in range(TOP_K):
        pltpu.make_async_copy(ys_ref.at[pl.ds(0, tm)], ybuf.at[slot, k], sem.at[slot]).wait()
    route = route_ref[...]
    lo0, hi0 = _unpack_halves(ybuf[slot, 0])
    lo1, hi1 = _unpack_halves(ybuf[slot, 1])
    g0, g1 = route[:, 2:3], route[:, 3:4]
    moe = jnp.concatenate([lo0 * g0 + lo1 * g1, hi0 * g0 + hi1 * g1], axis=1)
    x2 = x1_ref[...] + gtf_ref[0] * moe
    if final_norm:
        x2 = x2 * lax.rsqrt(jnp.mean(x2 * x2, axis=-1, keepdims=True) + EPS) * nf_ref[...]
    o_ref[...] = x2


def _combine(dest, ys, x1, route, gtf, nf, seq, final_norm):
    t, d = x1.shape
    tm = TM_ROWS
    tps = seq // tm
    grid_spec = pltpu.PrefetchScalarGridSpec(
        num_scalar_prefetch=1,
        grid=(t // tm,),
        in_specs=[pl.BlockSpec(memory_space=pl.ANY),
                  pl.BlockSpec((tm, d), lambda i, dest: (i, 0)),
                  pl.BlockSpec((tm, LANES), lambda i, dest: (i, 0)),
                  pl.BlockSpec((1, 1, d), lambda i, dest: (i // tps, 0, 0)),
                  pl.BlockSpec((1, d), lambda i, dest: (0, 0))],
        out_specs=pl.BlockSpec((tm, d), lambda i, dest: (i, 0)),
        scratch_shapes=[pltpu.VMEM((2, TOP_K, tm, ys.shape[1]), ys.dtype), pltpu.SemaphoreType.DMA((2,))],
    )
    return pl.pallas_call(
        functools.partial(_combine_kernel, final_norm),
        grid_spec=grid_spec,
        out_shape=jax.ShapeDtypeStruct((t, d), F32),
        compiler_params=_params(("arbitrary",)),
        name="combine",
    )(dest, ys, x1, route, gtf, nf)


def _block_tri(n, chunk, lower):
    r = jnp.arange(n)[:, None]
    c = jnp.arange(n)[None, :]
    same = (r // chunk) == (c // chunk)
    tri = (r >= c) if lower else (r <= c)
    return jnp.where(same & tri, 1.0, 0.0).astype(BF16)


def _layer(x2, c, bsz, seq, w_ada, b_ada, norm_mix, w_in, w_conv, a_log, dt_bias, gdn_norm,
           attn_norm, w_out, rel_bias, norm_ffn, w_rg, w_re, w_gate, w_up, w_down):
    t, d = x2.shape
    mod = _adaln(c, w_ada, b_ada)
    sh_a, sc_a, gt_a, sh_f, sc_f, gt_f = [m.reshape(bsz, 1, d) for m in jnp.split(mod, 6, axis=-1)]

    s1 = GDN_CONV_W
    s2 = s1 + GDN_V_W
    s4 = s2 + 2 * GDN_HEADS
    small = jnp.pad(w_in[:, s2:s4], ((0, 0), (0, SMALL_W - 2 * GDN_HEADS)))
    wcat = jnp.concatenate([w_in[:, :s2], small, w_in[:, s4:]], axis=1).astype(BF16)
    qkva, z, ba, *qkvb = _inproj(x2, sc_a, sh_a, norm_mix.reshape(1, d), wcat, w_conv, seq)

    pad4 = lambda v: jnp.pad(v.astype(F32), (GDN_HEADS, LANES - 2 * GDN_HEADS))
    gp = jnp.zeros((8, LANES), F32).at[0].set(pad4(a_log)).at[1].set(pad4(dt_bias))
    og = _gdn(qkva, ba, ba[:, :8].T, gp, _gp_cols(a_log, dt_bias),
              _block_tri(GDN_TOK, CHUNK, True), _block_tri(GDN_TOK, CHUNK, False), seq)

    outs, lses = [], []
    for (window, dil), qkvb_d in zip(DILATED_PAIRS, qkvb):
        o_i, lse_i = _attn_branch(qkvb_d, jnp.asarray(_bucket_table(window, dil)), rel_bias.astype(F32),
                                  bsz, seq, dil)
        outs.append(o_i)
        lses.append(lse_i)

    expand = jnp.where((jnp.arange(LANES)[:, None] == jnp.arange(ATTN_W)[None, :] // ATTN_DH), 1.0, 0.0).astype(BF16)
    lstrict = jnp.where(jnp.arange(TM_POST)[:, None] > jnp.arange(TM_POST)[None, :], 1.0, 0.0).astype(BF16)
    wr = jnp.pad(jnp.concatenate([w_rg, w_re], axis=1).astype(F32), ((0, 0), (0, LANES - N_GROUPS - N_EXPERTS)))
    wrh = wr.astype(BF16)
    wr_split = jnp.concatenate([wrh, (wr - wrh.astype(F32)).astype(BF16)], axis=1)
    x1, h2, route, cnt = _post(og, z, outs, lses, x2, gt_a, sc_f, sh_f,
                               gdn_norm.reshape(1, GDN_DV), attn_norm.reshape(1, ATTN_W),
                               w_out.astype(BF16), norm_ffn.reshape(1, d), wr_split, expand, lstrict, seq)

    blk = EXPERT_BLK
    counts = cnt[0, :N_EXPERTS].astype(jnp.int32)
    padded = (counts + blk - 1) // blk * blk
    pends = jnp.cumsum(padded)
    pstarts = pends - padded
    eids = route[:, 0:TOP_K].astype(jnp.int32)
    ranks = route[:, 4:4 + TOP_K].astype(jnp.int32)
    expert_ids = jnp.arange(N_EXPERTS, dtype=jnp.int32)
    seg_start = jnp.sum(jnp.where(eids[..., None] == expert_ids, pstarts, 0), axis=-1)
    dest = (seg_start + ranks).reshape(t * TOP_K)
    a = t * TOP_K
    p = -(-a // blk) * blk + N_EXPERTS * blk
    nblk = p // blk
    blk_start = jnp.arange(nblk, dtype=jnp.int32) * blk
    blk_e = jnp.minimum(jnp.sum((pends[None, :] <= blk_start[:, None]).astype(jnp.int32), axis=1),
                        N_EXPERTS - 1)
    nused = (pends[-1] // blk).astype(jnp.int32).reshape(1)

    xs = _dispatch(dest, pstarts + counts, padded - counts, nused, h2, p)
    ys = _experts(blk_e, nused, xs, w_gate, w_up, w_down)
    return ys, dest, x1, route, gt_f


def _gp_cols(a_log, dt_bias):
    z = jnp.zeros((8, LANES), F32)
    z = z.at[GDN_HEADS:2 * GDN_HEADS, 0].set(a_log.astype(F32))
    z = z.at[GDN_HEADS:2 * GDN_HEADS, 1].set(dt_bias.astype(F32))
    return z


def kernel(x, c, w_ada, b_ada, norm_mix, w_in, w_conv, a_log, dt_bias, gdn_norm, attn_norm, w_out,
           rel_bias, norm_ffn, w_router_group, w_router_expert, w_gate, w_up, w_down, norm_final):
    bsz, seq, d = x.shape
    depth = w_ada.shape[0]
    x2 = x.reshape(bsz * seq, d)
    for l in range(depth):
        ys, dest, x1, route, gt_f = _layer(
            x2, c, bsz, seq, w_ada[l], b_ada[l], norm_mix[l], w_in[l], w_conv[l], a_log[l], dt_bias[l],
            gdn_norm[l], attn_norm[l], w_out[l], rel_bias, norm_ffn[l], w_router_group[l],
            w_router_expert[l], w_gate[l], w_up[l], w_down[l])
        x2 = _combine(dest, ys, x1, route, gt_f, norm_final.reshape(1, d), seq, l == depth - 1)
    return x2.reshape(bsz, seq, d)
```

```python
import functools
import math

import jax
import jax.numpy as jnp
import numpy as np
from jax import lax
from jax.experimental import pallas as pl
from jax.experimental.pallas import tpu as pltpu

D_MODEL = 1024
GDN_HEADS = 4
GDN_DK = 128
GDN_DV = 128
CONV_K = 4
CHUNK = 64
ATTN_HEADS = 8
ATTN_DH = 64
DILATED_PAIRS = ((128, 1), (512, 4), (2048, 16))
QBLK = 128
NUM_BUCKETS = 32
REL_MAX_DIST = 2048
N_GROUPS = 4
EXPERTS_PER_GROUP = 8
N_EXPERTS = N_GROUPS * EXPERTS_PER_GROUP
TOP_K = 2
D_EXPERT = 256
EPS = 1e-6
NEG_INF = -1e30

GDN_QK_W = GDN_HEADS * GDN_DK
GDN_V_W = GDN_HEADS * GDN_DV
ATTN_W = ATTN_HEADS * ATTN_DH
GDN_CONV_W = 2 * GDN_QK_W + GDN_V_W
LANES = 128
SUBLANES = 8
SMALL_W = LANES
W_COLS = GDN_CONV_W + GDN_V_W + SMALL_W + 3 * ATTN_W

TM_PROJ = 512
CONV_BLK = 128
HEAD_ROWS = 16
TM_POST = 512
TM_ROWS = 512
GDN_TOK = 512
GDN_GROUP = 4
INV_BASE = 8
ATTN_SUB = 8
EXPERT_BLK = 512
ROW_UNROLL = 32
VMEM_LIMIT = 56 * 1024 * 1024

F32 = jnp.float32
BF16 = jnp.bfloat16
U32 = jnp.uint32
HIGHEST = lax.Precision.HIGHEST


def _sigmoid(x):
    return 1.0 / (1.0 + jnp.exp(-x))


def _dot(a, b, precision=None):
    return jnp.dot(a, b, preferred_element_type=F32, precision=precision)


def _dot_nt(a, b, precision=None):
    return lax.dot_general(a, b, (((1,), (1,)), ((), ())), preferred_element_type=F32,
                           precision=precision)


def _dot_tn(a, b, precision=None):
    return lax.dot_general(a, b, (((0,), (0,)), ((), ())), preferred_element_type=F32,
                           precision=precision)


def _params(sem):
    return pltpu.CompilerParams(dimension_semantics=sem, vmem_limit_bytes=VMEM_LIMIT)


def _adaln_kernel(c_ref, w_ref, b_ref, o_ref):
    o_ref[...] = _dot(c_ref[...], w_ref[...], HIGHEST) + b_ref[...]


def _adaln(c, w, b):
    bsz, d = c.shape
    n = w.shape[1]
    tn = 1024
    return pl.pallas_call(
        _adaln_kernel,
        grid=(n // tn,),
        in_specs=[pl.BlockSpec((bsz, d), lambda j: (0, 0)),
                  pl.BlockSpec((d, tn), lambda j: (0, j)),
                  pl.BlockSpec((1, tn), lambda j: (0, j))],
        out_specs=pl.BlockSpec((bsz, tn), lambda j: (0, j)),
        out_shape=jax.ShapeDtypeStruct((bsz, n), F32),
        compiler_params=_params(("arbitrary",)),
        name="adaln",
    )(c, w, b.reshape(1, n))


def _inproj_kernel(tiles_per_seq, x_ref, sc_ref, sh_ref, g_ref, w_ref, wconv_ref, shift_ref,
                   qkva_ref, z_ref, ba_ref, *rest):
    qkvb_refs = rest[:len(DILATED_PAIRS)]
    win_scr, pb_scr, pb_next_scr = rest[len(DILATED_PAIRS):]
    i = pl.program_id(0)
    tm = x_ref.shape[0]
    x = x_ref[...]
    h = x * lax.rsqrt(jnp.mean(x * x, axis=-1, keepdims=True) + EPS) * g_ref[...]
    h = h * (1.0 + sc_ref[0]) + sh_ref[0]
    hb = h.astype(BF16)

    @pl.when(i % tiles_per_seq == 0)
    def _():
        win_scr[0:HEAD_ROWS, :] = jnp.zeros((HEAD_ROWS, GDN_CONV_W), F32)

    pa = _dot(hb, w_ref[:, 0:GDN_CONV_W])
    pa16 = pa.astype(BF16)
    for b in range(tm // CONV_BLK):
        rs = slice(b * CONV_BLK, (b + 1) * CONV_BLK)
        acc = pa[rs] * wconv_ref[CONV_K - 1:CONV_K, :]
        for j in range(CONV_K - 1):
            if b == 0:
                shifted = _dot(shift_ref[j, :, CONV_BLK:], pa16[rs])
            else:
                shifted = _dot(shift_ref[j], pa16[(b - 1) * CONV_BLK:(b + 1) * CONV_BLK])
            acc = acc + shifted * wconv_ref[j:j + 1, :]
        qkva_ref[rs, :] = (acc * _sigmoid(acc)).astype(qkva_ref.dtype)
    win_scr[HEAD_ROWS:2 * HEAD_ROWS, :] = pa[0:HEAD_ROWS]
    acc = pa[0:HEAD_ROWS] * wconv_ref[CONV_K - 1:CONV_K, :]
    for j in range(CONV_K - 1):
        acc = acc + win_scr[pl.ds(HEAD_ROWS - (CONV_K - 1) + j, HEAD_ROWS), :] * wconv_ref[j:j + 1, :]
    qkva_ref[0:HEAD_ROWS, :] = (acc * _sigmoid(acc)).astype(qkva_ref.dtype)
    win_scr[0:HEAD_ROWS, :] = pa[tm - HEAD_ROWS:tm]

    c0 = GDN_CONV_W
    z_ref[...] = _dot(hb, w_ref[:, c0:c0 + GDN_V_W]).astype(z_ref.dtype)
    c0 += GDN_V_W
    ba_ref[...] = _dot(hb, w_ref[:, c0:c0 + SMALL_W])
    c0 += SMALL_W
    pb = _dot(hb, w_ref[:, c0:c0 + 3 * ATTN_W])
    ncol = 3 * ATTN_W // LANES
    for j in range(ncol):
        pb_scr[j] = pb[:, j * LANES:(j + 1) * LANES]
    prev_dil, prev_scr, next_scr = 1, pb_scr, pb_next_scr
    for bi, ((_, dil), ref) in enumerate(zip(DILATED_PAIRS, qkvb_refs)):
        if dil == 1:
            ref[...] = pb.astype(ref.dtype)
            continue
        f = dil // prev_dil
        n = tm // dil
        keep = bi + 1 < len(DILATED_PAIRS)
        for r in range(prev_dil):
            for q in range(f):
                r_new = r + prev_dil * q
                for j in range(ncol):
                    rows = prev_scr[j, pl.ds(r * (tm // prev_dil) + q, n, stride=f), :]
                    c1 = r_new * 3 * ATTN_W + j * LANES
                    ref[:, c1:c1 + LANES] = rows.astype(ref.dtype)
                    if keep:
                        next_scr[j, r_new * n:(r_new + 1) * n, :] = rows
        prev_dil, prev_scr, next_scr = dil, next_scr, prev_scr


def _inproj(x2, sc, sh, g, wcat, wconv, seq):
    t, d = x2.shape
    tm = TM_PROJ
    tps = seq // tm
    mod_spec = pl.BlockSpec((1, 1, d), lambda i: (i // tps, 0, 0))
    return pl.pallas_call(
        functools.partial(_inproj_kernel, tps),
        grid=(t // tm,),
        in_specs=[pl.BlockSpec((tm, d), lambda i: (i, 0)),
                  mod_spec, mod_spec,
                  pl.BlockSpec((1, d), lambda i: (0, 0)),
                  pl.BlockSpec((d, W_COLS), lambda i: (0, 0)),
                  pl.BlockSpec((CONV_K, GDN_CONV_W), lambda i: (0, 0)),
                  pl.BlockSpec((CONV_K - 1, CONV_BLK, 2 * CONV_BLK), lambda i: (0, 0, 0))],
        out_specs=[pl.BlockSpec((tm, GDN_CONV_W), lambda i: (i, 0)),
                   pl.BlockSpec((tm, GDN_V_W), lambda i: (i, 0)),
                   pl.BlockSpec((tm, SMALL_W), lambda i: (i, 0))]
                  + [pl.BlockSpec((tm // dil, dil * 3 * ATTN_W), lambda i: (i, 0)) for _, dil in DILATED_PAIRS],
        out_shape=[jax.ShapeDtypeStruct((t, GDN_CONV_W), BF16),
                   jax.ShapeDtypeStruct((t, GDN_V_W), BF16),
                   jax.ShapeDtypeStruct((t, SMALL_W), F32)]
                  + [jax.ShapeDtypeStruct((t // dil, dil * 3 * ATTN_W), BF16) for _, dil in DILATED_PAIRS],
        scratch_shapes=[pltpu.VMEM((2 * HEAD_ROWS, GDN_CONV_W), F32),
                        pltpu.VMEM((3 * ATTN_W // LANES, tm, LANES), F32),
                        pltpu.VMEM((3 * ATTN_W // LANES, tm, LANES), F32)],
        compiler_params=_params(("arbitrary",)),
        name="inproj",
    )(x2, sc, sh, g, wcat, wconv, _shift_bands())


def _shift_bands():
    t = np.arange(CONV_BLK)[:, None]
    c = np.arange(2 * CONV_BLK)[None, :]
    bands = [(c == CONV_BLK + t - (CONV_K - 1 - j)) for j in range(CONV_K - 1)]
    return jnp.asarray(np.stack(bands).astype(np.float32), dtype=BF16)


def _softplus(x):
    return jnp.maximum(x, 0.0) + jnp.log(1.0 + jnp.exp(-jnp.abs(x)))


def _unit_lower_inverses(n_list):
    c = n_list[0].shape[0]
    row = lax.broadcasted_iota(jnp.int32, (c, c), 0)
    col = lax.broadcasted_iota(jnp.int32, (c, c), 1)
    eye = jnp.where(row == col, 1.0, 0.0)
    same_base = (row // INV_BASE) == (col // INV_BASE)
    n0 = [jnp.where(same_base, n, 0.0) for n in n_list]
    p = [eye - x for x in n0]
    m = [x.astype(BF16) for x in n0]
    for j in range(int(math.log2(INV_BASE)) - 1):
        m = [_dot(x, x).astype(BF16) for x in m]
        p = [x + _dot(x.astype(BF16), y) for x, y in zip(p, m)]
    size = INV_BASE
    while size < c:
        sibling = jnp.logical_and((row // size) % 2 == 1, (col // size) == (row // size) - 1)
        cb = [jnp.where(sibling, n, 0.0).astype(BF16) for n in n_list]
        pb = [x.astype(BF16) for x in p]
        pc = [_dot(x, y).astype(BF16) for x, y in zip(pb, cb)]
        p = [x - _dot(y, z) for x, y, z in zip(p, pc, pb)]
        size *= 2
    return p


def _split3_bf16(x):
    h1 = x.astype(BF16)
    r1 = x - h1.astype(F32)
    h2 = r1.astype(BF16)
    h3 = (r1 - h2.astype(F32)).astype(BF16)
    return h1, h2, h3


def _gdn_kernel(steps_per_seq, qkv_ref, ba_ref, bat_ref, gp_ref, gpt_ref, lblk_ref, ublk_ref,
                o_ref, state_scr):
    i = pl.program_id(0)

    @pl.when(i % steps_per_seq == 0)
    def _():
        state_scr[...] = jnp.zeros_like(state_scr)

    nchunk = GDN_TOK // CHUNK
    ba = ba_ref[...]
    bat = bat_ref[...]
    a_vec = -jnp.exp(gp_ref[0:1, :])
    g_tile = a_vec * _softplus(ba + gp_ref[1:2, :])
    lblk = lblk_ref[...]
    g_cum = sum(_dot(lblk, part) for part in _split3_bf16(g_tile))
    a_col = -jnp.exp(gpt_ref[:, 0:1])
    gt_tile = a_col * _softplus(bat + gpt_ref[:, 1:2])
    ublk = ublk_ref[...]
    gt_cum = sum(_dot(part, ublk) for part in _split3_bf16(gt_tile))
    beta_tile = _sigmoid(ba)

    row = lax.broadcasted_iota(jnp.int32, (CHUNK, CHUNK), 0)
    col = lax.broadcasted_iota(jnp.int32, (CHUNK, CHUNK), 1)
    incl = row >= col
    strict = row > col

    for h0 in range(0, GDN_HEADS, GDN_GROUP):
        heads = range(h0, h0 + GDN_GROUP)
        kb_l, k_l, q_l, rhs_l, qg_l, kg_l, gl_l, inc_l, str_l = ([] for _ in range(9))
        for h in heads:
            q_raw = qkv_ref[:, h * GDN_DK:(h + 1) * GDN_DK].astype(F32)
            k_raw = qkv_ref[:, GDN_QK_W + h * GDN_DK:GDN_QK_W + (h + 1) * GDN_DK].astype(F32)
            v = qkv_ref[:, 2 * GDN_QK_W + h * GDN_DV:2 * GDN_QK_W + (h + 1) * GDN_DV].astype(F32)
            qn = q_raw * lax.rsqrt(jnp.sum(q_raw * q_raw, axis=-1, keepdims=True) + EPS) * (GDN_DK ** -0.5)
            kn = k_raw * lax.rsqrt(jnp.sum(k_raw * k_raw, axis=-1, keepdims=True) + EPS)
            beta = beta_tile[:, h:h + 1]
            gc_all = g_cum[:, GDN_HEADS + h:GDN_HEADS + h + 1]
            eg = jnp.exp(gc_all)
            kb = kn * beta
            rhs = jnp.concatenate([v * beta, kb * eg], axis=1).astype(BF16)
            qg = qn * eg
            kb16, k16, q16 = kb.astype(BF16), kn.astype(BF16), qn.astype(BF16)
            for c in range(nchunk):
                sl = slice(c * CHUNK, (c + 1) * CHUNK)
                g_c = gc_all[sl]
                g_r = gt_cum[GDN_HEADS + h:GDN_HEADS + h + 1, sl]
                dec = jnp.exp(g_c - g_r)
                inc_l.append(jnp.where(incl, dec, 0.0))
                str_l.append(jnp.where(strict, dec, 0.0))
                g_last = g_c[CHUNK - 1:CHUNK, :]
                kg_l.append((kn[sl] * jnp.exp(g_last - g_c)).astype(BF16))
                gl_l.append(jnp.exp(g_last))
                kb_l.append(kb16[sl])
                k_l.append(k16[sl])
                q_l.append(q16[sl])
                rhs_l.append(rhs[sl])
                qg_l.append(qg[sl])
        nprob = len(k_l)
        n_l = [_dot_nt(kb_l[j], k_l[j]) * str_l[j] for j in range(nprob)]
        aqk_l = [(_dot_nt(q_l[j], k_l[j]) * inc_l[j]).astype(BF16) for j in range(nprob)]
        tinv_l = _unit_lower_inverses(n_l)
        sol_l = [_dot(tinv_l[j].astype(BF16), rhs_l[j]).astype(BF16) for j in range(nprob)]
        a2_l = [_dot(aqk_l[j], sol_l[j]) for j in range(nprob)]
        k2_l = [_dot_tn(kg_l[j], sol_l[j]) for j in range(nprob)]
        states = [state_scr[h] for h in heads]
        for c in range(nchunk):
            sl = slice(c * CHUNK, (c + 1) * CHUNK)
            for hi, h in enumerate(heads):
                j = hi * nchunk + c
                sb = states[hi].astype(BF16)
                qeff = (qg_l[j] - a2_l[j][:, GDN_DV:]).astype(BF16)
                o = _dot(qeff, sb) + a2_l[j][:, :GDN_DV]
                states[hi] = gl_l[j] * states[hi] + k2_l[j][:, :GDN_DV] - _dot(k2_l[j][:, GDN_DV:].astype(BF16), sb)
                o_ref[sl, h * GDN_DV:(h + 1) * GDN_DV] = o.astype(o_ref.dtype)
        for hi, h in enumerate(heads):
            state_scr[h] = states[hi]


def _gdn(qkva, ba, bat, gp, gpt, lblk, ublk, seq):
    t = qkva.shape[0]
    tok = GDN_TOK
    sps = seq // tok
    const = lambda i: (0, 0)
    return pl.pallas_call(
        functools.partial(_gdn_kernel, sps),
        grid=(t // tok,),
        in_specs=[pl.BlockSpec((tok, GDN_CONV_W), lambda i: (i, 0)),
                  pl.BlockSpec((tok, SMALL_W), lambda i: (i, 0)),
                  pl.BlockSpec((8, tok), lambda i: (0, i)),
                  pl.BlockSpec((8, LANES), const),
                  pl.BlockSpec((8, LANES), const),
                  pl.BlockSpec((tok, tok), const),
                  pl.BlockSpec((tok, tok), const)],
        out_specs=pl.BlockSpec((tok, GDN_V_W), lambda i: (i, 0)),
        out_shape=jax.ShapeDtypeStruct((t, GDN_V_W), F32),
        scratch_shapes=[pltpu.VMEM((GDN_HEADS, GDN_DK, GDN_DV), F32)],
        compiler_params=_params(("arbitrary",)),
        name="gdn",
    )(qkva, ba, bat, gp, gpt, lblk, ublk)


def _attn_kernel(cur_ref, prev_ref, bucket_ref, rb_ref, o_ref, lse_ref, bias_ref):
    n = pl.program_id(2)

    @pl.when(jnp.logical_and(jnp.logical_and(pl.program_id(0) == 0, pl.program_id(1) == 0), n == 0))
    def _():
        bucket = bucket_ref[...]
        col = lax.broadcasted_iota(jnp.int32, (QBLK, 2 * QBLK), 1)
        for h in range(ATTN_HEADS):
            acc = jnp.full((QBLK, 2 * QBLK), NEG_INF, F32)
            for b in range(NUM_BUCKETS):
                acc = jnp.where(bucket == b, rb_ref[b, h], acc)
            bias_ref[0, h] = acc
            bias_ref[1, h] = jnp.where(col < QBLK, NEG_INF, acc)

    nsub = cur_ref.shape[1] // QBLK
    nres = cur_ref.shape[2] // (3 * ATTN_W)
    first = jnp.where(n == 0, 1, 0)
    w = ATTN_W
    q_all = [cur_ref[0, :, c * 3 * w:c * 3 * w + w] for c in range(nres)]
    k_all = [jnp.concatenate([prev_ref[0, :, c * 3 * w + w:c * 3 * w + 2 * w],
                              cur_ref[0, :, c * 3 * w + w:c * 3 * w + 2 * w]], axis=0) for c in range(nres)]
    v_all = [jnp.concatenate([prev_ref[0, :, c * 3 * w + 2 * w:(c + 1) * 3 * w],
                              cur_ref[0, :, c * 3 * w + 2 * w:(c + 1) * 3 * w]], axis=0) for c in range(nres)]
    lane = lax.broadcasted_iota(jnp.int32, (QBLK, LANES), 1)
    ones = jnp.ones((2 * QBLK, LANES), BF16)
    per_group = LANES // ATTN_DH
    ngrp = ATTN_HEADS // per_group
    scale = ATTN_DH ** -0.5
    mine = [jnp.logical_and(lane >= j * ATTN_DH, lane < (j + 1) * ATTN_DH) for j in range(per_group)]
    probs = [(c, u, h) for c in range(nres) for u in range(nsub) for h in range(ATTN_HEADS)]
    s_l = []
    for c, u, h in probs:
        grp, j = divmod(h, per_group)
        gs = slice(grp * LANES, (grp + 1) * LANES)
        qg = q_all[c][u * QBLK:(u + 1) * QBLK, gs] * scale
        table = first if u == 0 else 0
        s_l.append(_dot_nt(jnp.where(mine[j], qg, jnp.zeros_like(qg)), k_all[c][u * QBLK:(u + 2) * QBLK, gs])
                   + bias_ref[table, h])
    m_l = [jnp.max(s, axis=-1, keepdims=True) for s in s_l]
    p_l = [jnp.exp(s - m).astype(BF16) for s, m in zip(s_l, m_l)]
    v_ext = {(c, u, grp): jnp.concatenate(
                 [v_all[c][u * QBLK:(u + 2) * QBLK, grp * LANES:(grp + 1) * LANES], ones], axis=1)
             for c in range(nres) for u in range(nsub) for grp in range(ngrp)}
    pv_l = [_dot(p, v_ext[(c, u, h // per_group)]) for p, (c, u, h) in zip(p_l, probs)]
    for c in range(nres):
        for u in range(nsub):
            outs = []
            lse_tile = jnp.zeros((QBLK, LANES), F32)
            for grp in range(ngrp):
                o_grp = jnp.zeros((QBLK, LANES), F32)
                for j in range(per_group):
                    h = grp * per_group + j
                    idx = (c * nsub + u) * ATTN_HEADS + h
                    den = pv_l[idx][:, LANES:]
                    o_grp = jnp.where(mine[j], pv_l[idx][:, :LANES] / den, o_grp)
                    lse_tile = jnp.where(lane == h, m_l[idx] + jnp.log(den[:, 0:1]), lse_tile)
                outs.append(o_grp)
            rs = slice(u * QBLK, (u + 1) * QBLK)
            o_ref[0, rs, c * w:(c + 1) * w] = jnp.concatenate(outs, axis=1).astype(o_ref.dtype)
            lse_ref[0, rs, c * LANES:(c + 1) * LANES] = lse_tile


def _attn_branch(qkvb, bucket, rel_bias, bsz, seq, dil):
    ln = seq // dil
    sub = min(ATTN_SUB, ln // QBLK)
    rows = sub * QBLK
    assert seq % dil == 0 and ln % rows == 0, (seq, dil, rows)
    nb = ln // rows
    nres = min(max(ATTN_SUB // sub, 1), dil)
    assert dil % nres == 0, (dil, nres)
    xv = qkvb.reshape(bsz, ln, dil * 3 * ATTN_W)
    w = ATTN_W
    o, lse = pl.pallas_call(
        _attn_kernel,
        grid=(bsz, dil // nres, nb),
        in_specs=[pl.BlockSpec((1, rows, nres * 3 * w), lambda b, r, n: (b, n, r)),
                  pl.BlockSpec((1, QBLK, nres * 3 * w), lambda b, r, n: (b, jnp.maximum(n * sub - 1, 0), r)),
                  pl.BlockSpec((QBLK, 2 * QBLK), lambda b, r, n: (0, 0)),
                  pl.BlockSpec(memory_space=pltpu.SMEM)],
        out_specs=[pl.BlockSpec((1, rows, nres * w), lambda b, r, n: (b, n, r)),
                   pl.BlockSpec((1, rows, nres * LANES), lambda b, r, n: (b, n, r))],
        out_shape=[jax.ShapeDtypeStruct((bsz, ln, dil * w), BF16),
                   jax.ShapeDtypeStruct((bsz, ln, dil * LANES), F32)],
        scratch_shapes=[pltpu.VMEM((2, ATTN_HEADS, QBLK, 2 * QBLK), F32)],
        compiler_params=_params(("arbitrary", "arbitrary", "arbitrary")),
        name=f"attn_d{dil}",
    )(xv, xv, bucket, rel_bias)
    return o.reshape(bsz * ln, dil * w), lse.reshape(bsz * ln, dil * LANES)


def _bucket_table(window, dil):
    steps = window // dil
    qi = np.arange(QBLK)[:, None]
    kj = np.arange(2 * QBLK)[None, :]
    delta = qi + QBLK - kj
    dist = np.maximum(delta, 0) * dil
    max_exact = NUM_BUCKETS // 2
    d_f = np.maximum(dist, 1).astype(np.float32)
    large = max_exact + (np.log(d_f / max_exact) / math.log(REL_MAX_DIST / max_exact)
                         * (NUM_BUCKETS - max_exact)).astype(np.int32)
    bucket = np.where(dist < max_exact, dist, np.minimum(large, NUM_BUCKETS - 1))
    return np.where((delta >= 0) & (delta <= steps), bucket, -1).astype(np.int32)


def _split_bf16(x):
    hi = x.astype(BF16)
    lo = (x - hi.astype(F32)).astype(BF16)
    return hi, lo


def _pack_halves(x):
    w = x.shape[1] // 2
    lo = lax.bitcast_convert_type(x[:, :w].astype(BF16).astype(F32), U32)
    hi = lax.bitcast_convert_type(x[:, w:].astype(BF16).astype(F32), U32)
    return (hi & jnp.uint32(0xFFFF0000)) | (lo >> 16)


def _unpack_halves(u):
    lo = lax.bitcast_convert_type(u << 16, F32)
    hi = lax.bitcast_convert_type(u & jnp.uint32(0xFFFF0000), F32)
    return lo, hi


def _post_kernel(og_ref, z_ref, o1_ref, o2_ref, o3_ref, l1_ref, l2_ref, l3_ref, x_ref,
                 gta_ref, scf_ref, shf_ref, gnorm_ref, anorm_ref, wout_ref, nffn_ref,
                 wrh_ref, exp_ref, ls_ref,
                 x1_ref, h2_ref, route_ref, cnt_ref, carry_scr, *tok_scr):
    i = pl.program_id(0)
    tm = x_ref.shape[0]

    @pl.when(i == 0)
    def _():
        carry_scr[...] = jnp.zeros_like(carry_scr)

    nbr = len(DILATED_PAIRS)
    ob_scr, lse_scr = tok_scr[:nbr], tok_scr[nbr:]
    for (_, dil), o_ref, l_ref, o_s, l_s in zip(DILATED_PAIRS, (o1_ref, o2_ref, o3_ref),
                                                 (l1_ref, l2_ref, l3_ref), ob_scr, lse_scr):
        for r in range(dil):
            rows = pl.ds(r, tm // dil, stride=dil) if dil > 1 else slice(None)
            for j in range(ATTN_W // LANES):
                c1 = r * ATTN_W + j * LANES
                o_s[j, rows, :] = o_ref[:, c1:c1 + LANES].astype(F32)
            l_s[rows, :] = l_ref[:, r * LANES:(r + 1) * LANES]

    heads = []
    for h in range(GDN_HEADS):
        hs = slice(h * GDN_DV, (h + 1) * GDN_DV)
        seg = og_ref[:, hs]
        nrm = seg * lax.rsqrt(jnp.mean(seg * seg, axis=-1, keepdims=True) + EPS) * gnorm_ref[...]
        zz = z_ref[:, hs].astype(F32)
        heads.append((nrm * (zz * _sigmoid(zz))).astype(BF16))
    oa = jnp.concatenate(heads, axis=1)

    l1, l2, l3 = (l_s[...] for l_s in lse_scr)
    m = jnp.maximum(jnp.maximum(l1, l2), l3)
    e1, e2, e3 = jnp.exp(l1 - m), jnp.exp(l2 - m), jnp.exp(l3 - m)
    inv = 1.0 / (e1 + e2 + e3)
    ob = jnp.zeros((tm, ATTN_W), F32)
    for e, o_s in zip((e1, e2, e3), ob_scr):
        hi, lo = _split_bf16(e * inv)
        wexp = _dot(hi, exp_ref[...]) + _dot(lo, exp_ref[...])
        ob = ob + wexp * jnp.concatenate([o_s[j] for j in range(ATTN_W // LANES)], axis=1)
    ob = ob * lax.rsqrt(jnp.mean(ob * ob, axis=-1, keepdims=True) + EPS) * anorm_ref[...]

    mix = _dot(oa, wout_ref[0:GDN_V_W, :]) + _dot(ob.astype(BF16), wout_ref[GDN_V_W:, :])
    x1 = x_ref[...] + gta_ref[0] * mix
    x1_ref[...] = x1
    h2 = x1 * lax.rsqrt(jnp.mean(x1 * x1, axis=-1, keepdims=True) + EPS) * nffn_ref[...]
    h2 = h2 * (1.0 + scf_ref[0]) + shf_ref[0]
    h2_ref[...] = _pack_halves(h2)

    hh, hl = _split_bf16(h2)
    both = _dot(hh, wrh_ref[...])
    logits = both[:, :LANES] + both[:, LANES:] + _dot(hl, wrh_ref[:, :LANES])
    tm = logits.shape[0]
    lane = lax.broadcasted_iota(jnp.int32, (tm, LANES), 1).astype(F32)
    big = float(LANES)
    gmask = lane < N_GROUPS
    glog = jnp.where(gmask, logits, NEG_INF)
    gmax = jnp.max(glog, axis=-1, keepdims=True)
    gidx = jnp.min(jnp.where(jnp.logical_and(gmask, glog == gmax), lane, big), axis=-1, keepdims=True)
    gprob = 1.0 / jnp.sum(jnp.where(gmask, jnp.exp(glog - gmax), 0.0), axis=-1, keepdims=True)
    lo_lane = N_GROUPS + EXPERTS_PER_GROUP * gidx
    emask = jnp.logical_and(lane >= lo_lane, lane < lo_lane + EXPERTS_PER_GROUP)
    elog = jnp.where(emask, logits, NEG_INF)
    m1 = jnp.max(elog, axis=-1, keepdims=True)
    i1 = jnp.min(jnp.where(jnp.logical_and(emask, elog == m1), lane, big), axis=-1, keepdims=True)
    emask2 = jnp.logical_and(emask, lane != i1)
    elog2 = jnp.where(emask2, logits, NEG_INF)
    m2 = jnp.max(elog2, axis=-1, keepdims=True)
    i2 = jnp.min(jnp.where(jnp.logical_and(emask2, elog2 == m2), lane, big), axis=-1, keepdims=True)
    r = jnp.exp(m2 - m1)
    gate1 = gprob / (1.0 + r)
    gate2 = gprob * r / (1.0 + r)
    ex1 = i1 - N_GROUPS
    ex2 = i2 - N_GROUPS

    hit1 = lane == ex1
    hit2 = lane == ex2
    onehot = jnp.where(jnp.logical_or(hit1, hit2), 1.0, 0.0)
    pref = _dot(ls_ref[...], onehot.astype(BF16)) + carry_scr[...]
    rank1 = jnp.sum(jnp.where(hit1, pref, 0.0), axis=-1, keepdims=True)
    rank2 = jnp.sum(jnp.where(hit2, pref, 0.0), axis=-1, keepdims=True)
    carry = carry_scr[...] + jnp.sum(onehot, axis=0, keepdims=True)
    carry_scr[...] = carry
    cnt_ref[...] = jnp.broadcast_to(carry, cnt_ref.shape)

    route = jnp.zeros((tm, LANES), F32)
    for idx, val in enumerate((ex1, ex2, gate1, gate2, rank1, rank2)):
        route = jnp.where(lane == idx, val, route)
    route_ref[...] = route


def _post(og, z, os_, ls_, x2, gta, scf, shf, gnorm, anorm, wout, nffn, wrh, expand, lstrict, seq):
    t, d = x2.shape
    tm = TM_POST
    tps = seq // tm
    tile = lambda w: pl.BlockSpec((tm, w), lambda i: (i, 0))
    const2 = lambda a: pl.BlockSpec(a.shape, lambda i: (0, 0))
    mod_spec = pl.BlockSpec((1, 1, d), lambda i: (i // tps, 0, 0))
    return pl.pallas_call(
        _post_kernel,
        grid=(t // tm,),
        in_specs=[tile(GDN_V_W), tile(GDN_V_W)]
                 + [pl.BlockSpec((tm // dil, dil * ATTN_W), lambda i: (i, 0)) for _, dil in DILATED_PAIRS]
                 + [pl.BlockSpec((tm // dil, dil * LANES), lambda i: (i, 0)) for _, dil in DILATED_PAIRS]
                 + [tile(d),
                  mod_spec, mod_spec, mod_spec,
                  const2(gnorm), const2(anorm), const2(wout), const2(nffn),
                  const2(wrh), const2(expand), const2(lstrict)],
        out_specs=[tile(d), tile(d // 2), tile(LANES), pl.BlockSpec((8, LANES), lambda i: (0, 0))],
        out_shape=[jax.ShapeDtypeStruct((t, d), F32),
                   jax.ShapeDtypeStruct((t, d // 2), U32),
                   jax.ShapeDtypeStruct((t, LANES), F32),
                   jax.ShapeDtypeStruct((8, LANES), F32)],
        scratch_shapes=[pltpu.VMEM((1, LANES), F32)]
                       + [pltpu.VMEM((ATTN_W // LANES, tm, LANES), F32) for _ in DILATED_PAIRS]
                       + [pltpu.VMEM((tm, LANES), F32) for _ in DILATED_PAIRS],
        compiler_params=_params(("arbitrary",)),
        name="post",
    )(og, z, *os_, *ls_, x2, gta, scf, shf, gnorm, anorm, wout, nffn, wrh, expand, lstrict)


def _dispatch_kernel(dest_ref, fill_ref, pad_ref, nused_ref, h2_ref, xs_ref, zero_scr, sem, fill_sem):
    i = pl.program_id(0)
    tm = h2_ref.shape[0]

    @pl.when(i == 0)
    def _():
        zero_scr[...] = jnp.zeros_like(zero_scr)

        def pieces(e, act):
            base, pad = fill_ref[e], pad_ref[e]
            head = pad & (SUBLANES - 1)
            for j in range(SUBLANES - 1):
                @pl.when(j < head)
                def _():
                    act(pltpu.make_async_copy(zero_scr.at[pl.ds(0, 1)], xs_ref.at[pl.ds(base + j, 1)],
                                              fill_sem))
            off = base + head
            for bit in range(SUBLANES.bit_length() - 1, EXPERT_BLK.bit_length() - 1):
                size = 1 << bit

                @pl.when((pad >> bit) & 1 == 1)
                def _():
                    act(pltpu.make_async_copy(zero_scr.at[pl.ds(0, size)],
                                              xs_ref.at[pl.ds(pl.multiple_of(off, SUBLANES), size)], fill_sem))
                off = off + (pad & size)

        def fill(e, carry):
            pieces(e, lambda cp: cp.start())
            return carry

        def drain(e, carry):
            pieces(e, lambda cp: cp.wait())
            return carry

        lax.fori_loop(0, N_EXPERTS, fill, 0)
        lax.fori_loop(0, N_EXPERTS, drain, 0)

        def tail(b, act):
            for half in range(EXPERT_BLK // zero_scr.shape[0]):
                row0 = pl.multiple_of(b * EXPERT_BLK + half * zero_scr.shape[0], SUBLANES)
                act(pltpu.make_async_copy(zero_scr, xs_ref.at[pl.ds(row0, zero_scr.shape[0])], fill_sem))

        def tail_fill(b, carry):
            tail(b, lambda cp: cp.start())
            return carry

        def tail_drain(b, carry):
            tail(b, lambda cp: cp.wait())
            return carry

        nblk = xs_ref.shape[0] // EXPERT_BLK
        lax.fori_loop(nused_ref[0], nblk, tail_fill, 0)
        lax.fori_loop(nused_ref[0], nblk, tail_drain, 0)

    def start(r, carry):
        for k in range(TOP_K):
            d = dest_ref[(i * tm + r) * TOP_K + k]
            pltpu.make_async_copy(h2_ref.at[pl.ds(r, 1)], xs_ref.at[pl.ds(d, 1)], sem).start()
        return carry

    lax.fori_loop(0, tm, start, 0, unroll=ROW_UNROLL)
    for k in range(TOP_K):
        pltpu.make_async_copy(h2_ref, xs_ref.at[pl.ds(0, tm)], sem).wait()


def _dispatch(dest, fill_start, pad_rows, nused, h2, p):
    t, d = h2.shape
    tm = TM_ROWS
    grid_spec = pltpu.PrefetchScalarGridSpec(
        num_scalar_prefetch=4,
        grid=(t // tm,),
        in_specs=[pl.BlockSpec((tm, d), lambda i, *_: (i, 0))],
        out_specs=pl.BlockSpec(memory_space=pl.ANY),
        scratch_shapes=[pltpu.VMEM((EXPERT_BLK // 2, d), h2.dtype), pltpu.SemaphoreType.DMA,
                        pltpu.SemaphoreType.DMA],
    )
    return pl.pallas_call(
        _dispatch_kernel,
        grid_spec=grid_spec,
        out_shape=jax.ShapeDtypeStruct((p, d), h2.dtype),
        compiler_params=_params(("arbitrary",)),
        name="dispatch",
    )(dest, fill_start, pad_rows, nused, h2)


def _expert_kernel(blk_e_ref, nused_ref, xs_ref, wg_ref, wu_ref, wd_ref, ys_ref, wg16, wu16, wd16):
    b = pl.program_id(0)
    used = b < nused_ref[0]

    new_expert = jnp.logical_or(b == 0, blk_e_ref[b] != blk_e_ref[jnp.maximum(b - 1, 0)])

    @pl.when(jnp.logical_and(used, new_expert))
    def _():
        wg16[...] = wg_ref[0].astype(BF16)
        wu16[...] = wu_ref[0].astype(BF16)
        wd16[...] = wd_ref[0].astype(BF16)

    @pl.when(used)
    def _():
        half = xs_ref.shape[1]
        x_lo, x_hi = (v.astype(BF16) for v in _unpack_halves(xs_ref[...]))
        g = _dot(x_lo, wg16[0:half, :]) + _dot(x_hi, wg16[half:, :])
        u = _dot(x_lo, wu16[0:half, :]) + _dot(x_hi, wu16[half:, :])
        hid = (g * _sigmoid(g)) * u
        ys_ref[...] = _pack_halves(_dot(hid.astype(BF16), wd16[...]))

    @pl.when(b >= nused_ref[0])
    def _():
        ys_ref[...] = jnp.zeros_like(ys_ref)


def _experts(blk_e, nused, xs, wg, wu, wd):
    p, dh = xs.shape
    d = wg.shape[1]
    assert d == 2 * dh, (d, dh)
    blk = EXPERT_BLK
    row_map = lambda b, be, nu: (jnp.minimum(b, nu[0] - 1), 0)
    w_map = lambda b, be, nu: (be[b], 0, 0)
    grid_spec = pltpu.PrefetchScalarGridSpec(
        num_scalar_prefetch=2,
        grid=(p // blk,),
        in_specs=[pl.BlockSpec((blk, dh), row_map),
                  pl.BlockSpec((1, d, D_EXPERT), w_map),
                  pl.BlockSpec((1, d, D_EXPERT), w_map),
                  pl.BlockSpec((1, D_EXPERT, d), w_map)],
        out_specs=pl.BlockSpec((blk, dh), lambda b, be, nu: (b, 0)),
        scratch_shapes=[pltpu.VMEM((d, D_EXPERT), BF16), pltpu.VMEM((d, D_EXPERT), BF16),
                        pltpu.VMEM((D_EXPERT, d), BF16)],
    )
    return pl.pallas_call(
        _expert_kernel,
        grid_spec=grid_spec,
        out_shape=jax.ShapeDtypeStruct((p, dh), U32),
        compiler_params=_params(("arbitrary",)),
        name="experts",
    )(blk_e, nused, xs, wg, wu, wd)


def _combine_kernel(final_norm, dest_ref, ys_ref, x1_ref, route_ref, gtf_ref, nf_ref, o_ref, ybuf, sem):
    i = pl.program_id(0)
    tm = x1_ref.shape[0]
    slot = i % 2

    def gather_tile(tile, slot_):
        def start(r, carry):
            for k in range(TOP_K):
                d = dest_ref[(tile * tm + r) * TOP_K + k]
                pltpu.make_async_copy(ys_ref.at[pl.ds(d, 1)], ybuf.at[slot_, k, pl.ds(r, 1)],
                                      sem.at[slot_]).start()
            return carry
        lax.fori_loop(0, tm, start, 0, unroll=ROW_UNROLL)

    @pl.when(i == 0)
    def _():
        gather_tile(0, 0)

    @pl.when(i + 1 < pl.num_programs(0))
    def _():
        gather_tile(i + 1, 1 - slot)

    for k in range(TOP_K):
        pltpu.make_async_copy(ys_ref.at[pl.ds(0, tm)], ybuf.at[slot, k], sem.at[slot]).wait()
    route = route_ref[...]
    lo0, hi0 = _unpack_halves(ybuf[slot, 0])
    lo1, hi1 = _unpack_halves(ybuf[slot, 1])
    g0, g1 = route[:, 2:3], route[:, 3:4]
    moe = jnp.concatenate([lo0 * g0 + lo1 * g1, hi0 * g0 + hi1 * g1], axis=1)
    x2 = x1_ref[...] + gtf_ref[0] * moe
    if final_norm:
        x2 = x2 * lax.rsqrt(jnp.mean(x2 * x2, axis=-1, keepdims=True) + EPS) * nf_ref[...]
    o_ref[...] = x2


def _combine(dest, ys, x1, route, gtf, nf, seq, final_norm):
    t, d = x1.shape
    tm = TM_ROWS
    tps = seq // tm
    grid_spec = pltpu.PrefetchScalarGridSpec(
        num_scalar_prefetch=1,
        grid=(t // tm,),
        in_specs=[pl.BlockSpec(memory_space=pl.ANY),
                  pl.BlockSpec((tm, d), lambda i, dest: (i, 0)),
                  pl.BlockSpec((tm, LANES), lambda i, dest: (i, 0)),
                  pl.BlockSpec((1, 1, d), lambda i, dest: (i // tps, 0, 0)),
                  pl.BlockSpec((1, d), lambda i, dest: (0, 0))],
        out_specs=pl.BlockSpec((tm, d), lambda i, dest: (i, 0)),
        scratch_shapes=[pltpu.VMEM((2, TOP_K, tm, ys.shape[1]), ys.dtype), pltpu.SemaphoreType.DMA((2,))],
    )
    return pl.pallas_call(
        functools.partial(_combine_kernel, final_norm),
        grid_spec=grid_spec,
        out_shape=jax.ShapeDtypeStruct((t, d), F32),
        compiler_params=_params(("arbitrary",)),
        name="combine",
    )(dest, ys, x1, route, gtf, nf)


def _block_tri(n, chunk, lower):
    r = jnp.arange(n)[:, None]
    c = jnp.arange(n)[None, :]
    same = (r // chunk) == (c // chunk)
    tri = (r >= c) if lower else (r <= c)
    return jnp.where(same & tri, 1.0, 0.0).astype(BF16)


def _layer(x2, c, bsz, seq, w_ada, b_ada, norm_mix, w_in, w_conv, a_log, dt_bias, gdn_norm,
           attn_norm, w_out, rel_bias, norm_ffn, w_rg, w_re, w_gate, w_up, w_down):
    t, d = x2.shape
    mod = _adaln(c, w_ada, b_ada)
    sh_a, sc_a, gt_a, sh_f, sc_f, gt_f = [m.reshape(bsz, 1, d) for m in jnp.split(mod, 6, axis=-1)]

    s1 = GDN_CONV_W
    s2 = s1 + GDN_V_W
    s4 = s2 + 2 * GDN_HEADS
    small = jnp.pad(w_in[:, s2:s4], ((0, 0), (0, SMALL_W - 2 * GDN_HEADS)))
    wcat = jnp.concatenate([w_in[:, :s2], small, w_in[:, s4:]], axis=1).astype(BF16)
    qkva, z, ba, *qkvb = _inproj(x2, sc_a, sh_a, norm_mix.reshape(1, d), wcat, w_conv, seq)

    pad4 = lambda v: jnp.pad(v.astype(F32), (GDN_HEADS, LANES - 2 * GDN_HEADS))
    gp = jnp.zeros((8, LANES), F32).at[0].set(pad4(a_log)).at[1].set(pad4(dt_bias))
    og = _gdn(qkva, ba, ba[:, :8].T, gp, _gp_cols(a_log, dt_bias),
              _block_tri(GDN_TOK, CHUNK, True), _block_tri(GDN_TOK, CHUNK, False), seq)

    outs, lses = [], []
    for (window, dil), qkvb_d in zip(DILATED_PAIRS, qkvb):
        o_i, lse_i = _attn_branch(qkvb_d, jnp.asarray(_bucket_table(window, dil)), rel_bias.astype(F32),
                                  bsz, seq, dil)
        outs.append(o_i)
        lses.append(lse_i)

    expand = jnp.where((jnp.arange(LANES)[:, None] == jnp.arange(ATTN_W)[None, :] // ATTN_DH), 1.0, 0.0).astype(BF16)
    lstrict = jnp.where(jnp.arange(TM_POST)[:, None] > jnp.arange(TM_POST)[None, :], 1.0, 0.0).astype(BF16)
    wr = jnp.pad(jnp.concatenate([w_rg, w_re], axis=1).astype(F32), ((0, 0), (0, LANES - N_GROUPS - N_EXPERTS)))
    wrh = wr.astype(BF16)
    wr_split = jnp.concatenate([wrh, (wr - wrh.astype(F32)).astype(BF16)], axis=1)
    x1, h2, route, cnt = _post(og, z, outs, lses, x2, gt_a, sc_f, sh_f,
                               gdn_norm.reshape(1, GDN_DV), attn_norm.reshape(1, ATTN_W),
                               w_out.astype(BF16), norm_ffn.reshape(1, d), wr_split, expand, lstrict, seq)

    blk = EXPERT_BLK
    counts = cnt[0, :N_EXPERTS].astype(jnp.int32)
    padded = (counts + blk - 1) // blk * blk
    pends = jnp.cumsum(padded)
    pstarts = pends - padded
    eids = route[:, 0:TOP_K].astype(jnp.int32)
    ranks = route[:, 4:4 + TOP_K].astype(jnp.int32)
    expert_ids = jnp.arange(N_EXPERTS, dtype=jnp.int32)
    seg_start = jnp.sum(jnp.where(eids[..., None] == expert_ids, pstarts, 0), axis=-1)
    dest = (seg_start + ranks).reshape(t * TOP_K)
    a = t * TOP_K
    p = -(-a // blk) * blk + N_EXPERTS * blk
    nblk = p // blk
    blk_start = jnp.arange(nblk, dtype=jnp.int32) * blk
    blk_e = jnp.minimum(jnp.sum((pends[None, :] <= blk_start[:, None]).astype(jnp.int32), axis=1),
                        N_EXPERTS - 1)
    nused = (pends[-1] // blk).astype(jnp.int32).reshape(1)

    xs = _dispatch(dest, pstarts + counts, padded - counts, nused, h2, p)
    ys = _experts(blk_e, nused, xs, w_gate, w_up, w_down)
    return ys, dest, x1, route, gt_f


def _gp_cols(a_log, dt_bias):
    z = jnp.zeros((8, LANES), F32)
    z = z.at[GDN_HEADS:2 * GDN_HEADS, 0].set(a_log.astype(F32))
    z = z.at[GDN_HEADS:2 * GDN_HEADS, 1].set(dt_bias.astype(F32))
    return z


def kernel(x, c, w_ada, b_ada, norm_mix, w_in, w_conv, a_log, dt_bias, gdn_norm, attn_norm, w_out,
           rel_bias, norm_ffn, w_router_group, w_router_expert, w_gate, w_up, w_down, norm_final):
    bsz, seq, d = x.shape
    depth = w_ada.shape[0]
    x2 = x.reshape(bsz * seq, d)
    for l in range(depth):
        ys, dest, x1, route, gt_f = _layer(
            x2, c, bsz, seq, w_ada[l], b_ada[l], norm_mix[l], w_in[l], w_conv[l], a_log[l], dt_bias[l],
            gdn_norm[l], attn_norm[l], w_out[l], rel_bias, norm_ffn[l], w_router_group[l],
            w_router_expert[l], w_gate[l], w_up[l], w_down[l])
        x2 = _combine(dest, ys, x1, route, gt_f, norm_final.reshape(1, d), seq, l == depth - 1)
    return x2.reshape(bsz, seq, d)
```

```python
import functools
import math

import jax
import jax.numpy as jnp
import numpy as np
from jax import lax
from jax.experimental import pallas as pl
from jax.experimental.pallas import tpu as pltpu

D_MODEL = 1024
GDN_HEADS = 4
GDN_DK = 128
GDN_DV = 128
CONV_K = 4
CHUNK = 64
ATTN_HEADS = 8
ATTN_DH = 64
DILATED_PAIRS = ((128, 1), (512, 4), (2048, 16))
QBLK = 128
NUM_BUCKETS = 32
REL_MAX_DIST = 2048
N_GROUPS = 4
EXPERTS_PER_GROUP = 8
N_EXPERTS = N_GROUPS * EXPERTS_PER_GROUP
TOP_K = 2
D_EXPERT = 256
EPS = 1e-6
NEG_INF = -1e30

GDN_QK_W = GDN_HEADS * GDN_DK
GDN_V_W = GDN_HEADS * GDN_DV
ATTN_W = ATTN_HEADS * ATTN_DH
GDN_CONV_W = 2 * GDN_QK_W + GDN_V_W
LANES = 128
SUBLANES = 8
SMALL_W = LANES
W_COLS = GDN_CONV_W + GDN_V_W + SMALL_W + 3 * ATTN_W

TM_PROJ = 512
CONV_BLK = 128
HEAD_ROWS = 16
TM_POST = 512
TM_ROWS = 1024
GDN_TOK = 512
GDN_GROUP = 4
INV_BASE = 8
ATTN_SUB = 8
EXPERT_BLK = 512
ROW_UNROLL = 32
VMEM_LIMIT = 56 * 1024 * 1024

F32 = jnp.float32
BF16 = jnp.bfloat16
U32 = jnp.uint32
HIGHEST = lax.Precision.HIGHEST


def _sigmoid(x):
    return 1.0 / (1.0 + jnp.exp(-x))


def _dot(a, b, precision=None):
    return jnp.dot(a, b, preferred_element_type=F32, precision=precision)


def _dot_nt(a, b, precision=None):
    return lax.dot_general(a, b, (((1,), (1,)), ((), ())), preferred_element_type=F32,
                           precision=precision)


def _dot_tn(a, b, precision=None):
    return lax.dot_general(a, b, (((0,), (0,)), ((), ())), preferred_element_type=F32,
                           precision=precision)


def _params(sem):
    return pltpu.CompilerParams(dimension_semantics=sem, vmem_limit_bytes=VMEM_LIMIT)


def _adaln_kernel(c_ref, w_ref, b_ref, o_ref):
    o_ref[...] = _dot(c_ref[...], w_ref[...], HIGHEST) + b_ref[...]


def _adaln(c, w, b):
    bsz, d = c.shape
    n = w.shape[1]
    tn = 1024
    return pl.pallas_call(
        _adaln_kernel,
        grid=(n // tn,),
        in_specs=[pl.BlockSpec((bsz, d), lambda j: (0, 0)),
                  pl.BlockSpec((d, tn), lambda j: (0, j)),
                  pl.BlockSpec((1, tn), lambda j: (0, j))],
        out_specs=pl.BlockSpec((bsz, tn), lambda j: (0, j)),
        out_shape=jax.ShapeDtypeStruct((bsz, n), F32),
        compiler_params=_params(("arbitrary",)),
        name="adaln",
    )(c, w, b.reshape(1, n))


def _inproj_kernel(tiles_per_seq, x_ref, sc_ref, sh_ref, g_ref, w_ref, wconv_ref, shift_ref,
                   qkva_ref, z_ref, ba_ref, *rest):
    qkvb_refs = rest[:len(DILATED_PAIRS)]
    win_scr, pb_scr, pb_next_scr = rest[len(DILATED_PAIRS):]
    i = pl.program_id(0)
    tm = x_ref.shape[0]
    x = x_ref[...]
    h = x * lax.rsqrt(jnp.mean(x * x, axis=-1, keepdims=True) + EPS) * g_ref[...]
    h = h * (1.0 + sc_ref[0]) + sh_ref[0]
    hb = h.astype(BF16)

    @pl.when(i % tiles_per_seq == 0)
    def _():
        win_scr[0:HEAD_ROWS, :] = jnp.zeros((HEAD_ROWS, GDN_CONV_W), F32)

    pa = _dot(hb, w_ref[:, 0:GDN_CONV_W])
    pa16 = pa.astype(BF16)
    for b in range(tm // CONV_BLK):
        rs = slice(b * CONV_BLK, (b + 1) * CONV_BLK)
        acc = pa[rs] * wconv_ref[CONV_K - 1:CONV_K, :]
        for j in range(CONV_K - 1):
            if b == 0:
                shifted = _dot(shift_ref[j, :, CONV_BLK:], pa16[rs])
            else:
                shifted = _dot(shift_ref[j], pa16[(b - 1) * CONV_BLK:(b + 1) * CONV_BLK])
            acc = acc + shifted * wconv_ref[j:j + 1, :]
        qkva_ref[rs, :] = (acc * _sigmoid(acc)).astype(qkva_ref.dtype)
    win_scr[HEAD_ROWS:2 * HEAD_ROWS, :] = pa[0:HEAD_ROWS]
    acc = pa[0:HEAD_ROWS] * wconv_ref[CONV_K - 1:CONV_K, :]
    for j in range(CONV_K - 1):
        acc = acc + win_scr[pl.ds(HEAD_ROWS - (CONV_K - 1) + j, HEAD_ROWS), :] * wconv_ref[j:j + 1, :]
    qkva_ref[0:HEAD_ROWS, :] = (acc * _sigmoid(acc)).astype(qkva_ref.dtype)
    win_scr[0:HEAD_ROWS, :] = pa[tm - HEAD_ROWS:tm]

    c0 = GDN_CONV_W
    z_ref[...] = _dot(hb, w_ref[:, c0:c0 + GDN_V_W]).astype(z_ref.dtype)
    c0 += GDN_V_W
    ba_ref[...] = _dot(hb, w_ref[:, c0:c0 + SMALL_W])
    c0 += SMALL_W
    pb = _dot(hb, w_ref[:, c0:c0 + 3 * ATTN_W])
    ncol = 3 * ATTN_W // LANES
    for j in range(ncol):
        pb_scr[j] = pb[:, j * LANES:(j + 1) * LANES]
    prev_dil, prev_scr, next_scr = 1, pb_scr, pb_next_scr
    for bi, ((_, dil), ref) in enumerate(zip(DILATED_PAIRS, qkvb_refs)):
        if dil == 1:
            ref[...] = pb.astype(ref.dtype)
            continue
        f = dil // prev_dil
        n = tm // dil
        keep = bi + 1 < len(DILATED_PAIRS)
        for r in range(prev_dil):
            for q in range(f):
                r_new = r + prev_dil * q
                for j in range(ncol):
                    rows = prev_scr[j, pl.ds(r * (tm // prev_dil) + q, n, stride=f), :]
                    c1 = r_new * 3 * ATTN_W + j * LANES
                    ref[:, c1:c1 + LANES] = rows.astype(ref.dtype)
                    if keep:
                        next_scr[j, r_new * n:(r_new + 1) * n, :] = rows
        prev_dil, prev_scr, next_scr = dil, next_scr, prev_scr


def _inproj(x2, sc, sh, g, wcat, wconv, seq):
    t, d = x2.shape
    tm = TM_PROJ
    tps = seq // tm
    mod_spec = pl.BlockSpec((1, 1, d), lambda i: (i // tps, 0, 0))
    return pl.pallas_call(
        functools.partial(_inproj_kernel, tps),
        grid=(t // tm,),
        in_specs=[pl.BlockSpec((tm, d), lambda i: (i, 0)),
                  mod_spec, mod_spec,
                  pl.BlockSpec((1, d), lambda i: (0, 0)),
                  pl.BlockSpec((d, W_COLS), lambda i: (0, 0)),
                  pl.BlockSpec((CONV_K, GDN_CONV_W), lambda i: (0, 0)),
                  pl.BlockSpec((CONV_K - 1, CONV_BLK, 2 * CONV_BLK), lambda i: (0, 0, 0))],
        out_specs=[pl.BlockSpec((tm, GDN_CONV_W), lambda i: (i, 0)),
                   pl.BlockSpec((tm, GDN_V_W), lambda i: (i, 0)),
                   pl.BlockSpec((tm, SMALL_W), lambda i: (i, 0))]
                  + [pl.BlockSpec((tm // dil, dil * 3 * ATTN_W), lambda i: (i, 0)) for _, dil in DILATED_PAIRS],
        out_shape=[jax.ShapeDtypeStruct((t, GDN_CONV_W), BF16),
                   jax.ShapeDtypeStruct((t, GDN_V_W), BF16),
                   jax.ShapeDtypeStruct((t, SMALL_W), F32)]
                  + [jax.ShapeDtypeStruct((t // dil, dil * 3 * ATTN_W), BF16) for _, dil in DILATED_PAIRS],
        scratch_shapes=[pltpu.VMEM((2 * HEAD_ROWS, GDN_CONV_W), F32),
                        pltpu.VMEM((3 * ATTN_W // LANES, tm, LANES), F32),
                        pltpu.VMEM((3 * ATTN_W // LANES, tm, LANES), F32)],
        compiler_params=_params(("arbitrary",)),
        name="inproj",
    )(x2, sc, sh, g, wcat, wconv, _shift_bands())


def _shift_bands():
    t = np.arange(CONV_BLK)[:, None]
    c = np.arange(2 * CONV_BLK)[None, :]
    bands = [(c == CONV_BLK + t - (CONV_K - 1 - j)) for j in range(CONV_K - 1)]
    return jnp.asarray(np.stack(bands).astype(np.float32), dtype=BF16)


def _softplus(x):
    return jnp.maximum(x, 0.0) + jnp.log(1.0 + jnp.exp(-jnp.abs(x)))


def _unit_lower_inverses(n_list):
    c = n_list[0].shape[0]
    row = lax.broadcasted_iota(jnp.int32, (c, c), 0)
    col = lax.broadcasted_iota(jnp.int32, (c, c), 1)
    eye = jnp.where(row == col, 1.0, 0.0)
    same_base = (row // INV_BASE) == (col // INV_BASE)
    n0 = [jnp.where(same_base, n, 0.0) for n in n_list]
    p = [eye - x for x in n0]
    m = [x.astype(BF16) for x in n0]
    for j in range(int(math.log2(INV_BASE)) - 1):
        m = [_dot(x, x).astype(BF16) for x in m]
        p = [x + _dot(x.astype(BF16), y) for x, y in zip(p, m)]
    size = INV_BASE
    while size < c:
        sibling = jnp.logical_and((row // size) % 2 == 1, (col // size) == (row // size) - 1)
        cb = [jnp.where(sibling, n, 0.0).astype(BF16) for n in n_list]
        pb = [x.astype(BF16) for x in p]
        pc = [_dot(x, y).astype(BF16) for x, y in zip(pb, cb)]
        p = [x - _dot(y, z) for x, y, z in zip(p, pc, pb)]
        size *= 2
    return p


def _split3_bf16(x):
    h1 = x.astype(BF16)
    r1 = x - h1.astype(F32)
    h2 = r1.astype(BF16)
    h3 = (r1 - h2.astype(F32)).astype(BF16)
    return h1, h2, h3


def _gdn_kernel(steps_per_seq, qkv_ref, ba_ref, bat_ref, gp_ref, gpt_ref, lblk_ref, ublk_ref,
                o_ref, state_scr):
    i = pl.program_id(0)

    @pl.when(i % steps_per_seq == 0)
    def _():
        state_scr[...] = jnp.zeros_like(state_scr)

    nchunk = GDN_TOK // CHUNK
    ba = ba_ref[...]
    bat = bat_ref[...]
    a_vec = -jnp.exp(gp_ref[0:1, :])
    g_tile = a_vec * _softplus(ba + gp_ref[1:2, :])
    lblk = lblk_ref[...]
    g_cum = sum(_dot(lblk, part) for part in _split3_bf16(g_tile))
    a_col = -jnp.exp(gpt_ref[:, 0:1])
    gt_tile = a_col * _softplus(bat + gpt_ref[:, 1:2])
    ublk = ublk_ref[...]
    gt_cum = sum(_dot(part, ublk) for part in _split3_bf16(gt_tile))
    beta_tile = _sigmoid(ba)

    row = lax.broadcasted_iota(jnp.int32, (CHUNK, CHUNK), 0)
    col = lax.broadcasted_iota(jnp.int32, (CHUNK, CHUNK), 1)
    incl = row >= col
    strict = row > col

    for h0 in range(0, GDN_HEADS, GDN_GROUP):
        heads = range(h0, h0 + GDN_GROUP)
        kb_l, k_l, q_l, rhs_l, qg_l, kg_l, gl_l, inc_l, str_l = ([] for _ in range(9))
        for h in heads:
            q_raw = qkv_ref[:, h * GDN_DK:(h + 1) * GDN_DK].astype(F32)
            k_raw = qkv_ref[:, GDN_QK_W + h * GDN_DK:GDN_QK_W + (h + 1) * GDN_DK].astype(F32)
            v = qkv_ref[:, 2 * GDN_QK_W + h * GDN_DV:2 * GDN_QK_W + (h + 1) * GDN_DV].astype(F32)
            qn = q_raw * lax.rsqrt(jnp.sum(q_raw * q_raw, axis=-1, keepdims=True) + EPS) * (GDN_DK ** -0.5)
            kn = k_raw * lax.rsqrt(jnp.sum(k_raw * k_raw, axis=-1, keepdims=True) + EPS)
            beta = beta_tile[:, h:h + 1]
            gc_all = g_cum[:, GDN_HEADS + h:GDN_HEADS + h + 1]
            eg = jnp.exp(gc_all)
            kb = kn * beta
            rhs = jnp.concatenate([v * beta, kb * eg], axis=1).astype(BF16)
            qg = qn * eg
            kb16, k16, q16 = kb.astype(BF16), kn.astype(BF16), qn.astype(BF16)
            for c in range(nchunk):
                sl = slice(c * CHUNK, (c + 1) * CHUNK)
                g_c = gc_all[sl]
                g_r = gt_cum[GDN_HEADS + h:GDN_HEADS + h + 1, sl]
                dec = jnp.exp(g_c - g_r)
                inc_l.append(jnp.where(incl, dec, 0.0))
                str_l.append(jnp.where(strict, dec, 0.0))
                g_last = g_c[CHUNK - 1:CHUNK, :]
                kg_l.append((kn[sl] * jnp.exp(g_last - g_c)).astype(BF16))
                gl_l.append(jnp.exp(g_last))
                kb_l.append(kb16[sl])
                k_l.append(k16[sl])
                q_l.append(q16[sl])
                rhs_l.append(rhs[sl])
                qg_l.append(qg[sl])
        nprob = len(k_l)
        n_l = [_dot_nt(kb_l[j], k_l[j]) * str_l[j] for j in range(nprob)]
        aqk_l = [(_dot_nt(q_l[j], k_l[j]) * inc_l[j]).astype(BF16) for j in range(nprob)]
        tinv_l = _unit_lower_inverses(n_l)
        sol_l = [_dot(tinv_l[j].astype(BF16), rhs_l[j]).astype(BF16) for j in range(nprob)]
        a2_l = [_dot(aqk_l[j], sol_l[j]) for j in range(nprob)]
        k2_l = [_dot_tn(kg_l[j], sol_l[j]) for j in range(nprob)]
        states = [state_scr[h] for h in heads]
        for c in range(nchunk):
            sl = slice(c * CHUNK, (c + 1) * CHUNK)
            for hi, h in enumerate(heads):
                j = hi * nchunk + c
                sb = states[hi].astype(BF16)
                qeff = (qg_l[j] - a2_l[j][:, GDN_DV:]).astype(BF16)
                o = _dot(qeff, sb) + a2_l[j][:, :GDN_DV]
                states[hi] = gl_l[j] * states[hi] + k2_l[j][:, :GDN_DV] - _dot(k2_l[j][:, GDN_DV:].astype(BF16), sb)
                o_ref[sl, h * GDN_DV:(h + 1) * GDN_DV] = o.astype(o_ref.dtype)
        for hi, h in enumerate(heads):
            state_scr[h] = states[hi]


def _gdn(qkva, ba, bat, gp, gpt, lblk, ublk, seq):
    t = qkva.shape[0]
    tok = GDN_TOK
    sps = seq // tok
    const = lambda i: (0, 0)
    return pl.pallas_call(
        functools.partial(_gdn_kernel, sps),
        grid=(t // tok,),
        in_specs=[pl.BlockSpec((tok, GDN_CONV_W), lambda i: (i, 0)),
                  pl.BlockSpec((tok, SMALL_W), lambda i: (i, 0)),
                  pl.BlockSpec((8, tok), lambda i: (0, i)),
                  pl.BlockSpec((8, LANES), const),
                  pl.BlockSpec((8, LANES), const),
                  pl.BlockSpec((tok, tok), const),
                  pl.BlockSpec((tok, tok), const)],
        out_specs=pl.BlockSpec((tok, GDN_V_W), lambda i: (i, 0)),
        out_shape=jax.ShapeDtypeStruct((t, GDN_V_W), F32),
        scratch_shapes=[pltpu.VMEM((GDN_HEADS, GDN_DK, GDN_DV), F32)],
        compiler_params=_params(("arbitrary",)),
        name="gdn",
    )(qkva, ba, bat, gp, gpt, lblk, ublk)


def _attn_kernel(cur_ref, prev_ref, bucket_ref, rb_ref, o_ref, lse_ref, bias_ref):
    n = pl.program_id(2)

    @pl.when(jnp.logical_and(jnp.logical_and(pl.program_id(0) == 0, pl.program_id(1) == 0), n == 0))
    def _():
        bucket = bucket_ref[...]
        col = lax.broadcasted_iota(jnp.int32, (QBLK, 2 * QBLK), 1)
        for h in range(ATTN_HEADS):
            acc = jnp.full((QBLK, 2 * QBLK), NEG_INF, F32)
            for b in range(NUM_BUCKETS):
                acc = jnp.where(bucket == b, rb_ref[b, h], acc)
            bias_ref[0, h] = acc
            bias_ref[1, h] = jnp.where(col < QBLK, NEG_INF, acc)

    nsub = cur_ref.shape[1] // QBLK
    nres = cur_ref.shape[2] // (3 * ATTN_W)
    first = jnp.where(n == 0, 1, 0)
    w = ATTN_W
    q_all = [cur_ref[0, :, c * 3 * w:c * 3 * w + w] for c in range(nres)]
    k_all = [jnp.concatenate([prev_ref[0, :, c * 3 * w + w:c * 3 * w + 2 * w],
                              cur_ref[0, :, c * 3 * w + w:c * 3 * w + 2 * w]], axis=0) for c in range(nres)]
    v_all = [jnp.concatenate([prev_ref[0, :, c * 3 * w + 2 * w:(c + 1) * 3 * w],
                              cur_ref[0, :, c * 3 * w + 2 * w:(c + 1) * 3 * w]], axis=0) for c in range(nres)]
    lane = lax.broadcasted_iota(jnp.int32, (QBLK, LANES), 1)
    ones = jnp.ones((2 * QBLK, LANES), BF16)
    per_group = LANES // ATTN_DH
    ngrp = ATTN_HEADS // per_group
    scale = ATTN_DH ** -0.5
    mine = [jnp.logical_and(lane >= j * ATTN_DH, lane < (j + 1) * ATTN_DH) for j in range(per_group)]
    probs = [(c, u, h) for c in range(nres) for u in range(nsub) for h in range(ATTN_HEADS)]
    s_l = []
    for c, u, h in probs:
        grp, j = divmod(h, per_group)
        gs = slice(grp * LANES, (grp + 1) * LANES)
        qg = q_all[c][u * QBLK:(u + 1) * QBLK, gs] * scale
        table = first if u == 0 else 0
        s_l.append(_dot_nt(jnp.where(mine[j], qg, jnp.zeros_like(qg)), k_all[c][u * QBLK:(u + 2) * QBLK, gs])
                   + bias_ref[table, h])
    m_l = [jnp.max(s, axis=-1, keepdims=True) for s in s_l]
    p_l = [jnp.exp(s - m).astype(BF16) for s, m in zip(s_l, m_l)]
    v_ext = {(c, u, grp): jnp.concatenate(
                 [v_all[c][u * QBLK:(u + 2) * QBLK, grp * LANES:(grp + 1) * LANES], ones], axis=1)
             for c in range(nres) for u in range(nsub) for grp in range(ngrp)}
    pv_l = [_dot(p, v_ext[(c, u, h // per_group)]) for p, (c, u, h) in zip(p_l, probs)]
    for c in range(nres):
        for u in range(nsub):
            outs = []
            lse_tile = jnp.zeros((QBLK, LANES), F32)
            for grp in range(ngrp):
                o_grp = jnp.zeros((QBLK, LANES), F32)
                for j in range(per_group):
                    h = grp * per_group + j
                    idx = (c * nsub + u) * ATTN_HEADS + h
                    den = pv_l[idx][:, LANES:]
                    o_grp = jnp.where(mine[j], pv_l[idx][:, :LANES] / den, o_grp)
                    lse_tile = jnp.where(lane == h, m_l[idx] + jnp.log(den[:, 0:1]), lse_tile)
                outs.append(o_grp)
            rs = slice(u * QBLK, (u + 1) * QBLK)
            o_ref[0, rs, c * w:(c + 1) * w] = jnp.concatenate(outs, axis=1).astype(o_ref.dtype)
            lse_ref[0, rs, c * LANES:(c + 1) * LANES] = lse_tile


def _attn_branch(qkvb, bucket, rel_bias, bsz, seq, dil):
    ln = seq // dil
    sub = min(ATTN_SUB, ln // QBLK)
    rows = sub * QBLK
    assert seq % dil == 0 and ln % rows == 0, (seq, dil, rows)
    nb = ln // rows
    nres = min(max(ATTN_SUB // sub, 1), dil)
    assert dil % nres == 0, (dil, nres)
    xv = qkvb.reshape(bsz, ln, dil * 3 * ATTN_W)
    w = ATTN_W
    o, lse = pl.pallas_call(
        _attn_kernel,
        grid=(bsz, dil // nres, nb),
        in_specs=[pl.BlockSpec((1, rows, nres * 3 * w), lambda b, r, n: (b, n, r)),
                  pl.BlockSpec((1, QBLK, nres * 3 * w), lambda b, r, n: (b, jnp.maximum(n * sub - 1, 0), r)),
                  pl.BlockSpec((QBLK, 2 * QBLK), lambda b, r, n: (0, 0)),
                  pl.BlockSpec(memory_space=pltpu.SMEM)],
        out_specs=[pl.BlockSpec((1, rows, nres * w), lambda b, r, n: (b, n, r)),
                   pl.BlockSpec((1, rows, nres * LANES), lambda b, r, n: (b, n, r))],
        out_shape=[jax.ShapeDtypeStruct((bsz, ln, dil * w), BF16),
                   jax.ShapeDtypeStruct((bsz, ln, dil * LANES), F32)],
        scratch_shapes=[pltpu.VMEM((2, ATTN_HEADS, QBLK, 2 * QBLK), F32)],
        compiler_params=_params(("arbitrary", "arbitrary", "arbitrary")),
        name=f"attn_d{dil}",
    )(xv, xv, bucket, rel_bias)
    return o.reshape(bsz * ln, dil * w), lse.reshape(bsz * ln, dil * LANES)


def _bucket_table(window, dil):
    steps = window // dil
    qi = np.arange(QBLK)[:, None]
    kj = np.arange(2 * QBLK)[None, :]
    delta = qi + QBLK - kj
    dist = np.maximum(delta, 0) * dil
    max_exact = NUM_BUCKETS // 2
    d_f = np.maximum(dist, 1).astype(np.float32)
    large = max_exact + (np.log(d_f / max_exact) / math.log(REL_MAX_DIST / max_exact)
                         * (NUM_BUCKETS - max_exact)).astype(np.int32)
    bucket = np.where(dist < max_exact, dist, np.minimum(large, NUM_BUCKETS - 1))
    return np.where((delta >= 0) & (delta <= steps), bucket, -1).astype(np.int32)


def _split_bf16(x):
    hi = x.astype(BF16)
    lo = (x - hi.astype(F32)).astype(BF16)
    return hi, lo


def _pack_halves(x):
    w = x.shape[1] // 2
    lo = lax.bitcast_convert_type(x[:, :w].astype(BF16).astype(F32), U32)
    hi = lax.bitcast_convert_type(x[:, w:].astype(BF16).astype(F32), U32)
    return (hi & jnp.uint32(0xFFFF0000)) | (lo >> 16)


def _unpack_halves(u):
    lo = lax.bitcast_convert_type(u << 16, F32)
    hi = lax.bitcast_convert_type(u & jnp.uint32(0xFFFF0000), F32)
    return lo, hi


def _post_kernel(og_ref, z_ref, o1_ref, o2_ref, o3_ref, l1_ref, l2_ref, l3_ref, x_ref,
                 gta_ref, scf_ref, shf_ref, gnorm_ref, anorm_ref, wout_ref, nffn_ref,
                 wrh_ref, exp_ref, ls_ref,
                 x1_ref, h2_ref, route_ref, cnt_ref, carry_scr, *tok_scr):
    i = pl.program_id(0)
    tm = x_ref.shape[0]

    @pl.when(i == 0)
    def _():
        carry_scr[...] = jnp.zeros_like(carry_scr)

    nbr = len(DILATED_PAIRS)
    ob_scr, lse_scr = tok_scr[:nbr], tok_scr[nbr:]
    for (_, dil), o_ref, l_ref, o_s, l_s in zip(DILATED_PAIRS, (o1_ref, o2_ref, o3_ref),
                                                 (l1_ref, l2_ref, l3_ref), ob_scr, lse_scr):
        for r in range(dil):
            rows = pl.ds(r, tm // dil, stride=dil) if dil > 1 else slice(None)
            for j in range(ATTN_W // LANES):
                c1 = r * ATTN_W + j * LANES
                o_s[j, rows, :] = o_ref[:, c1:c1 + LANES].astype(F32)
            l_s[rows, :] = l_ref[:, r * LANES:(r + 1) * LANES]

    heads = []
    for h in range(GDN_HEADS):
        hs = slice(h * GDN_DV, (h + 1) * GDN_DV)
        seg = og_ref[:, hs]
        nrm = seg * lax.rsqrt(jnp.mean(seg * seg, axis=-1, keepdims=True) + EPS) * gnorm_ref[...]
        zz = z_ref[:, hs].astype(F32)
        heads.append((nrm * (zz * _sigmoid(zz))).astype(BF16))
    oa = jnp.concatenate(heads, axis=1)

    l1, l2, l3 = (l_s[...] for l_s in lse_scr)
    m = jnp.maximum(jnp.maximum(l1, l2), l3)
    e1, e2, e3 = jnp.exp(l1 - m), jnp.exp(l2 - m), jnp.exp(l3 - m)
    inv = 1.0 / (e1 + e2 + e3)
    ob = jnp.zeros((tm, ATTN_W), F32)
    for e, o_s in zip((e1, e2, e3), ob_scr):
        hi, lo = _split_bf16(e * inv)
        wexp = _dot(hi, exp_ref[...]) + _dot(lo, exp_ref[...])
        ob = ob + wexp * jnp.concatenate([o_s[j] for j in range(ATTN_W // LANES)], axis=1)
    ob = ob * lax.rsqrt(jnp.mean(ob * ob, axis=-1, keepdims=True) + EPS) * anorm_ref[...]

    mix = _dot(oa, wout_ref[0:GDN_V_W, :]) + _dot(ob.astype(BF16), wout_ref[GDN_V_W:, :])
    x1 = x_ref[...] + gta_ref[0] * mix
    x1_ref[...] = x1
    h2 = x1 * lax.rsqrt(jnp.mean(x1 * x1, axis=-1, keepdims=True) + EPS) * nffn_ref[...]
    h2 = h2 * (1.0 + scf_ref[0]) + shf_ref[0]
    h2_ref[...] = _pack_halves(h2)

    hh, hl = _split_bf16(h2)
    both = _dot(hh, wrh_ref[...])
    logits = both[:, :LANES] + both[:, LANES:] + _dot(hl, wrh_ref[:, :LANES])
    tm = logits.shape[0]
    lane = lax.broadcasted_iota(jnp.int32, (tm, LANES), 1).astype(F32)
    big = float(LANES)
    gmask = lane < N_GROUPS
    glog = jnp.where(gmask, logits, NEG_INF)
    gmax = jnp.max(glog, axis=-1, keepdims=True)
    gidx = jnp.min(jnp.where(jnp.logical_and(gmask, glog == gmax), lane, big), axis=-1, keepdims=True)
    gprob = 1.0 / jnp.sum(jnp.where(gmask, jnp.exp(glog - gmax), 0.0), axis=-1, keepdims=True)
    lo_lane = N_GROUPS + EXPERTS_PER_GROUP * gidx
    emask = jnp.logical_and(lane >= lo_lane, lane < lo_lane + EXPERTS_PER_GROUP)
    elog = jnp.where(emask, logits, NEG_INF)
    m1 = jnp.max(elog, axis=-1, keepdims=True)
    i1 = jnp.min(jnp.where(jnp.logical_and(emask, elog == m1), lane, big), axis=-1, keepdims=True)
    emask2 = jnp.logical_and(emask, lane != i1)
    elog2 = jnp.where(emask2, logits, NEG_INF)
    m2 = jnp.max(elog2, axis=-1, keepdims=True)
    i2 = jnp.min(jnp.where(jnp.logical_and(emask2, elog2 == m2), lane, big), axis=-1, keepdims=True)
    r = jnp.exp(m2 - m1)
    gate1 = gprob / (1.0 + r)
    gate2 = gprob * r / (1.0 + r)
    ex1 = i1 - N_GROUPS
    ex2 = i2 - N_GROUPS

    hit1 = lane == ex1
    hit2 = lane == ex2
    onehot = jnp.where(jnp.logical_or(hit1, hit2), 1.0, 0.0)
    pref = _dot(ls_ref[...], onehot.astype(BF16)) + carry_scr[...]
    rank1 = jnp.sum(jnp.where(hit1, pref, 0.0), axis=-1, keepdims=True)
    rank2 = jnp.sum(jnp.where(hit2, pref, 0.0), axis=-1, keepdims=True)
    carry = carry_scr[...] + jnp.sum(onehot, axis=0, keepdims=True)
    carry_scr[...] = carry
    cnt_ref[...] = jnp.broadcast_to(carry, cnt_ref.shape)

    route = jnp.zeros((tm, LANES), F32)
    for idx, val in enumerate((ex1, ex2, gate1, gate2, rank1, rank2)):
        route = jnp.where(lane == idx, val, route)
    route_ref[...] = route


def _post(og, z, os_, ls_, x2, gta, scf, shf, gnorm, anorm, wout, nffn, wrh, expand, lstrict, seq):
    t, d = x2.shape
    tm = TM_POST
    tps = seq // tm
    tile = lambda w: pl.BlockSpec((tm, w), lambda i: (i, 0))
    const2 = lambda a: pl.BlockSpec(a.shape, lambda i: (0, 0))
    mod_spec = pl.BlockSpec((1, 1, d), lambda i: (i // tps, 0, 0))
    return pl.pallas_call(
        _post_kernel,
        grid=(t // tm,),
        in_specs=[tile(GDN_V_W), tile(GDN_V_W)]
                 + [pl.BlockSpec((tm // dil, dil * ATTN_W), lambda i: (i, 0)) for _, dil in DILATED_PAIRS]
                 + [pl.BlockSpec((tm // dil, dil * LANES), lambda i: (i, 0)) for _, dil in DILATED_PAIRS]
                 + [tile(d),
                  mod_spec, mod_spec, mod_spec,
                  const2(gnorm), const2(anorm), const2(wout), const2(nffn),
                  const2(wrh), const2(expand), const2(lstrict)],
        out_specs=[tile(d), tile(d // 2), tile(LANES), pl.BlockSpec((8, LANES), lambda i: (0, 0))],
        out_shape=[jax.ShapeDtypeStruct((t, d), F32),
                   jax.ShapeDtypeStruct((t, d // 2), U32),
                   jax.ShapeDtypeStruct((t, LANES), F32),
                   jax.ShapeDtypeStruct((8, LANES), F32)],
        scratch_shapes=[pltpu.VMEM((1, LANES), F32)]
                       + [pltpu.VMEM((ATTN_W // LANES, tm, LANES), F32) for _ in DILATED_PAIRS]
                       + [pltpu.VMEM((tm, LANES), F32) for _ in DILATED_PAIRS],
        compiler_params=_params(("arbitrary",)),
        name="post",
    )(og, z, *os_, *ls_, x2, gta, scf, shf, gnorm, anorm, wout, nffn, wrh, expand, lstrict)


def _dispatch_kernel(dest_ref, fill_ref, pad_ref, nused_ref, h2_ref, xs_ref, zero_scr, sem, fill_sem):
    i = pl.program_id(0)
    tm = h2_ref.shape[0]

    @pl.when(i == 0)
    def _():
        zero_scr[...] = jnp.zeros_like(zero_scr)

        def pieces(e, act):
            base, pad = fill_ref[e], pad_ref[e]
            head = pad & (SUBLANES - 1)
            for j in range(SUBLANES - 1):
                @pl.when(j < head)
                def _():
                    act(pltpu.make_async_copy(zero_scr.at[pl.ds(0, 1)], xs_ref.at[pl.ds(base + j, 1)],
                                              fill_sem))
            off = base + head
            for bit in range(SUBLANES.bit_length() - 1, EXPERT_BLK.bit_length() - 1):
                size = 1 << bit

                @pl.when((pad >> bit) & 1 == 1)
                def _():
                    act(pltpu.make_async_copy(zero_scr.at[pl.ds(0, size)],
                                              xs_ref.at[pl.ds(pl.multiple_of(off, SUBLANES), size)], fill_sem))
                off = off + (pad & size)

        def fill(e, carry):
            pieces(e, lambda cp: cp.start())
            return carry

        def drain(e, carry):
            pieces(e, lambda cp: cp.wait())
            return carry

        lax.fori_loop(0, N_EXPERTS, fill, 0)
        lax.fori_loop(0, N_EXPERTS, drain, 0)

        def tail(b, act):
            for half in range(EXPERT_BLK // zero_scr.shape[0]):
                row0 = pl.multiple_of(b * EXPERT_BLK + half * zero_scr.shape[0], SUBLANES)
                act(pltpu.make_async_copy(zero_scr, xs_ref.at[pl.ds(row0, zero_scr.shape[0])], fill_sem))

        def tail_fill(b, carry):
            tail(b, lambda cp: cp.start())
            return carry

        def tail_drain(b, carry):
            tail(b, lambda cp: cp.wait())
            return carry

        nblk = xs_ref.shape[0] // EXPERT_BLK
        lax.fori_loop(nused_ref[0], nblk, tail_fill, 0)
        lax.fori_loop(nused_ref[0], nblk, tail_drain, 0)

    def start(r, carry):
        for k in range(TOP_K):
            d = dest_ref[(i * tm + r) * TOP_K + k]
            pltpu.make_async_copy(h2_ref.at[pl.ds(r, 1)], xs_ref.at[pl.ds(d, 1)], sem).start()
        return carry

    lax.fori_loop(0, tm, start, 0, unroll=ROW_UNROLL)
    for k in range(TOP_K):
        pltpu.make_async_copy(h2_ref, xs_ref.at[pl.ds(0, tm)], sem).wait()


def _dispatch(dest, fill_start, pad_rows, nused, h2, p):
    t, d = h2.shape
    tm = TM_ROWS
    grid_spec = pltpu.PrefetchScalarGridSpec(
        num_scalar_prefetch=4,
        grid=(t // tm,),
        in_specs=[pl.BlockSpec((tm, d), lambda i, *_: (i, 0))],
        out_specs=pl.BlockSpec(memory_space=pl.ANY),
        scratch_shapes=[pltpu.VMEM((EXPERT_BLK // 2, d), h2.dtype), pltpu.SemaphoreType.DMA,
                        pltpu.SemaphoreType.DMA],
    )
    return pl.pallas_call(
        _dispatch_kernel,
        grid_spec=grid_spec,
        out_shape=jax.ShapeDtypeStruct((p, d), h2.dtype),
        compiler_params=_params(("arbitrary",)),
        name="dispatch",
    )(dest, fill_start, pad_rows, nused, h2)


def _expert_kernel(blk_e_ref, nused_ref, xs_ref, wg_ref, wu_ref, wd_ref, ys_ref, wg16, wu16, wd16):
    b = pl.program_id(0)
    used = b < nused_ref[0]

    new_expert = jnp.logical_or(b == 0, blk_e_ref[b] != blk_e_ref[jnp.maximum(b - 1, 0)])

    @pl.when(jnp.logical_and(used, new_expert))
    def _():
        wg16[...] = wg_ref[0].astype(BF16)
        wu16[...] = wu_ref[0].astype(BF16)
        wd16[...] = wd_ref[0].astype(BF16)

    @pl.when(used)
    def _():
        half = xs_ref.shape[1]
        x_lo, x_hi = (v.astype(BF16) for v in _unpack_halves(xs_ref[...]))
        g = _dot(x_lo, wg16[0:half, :]) + _dot(x_hi, wg16[half:, :])
        u = _dot(x_lo, wu16[0:half, :]) + _dot(x_hi, wu16[half:, :])
        hid = (g * _sigmoid(g)) * u
        ys_ref[...] = _pack_halves(_dot(hid.astype(BF16), wd16[...]))

    @pl.when(b >= nused_ref[0])
    def _():
        ys_ref[...] = jnp.zeros_like(ys_ref)


def _experts(blk_e, nused, xs, wg, wu, wd):
    p, dh = xs.shape
    d = wg.shape[1]
    assert d == 2 * dh, (d, dh)
    blk = EXPERT_BLK
    row_map = lambda b, be, nu: (jnp.minimum(b, nu[0] - 1), 0)
    w_map = lambda b, be, nu: (be[b], 0, 0)
    grid_spec = pltpu.PrefetchScalarGridSpec(
        num_scalar_prefetch=2,
        grid=(p // blk,),
        in_specs=[pl.BlockSpec((blk, dh), row_map),
                  pl.BlockSpec((1, d, D_EXPERT), w_map),
                  pl.BlockSpec((1, d, D_EXPERT), w_map),
                  pl.BlockSpec((1, D_EXPERT, d), w_map)],
        out_specs=pl.BlockSpec((blk, dh), lambda b, be, nu: (b, 0)),
        scratch_shapes=[pltpu.VMEM((d, D_EXPERT), BF16), pltpu.VMEM((d, D_EXPERT), BF16),
                        pltpu.VMEM((D_EXPERT, d), BF16)],
    )
    return pl.pallas_call(
        _expert_kernel,
        grid_spec=grid_spec,
        out_shape=jax.ShapeDtypeStruct((p, dh), U32),
        compiler_params=_params(("arbitrary",)),
        name="experts",
    )(blk_e, nused, xs, wg, wu, wd)


def _combine_kernel(final_norm, dest_ref, ys_ref, x1_ref, route_ref, gtf_ref, nf_ref, o_ref, ybuf, sem):
    i = pl.program_id(0)
    tm = x1_ref.shape[0]
    slot = i % 2

    def gather_tile(tile, slot_):
        def start(r, carry):
            for k in range(TOP_K):
                d = dest_ref[(tile * tm + r) * TOP_K + k]
                pltpu.make_async_copy(ys_ref.at[pl.ds(d, 1)], ybuf.at[slot_, k, pl.ds(r, 1)],
                                      sem.at[slot_]).start()
            return carry
        lax.fori_loop(0, tm, start, 0, unroll=ROW_UNROLL)

    @pl.when(i == 0)
    def _():
        gather_tile(0, 0)

    @pl.when(i + 1 < pl.num_programs(0))
    def _():
        gather_tile(i + 1, 1 - slot)

    for k in range(TOP_K):
        pltpu.make_async_copy(ys_ref.at[pl.ds(0, tm)], ybuf.at[slot, k], sem.at[slot]).wait()
    route = route_ref[...]
    lo0, hi0 = _unpack_halves(ybuf[slot, 0])
    lo1, hi1 = _unpack_halves(ybuf[slot, 1])
    g0, g1 = route[:, 2:3], route[:, 3:4]
    moe = jnp.concatenate([lo0 * g0 + lo1 * g1, hi0 * g0 + hi1 * g1], axis=1)
    x2 = x1_ref[...] + gtf_ref[0] * moe
    if final_norm:
        x2 = x2 * lax.rsqrt(jnp.mean(x2 * x2, axis=-1, keepdims=True) + EPS) * nf_ref[...]
    o_ref[...] = x2


def _combine(dest, ys, x1, route, gtf, nf, seq, final_norm):
    t, d = x1.shape
    tm = TM_ROWS
    tps = seq // tm
    grid_spec = pltpu.PrefetchScalarGridSpec(
        num_scalar_prefetch=1,
        grid=(t // tm,),
        in_specs=[pl.BlockSpec(memory_space=pl.ANY),
                  pl.BlockSpec((tm, d), lambda i, dest: (i, 0)),
                  pl.BlockSpec((tm, LANES), lambda i, dest: (i, 0)),
                  pl.BlockSpec((1, 1, d), lambda i, dest: (i // tps, 0, 0)),
                  pl.BlockSpec((1, d), lambda i, dest: (0, 0))],
        out_specs=pl.BlockSpec((tm, d), lambda i, dest: (i, 0)),
        scratch_shapes=[pltpu.VMEM((2, TOP_K, tm, ys.shape[1]), ys.dtype), pltpu.SemaphoreType.DMA((2,))],
    )
    return pl.pallas_call(
        functools.partial(_combine_kernel, final_norm),
        grid_spec=grid_spec,
        out_shape=jax.ShapeDtypeStruct((t, d), F32),
        compiler_params=_params(("arbitrary",)),
        name="combine",
    )(dest, ys, x1, route, gtf, nf)


def _block_tri(n, chunk, lower):
    r = jnp.arange(n)[:, None]
    c = jnp.arange(n)[None, :]
    same = (r // chunk) == (c // chunk)
    tri = (r >= c) if lower else (r <= c)
    return jnp.where(same & tri, 1.0, 0.0).astype(BF16)


def _layer(x2, c, bsz, seq, w_ada, b_ada, norm_mix, w_in, w_conv, a_log, dt_bias, gdn_norm,
           attn_norm, w_out, rel_bias, norm_ffn, w_rg, w_re, w_gate, w_up, w_down):
    t, d = x2.shape
    mod = _adaln(c, w_ada, b_ada)
    sh_a, sc_a, gt_a, sh_f, sc_f, gt_f = [m.reshape(bsz, 1, d) for m in jnp.split(mod, 6, axis=-1)]

    s1 = GDN_CONV_W
    s2 = s1 + GDN_V_W
    s4 = s2 + 2 * GDN_HEADS
    small = jnp.pad(w_in[:, s2:s4], ((0, 0), (0, SMALL_W - 2 * GDN_HEADS)))
    wcat = jnp.concatenate([w_in[:, :s2], small, w_in[:, s4:]], axis=1).astype(BF16)
    qkva, z, ba, *qkvb = _inproj(x2, sc_a, sh_a, norm_mix.reshape(1, d), wcat, w_conv, seq)

    pad4 = lambda v: jnp.pad(v.astype(F32), (GDN_HEADS, LANES - 2 * GDN_HEADS))
    gp = jnp.zeros((8, LANES), F32).at[0].set(pad4(a_log)).at[1].set(pad4(dt_bias))
    og = _gdn(qkva, ba, ba[:, :8].T, gp, _gp_cols(a_log, dt_bias),
              _block_tri(GDN_TOK, CHUNK, True), _block_tri(GDN_TOK, CHUNK, False), seq)

    outs, lses = [], []
    for (window, dil), qkvb_d in zip(DILATED_PAIRS, qkvb):
        o_i, lse_i = _attn_branch(qkvb_d, jnp.asarray(_bucket_table(window, dil)), rel_bias.astype(F32),
                                  bsz, seq, dil)
        outs.append(o_i)
        lses.append(lse_i)

    expand = jnp.where((jnp.arange(LANES)[:, None] == jnp.arange(ATTN_W)[None, :] // ATTN_DH), 1.0, 0.0).astype(BF16)
    lstrict = jnp.where(jnp.arange(TM_POST)[:, None] > jnp.arange(TM_POST)[None, :], 1.0, 0.0).astype(BF16)
    wr = jnp.pad(jnp.concatenate([w_rg, w_re], axis=1).astype(F32), ((0, 0), (0, LANES - N_GROUPS - N_EXPERTS)))
    wrh = wr.astype(BF16)
    wr_split = jnp.concatenate([wrh, (wr - wrh.astype(F32)).astype(BF16)], axis=1)
    x1, h2, route, cnt = _post(og, z, outs, lses, x2, gt_a, sc_f, sh_f,
                               gdn_norm.reshape(1, GDN_DV), attn_norm.reshape(1, ATTN_W),
                               w_out.astype(BF16), norm_ffn.reshape(1, d), wr_split, expand, lstrict, seq)

    blk = EXPERT_BLK
    counts = cnt[0, :N_EXPERTS].astype(jnp.int32)
    padded = (counts + blk - 1) // blk * blk
    pends = jnp.cumsum(padded)
    pstarts = pends - padded
    eids = route[:, 0:TOP_K].astype(jnp.int32)
    ranks = route[:, 4:4 + TOP_K].astype(jnp.int32)
    expert_ids = jnp.arange(N_EXPERTS, dtype=jnp.int32)
    seg_start = jnp.sum(jnp.where(eids[..., None] == expert_ids, pstarts, 0), axis=-1)
    dest = (seg_start + ranks).reshape(t * TOP_K)
    a = t * TOP_K
    p = -(-a // blk) * blk + N_EXPERTS * blk
    nblk = p // blk
    blk_start = jnp.arange(nblk, dtype=jnp.int32) * blk
    blk_e = jnp.minimum(jnp.sum((pends[None, :] <= blk_start[:, None]).astype(jnp.int32), axis=1),
                        N_EXPERTS - 1)
    nused = (pends[-1] // blk).astype(jnp.int32).reshape(1)

    xs = _dispatch(dest, pstarts + counts, padded - counts, nused, h2, p)
    ys = _experts(blk_e, nused, xs, w_gate, w_up, w_down)
    return ys, dest, x1, route, gt_f


def _gp_cols(a_log, dt_bias):
    z = jnp.zeros((8, LANES), F32)
    z = z.at[GDN_HEADS:2 * GDN_HEADS, 0].set(a_log.astype(F32))
    z = z.at[GDN_HEADS:2 * GDN_HEADS, 1].set(dt_bias.astype(F32))
    return z


def kernel(x, c, w_ada, b_ada, norm_mix, w_in, w_conv, a_log, dt_bias, gdn_norm, attn_norm, w_out,
           rel_bias, norm_ffn, w_router_group, w_router_expert, w_gate, w_up, w_down, norm_final):
    bsz, seq, d = x.shape
    depth = w_ada.shape[0]
    x2 = x.reshape(bsz * seq, d)
    for l in range(depth):
        ys, dest, x1, route, gt_f = _layer(
            x2, c, bsz, seq, w_ada[l], b_ada[l], norm_mix[l], w_in[l], w_conv[l], a_log[l], dt_bias[l],
            gdn_norm[l], attn_norm[l], w_out[l], rel_bias, norm_ffn[l], w_router_group[l],
            w_router_expert[l], w_gate[l], w_up[l], w_down[l])
        x2 = _combine(dest, ys, x1, route, gt_f, norm_final.reshape(1, d), seq, l == depth - 1)
    return x2.reshape(bsz, seq, d)
```
